```python
import jax, jax.numpy as jnp
from jax import lax
import numpy as np

D_MODEL = 1024
BATCH = 8
SEQ = 2048
DEPTH = 2
DEC_BATCH = 8
DEC_SEQ = 16
PAST_LEN = 4096

CHUNK = 64
SUB = 16
Q_BLOCK = 128
N_MEM = 256
HG_HEADS = 4
HG_DIM = 128
HG_WIDTH = HG_HEADS * HG_DIM
FX_HEADS = 4
FX_DIM = 64
FX_WIDTH = FX_HEADS * FX_DIM
MEM_HEADS = 4
MEM_DIM = 64
MEM_WIDTH = MEM_HEADS * MEM_DIM
MIX_WIDTH = HG_WIDTH + FX_WIDTH + MEM_WIDTH
IN_WIDTH = 4 * HG_WIDTH + 4 * FX_WIDTH + FX_HEADS + 2 * MEM_WIDTH
DN_ALPHA = (2 * DEPTH) ** 0.25
DN_BETA = (8 * DEPTH) ** -0.25
LN_EPS = 1e-5
RMS_EPS = 1e-6

kernel_name = 'hgrn2_fox_memory_deepnorm_stream_step'


def _layernorm(x, g, b):
    xf = x.astype(jnp.float32)
    mu = jnp.mean(xf, -1, keepdims=True)
    var = jnp.mean(jnp.square(xf - mu), -1, keepdims=True)
    return ((xf - mu) * lax.rsqrt(var + LN_EPS) * g.astype(jnp.float32) + b.astype(jnp.float32)).astype(x.dtype)


def _rmsnorm(x, g):
    return x * lax.rsqrt(jnp.mean(jnp.square(x), -1, keepdims=True) + RMS_EPS) * g.astype(jnp.float32)


def _split_in(h):
    sizes = (HG_WIDTH,) * 4 + (FX_WIDTH,) * 4 + (FX_HEADS,) + (MEM_WIDTH,) * 2
    idx = np.cumsum(sizes)[:-1].tolist()
    return jnp.split(h, idx, axis=-1)


def _hgrn2_chunk(S, inp):
    q, k, v, g = inp
    B_, H_, C, K = q.shape
    NS = C // SUB
    G = jnp.cumsum(g, axis=2)
    o = jnp.einsum('bhtk,bhkv->bhtv', q * jnp.exp(G), S)
    Gs = G.reshape(B_, H_, NS, SUB, K)
    qs = q.reshape(B_, H_, NS, SUB, K)
    ks = k.reshape(B_, H_, NS, SUB, K)
    ref = jnp.concatenate([jnp.zeros_like(Gs[:, :, :1, 0]), Gs[:, :, :-1, -1]], axis=2)
    q_ref = qs * jnp.exp(Gs - ref[:, :, :, None])
    pos = jnp.arange(C)
    before = pos[None, :] < (jnp.arange(NS) * SUB)[:, None]
    k_ref = k[:, :, None] * jnp.exp(jnp.where(before[None, None, :, :, None],
                                              ref[:, :, :, None] - G[:, :, None], -jnp.inf))
    a_off = jnp.einsum('bhitk,bhisk->bhits', q_ref, k_ref)
    tri = pos[:SUB, None] >= pos[None, :SUB]
    diff = Gs[:, :, :, :, None, :] - Gs[:, :, :, None, :, :]
    dec = jnp.exp(jnp.where(tri[:, :, None], diff, -jnp.inf))
    a_diag = jnp.einsum('bhitk,bhisk,bhitsk->bhits', qs, ks, dec)
    a_diag = jnp.einsum('bhits,ij->bhitjs', a_diag, jnp.eye(NS, dtype=a_diag.dtype)).reshape(B_, H_, NS, SUB, C)
    a = (a_off + a_diag).reshape(B_, H_, C, C)
    o = o + jnp.einsum('bhts,bhsv->bhtv', a, v)
    G_last = G[:, :, -1]
    k_end = k * jnp.exp(G_last[:, :, None] - G)
    S = jnp.exp(G_last)[..., None] * S + jnp.einsum('bhsk,bhsv->bhkv', k_end, v)
    return S, o


def _hgrn2(S0, q, k, v, g):
    B_, T, H_, _ = q.shape
    pad = (-T) % CHUNK
    padt = lambda a: jnp.pad(a, ((0, 0), (0, pad), (0, 0), (0, 0)))
    n = (T + pad) // CHUNK
    chunks = lambda a: padt(a).reshape(B_, n, CHUNK, H_, a.shape[-1]).transpose(1, 0, 3, 2, 4)
    S, o = lax.scan(_hgrn2_chunk, S0, (chunks(q), chunks(k), chunks(v), chunks(g)))
    o = o.transpose(1, 0, 3, 2, 4).reshape(B_, n * CHUNK, H_, -1)[:, :T]
    return o, S


def _fox_block(q, k, v, Dq, Dk, qpos):
    s = jnp.einsum('bthd,bshd->bhts', q, k).astype(jnp.float32) * (FX_DIM ** -0.5)
    s = s + jnp.swapaxes(Dq, 1, 2)[..., :, None] - jnp.swapaxes(Dk, 1, 2)[..., None, :]
    mask = jnp.arange(k.shape[1])[None, :] <= qpos[:, None]
    p = jax.nn.softmax(jnp.where(mask, s, -jnp.inf), axis=-1)
    return jnp.einsum('bhts,bshd->bthd', p.astype(v.dtype), v)


def _fox_prompt(q, k, v, logf):
    B_, T, H_, Dh = q.shape
    nb = T // Q_BLOCK
    D = jnp.cumsum(logf, axis=1)
    qb = jnp.swapaxes(q.reshape(B_, nb, Q_BLOCK, H_, Dh), 0, 1)
    Db = jnp.swapaxes(D.reshape(B_, nb, Q_BLOCK, H_), 0, 1)
    pos = (jnp.arange(nb) * Q_BLOCK)[:, None] + jnp.arange(Q_BLOCK)[None, :]
    out = lax.map(lambda a: _fox_block(a[0], k, v, a[1], D, a[2]), (qb, Db, pos))
    return jnp.swapaxes(out, 0, 1).reshape(B_, T, H_, Dh)


def _fox_sample(q, k, v, logf, past_k, past_v, past_logf):
    P = past_k.shape[1]
    T = q.shape[1]
    k_all = jnp.concatenate([past_k.astype(k.dtype), k], axis=1)
    v_all = jnp.concatenate([past_v.astype(v.dtype), v], axis=1)
    D = jnp.cumsum(jnp.concatenate([past_logf.astype(jnp.float32), logf], axis=1), axis=1)
    return _fox_block(q, k_all, v_all, D[:, P:], D, P + jnp.arange(T))


def _cross(q, mk, mv):
    s = jnp.einsum('bthd,bmhd->bhtm', q, mk.astype(q.dtype)).astype(jnp.float32) * (MEM_DIM ** -0.5)
    p = jax.nn.softmax(s, axis=-1)
    return jnp.einsum('bhtm,bmhd->bthd', p.astype(q.dtype), mv.astype(q.dtype))


def _layer(x, w_in, b_forget, lb, hg_g, w_out, ln_g, ln_b, hg_state, mem_k, mem_v,
           past_k=None, past_v=None, past_logf=None):
    f32 = jnp.float32
    B_, T, _ = x.shape
    hq, hf, hi, hgate, fq, fk, fv, fgate, fflog, mq, mgate = _split_in(x @ w_in)
    hd = lambda a, n: a.reshape(B_, T, n, -1)
    forget = lb + (1.0 - lb) * jax.nn.sigmoid(hf.astype(f32))
    o_hg, hg_new = _hgrn2(hg_state.astype(f32), hd(jax.nn.silu(hq.astype(f32)), HG_HEADS),
                          hd(1.0 - forget, HG_HEADS), hd(hi.astype(f32), HG_HEADS),
                          hd(jnp.log(forget), HG_HEADS))
    o_hg = _rmsnorm(o_hg, hg_g).reshape(B_, T, HG_WIDTH) * jax.nn.silu(hgate.astype(f32))
    q, k, v = hd(fq, FX_HEADS), hd(fk, FX_HEADS), hd(fv, FX_HEADS)
    logf = jax.nn.log_sigmoid((fflog + b_forget).astype(f32))
    if past_k is None:
        o_fx = _fox_prompt(q, k, v, logf)
    else:
        o_fx = _fox_sample(q, k, v, logf, past_k, past_v, past_logf)
    o_fx = o_fx.reshape(B_, T, FX_WIDTH).astype(f32) * jax.nn.silu(fgate.astype(f32))
    o_mem = _cross(hd(mq, MEM_HEADS), mem_k, mem_v).reshape(B_, T, MEM_WIDTH).astype(f32) * jax.nn.silu(mgate.astype(f32))
    mix = jnp.concatenate([o_hg, o_fx, o_mem], axis=-1).astype(x.dtype)
    y = _layernorm(DN_ALPHA * x + mix @ w_out, ln_g, ln_b)
    return y, hg_new.astype(x.dtype), k, v, logf.astype(x.dtype)


def setup_inputs(seed: int = 0) -> dict:
    key = jax.random.key(seed)
    ks = jax.random.split(key, 16)
    nrm = lambda i, shape: jax.random.normal(ks[i], shape, jnp.float32)
    return {
        'x_prompt': nrm(0, (BATCH, SEQ, D_MODEL)),
        'x_sample': nrm(1, (DEC_BATCH, DEC_SEQ, D_MODEL)),
        'mem_prompt': nrm(2, (BATCH, N_MEM, D_MODEL)),
        'state_hgrn': 0.5 * nrm(3, (DEPTH, DEC_BATCH, HG_HEADS, HG_DIM, HG_DIM)),
        'cache_fox_k': nrm(4, (DEPTH, DEC_BATCH, PAST_LEN, FX_HEADS, FX_DIM)),
        'cache_fox_v': nrm(5, (DEPTH, DEC_BATCH, PAST_LEN, FX_HEADS, FX_DIM)),
        'cache_fox_logf': jax.nn.log_sigmoid(2.0 + nrm(6, (DEPTH, DEC_BATCH, PAST_LEN, FX_HEADS))),
        'cache_mem_k': nrm(7, (DEPTH, DEC_BATCH, N_MEM, MEM_HEADS, MEM_DIM)),
        'cache_mem_v': nrm(8, (DEPTH, DEC_BATCH, N_MEM, MEM_HEADS, MEM_DIM)),
        'w_in': nrm(9, (DEPTH, D_MODEL, IN_WIDTH)) * D_MODEL ** -0.5,
        'b_fox_forget': 1.0 + 0.5 * nrm(10, (DEPTH, FX_HEADS)),
        'hgrn_lower_bounds': 0.1 * nrm(11, (DEPTH, HG_WIDTH)),
        'hgrn_norm_g': 1.0 + 0.02 * nrm(12, (DEPTH, HG_DIM)),
        'w_mem_kv': nrm(13, (DEPTH, D_MODEL, 2 * MEM_WIDTH)) * D_MODEL ** -0.5,
        'w_out': nrm(14, (DEPTH, MIX_WIDTH, D_MODEL)) * (MIX_WIDTH ** -0.5 * DN_BETA),
        'ln_g': 1.0 + 0.02 * nrm(15, (DEPTH, D_MODEL)),
        'ln_b': 0.02 * jax.random.normal(jax.random.fold_in(key, 99), (DEPTH, D_MODEL), jnp.float32),
    }


def reference(x_prompt, x_sample, mem_prompt, state_hgrn, cache_fox_k, cache_fox_v, cache_fox_logf,
              cache_mem_k, cache_mem_v, w_in, b_fox_forget, hgrn_lower_bounds, hgrn_norm_g,
              w_mem_kv, w_out, ln_g, ln_b):
    sm = jax.nn.softmax(hgrn_lower_bounds.astype(jnp.float32), axis=0)
    lower = jnp.cumsum(sm, axis=0) - sm[0]
    Bp = x_prompt.shape[0]
    yp, ys = x_prompt, x_sample
    p_hg, p_k, p_v, p_lf, p_mk, p_mv = [], [], [], [], [], []
    s_hg, s_k, s_v, s_lf = [], [], [], []
    for l in range(DEPTH):
        w = (w_in[l], b_fox_forget[l], lower[l], hgrn_norm_g[l], w_out[l], ln_g[l], ln_b[l])
        mk, mv = jnp.split(mem_prompt @ w_mem_kv[l], 2, axis=-1)
        mk = mk.reshape(Bp, N_MEM, MEM_HEADS, MEM_DIM)
        mv = mv.reshape(Bp, N_MEM, MEM_HEADS, MEM_DIM)
        hg0 = jnp.zeros((Bp, HG_HEADS, HG_DIM, HG_DIM), jnp.float32)
        yp, hg, k, v, lf = _layer(yp, *w, hg0, mk, mv)
        p_hg.append(hg); p_k.append(k); p_v.append(v); p_lf.append(lf); p_mk.append(mk); p_mv.append(mv)
        ys, hg, k, v, lf = _layer(ys, *w, state_hgrn[l], cache_mem_k[l], cache_mem_v[l],
                                  cache_fox_k[l], cache_fox_v[l], cache_fox_logf[l])
        s_hg.append(hg); s_k.append(k); s_v.append(v); s_lf.append(lf)
    return (yp, ys, jnp.stack(p_hg), jnp.stack(p_k), jnp.stack(p_v), jnp.stack(p_lf),
            jnp.stack(p_mk), jnp.stack(p_mv), jnp.stack(s_hg), jnp.stack(s_k), jnp.stack(s_v),
            jnp.stack(s_lf))
```

```python
import functools

import jax
import jax.numpy as jnp
from jax import lax
from jax.experimental import pallas as pl
from jax.experimental.pallas import tpu as pltpu

F32 = jnp.float32
BF16 = jnp.bfloat16

HG_HEADS = 4
HG_DIM = 128
HG_WIDTH = HG_HEADS * HG_DIM
FX_HEADS = 4
FX_DIM = 64
FX_WIDTH = FX_HEADS * FX_DIM
MEM_HEADS = 4
MEM_DIM = 64
MEM_WIDTH = MEM_HEADS * MEM_DIM
SUB = 16
LN_EPS = 1e-5
RMS_EPS = 1e-6
LANES = 128
SUBLANES = 8
VMEM_LIMIT = 56 * 1024 * 1024

C_HQ, C_HK, C_HV, C_HG, C_HGATE = 0, 512, 1024, 1536, 2048
C_FQ, C_FGATE, C_MQ, C_MGATE = 2560, 2816, 3072, 3328
ACT_WIDTH = 3584
W_HQ, W_HF, W_HI, W_HGATE, W_FQ, W_FK, W_FV, W_FGATE, W_MQ, W_MGATE = (
    0, 512, 1024, 1536, 2048, 2304, 2560, 2816, 3072, 3328)
W_MAIN = 3584


def _nt(a, b, precision=None):
    return lax.dot_general(a, b, (((1,), (1,)), ((), ())), precision=precision,
                           preferred_element_type=F32)


def _mm(a, b, precision=None):
    return jnp.dot(a, b, precision=precision, preferred_element_type=F32)


def _silu(x):
    return x * jax.nn.sigmoid(x)


def _log_sigmoid(x):
    return jnp.minimum(x, 0.0) - jnp.log(1.0 + jnp.exp(-jnp.abs(x)))


def _proj_kernel(x_ref, w_ref, wfl_ref, wflt_ref, bfl_ref, bflt_ref, lbs_ref,
                 act_ref, k_ref, v_ref, lf_ref, dn_ref, dt_ref, cn_ref, ct_ref,
                 *, layer, seg, tiles_per_seq):
    tm = x_ref.shape[0]
    xb = x_ref[...].astype(BF16)

    def proj(c0, width):
        return _mm(xb, w_ref[:, c0:c0 + width])

    lbs = lbs_ref[...]
    e = jnp.exp(lbs - jnp.max(lbs, axis=0, keepdims=True))
    sm = e / jnp.sum(e, axis=0, keepdims=True)
    lb = jnp.sum(sm[:layer + 1], axis=0, keepdims=True) - sm[0:1]

    act_ref[:, C_HQ:C_HQ + HG_WIDTH] = _silu(proj(W_HQ, HG_WIDTH))
    forget = lb + (1.0 - lb) * jax.nn.sigmoid(proj(W_HF, HG_WIDTH))
    act_ref[:, C_HK:C_HK + HG_WIDTH] = 1.0 - forget
    act_ref[:, C_HG:C_HG + HG_WIDTH] = jnp.log(forget)
    act_ref[:, C_HV:C_HV + HG_WIDTH] = proj(W_HI, HG_WIDTH)
    act_ref[:, C_HGATE:C_HGATE + HG_WIDTH] = _silu(proj(W_HGATE, HG_WIDTH))
    act_ref[:, C_FQ:C_FQ + FX_WIDTH] = proj(W_FQ, FX_WIDTH)
    k_ref[...] = proj(W_FK, FX_WIDTH)
    v_ref[...] = proj(W_FV, FX_WIDTH)
    act_ref[:, C_FGATE:C_FGATE + FX_WIDTH] = _silu(proj(W_FGATE, FX_WIDTH))
    act_ref[:, C_MQ:C_MQ + MEM_WIDTH] = proj(W_MQ, MEM_WIDTH)
    act_ref[:, C_MGATE:C_MGATE + MEM_WIDTH] = _silu(proj(W_MGATE, MEM_WIDTH))

    lf_nat = _log_sigmoid(_mm(xb, wfl_ref[...]) + bfl_ref[...])
    lf_t = _log_sigmoid(_nt(wflt_ref[...], xb) + bflt_ref[...])
    lf_ref[...] = lf_nat[:, :FX_HEADS]

    r = lax.broadcasted_iota(jnp.int32, (tm, tm), 0)
    c = lax.broadcasted_iota(jnp.int32, (tm, tm), 1)
    same = (r // seg) == (c // seg)
    lower = jnp.where((c <= r) & same, 1.0, 0.0).astype(F32)
    upper = jnp.where((r <= c) & same, 1.0, 0.0).astype(F32)
    d_nat = _mm(lower, lf_nat, precision=lax.Precision.HIGHEST)
    d_t = _mm(lf_t, upper, precision=lax.Precision.HIGHEST)
    if tiles_per_seq > 1:
        @pl.when(pl.program_id(0) % tiles_per_seq == 0)
        def _():
            cn_ref[...] = jnp.zeros_like(cn_ref)
            ct_ref[...] = jnp.zeros_like(ct_ref)
        d_nat = d_nat + cn_ref[0:1, :]
        d_t = d_t + ct_ref[:, 0:1]
        cn_ref[...] = jnp.broadcast_to(d_nat[tm - 1:tm, :], cn_ref.shape)
        ct_ref[...] = jnp.broadcast_to(d_t[:, tm - 1:tm], ct_ref.shape)
    dn_ref[...] = d_nat
    dt_ref[0] = d_t


def _proj(x, w_main, w_fl, w_flt, b_fl, b_flt, lbs, *, layer, seq_len, tm):
    m, d = x.shape
    ntiles = m // tm
    seg = min(seq_len, tm)
    tiles_per_seq = max(seq_len // tm, 1)
    kern = functools.partial(_proj_kernel, layer=layer, seg=seg, tiles_per_seq=tiles_per_seq)
    full = lambda shape: pl.BlockSpec(shape, lambda i: (0,) * len(shape))
    return pl.pallas_call(
        kern,
        grid=(ntiles,),
        in_specs=[
            pl.BlockSpec((tm, d), lambda i: (i, 0)),
            full(w_main.shape), full(w_fl.shape), full(w_flt.shape),
            full(b_fl.shape), full(b_flt.shape), full(lbs.shape),
        ],
        out_specs=[
            pl.BlockSpec((tm, ACT_WIDTH), lambda i: (i, 0)),
            pl.BlockSpec((tm, FX_WIDTH), lambda i: (i, 0)),
            pl.BlockSpec((tm, FX_WIDTH), lambda i: (i, 0)),
            pl.BlockSpec((tm, FX_HEADS), lambda i: (i, 0)),
            pl.BlockSpec((tm, LANES), lambda i: (i, 0)),
            pl.BlockSpec((1, SUBLANES, tm), lambda i: (i, 0, 0)),
        ],
        out_shape=[
            jax.ShapeDtypeStruct((m, ACT_WIDTH), F32),
            jax.ShapeDtypeStruct((m, FX_WIDTH), F32),
            jax.ShapeDtypeStruct((m, FX_WIDTH), F32),
            jax.ShapeDtypeStruct((m, FX_HEADS), F32),
            jax.ShapeDtypeStruct((m, LANES), F32),
            jax.ShapeDtypeStruct((ntiles, SUBLANES, tm), F32),
        ],
        scratch_shapes=[pltpu.VMEM((SUBLANES, LANES), F32), pltpu.VMEM((SUBLANES, LANES), F32)],
        compiler_params=pltpu.CompilerParams(
            dimension_semantics=("arbitrary",), vmem_limit_bytes=VMEM_LIMIT),
        name="proj",
    )(x, w_main, w_fl, w_flt, b_fl, b_flt, lbs)


def _hgrn_step(q, k, v, g, st, lower, rows):
    gc = _mm(lower, g, precision=lax.Precision.HIGHEST)
    o = _nt((q * jnp.exp(gc)).astype(BF16), st.astype(BF16))
    for s in range(SUB):
        dec = jnp.exp(jnp.where(rows >= s, gc - gc[s:s + 1, :], -jnp.inf))
        w = jnp.sum(q * dec * k[s:s + 1, :], axis=-1, keepdims=True)
        o = o + w * v[s:s + 1, :]
    g_last = gc[SUB - 1:SUB, :]
    k_end = k * jnp.exp(g_last - gc)
    upd = lax.dot_general(v.astype(BF16), k_end.astype(BF16), (((0,), (0,)), ((), ())),
                          preferred_element_type=F32)
    return o, st * jnp.exp(g_last) + upd


def _hgrn_kernel(*refs, nsub, has_init):
    if has_init:
        q_ref, k_ref, v_ref, g_ref, gate_ref, ng_ref, s0_ref, o_ref, so_ref, st_ref = refs
    else:
        q_ref, k_ref, v_ref, g_ref, gate_ref, ng_ref, o_ref, so_ref, st_ref = refs
    t = pl.program_id(2)

    @pl.when(t == 0)
    def _():
        if has_init:
            st_ref[...] = s0_ref[0, 0].T
        else:
            st_ref[...] = jnp.zeros_like(st_ref)

    r = lax.broadcasted_iota(jnp.int32, (SUB, SUB), 0)
    c = lax.broadcasted_iota(jnp.int32, (SUB, SUB), 1)
    lower = jnp.where(c <= r, 1.0, 0.0).astype(F32)
    rows = lax.broadcasted_iota(jnp.int32, (SUB, HG_DIM), 0)
    ng = ng_ref[...]

    def step(i, carry):
        r0 = pl.multiple_of(i * SUB, SUB)
        sl = pl.ds(r0, SUB)
        o, st = _hgrn_step(q_ref[sl, :], k_ref[sl, :], v_ref[sl, :], g_ref[sl, :],
                           st_ref[...], lower, rows)
        st_ref[...] = st
        o = o * lax.rsqrt(jnp.mean(o * o, axis=-1, keepdims=True) + RMS_EPS) * ng
        o_ref[sl, :] = (o * gate_ref[sl, :]).astype(o_ref.dtype)
        return carry

    lax.fori_loop(0, nsub, step, 0)

    @pl.when(t == pl.num_programs(2) - 1)
    def _():
        so_ref[0, 0] = st_ref[...].T


def _hgrn(act, norm_g, s0, *, batch, seq_len, tt):
    m = act.shape[0]
    nt = seq_len // tt
    has_init = s0 is not None
    blk = lambda cb: pl.BlockSpec((tt, HG_DIM), lambda b, h, t, cb=cb: (b * nt + t, cb // HG_DIM + h))
    in_specs = [blk(C_HQ), blk(C_HK), blk(C_HV), blk(C_HG), blk(C_HGATE),
                pl.BlockSpec((1, HG_DIM), lambda b, h, t: (0, 0))]
    args = [act, act, act, act, act, norm_g]
    if has_init:
        in_specs.append(pl.BlockSpec((1, 1, HG_DIM, HG_DIM), lambda b, h, t: (b, h, 0, 0)))
        args.append(s0)
    return pl.pallas_call(
        functools.partial(_hgrn_kernel, nsub=tt // SUB, has_init=has_init),
        grid=(batch, HG_HEADS, nt),
        in_specs=in_specs,
        out_specs=[
            pl.BlockSpec((tt, HG_DIM), lambda b, h, t: (b * nt + t, h)),
            pl.BlockSpec((1, 1, HG_DIM, HG_DIM), lambda b, h, t: (b, h, 0, 0)),
        ],
        out_shape=[
            jax.ShapeDtypeStruct((m, HG_WIDTH), BF16),
            jax.ShapeDtypeStruct((batch, HG_HEADS, HG_DIM, HG_DIM), F32),
        ],
        scratch_shapes=[pltpu.VMEM((HG_DIM, HG_DIM), F32)],
        compiler_params=pltpu.CompilerParams(
            dimension_semantics=("arbitrary", "arbitrary", "arbitrary"), vmem_limit_bytes=VMEM_LIMIT),
        name="hgrn",
    )(*args)


def _head_mask(hh):
    lane = lax.broadcasted_iota(jnp.int32, (1, LANES), 1)
    return (lane // FX_DIM) == hh


def _cross(mq_pair, mk_pair, mv_pair, hh):
    qm = jnp.where(_head_mask(hh), mq_pair * (MEM_DIM ** -0.5), 0.0).astype(BF16)
    s = _nt(qm, mk_pair.astype(BF16))
    p = jnp.exp(s - jnp.max(s, axis=-1, keepdims=True))
    o = _mm(p.astype(BF16), mv_pair.astype(BF16))
    return o / jnp.sum(p, axis=-1, keepdims=True)


def _softmax_update(s, vb, m, l, acc):
    m_new = jnp.maximum(m, jnp.max(s, axis=-1, keepdims=True))
    alpha = jnp.exp(m - m_new)
    p = jnp.exp(s - m_new)
    l = alpha * l + jnp.sum(p, axis=-1, keepdims=True)
    acc = alpha * acc + _mm(p.astype(BF16), vb)
    return m_new, l, acc


def _attn_prompt_kernel(fq_ref, fgate_ref, mq_ref, mgate_ref, k_ref, v_ref, dn_ref, dt_ref,
                        mk_ref, mv_ref, o_ref):
    tb = fq_ref.shape[0]
    qi = pl.program_id(1)
    row = lax.broadcasted_iota(jnp.int32, (tb, tb), 0)
    col = lax.broadcasted_iota(jnp.int32, (tb, tb), 1)
    causal = col <= row
    dn = dn_ref[...]
    for p in range(FX_HEADS // 2):
        ls = slice(p * LANES, (p + 1) * LANES)
        q_pair = fq_ref[:, ls] * (FX_DIM ** -0.5)
        heads = []
        for hh in range(2):
            h = 2 * p + hh
            qm = jnp.where(_head_mask(hh), q_pair, 0.0).astype(BF16)
            dq = dn[:, h:h + 1]

            def scores(j, qm=qm, dq=dq, h=h, ls=ls):
                r0 = pl.multiple_of(j * tb, tb)
                kb = k_ref[pl.ds(r0, tb), ls].astype(BF16)
                vb = v_ref[pl.ds(r0, tb), ls].astype(BF16)
                dk = dt_ref[j, h:h + 1, :]
                return _nt(qm, kb) + dq - dk, vb

            def body(j, carry, scores=scores):
                s, vb = scores(j)
                return _softmax_update(s, vb, *carry)

            init = (jnp.full((tb, 1), -jnp.inf, F32), jnp.zeros((tb, 1), F32),
                    jnp.zeros((tb, LANES), F32))
            carry = lax.fori_loop(0, qi, body, init)
            s, vb = scores(qi)
            m, l, acc = _softmax_update(jnp.where(causal, s, -jnp.inf), vb, *carry)
            heads.append(acc / l)
        o_fx = jnp.where(_head_mask(0), heads[0], heads[1]) * fgate_ref[:, ls]
        o_ref[:, ls] = o_fx.astype(o_ref.dtype)

        mq_pair = mq_ref[:, ls]
        om = [_cross(mq_pair, mk_ref[:, ls], mv_ref[:, ls], hh) for hh in range(2)]
        o_mem = jnp.where(_head_mask(0), om[0], om[1]) * mgate_ref[:, ls]
        lm = slice(FX_WIDTH + p * LANES, FX_WIDTH + (p + 1) * LANES)
        o_ref[:, lm] = o_mem.astype(o_ref.dtype)


def _attn_prompt(act, k, v, d_nat, d_t, mk, mv, *, batch, seq_len, tb):
    m = act.shape[0]
    nq = seq_len // tb
    n_mem = mk.shape[0] // batch
    ablk = lambda c0: pl.BlockSpec((tb, FX_WIDTH), lambda b, q, c0=c0: (b * nq + q, c0 // FX_WIDTH))
    return pl.pallas_call(
        _attn_prompt_kernel,
        grid=(batch, nq),
        in_specs=[
            ablk(C_FQ), ablk(C_FGATE), ablk(C_MQ), ablk(C_MGATE),
            pl.BlockSpec((seq_len, FX_WIDTH), lambda b, q: (b, 0)),
            pl.BlockSpec((seq_len, FX_WIDTH), lambda b, q: (b, 0)),
            pl.BlockSpec((tb, LANES), lambda b, q: (b * nq + q, 0)),
            pl.BlockSpec((nq, SUBLANES, tb), lambda b, q: (b, 0, 0)),
            pl.BlockSpec((n_mem, MEM_WIDTH), lambda b, q: (b, 0)),
            pl.BlockSpec((n_mem, MEM_WIDTH), lambda b, q: (b, 0)),
        ],
        out_specs=pl.BlockSpec((tb, FX_WIDTH + MEM_WIDTH), lambda b, q: (b * nq + q, 0)),
        out_shape=jax.ShapeDtypeStruct((m, FX_WIDTH + MEM_WIDTH), BF16),
        compiler_params=pltpu.CompilerParams(
            dimension_semantics=("arbitrary", "arbitrary"), vmem_limit_bytes=VMEM_LIMIT),
        name="attn_prompt",
    )(act, act, act, act, k, v, d_nat, d_t, mk, mv)


def _attn_sample_kernel(fq_ref, fgate_ref, mq_ref, mgate_ref, kn_ref, vn_ref, dn_ref, dt_ref,
                        pk_ref, pv_ref, plf_ref, mk_ref, mv_ref, o_ref, *, chunk):
    ts = fq_ref.shape[0]
    n_new = kn_ref.shape[0]
    past = pk_ref.shape[1]
    nchunks = past // chunk
    b = pl.program_id(0)

    r = lax.broadcasted_iota(jnp.int32, (chunk, chunk), 0)
    c = lax.broadcasted_iota(jnp.int32, (chunk, chunk), 1)
    upper = jnp.where(r <= c, 1.0, 0.0).astype(F32)
    total = jnp.zeros((SUBLANES, 1), F32)
    d_past = []
    for ci in range(nchunks):
        d = _mm(plf_ref[0, :, ci * chunk:(ci + 1) * chunk], upper,
                precision=lax.Precision.HIGHEST) + total
        d_past.append(d)
        total = d[:, chunk - 1:chunk]

    row = lax.broadcasted_iota(jnp.int32, (ts, n_new), 0)
    col = lax.broadcasted_iota(jnp.int32, (ts, n_new), 1)
    visible = (col // ts == b) & (col % ts <= row)
    dn = dn_ref[...]
    dt_new = dt_ref[0]
    for p in range(FX_HEADS // 2):
        ls = slice(p * LANES, (p + 1) * LANES)
        q_pair = fq_ref[:, ls] * (FX_DIM ** -0.5)
        heads = []
        for hh in range(2):
            h = 2 * p + hh
            qm = jnp.where(_head_mask(hh), q_pair, 0.0).astype(BF16)
            tot_h = total[h:h + 1, :]
            dq = tot_h + dn[:, h:h + 1]
            carry = (jnp.full((ts, 1), -jnp.inf, F32), jnp.zeros((ts, 1), F32),
                     jnp.zeros((ts, LANES), F32))
            for ci in range(nchunks):
                cs = slice(ci * chunk, (ci + 1) * chunk)
                s = _nt(qm, pk_ref[0, cs, ls].astype(BF16)) + dq - d_past[ci][h:h + 1, :]
                carry = _softmax_update(s, pv_ref[0, cs, ls].astype(BF16), *carry)
            s = _nt(qm, kn_ref[:, ls].astype(BF16)) + dq - (tot_h + dt_new[h:h + 1, :])
            m, l, acc = _softmax_update(jnp.where(visible, s, -jnp.inf),
                                        vn_ref[:, ls].astype(BF16), *carry)
            heads.append(acc / l)
        o_fx = jnp.where(_head_mask(0), heads[0], heads[1]) * fgate_ref[:, ls]
        o_ref[:, ls] = o_fx.astype(o_ref.dtype)

        mq_pair = mq_ref[:, ls]
        om = [_cross(mq_pair, mk_ref[0, :, ls], mv_ref[0, :, ls], hh) for hh in range(2)]
        o_mem = jnp.where(_head_mask(0), om[0], om[1]) * mgate_ref[:, ls]
        lm = slice(FX_WIDTH + p * LANES, FX_WIDTH + (p + 1) * LANES)
        o_ref[:, lm] = o_mem.astype(o_ref.dtype)


def _attn_sample(act, k_new, v_new, d_nat, d_t, past_k, past_v, past_lf_t, mk, mv, *, batch, ts):
    m = act.shape[0]
    past = past_k.shape[1]
    n_mem = mk.shape[1]
    chunk = min(past, 256)
    ablk = lambda c0: pl.BlockSpec((ts, FX_WIDTH), lambda b, c0=c0: (b, c0 // FX_WIDTH))
    return pl.pallas_call(
        functools.partial(_attn_sample_kernel, chunk=chunk),
        grid=(batch,),
        in_specs=[
            ablk(C_FQ), ablk(C_FGATE), ablk(C_MQ), ablk(C_MGATE),
            pl.BlockSpec((m, FX_WIDTH), lambda b: (0, 0)),
            pl.BlockSpec((m, FX_WIDTH), lambda b: (0, 0)),
            pl.BlockSpec((ts, LANES), lambda b: (b, 0)),
            pl.BlockSpec((1, SUBLANES, m), lambda b: (0, 0, 0)),
            pl.BlockSpec((1, past, FX_WIDTH), lambda b: (b, 0, 0)),
            pl.BlockSpec((1, past, FX_WIDTH), lambda b: (b, 0, 0)),
            pl.BlockSpec((1, SUBLANES, past), lambda b: (b, 0, 0)),
            pl.BlockSpec((1, n_mem, MEM_WIDTH), lambda b: (b, 0, 0)),
            pl.BlockSpec((1, n_mem, MEM_WIDTH), lambda b: (b, 0, 0)),
        ],
        out_specs=pl.BlockSpec((ts, FX_WIDTH + MEM_WIDTH), lambda b: (b, 0)),
        out_shape=jax.ShapeDtypeStruct((m, FX_WIDTH + MEM_WIDTH), BF16),
        compiler_params=pltpu.CompilerParams(
            dimension_semantics=("arbitrary",), vmem_limit_bytes=VMEM_LIMIT),
        name="attn_sample",
    )(act, act, act, act, k_new, v_new, d_nat, d_t, past_k, past_v, past_lf_t, mk, mv)


def _memkv_kernel(x_ref, w_ref, k_ref, v_ref):
    kv = _mm(x_ref[...].astype(BF16), w_ref[...])
    k_ref[...] = kv[:, :MEM_WIDTH]
    v_ref[...] = kv[:, MEM_WIDTH:]


def _memkv(mem, w, *, tm):
    m, d = mem.shape
    return pl.pallas_call(
        _memkv_kernel,
        grid=(m // tm,),
        in_specs=[pl.BlockSpec((tm, d), lambda i: (i, 0)),
                  pl.BlockSpec(w.shape, lambda i: (0, 0))],
        out_specs=[pl.BlockSpec((tm, MEM_WIDTH), lambda i: (i, 0)),
                   pl.BlockSpec((tm, MEM_WIDTH), lambda i: (i, 0))],
        out_shape=[jax.ShapeDtypeStruct((m, MEM_WIDTH), F32),
                   jax.ShapeDtypeStruct((m, MEM_WIDTH), F32)],
        compiler_params=pltpu.CompilerParams(
            dimension_semantics=("arbitrary",), vmem_limit_bytes=VMEM_LIMIT),
        name="memkv",
    )(mem, w)


def _out_kernel(x_ref, hg_ref, at_ref, w_ref, g_ref, b_ref, y_ref, *, alpha):
    acc = _mm(hg_ref[...], w_ref[0:HG_WIDTH, :]) + _mm(at_ref[...], w_ref[HG_WIDTH:, :])
    z = alpha * x_ref[...] + acc
    mu = jnp.mean(z, axis=-1, keepdims=True)
    zc = z - mu
    var = jnp.mean(zc * zc, axis=-1, keepdims=True)
    y_ref[...] = zc * lax.rsqrt(var + LN_EPS) * g_ref[...] + b_ref[...]


def _out(x, mix_hg, mix_at, w, g, b, *, alpha, tm):
    m, d = x.shape
    return pl.pallas_call(
        functools.partial(_out_kernel, alpha=alpha),
        grid=(m // tm,),
        in_specs=[
            pl.BlockSpec((tm, d), lambda i: (i, 0)),
            pl.BlockSpec((tm, HG_WIDTH), lambda i: (i, 0)),
            pl.BlockSpec((tm, FX_WIDTH + MEM_WIDTH), lambda i: (i, 0)),
            pl.BlockSpec(w.shape, lambda i: (0, 0)),
            pl.BlockSpec((1, d), lambda i: (0, 0)),
            pl.BlockSpec((1, d), lambda i: (0, 0)),
        ],
        out_specs=pl.BlockSpec((tm, d), lambda i: (i, 0)),
        out_shape=jax.ShapeDtypeStruct((m, d), F32),
        compiler_params=pltpu.CompilerParams(
            dimension_semantics=("arbitrary",), vmem_limit_bytes=VMEM_LIMIT),
        name="out",
    )(x, mix_hg, mix_at, w, g, b)


def _pick_tile(n, target):
    t = min(n, target)
    while n % t:
        t //= 2
    return t


def kernel(x_prompt, x_sample, mem_prompt, state_hgrn, cache_fox_k, cache_fox_v, cache_fox_logf,
           cache_mem_k, cache_mem_v, w_in, b_fox_forget, hgrn_lower_bounds, hgrn_norm_g,
           w_mem_kv, w_out, ln_g, ln_b):
    depth, d_model, _ = w_in.shape
    bp, seq, _ = x_prompt.shape
    bs, dseq, _ = x_sample.shape
    past = cache_fox_k.shape[2]
    n_mem = mem_prompt.shape[1]
    alpha = (2 * depth) ** 0.25

    fl0 = 4 * HG_WIDTH + 4 * FX_WIDTH
    w_main = jnp.concatenate([w_in[:, :, :fl0], w_in[:, :, fl0 + FX_HEADS:]], axis=-1).astype(BF16)
    w_flc = w_in[:, :, fl0:fl0 + FX_HEADS]
    w_fl = jnp.pad(w_flc, ((0, 0), (0, 0), (0, LANES - FX_HEADS))).astype(BF16)
    w_flt = jnp.pad(jnp.swapaxes(w_flc, 1, 2), ((0, 0), (0, SUBLANES - FX_HEADS), (0, 0))).astype(BF16)
    b_fl = jnp.pad(b_fox_forget, ((0, 0), (0, LANES - FX_HEADS)))[:, None, :]
    b_flt = jnp.pad(b_fox_forget, ((0, 0), (0, SUBLANES - FX_HEADS)))[:, :, None]
    lbs = hgrn_lower_bounds.astype(F32)
    w_memb = w_mem_kv.astype(BF16)
    w_outb = w_out.astype(BF16)

    past_k = cache_fox_k.reshape(depth, bs, past, FX_WIDTH)
    past_v = cache_fox_v.reshape(depth, bs, past, FX_WIDTH)
    past_lf_t = jnp.pad(jnp.swapaxes(cache_fox_logf, 2, 3),
                        ((0, 0), (0, 0), (0, SUBLANES - FX_HEADS), (0, 0)))
    smem_k = cache_mem_k.reshape(depth, bs, n_mem, MEM_WIDTH)
    smem_v = cache_mem_v.reshape(depth, bs, n_mem, MEM_WIDTH)

    mp, ms = bp * seq, bs * dseq
    tb = _pick_tile(seq, 256)
    tt = _pick_tile(seq, 512)
    yp = x_prompt.reshape(mp, d_model)
    ys = x_sample.reshape(ms, d_model)
    mem2 = mem_prompt.reshape(bp * n_mem, d_model)

    p_hg, p_k, p_v, p_lf, p_mk, p_mv = [], [], [], [], [], []
    s_hg, s_k, s_v, s_lf = [], [], [], []
    for l in range(depth):
        wl = (w_main[l], w_fl[l], w_flt[l], b_fl[l], b_flt[l], lbs)
        ng = hgrn_norm_g[l][None, :]
        g_ln, b_ln = ln_g[l][None, :], ln_b[l][None, :]

        mk, mv = _memkv(mem2, w_memb[l], tm=_pick_tile(bp * n_mem, 512))
        act, k, v, lf, d_nat, d_t = _proj(yp, *wl, layer=l, seq_len=seq, tm=tb)
        mix_hg, hg = _hgrn(act, ng, None, batch=bp, seq_len=seq, tt=tt)
        mix_at = _attn_prompt(act, k, v, d_nat, d_t, mk, mv, batch=bp, seq_len=seq, tb=tb)
        yp = _out(yp, mix_hg, mix_at, w_outb[l], g_ln, b_ln, alpha=alpha, tm=tb)
        p_hg.append(hg); p_k.append(k); p_v.append(v); p_lf.append(lf); p_mk.append(mk); p_mv.append(mv)

        act, k, v, lf, d_nat, d_t = _proj(ys, *wl, layer=l, seq_len=dseq, tm=ms)
        mix_hg, hg = _hgrn(act, ng, state_hgrn[l], batch=bs, seq_len=dseq, tt=dseq)
        mix_at = _attn_sample(act, k, v, d_nat, d_t, past_k[l], past_v[l], past_lf_t[l],
                              smem_k[l], smem_v[l], batch=bs, ts=dseq)
        ys = _out(ys, mix_hg, mix_at, w_outb[l], g_ln, b_ln, alpha=alpha, tm=ms)
        s_hg.append(hg); s_k.append(k); s_v.append(v); s_lf.append(lf)

    st = jnp.stack
    return (yp.reshape(bp, seq, d_model), ys.reshape(bs, dseq, d_model),
            st(p_hg),
            st(p_k).reshape(depth, bp, seq, FX_HEADS, FX_DIM),
            st(p_v).reshape(depth, bp, seq, FX_HEADS, FX_DIM),
            st(p_lf).reshape(depth, bp, seq, FX_HEADS),
            st(p_mk).reshape(depth, bp, n_mem, MEM_HEADS, MEM_DIM),
            st(p_mv).reshape(depth, bp, n_mem, MEM_HEADS, MEM_DIM),
            st(s_hg),
            st(s_k).reshape(depth, bs, dseq, FX_HEADS, FX_DIM),
            st(s_v).reshape(depth, bs, dseq, FX_HEADS, FX_DIM),
            st(s_lf).reshape(depth, bs, dseq, FX_HEADS))
```

```python
import functools

import jax
import jax.numpy as jnp
from jax import lax
from jax.experimental import pallas as pl
from jax.experimental.pallas import tpu as pltpu

F32 = jnp.float32
BF16 = jnp.bfloat16

HG_HEADS = 4
HG_DIM = 128
HG_WIDTH = HG_HEADS * HG_DIM
FX_HEADS = 4
FX_DIM = 64
FX_WIDTH = FX_HEADS * FX_DIM
MEM_HEADS = 4
MEM_DIM = 64
MEM_WIDTH = MEM_HEADS * MEM_DIM
HG_GROUP = 128
LN_EPS = 1e-5
RMS_EPS = 1e-6
LANES = 128
SUBLANES = 8
VMEM_LIMIT = 56 * 1024 * 1024

C_HQ, C_HK, C_HV, C_HG, C_HGATE = 0, 512, 1024, 1536, 2048
C_FQ, C_FGATE, C_MQ, C_MGATE = 2560, 2816, 3072, 3328
ACT_WIDTH = 3584
W_HQ, W_HF, W_HI, W_HGATE, W_FQ, W_FK, W_FV, W_FGATE, W_MQ, W_MGATE = (
    0, 512, 1024, 1536, 2048, 2304, 2560, 2816, 3072, 3328)
W_MAIN = 3584


def _nt(a, b, precision=None):
    return lax.dot_general(a, b, (((1,), (1,)), ((), ())), precision=precision,
                           preferred_element_type=F32)


def _mm(a, b, precision=None):
    return jnp.dot(a, b, precision=precision, preferred_element_type=F32)


def _silu(x):
    return x * jax.nn.sigmoid(x)


def _log_sigmoid(x):
    return jnp.minimum(x, 0.0) - jnp.log(1.0 + jnp.exp(-jnp.abs(x)))


def _proj_kernel(x_ref, w_ref, wfl_ref, wflt_ref, bfl_ref, bflt_ref, lbs_ref,
                 act_ref, k_ref, v_ref, lf_ref, dn_ref, dt_ref, cn_ref, ct_ref,
                 *, layer, seg, tiles_per_seq):
    tm = x_ref.shape[0]
    xb = x_ref[...].astype(BF16)

    def proj(c0, width):
        return _mm(xb, w_ref[:, c0:c0 + width])

    lbs = lbs_ref[...]
    e = jnp.exp(lbs - jnp.max(lbs, axis=0, keepdims=True))
    sm = e / jnp.sum(e, axis=0, keepdims=True)
    lb = jnp.sum(sm[:layer + 1], axis=0, keepdims=True) - sm[0:1]

    act_ref[:, C_HQ:C_HQ + HG_WIDTH] = _silu(proj(W_HQ, HG_WIDTH))
    forget = lb + (1.0 - lb) * jax.nn.sigmoid(proj(W_HF, HG_WIDTH))
    act_ref[:, C_HK:C_HK + HG_WIDTH] = 1.0 - forget
    act_ref[:, C_HG:C_HG + HG_WIDTH] = jnp.log(forget)
    act_ref[:, C_HV:C_HV + HG_WIDTH] = proj(W_HI, HG_WIDTH)
    act_ref[:, C_HGATE:C_HGATE + HG_WIDTH] = _silu(proj(W_HGATE, HG_WIDTH))
    act_ref[:, C_FQ:C_FQ + FX_WIDTH] = proj(W_FQ, FX_WIDTH)
    k_ref[...] = proj(W_FK, FX_WIDTH)
    v_ref[...] = proj(W_FV, FX_WIDTH)
    act_ref[:, C_FGATE:C_FGATE + FX_WIDTH] = _silu(proj(W_FGATE, FX_WIDTH))
    act_ref[:, C_MQ:C_MQ + MEM_WIDTH] = proj(W_MQ, MEM_WIDTH)
    act_ref[:, C_MGATE:C_MGATE + MEM_WIDTH] = _silu(proj(W_MGATE, MEM_WIDTH))

    lf_nat = _log_sigmoid(_mm(xb, wfl_ref[...]) + bfl_ref[...])
    lf_t = _log_sigmoid(_nt(wflt_ref[...], xb) + bflt_ref[...])
    lf_ref[...] = lf_nat[:, :FX_HEADS]

    r = lax.broadcasted_iota(jnp.int32, (tm, tm), 0)
    c = lax.broadcasted_iota(jnp.int32, (tm, tm), 1)
    same = (r // seg) == (c // seg)
    lower = jnp.where((c <= r) & same, 1.0, 0.0).astype(F32)
    upper = jnp.where((r <= c) & same, 1.0, 0.0).astype(F32)
    d_nat = _mm(lower, lf_nat, precision=lax.Precision.HIGHEST)
    d_t = _mm(lf_t, upper, precision=lax.Precision.HIGHEST)
    if tiles_per_seq > 1:
        @pl.when(pl.program_id(0) % tiles_per_seq == 0)
        def _():
            cn_ref[...] = jnp.zeros_like(cn_ref)
            ct_ref[...] = jnp.zeros_like(ct_ref)
        d_nat = d_nat + cn_ref[0:1, :]
        d_t = d_t + ct_ref[:, 0:1]
        cn_ref[...] = jnp.broadcast_to(d_nat[tm - 1:tm, :], cn_ref.shape)
        ct_ref[...] = jnp.broadcast_to(d_t[:, tm - 1:tm], ct_ref.shape)
    dn_ref[...] = d_nat
    dt_ref[0] = d_t


def _proj(x, w_main, w_fl, w_flt, b_fl, b_flt, lbs, *, layer, seq_len, tm):
    m, d = x.shape
    ntiles = m // tm
    seg = min(seq_len, tm)
    tiles_per_seq = max(seq_len // tm, 1)
    kern = functools.partial(_proj_kernel, layer=layer, seg=seg, tiles_per_seq=tiles_per_seq)
    full = lambda shape: pl.BlockSpec(shape, lambda i: (0,) * len(shape))
    return pl.pallas_call(
        kern,
        grid=(ntiles,),
        in_specs=[
            pl.BlockSpec((tm, d), lambda i: (i, 0)),
            full(w_main.shape), full(w_fl.shape), full(w_flt.shape),
            full(b_fl.shape), full(b_flt.shape), full(lbs.shape),
        ],
        out_specs=[
            pl.BlockSpec((tm, ACT_WIDTH), lambda i: (i, 0)),
            pl.BlockSpec((tm, FX_WIDTH), lambda i: (i, 0)),
            pl.BlockSpec((tm, FX_WIDTH), lambda i: (i, 0)),
            pl.BlockSpec((tm, FX_HEADS), lambda i: (i, 0)),
            pl.BlockSpec((tm, LANES), lambda i: (i, 0)),
            pl.BlockSpec((1, SUBLANES, tm), lambda i: (i, 0, 0)),
        ],
        out_shape=[
            jax.ShapeDtypeStruct((m, ACT_WIDTH), F32),
            jax.ShapeDtypeStruct((m, FX_WIDTH), F32),
            jax.ShapeDtypeStruct((m, FX_WIDTH), F32),
            jax.ShapeDtypeStruct((m, FX_HEADS), F32),
            jax.ShapeDtypeStruct((m, LANES), F32),
            jax.ShapeDtypeStruct((ntiles, SUBLANES, tm), F32),
        ],
        scratch_shapes=[pltpu.VMEM((SUBLANES, LANES), F32), pltpu.VMEM((SUBLANES, LANES), F32)],
        compiler_params=pltpu.CompilerParams(
            dimension_semantics=("arbitrary",), vmem_limit_bytes=VMEM_LIMIT),
        name="proj",
    )(x, w_main, w_fl, w_flt, b_fl, b_flt, lbs)


def _cumsum_rows(x, rows):
    n = x.shape[0]
    d = 1
    while d < n:
        x = x + jnp.where(rows >= d, pltpu.roll(x, d, axis=0), 0.0)
        d *= 2
    return x


def _half_ref(gc, m, rows):
    n, w = gc.shape
    if 2 * m >= SUBLANES:
        g3 = gc.reshape(n // (2 * m), 2 * m, w)
        return jnp.broadcast_to(g3[:, m - 1:m, :], g3.shape).reshape(n, w)
    pos = rows % (2 * m)
    if m == 2:
        return jnp.where(pos == 0, pltpu.roll(gc, n - 1, axis=0),
                         jnp.where(pos == 1, gc,
                                   jnp.where(pos == 2, pltpu.roll(gc, 1, axis=0),
                                             pltpu.roll(gc, 2, axis=0))))
    return jnp.where(pos == 0, gc, pltpu.roll(gc, 1, axis=0))


def _hgrn_group(q, k, v, g, st):
    n = q.shape[0]
    rows = lax.broadcasted_iota(jnp.int32, (n, HG_DIM), 0)
    rr = lax.broadcasted_iota(jnp.int32, (n, n), 0)
    cc = lax.broadcasted_iota(jnp.int32, (n, n), 1)
    diff_bits = rr ^ cc
    gc = _cumsum_rows(g, rows)
    o = _nt((q * jnp.exp(gc)).astype(BF16), st.astype(BF16))
    a = jnp.zeros((n, n), F32)
    m = n // 2
    while m >= 1:
        ref = _half_ref(gc, m, rows)
        first = (rows % (2 * m)) < m
        f = jnp.exp(jnp.where(first, ref - gc, gc - ref))
        a_m = _nt((q * f).astype(BF16), (k * f).astype(BF16))
        a = jnp.where((diff_bits >= m) & (diff_bits < 2 * m), a_m, a)
        m //= 2
    a = jnp.where(rr > cc, a, 0.0)
    o = o + _mm(a.astype(BF16), v.astype(BF16)) + jnp.sum(q * k, axis=-1, keepdims=True) * v
    g_last = gc[n - 1:n, :]
    k_end = k * jnp.exp(g_last - gc)
    upd = lax.dot_general(v.astype(BF16), k_end.astype(BF16), (((0,), (0,)), ((), ())),
                          preferred_element_type=F32)
    return o, st * jnp.exp(g_last) + upd


def _hgrn_kernel(*refs, group, has_init):
    if has_init:
        q_ref, k_ref, v_ref, g_ref, gate_ref, ng_ref, s0_ref, o_ref, so_ref, st_ref = refs
    else:
        q_ref, k_ref, v_ref, g_ref, gate_ref, ng_ref, o_ref, so_ref, st_ref = refs
    t = pl.program_id(2)

    @pl.when(t == 0)
    def _():
        if has_init:
            st_ref[...] = s0_ref[0, 0].T
        else:
            st_ref[...] = jnp.zeros_like(st_ref)

    ng = ng_ref[...]

    def step(i, st):
        sl = pl.ds(pl.multiple_of(i * group, group), group)
        o, st = _hgrn_group(q_ref[sl, :], k_ref[sl, :], v_ref[sl, :], g_ref[sl, :], st)
        o = o * lax.rsqrt(jnp.mean(o * o, axis=-1, keepdims=True) + RMS_EPS) * ng
        o_ref[sl, :] = (o * gate_ref[sl, :]).astype(o_ref.dtype)
        return st

    st_ref[...] = lax.fori_loop(0, q_ref.shape[0] // group, step, st_ref[...])

    @pl.when(t == pl.num_programs(2) - 1)
    def _():
        so_ref[0, 0] = st_ref[...].T


def _hgrn(act, norm_g, s0, *, batch, seq_len, tt):
    m = act.shape[0]
    nt = seq_len // tt
    has_init = s0 is not None
    group = min(tt, HG_GROUP)
    blk = lambda cb: pl.BlockSpec((tt, HG_DIM), lambda b, h, t, cb=cb: (b * nt + t, cb // HG_DIM + h))
    in_specs = [blk(C_HQ), blk(C_HK), blk(C_HV), blk(C_HG), blk(C_HGATE),
                pl.BlockSpec((1, HG_DIM), lambda b, h, t: (0, 0))]
    args = [act, act, act, act, act, norm_g]
    if has_init:
        in_specs.append(pl.BlockSpec((1, 1, HG_DIM, HG_DIM), lambda b, h, t: (b, h, 0, 0)))
        args.append(s0)
    return pl.pallas_call(
        functools.partial(_hgrn_kernel, group=group, has_init=has_init),
        grid=(batch, HG_HEADS, nt),
        in_specs=in_specs,
        out_specs=[
            pl.BlockSpec((tt, HG_DIM), lambda b, h, t: (b * nt + t, h)),
            pl.BlockSpec((1, 1, HG_DIM, HG_DIM), lambda b, h, t: (b, h, 0, 0)),
        ],
        out_shape=[
            jax.ShapeDtypeStruct((m, HG_WIDTH), BF16),
            jax.ShapeDtypeStruct((batch, HG_HEADS, HG_DIM, HG_DIM), F32),
        ],
        scratch_shapes=[pltpu.VMEM((HG_DIM, HG_DIM), F32)],
        compiler_params=pltpu.CompilerParams(
            dimension_semantics=("arbitrary", "arbitrary", "arbitrary"), vmem_limit_bytes=VMEM_LIMIT),
        name="hgrn",
    )(*args)


def _head_mask(hh):
    lane = lax.broadcasted_iota(jnp.int32, (1, LANES), 1)
    return (lane // FX_DIM) == hh


def _cross(mq_pair, mk_pair, mv_pair, hh):
    qm = jnp.where(_head_mask(hh), mq_pair * (MEM_DIM ** -0.5), 0.0).astype(BF16)
    s = _nt(qm, mk_pair.astype(BF16))
    p = jnp.exp(s - jnp.max(s, axis=-1, keepdims=True))
    o = _mm(p.astype(BF16), mv_pair.astype(BF16))
    return o / jnp.sum(p, axis=-1, keepdims=True)


def _softmax_update(s, vb, m, l, acc):
    m_new = jnp.maximum(m, jnp.max(s, axis=-1, keepdims=True))
    alpha = jnp.exp(m - m_new)
    p = jnp.exp(s - m_new)
    l = alpha * l + jnp.sum(p, axis=-1, keepdims=True)
    acc = alpha * acc + _mm(p.astype(BF16), vb)
    return m_new, l, acc


def _attn_prompt_kernel(fq_ref, fgate_ref, mq_ref, mgate_ref, k_ref, v_ref, dn_ref, dt_ref,
                        mk_ref, mv_ref, o_ref):
    tb = fq_ref.shape[0]
    qi = pl.program_id(1)
    row = lax.broadcasted_iota(jnp.int32, (tb, tb), 0)
    col = lax.broadcasted_iota(jnp.int32, (tb, tb), 1)
    causal = col <= row
    dn = dn_ref[...]
    for p in range(FX_HEADS // 2):
        ls = slice(p * LANES, (p + 1) * LANES)
        q_pair = fq_ref[:, ls] * (FX_DIM ** -0.5)
        heads = []
        for hh in range(2):
            h = 2 * p + hh
            qm = jnp.where(_head_mask(hh), q_pair, 0.0).astype(BF16)
            dq = dn[:, h:h + 1]

            def scores(j, qm=qm, dq=dq, h=h, ls=ls):
                r0 = pl.multiple_of(j * tb, tb)
                kb = k_ref[pl.ds(r0, tb), ls].astype(BF16)
                vb = v_ref[pl.ds(r0, tb), ls].astype(BF16)
                dk = dt_ref[j, h:h + 1, :]
                return _nt(qm, kb) + dq - dk, vb

            def body(j, carry, scores=scores):
                s, vb = scores(j)
                return _softmax_update(s, vb, *carry)

            init = (jnp.full((tb, 1), -jnp.inf, F32), jnp.zeros((tb, 1), F32),
                    jnp.zeros((tb, LANES), F32))
            carry = lax.fori_loop(0, qi, body, init)
            s, vb = scores(qi)
            m, l, acc = _softmax_update(jnp.where(causal, s, -jnp.inf), vb, *carry)
            heads.append(acc / l)
        o_fx = jnp.where(_head_mask(0), heads[0], heads[1]) * fgate_ref[:, ls]
        o_ref[:, ls] = o_fx.astype(o_ref.dtype)

        mq_pair = mq_ref[:, ls]
        om = [_cross(mq_pair, mk_ref[:, ls], mv_ref[:, ls], hh) for hh in range(2)]
        o_mem = jnp.where(_head_mask(0), om[0], om[1]) * mgate_ref[:, ls]
        lm = slice(FX_WIDTH + p * LANES, FX_WIDTH + (p + 1) * LANES)
        o_ref[:, lm] = o_mem.astype(o_ref.dtype)


def _attn_prompt(act, k, v, d_nat, d_t, mk, mv, *, batch, seq_len, tb):
    m = act.shape[0]
    nq = seq_len // tb
    n_mem = mk.shape[0] // batch
    ablk = lambda c0: pl.BlockSpec((tb, FX_WIDTH), lambda b, q, c0=c0: (b * nq + q, c0 // FX_WIDTH))
    return pl.pallas_call(
        _attn_prompt_kernel,
        grid=(batch, nq),
        in_specs=[
            ablk(C_FQ), ablk(C_FGATE), ablk(C_MQ), ablk(C_MGATE),
            pl.BlockSpec((seq_len, FX_WIDTH), lambda b, q: (b, 0)),
            pl.BlockSpec((seq_len, FX_WIDTH), lambda b, q: (b, 0)),
            pl.BlockSpec((tb, LANES), lambda b, q: (b * nq + q, 0)),
            pl.BlockSpec((nq, SUBLANES, tb), lambda b, q: (b, 0, 0)),
            pl.BlockSpec((n_mem, MEM_WIDTH), lambda b, q: (b, 0)),
            pl.BlockSpec((n_mem, MEM_WIDTH), lambda b, q: (b, 0)),
        ],
        out_specs=pl.BlockSpec((tb, FX_WIDTH + MEM_WIDTH), lambda b, q: (b * nq + q, 0)),
        out_shape=jax.ShapeDtypeStruct((m, FX_WIDTH + MEM_WIDTH), BF16),
        compiler_params=pltpu.CompilerParams(
            dimension_semantics=("arbitrary", "arbitrary"), vmem_limit_bytes=VMEM_LIMIT),
        name="attn_prompt",
    )(act, act, act, act, k, v, d_nat, d_t, mk, mv)


def _attn_sample_kernel(fq_ref, fgate_ref, mq_ref, mgate_ref, kn_ref, vn_ref, dn_ref, dt_ref,
                        pk_ref, pv_ref, plf_ref, mk_ref, mv_ref, o_ref, *, chunk):
    ts = fq_ref.shape[0]
    n_new = kn_ref.shape[0]
    past = pk_ref.shape[1]
    nchunks = past // chunk
    b = pl.program_id(0)

    r = lax.broadcasted_iota(jnp.int32, (chunk, chunk), 0)
    c = lax.broadcasted_iota(jnp.int32, (chunk, chunk), 1)
    upper = jnp.where(r <= c, 1.0, 0.0).astype(F32)
    total = jnp.zeros((SUBLANES, 1), F32)
    d_past = []
    for ci in range(nchunks):
        d = _mm(plf_ref[0, :, ci * chunk:(ci + 1) * chunk], upper,
                precision=lax.Precision.HIGHEST) + total
        d_past.append(d)
        total = d[:, chunk - 1:chunk]

    row = lax.broadcasted_iota(jnp.int32, (ts, n_new), 0)
    col = lax.broadcasted_iota(jnp.int32, (ts, n_new), 1)
    visible = (col // ts == b) & (col % ts <= row)
    dn = dn_ref[...]
    dt_new = dt_ref[0]
    for p in range(FX_HEADS // 2):
        ls = slice(p * LANES, (p + 1) * LANES)
        q_pair = fq_ref[:, ls] * (FX_DIM ** -0.5)
        heads = []
        for hh in range(2):
            h = 2 * p + hh
            qm = jnp.where(_head_mask(hh), q_pair, 0.0).astype(BF16)
            tot_h = total[h:h + 1, :]
            dq = tot_h + dn[:, h:h + 1]
            carry = (jnp.full((ts, 1), -jnp.inf, F32), jnp.zeros((ts, 1), F32),
                     jnp.zeros((ts, LANES), F32))
            for ci in range(nchunks):
                cs = slice(ci * chunk, (ci + 1) * chunk)
                s = _nt(qm, pk_ref[0, cs, ls].astype(BF16)) + dq - d_past[ci][h:h + 1, :]
                carry = _softmax_update(s, pv_ref[0, cs, ls].astype(BF16), *carry)
            s = _nt(qm, kn_ref[:, ls].astype(BF16)) + dq - (tot_h + dt_new[h:h + 1, :])
            m, l, acc = _softmax_update(jnp.where(visible, s, -jnp.inf),
                                        vn_ref[:, ls].astype(BF16), *carry)
            heads.append(acc / l)
        o_fx = jnp.where(_head_mask(0), heads[0], heads[1]) * fgate_ref[:, ls]
        o_ref[:, ls] = o_fx.astype(o_ref.dtype)

        mq_pair = mq_ref[:, ls]
        om = [_cross(mq_pair, mk_ref[0, :, ls], mv_ref[0, :, ls], hh) for hh in range(2)]
        o_mem = jnp.where(_head_mask(0), om[0], om[1]) * mgate_ref[:, ls]
        lm = slice(FX_WIDTH + p * LANES, FX_WIDTH + (p + 1) * LANES)
        o_ref[:, lm] = o_mem.astype(o_ref.dtype)


def _attn_sample(act, k_new, v_new, d_nat, d_t, past_k, past_v, past_lf_t, mk, mv, *, batch, ts):
    m = act.shape[0]
    past = past_k.shape[1]
    n_mem = mk.shape[1]
    chunk = min(past, 256)
    ablk = lambda c0: pl.BlockSpec((ts, FX_WIDTH), lambda b, c0=c0: (b, c0 // FX_WIDTH))
    return pl.pallas_call(
        functools.partial(_attn_sample_kernel, chunk=chunk),
        grid=(batch,),
        in_specs=[
            ablk(C_FQ), ablk(C_FGATE), ablk(C_MQ), ablk(C_MGATE),
            pl.BlockSpec((m, FX_WIDTH), lambda b: (0, 0)),
            pl.BlockSpec((m, FX_WIDTH), lambda b: (0, 0)),
            pl.BlockSpec((ts, LANES), lambda b: (b, 0)),
            pl.BlockSpec((1, SUBLANES, m), lambda b: (0, 0, 0)),
            pl.BlockSpec((1, past, FX_WIDTH), lambda b: (b, 0, 0)),
            pl.BlockSpec((1, past, FX_WIDTH), lambda b: (b, 0, 0)),
            pl.BlockSpec((1, SUBLANES, past), lambda b: (b, 0, 0)),
            pl.BlockSpec((1, n_mem, MEM_WIDTH), lambda b: (b, 0, 0)),
            pl.BlockSpec((1, n_mem, MEM_WIDTH), lambda b: (b, 0, 0)),
        ],
        out_specs=pl.BlockSpec((ts, FX_WIDTH + MEM_WIDTH), lambda b: (b, 0)),
        out_shape=jax.ShapeDtypeStruct((m, FX_WIDTH + MEM_WIDTH), BF16),
        compiler_params=pltpu.CompilerParams(
            dimension_semantics=("arbitrary",), vmem_limit_bytes=VMEM_LIMIT),
        name="attn_sample",
    )(act, act, act, act, k_new, v_new, d_nat, d_t, past_k, past_v, past_lf_t, mk, mv)


def _memkv_kernel(x_ref, w_ref, k_ref, v_ref):
    kv = _mm(x_ref[...].astype(BF16), w_ref[...])
    k_ref[...] = kv[:, :MEM_WIDTH]
    v_ref[...] = kv[:, MEM_WIDTH:]


def _memkv(mem, w, *, tm):
    m, d = mem.shape
    return pl.pallas_call(
        _memkv_kernel,
        grid=(m // tm,),
        in_specs=[pl.BlockSpec((tm, d), lambda i: (i, 0)),
                  pl.BlockSpec(w.shape, lambda i: (0, 0))],
        out_specs=[pl.BlockSpec((tm, MEM_WIDTH), lambda i: (i, 0)),
                   pl.BlockSpec((tm, MEM_WIDTH), lambda i: (i, 0))],
        out_shape=[jax.ShapeDtypeStruct((m, MEM_WIDTH), F32),
                   jax.ShapeDtypeStruct((m, MEM_WIDTH), F32)],
        compiler_params=pltpu.CompilerParams(
            dimension_semantics=("arbitrary",), vmem_limit_bytes=VMEM_LIMIT),
        name="memkv",
    )(mem, w)


def _out_kernel(x_ref, hg_ref, at_ref, w_ref, g_ref, b_ref, y_ref, *, alpha):
    acc = _mm(hg_ref[...], w_ref[0:HG_WIDTH, :]) + _mm(at_ref[...], w_ref[HG_WIDTH:, :])
    z = alpha * x_ref[...] + acc
    mu = jnp.mean(z, axis=-1, keepdims=True)
    zc = z - mu
    var = jnp.mean(zc * zc, axis=-1, keepdims=True)
    y_ref[...] = zc * lax.rsqrt(var + LN_EPS) * g_ref[...] + b_ref[...]


def _out(x, mix_hg, mix_at, w, g, b, *, alpha, tm):
    m, d = x.shape
    return pl.pallas_call(
        functools.partial(_out_kernel, alpha=alpha),
        grid=(m // tm,),
        in_specs=[
            pl.BlockSpec((tm, d), lambda i: (i, 0)),
            pl.BlockSpec((tm, HG_WIDTH), lambda i: (i, 0)),
            pl.BlockSpec((tm, FX_WIDTH + MEM_WIDTH), lambda i: (i, 0)),
            pl.BlockSpec(w.shape, lambda i: (0, 0)),
            pl.BlockSpec((1, d), lambda i: (0, 0)),
            pl.BlockSpec((1, d), lambda i: (0, 0)),
        ],
        out_specs=pl.BlockSpec((tm, d), lambda i: (i, 0)),
        out_shape=jax.ShapeDtypeStruct((m, d), F32),
        compiler_params=pltpu.CompilerParams(
            dimension_semantics=("arbitrary",), vmem_limit_bytes=VMEM_LIMIT),
        name="out",
    )(x, mix_hg, mix_at, w, g, b)


def _pick_tile(n, target):
    t = min(n, target)
    while n % t:
        t //= 2
    return t


def kernel(x_prompt, x_sample, mem_prompt, state_hgrn, cache_fox_k, cache_fox_v, cache_fox_logf,
           cache_mem_k, cache_mem_v, w_in, b_fox_forget, hgrn_lower_bounds, hgrn_norm_g,
           w_mem_kv, w_out, ln_g, ln_b):
    depth, d_model, _ = w_in.shape
    bp, seq, _ = x_prompt.shape
    bs, dseq, _ = x_sample.shape
    past = cache_fox_k.shape[2]
    n_mem = mem_prompt.shape[1]
    alpha = (2 * depth) ** 0.25

    fl0 = 4 * HG_WIDTH + 4 * FX_WIDTH
    w_main = jnp.concatenate([w_in[:, :, :fl0], w_in[:, :, fl0 + FX_HEADS:]], axis=-1).astype(BF16)
    w_flc = w_in[:, :, fl0:fl0 + FX_HEADS]
    w_fl = jnp.pad(w_flc, ((0, 0), (0, 0), (0, LANES - FX_HEADS))).astype(BF16)
    w_flt = jnp.pad(jnp.swapaxes(w_flc, 1, 2), ((0, 0), (0, SUBLANES - FX_HEADS), (0, 0))).astype(BF16)
    b_fl = jnp.pad(b_fox_forget, ((0, 0), (0, LANES - FX_HEADS)))[:, None, :]
    b_flt = jnp.pad(b_fox_forget, ((0, 0), (0, SUBLANES - FX_HEADS)))[:, :, None]
    lbs = hgrn_lower_bounds.astype(F32)
    w_memb = w_mem_kv.astype(BF16)
    w_outb = w_out.astype(BF16)

    past_k = cache_fox_k.reshape(depth, bs, past, FX_WIDTH)
    past_v = cache_fox_v.reshape(depth, bs, past, FX_WIDTH)
    past_lf_t = jnp.pad(jnp.swapaxes(cache_fox_logf, 2, 3),
                        ((0, 0), (0, 0), (0, SUBLANES - FX_HEADS), (0, 0)))
    smem_k = cache_mem_k.reshape(depth, bs, n_mem, MEM_WIDTH)
    smem_v = cache_mem_v.reshape(depth, bs, n_mem, MEM_WIDTH)

    mp, ms = bp * seq, bs * dseq
    tb = _pick_tile(seq, 256)
    tt = _pick_tile(seq, 2048)
    yp = x_prompt.reshape(mp, d_model)
    ys = x_sample.reshape(ms, d_model)
    mem2 = mem_prompt.reshape(bp * n_mem, d_model)

    p_hg, p_k, p_v, p_lf, p_mk, p_mv = [], [], [], [], [], []
    s_hg, s_k, s_v, s_lf = [], [], [], []
    for l in range(depth):
        wl = (w_main[l], w_fl[l], w_flt[l], b_fl[l], b_flt[l], lbs)
        ng = hgrn_norm_g[l][None, :]
        g_ln, b_ln = ln_g[l][None, :], ln_b[l][None, :]

        mk, mv = _memkv(mem2, w_memb[l], tm=_pick_tile(bp * n_mem, 512))
        act, k, v, lf, d_nat, d_t = _proj(yp, *wl, layer=l, seq_len=seq, tm=tb)
        mix_hg, hg = _hgrn(act, ng, None, batch=bp, seq_len=seq, tt=tt)
        mix_at = _attn_prompt(act, k, v, d_nat, d_t, mk, mv, batch=bp, seq_len=seq, tb=tb)
        yp = _out(yp, mix_hg, mix_at, w_outb[l], g_ln, b_ln, alpha=alpha, tm=tb)
        p_hg.append(hg); p_k.append(k); p_v.append(v); p_lf.append(lf); p_mk.append(mk); p_mv.append(mv)

        act, k, v, lf, d_nat, d_t = _proj(ys, *wl, layer=l, seq_len=dseq, tm=ms)
        mix_hg, hg = _hgrn(act, ng, state_hgrn[l], batch=bs, seq_len=dseq, tt=dseq)
        mix_at = _attn_sample(act, k, v, d_nat, d_t, past_k[l], past_v[l], past_lf_t[l],
                              smem_k[l], smem_v[l], batch=bs, ts=dseq)
        ys = _out(ys, mix_hg, mix_at, w_outb[l], g_ln, b_ln, alpha=alpha, tm=ms)
        s_hg.append(hg); s_k.append(k); s_v.append(v); s_lf.append(lf)

    st = jnp.stack
    return (yp.reshape(bp, seq, d_model), ys.reshape(bs, dseq, d_model),
            st(p_hg),
            st(p_k).reshape(depth, bp, seq, FX_HEADS, FX_DIM),
            st(p_v).reshape(depth, bp, seq, FX_HEADS, FX_DIM),
            st(p_lf).reshape(depth, bp, seq, FX_HEADS),
            st(p_mk).reshape(depth, bp, n_mem, MEM_HEADS, MEM_DIM),
            st(p_mv).reshape(depth, bp, n_mem, MEM_HEADS, MEM_DIM),
            st(s_hg),
            st(s_k).reshape(depth, bs, dseq, FX_HEADS, FX_DIM),
            st(s_v).reshape(depth, bs, dseq, FX_HEADS, FX_DIM),
            st(s_lf).reshape(depth, bs, dseq, FX_HEADS))
```

```python
import functools

import jax
import jax.numpy as jnp
from jax import lax
from jax.experimental import pallas as pl
from jax.experimental.pallas import tpu as pltpu

F32 = jnp.float32
BF16 = jnp.bfloat16

HG_HEADS = 4
HG_DIM = 128
HG_WIDTH = HG_HEADS * HG_DIM
FX_HEADS = 4
FX_DIM = 64
FX_WIDTH = FX_HEADS * FX_DIM
MEM_HEADS = 4
MEM_DIM = 64
MEM_WIDTH = MEM_HEADS * MEM_DIM
HG_GROUP = 128
LN_EPS = 1e-5
RMS_EPS = 1e-6
LANES = 128
SUBLANES = 8
VMEM_LIMIT = 56 * 1024 * 1024

C_HQ, C_HK, C_HV, C_HG, C_HGATE = 0, 512, 1024, 1536, 2048
C_FQ, C_FGATE, C_MQ, C_MGATE = 2560, 2816, 3072, 3328
ACT_WIDTH = 3584
W_HQ, W_HF, W_HI, W_HGATE, W_FQ, W_FK, W_FV, W_FGATE, W_MQ, W_MGATE = (
    0, 512, 1024, 1536, 2048, 2304, 2560, 2816, 3072, 3328)
W_MAIN = 3584


def _nt(a, b, precision=None):
    return lax.dot_general(a, b, (((1,), (1,)), ((), ())), precision=precision,
                           preferred_element_type=F32)


def _mm(a, b, precision=None):
    return jnp.dot(a, b, precision=precision, preferred_element_type=F32)


def _silu(x):
    return x * jax.nn.sigmoid(x)


def _log_sigmoid(x):
    return jnp.minimum(x, 0.0) - jnp.log(1.0 + jnp.exp(-jnp.abs(x)))


def _proj_kernel(x_ref, w_ref, wfl_ref, wflt_ref, bfl_ref, bflt_ref, lbs_ref,
                 act_ref, k_ref, v_ref, lf_ref, dn_ref, dt_ref, cn_ref, ct_ref,
                 *, layer, seg, tiles_per_seq):
    tm = x_ref.shape[0]
    xb = x_ref[...].astype(BF16)

    def proj(c0, width):
        return _mm(xb, w_ref[:, c0:c0 + width])

    lbs = lbs_ref[...]
    e = jnp.exp(lbs - jnp.max(lbs, axis=0, keepdims=True))
    sm = e / jnp.sum(e, axis=0, keepdims=True)
    lb = jnp.sum(sm[:layer + 1], axis=0, keepdims=True) - sm[0:1]

    act_ref[:, C_HQ:C_HQ + HG_WIDTH] = _silu(proj(W_HQ, HG_WIDTH))
    forget = lb + (1.0 - lb) * jax.nn.sigmoid(proj(W_HF, HG_WIDTH))
    act_ref[:, C_HK:C_HK + HG_WIDTH] = 1.0 - forget
    act_ref[:, C_HG:C_HG + HG_WIDTH] = jnp.log(forget)
    act_ref[:, C_HV:C_HV + HG_WIDTH] = proj(W_HI, HG_WIDTH)
    act_ref[:, C_HGATE:C_HGATE + HG_WIDTH] = _silu(proj(W_HGATE, HG_WIDTH))
    act_ref[:, C_FQ:C_FQ + FX_WIDTH] = proj(W_FQ, FX_WIDTH)
    k_ref[...] = proj(W_FK, FX_WIDTH)
    v_ref[...] = proj(W_FV, FX_WIDTH)
    act_ref[:, C_FGATE:C_FGATE + FX_WIDTH] = _silu(proj(W_FGATE, FX_WIDTH))
    act_ref[:, C_MQ:C_MQ + MEM_WIDTH] = proj(W_MQ, MEM_WIDTH)
    act_ref[:, C_MGATE:C_MGATE + MEM_WIDTH] = _silu(proj(W_MGATE, MEM_WIDTH))

    lf_nat = _log_sigmoid(_mm(xb, wfl_ref[...]) + bfl_ref[...])
    lf_t = _log_sigmoid(_nt(wflt_ref[...], xb) + bflt_ref[...])
    lf_ref[...] = lf_nat[:, :FX_HEADS]

    r = lax.broadcasted_iota(jnp.int32, (tm, tm), 0)
    c = lax.broadcasted_iota(jnp.int32, (tm, tm), 1)
    same = (r // seg) == (c // seg)
    lower = jnp.where((c <= r) & same, 1.0, 0.0).astype(F32)
    upper = jnp.where((r <= c) & same, 1.0, 0.0).astype(F32)
    d_nat = _mm(lower, lf_nat, precision=lax.Precision.HIGHEST)
    d_t = _mm(lf_t, upper, precision=lax.Precision.HIGHEST)
    if tiles_per_seq > 1:
        @pl.when(pl.program_id(0) % tiles_per_seq == 0)
        def _():
            cn_ref[...] = jnp.zeros_like(cn_ref)
            ct_ref[...] = jnp.zeros_like(ct_ref)
        d_nat = d_nat + cn_ref[0:1, :]
        d_t = d_t + ct_ref[:, 0:1]
        cn_ref[...] = jnp.broadcast_to(d_nat[tm - 1:tm, :], cn_ref.shape)
        ct_ref[...] = jnp.broadcast_to(d_t[:, tm - 1:tm], ct_ref.shape)
    dn_ref[...] = d_nat
    dt_ref[0] = d_t


def _proj(x, w_main, w_fl, w_flt, b_fl, b_flt, lbs, *, layer, seq_len, tm):
    m, d = x.shape
    ntiles = m // tm
    seg = min(seq_len, tm)
    tiles_per_seq = max(seq_len // tm, 1)
    kern = functools.partial(_proj_kernel, layer=layer, seg=seg, tiles_per_seq=tiles_per_seq)
    full = lambda shape: pl.BlockSpec(shape, lambda i: (0,) * len(shape))
    return pl.pallas_call(
        kern,
        grid=(ntiles,),
        in_specs=[
            pl.BlockSpec((tm, d), lambda i: (i, 0)),
            full(w_main.shape), full(w_fl.shape), full(w_flt.shape),
            full(b_fl.shape), full(b_flt.shape), full(lbs.shape),
        ],
        out_specs=[
            pl.BlockSpec((tm, ACT_WIDTH), lambda i: (i, 0)),
            pl.BlockSpec((tm, FX_WIDTH), lambda i: (i, 0)),
            pl.BlockSpec((tm, FX_WIDTH), lambda i: (i, 0)),
            pl.BlockSpec((tm, FX_HEADS), lambda i: (i, 0)),
            pl.BlockSpec((tm, LANES), lambda i: (i, 0)),
            pl.BlockSpec((1, SUBLANES, tm), lambda i: (i, 0, 0)),
        ],
        out_shape=[
            jax.ShapeDtypeStruct((m, ACT_WIDTH), F32),
            jax.ShapeDtypeStruct((m, FX_WIDTH), F32),
            jax.ShapeDtypeStruct((m, FX_WIDTH), F32),
            jax.ShapeDtypeStruct((m, FX_HEADS), F32),
            jax.ShapeDtypeStruct((m, LANES), F32),
            jax.ShapeDtypeStruct((ntiles, SUBLANES, tm), F32),
        ],
        scratch_shapes=[pltpu.VMEM((SUBLANES, LANES), F32), pltpu.VMEM((SUBLANES, LANES), F32)],
        compiler_params=pltpu.CompilerParams(
            dimension_semantics=("arbitrary",), vmem_limit_bytes=VMEM_LIMIT),
        name="proj",
    )(x, w_main, w_fl, w_flt, b_fl, b_flt, lbs)


def _cumsum_rows(x, rows):
    n = x.shape[0]
    d = 1
    while d < n:
        x = x + jnp.where(rows >= d, pltpu.roll(x, d, axis=0), 0.0)
        d *= 2
    return x


def _half_ref(gc, m, rows):
    n, w = gc.shape
    if 2 * m >= SUBLANES:
        g3 = gc.reshape(n // (2 * m), 2 * m, w)
        return jnp.broadcast_to(g3[:, m - 1:m, :], g3.shape).reshape(n, w)
    pos = rows % (2 * m)
    if m == 2:
        return jnp.where(pos == 0, pltpu.roll(gc, n - 1, axis=0),
                         jnp.where(pos == 1, gc,
                                   jnp.where(pos == 2, pltpu.roll(gc, 1, axis=0),
                                             pltpu.roll(gc, 2, axis=0))))
    return jnp.where(pos == 0, gc, pltpu.roll(gc, 1, axis=0))


def _hgrn_group(q, k, v, g, st):
    n = q.shape[0]
    rows = lax.broadcasted_iota(jnp.int32, (n, HG_DIM), 0)
    rr = lax.broadcasted_iota(jnp.int32, (n, n), 0)
    cc = lax.broadcasted_iota(jnp.int32, (n, n), 1)
    diff_bits = rr ^ cc
    gc = _cumsum_rows(g, rows)
    o = _nt((q * jnp.exp(gc)).astype(BF16), st.astype(BF16))
    a = jnp.zeros((n, n), F32)
    m = n // 2
    while m >= 1:
        ref = _half_ref(gc, m, rows)
        first = (rows % (2 * m)) < m
        f = jnp.exp(jnp.where(first, ref - gc, gc - ref))
        a_m = _nt((q * f).astype(BF16), (k * f).astype(BF16))
        a = jnp.where((diff_bits >= m) & (diff_bits < 2 * m), a_m, a)
        m //= 2
    a = jnp.where(rr > cc, a, 0.0)
    o = o + _mm(a.astype(BF16), v.astype(BF16)) + jnp.sum(q * k, axis=-1, keepdims=True) * v
    g_last = gc[n - 1:n, :]
    k_end = k * jnp.exp(g_last - gc)
    upd = lax.dot_general(v.astype(BF16), k_end.astype(BF16), (((0,), (0,)), ((), ())),
                          preferred_element_type=F32)
    return o, st * jnp.exp(g_last) + upd


def _hgrn_kernel(*refs, group, has_init):
    if has_init:
        q_ref, k_ref, v_ref, g_ref, gate_ref, ng_ref, s0_ref, o_ref, so_ref, st_ref = refs
    else:
        q_ref, k_ref, v_ref, g_ref, gate_ref, ng_ref, o_ref, so_ref, st_ref = refs
    t = pl.program_id(2)

    @pl.when(t == 0)
    def _():
        if has_init:
            st_ref[...] = s0_ref[0, 0].T
        else:
            st_ref[...] = jnp.zeros_like(st_ref)

    ng = ng_ref[...]

    def step(i, st):
        sl = pl.ds(pl.multiple_of(i * group, group), group)
        o, st = _hgrn_group(q_ref[sl, :], k_ref[sl, :], v_ref[sl, :], g_ref[sl, :], st)
        o = o * lax.rsqrt(jnp.mean(o * o, axis=-1, keepdims=True) + RMS_EPS) * ng
        o_ref[sl, :] = (o * gate_ref[sl, :]).astype(o_ref.dtype)
        return st

    st_ref[...] = lax.fori_loop(0, q_ref.shape[0] // group, step, st_ref[...])

    @pl.when(t == pl.num_programs(2) - 1)
    def _():
        so_ref[0, 0] = st_ref[...].T


def _hgrn(act, norm_g, s0, *, batch, seq_len, tt):
    m = act.shape[0]
    nt = seq_len // tt
    has_init = s0 is not None
    group = min(tt, HG_GROUP)
    blk = lambda cb: pl.BlockSpec((tt, HG_DIM), lambda b, h, t, cb=cb: (b * nt + t, cb // HG_DIM + h))
    in_specs = [blk(C_HQ), blk(C_HK), blk(C_HV), blk(C_HG), blk(C_HGATE),
                pl.BlockSpec((1, HG_DIM), lambda b, h, t: (0, 0))]
    args = [act, act, act, act, act, norm_g]
    if has_init:
        in_specs.append(pl.BlockSpec((1, 1, HG_DIM, HG_DIM), lambda b, h, t: (b, h, 0, 0)))
        args.append(s0)
    return pl.pallas_call(
        functools.partial(_hgrn_kernel, group=group, has_init=has_init),
        grid=(batch, HG_HEADS, nt),
        in_specs=in_specs,
        out_specs=[
            pl.BlockSpec((tt, HG_DIM), lambda b, h, t: (b * nt + t, h)),
            pl.BlockSpec((1, 1, HG_DIM, HG_DIM), lambda b, h, t: (b, h, 0, 0)),
        ],
        out_shape=[
            jax.ShapeDtypeStruct((m, HG_WIDTH), BF16),
            jax.ShapeDtypeStruct((batch, HG_HEADS, HG_DIM, HG_DIM), F32),
        ],
        scratch_shapes=[pltpu.VMEM((HG_DIM, HG_DIM), F32)],
        compiler_params=pltpu.CompilerParams(
            dimension_semantics=("arbitrary", "arbitrary", "arbitrary"), vmem_limit_bytes=VMEM_LIMIT),
        name="hgrn",
    )(*args)


def _head_mask(hh):
    lane = lax.broadcasted_iota(jnp.int32, (1, LANES), 1)
    return (lane // FX_DIM) == hh


def _cross(mq_pair, mk_pair, mv_pair, hh):
    qm = jnp.where(_head_mask(hh), mq_pair * (MEM_DIM ** -0.5), 0.0).astype(BF16)
    s = _nt(qm, mk_pair.astype(BF16))
    p = jnp.exp(s - jnp.max(s, axis=-1, keepdims=True))
    o = _mm(p.astype(BF16), mv_pair.astype(BF16))
    return o / jnp.sum(p, axis=-1, keepdims=True)


def _softmax_update(s, vb, m, l, acc):
    m_new = jnp.maximum(m, jnp.max(s, axis=-1, keepdims=True))
    alpha = jnp.exp(m - m_new)
    p = jnp.exp(s - m_new)
    l = alpha * l + jnp.sum(p, axis=-1, keepdims=True)
    acc = alpha * acc + _mm(p.astype(BF16), vb)
    return m_new, l, acc


AUG = 3


def _split3(x):
    hi = x.astype(BF16).astype(F32)
    r = x - hi
    mid = r.astype(BF16).astype(F32)
    return hi, mid, r - mid


def _aug_operand(x_pair, d, hh, is_query):
    lane = lax.broadcasted_iota(jnp.int32, (1, LANES), 1)
    a = lane - (1 - hh) * FX_DIM
    hi, mid, lo = _split3(d if is_query else -d)
    p0, o0 = (0, AUG) if is_query else (AUG, 0)
    extra = jnp.where(a == p0, hi, jnp.where(a == p0 + 1, mid, jnp.where(a == p0 + 2, lo,
                      jnp.where((a >= o0) & (a < o0 + AUG), 1.0, 0.0))))
    return jnp.where(_head_mask(hh), x_pair, extra).astype(BF16)


def _attn_prompt_kernel(fq_ref, fgate_ref, mq_ref, mgate_ref, k_ref, v_ref, dn_ref, mk_ref, mv_ref,
                        o_ref, ka_ref, vm_ref, mkb_ref, mvm_ref):
    tb = fq_ref.shape[0]
    nq = ka_ref.shape[0]
    qi = pl.program_id(1)
    pair_lanes = lambda h: slice((h // 2) * LANES, (h // 2 + 1) * LANES)

    @pl.when(qi == 0)
    def _():
        def build(j, c):
            rows = pl.ds(pl.multiple_of(j * tb, tb), tb)
            dn = dn_ref[rows, :]
            for h in range(FX_HEADS):
                ka_ref[j, h] = _aug_operand(k_ref[rows, pair_lanes(h)], dn[:, h:h + 1], h % 2, False)
                vm_ref[j, h] = jnp.where(_head_mask(h % 2), v_ref[rows, pair_lanes(h)], 0.0).astype(BF16)
            return c
        lax.fori_loop(0, nq, build, 0)
        mkb_ref[...] = mk_ref[...].astype(BF16)
        for h in range(MEM_HEADS):
            mvm_ref[h] = jnp.where(_head_mask(h % 2), mv_ref[:, pair_lanes(h)], 0.0).astype(BF16)

    dnq = dn_ref[pl.ds(pl.multiple_of(qi * tb, tb), tb), :]
    qa = [_aug_operand(fq_ref[:, pair_lanes(h)] * (FX_DIM ** -0.5), dnq[:, h:h + 1], h % 2, True)
          for h in range(FX_HEADS)]
    row = lax.broadcasted_iota(jnp.int32, (tb, tb), 0)
    col = lax.broadcasted_iota(jnp.int32, (tb, tb), 1)

    def update(j, carry, causal):
        out = []
        for h in range(FX_HEADS):
            s = _nt(qa[h], ka_ref[j, h])
            if causal:
                s = jnp.where(col <= row, s, -jnp.inf)
            out.append(_softmax_update(s, vm_ref[j, h], *carry[h]))
        return tuple(out)

    init = tuple((jnp.full((tb, 1), -jnp.inf, F32), jnp.zeros((tb, 1), F32),
                  jnp.zeros((tb, LANES), F32)) for _ in range(FX_HEADS))
    carry = lax.fori_loop(0, qi, lambda j, c: update(j, c, False), init)
    carry = update(qi, carry, True)
    for p in range(FX_HEADS // 2):
        (_, l0, a0), (_, l1, a1) = carry[2 * p], carry[2 * p + 1]
        ls = pair_lanes(2 * p)
        o_ref[:, ls] = ((a0 / l0 + a1 / l1) * fgate_ref[:, ls]).astype(o_ref.dtype)

    for p in range(MEM_HEADS // 2):
        ls = pair_lanes(2 * p)
        mq_pair = mq_ref[:, ls] * (MEM_DIM ** -0.5)
        o_mem = 0.0
        for hh in range(2):
            qm = jnp.where(_head_mask(hh), mq_pair, 0.0).astype(BF16)
            s = _nt(qm, mkb_ref[:, ls])
            pr = jnp.exp(s - jnp.max(s, axis=-1, keepdims=True))
            o_mem = o_mem + _mm(pr.astype(BF16), mvm_ref[2 * p + hh]) / jnp.sum(pr, axis=-1, keepdims=True)
        lm = slice(FX_WIDTH + p * LANES, FX_WIDTH + (p + 1) * LANES)
        o_ref[:, lm] = (o_mem * mgate_ref[:, ls]).astype(o_ref.dtype)


def _attn_prompt(act, k, v, d_nat, mk, mv, *, batch, seq_len, tb):
    m = act.shape[0]
    nq = seq_len // tb
    n_mem = mk.shape[0] // batch
    ablk = lambda c0: pl.BlockSpec((tb, FX_WIDTH), lambda b, q, c0=c0: (b * nq + q, c0 // FX_WIDTH))
    return pl.pallas_call(
        _attn_prompt_kernel,
        grid=(batch, nq),
        in_specs=[
            ablk(C_FQ), ablk(C_FGATE), ablk(C_MQ), ablk(C_MGATE),
            pl.BlockSpec((seq_len, FX_WIDTH), lambda b, q: (b, 0)),
            pl.BlockSpec((seq_len, FX_WIDTH), lambda b, q: (b, 0)),
            pl.BlockSpec((seq_len, LANES), lambda b, q: (b, 0)),
            pl.BlockSpec((n_mem, MEM_WIDTH), lambda b, q: (b, 0)),
            pl.BlockSpec((n_mem, MEM_WIDTH), lambda b, q: (b, 0)),
        ],
        out_specs=pl.BlockSpec((tb, FX_WIDTH + MEM_WIDTH), lambda b, q: (b * nq + q, 0)),
        out_shape=jax.ShapeDtypeStruct((m, FX_WIDTH + MEM_WIDTH), BF16),
        scratch_shapes=[
            pltpu.VMEM((nq, FX_HEADS, tb, LANES), BF16),
            pltpu.VMEM((nq, FX_HEADS, tb, LANES), BF16),
            pltpu.VMEM((n_mem, MEM_WIDTH), BF16),
            pltpu.VMEM((MEM_HEADS, n_mem, LANES), BF16),
        ],
        compiler_params=pltpu.CompilerParams(
            dimension_semantics=("arbitrary", "arbitrary"), vmem_limit_bytes=VMEM_LIMIT),
        name="attn_prompt",
    )(act, act, act, act, k, v, d_nat, mk, mv)


def _attn_sample_kernel(fq_ref, fgate_ref, mq_ref, mgate_ref, kn_ref, vn_ref, dn_ref, dt_ref,
                        pk_ref, pv_ref, plf_ref, mk_ref, mv_ref, o_ref, *, chunk):
    ts = fq_ref.shape[0]
    n_new = kn_ref.shape[0]
    past = pk_ref.shape[1]
    nchunks = past // chunk
    b = pl.program_id(0)

    r = lax.broadcasted_iota(jnp.int32, (chunk, chunk), 0)
    c = lax.broadcasted_iota(jnp.int32, (chunk, chunk), 1)
    upper = jnp.where(r <= c, 1.0, 0.0).astype(F32)
    total = jnp.zeros((SUBLANES, 1), F32)
    d_past = []
    for ci in range(nchunks):
        d = _mm(plf_ref[0, :, ci * chunk:(ci + 1) * chunk], upper,
                precision=lax.Precision.HIGHEST) + total
        d_past.append(d)
        total = d[:, chunk - 1:chunk]

    row = lax.broadcasted_iota(jnp.int32, (ts, n_new), 0)
    col = lax.broadcasted_iota(jnp.int32, (ts, n_new), 1)
    visible = (col // ts == b) & (col % ts <= row)
    dn = dn_ref[...]
    dt_new = dt_ref[0]
    for p in range(FX_HEADS // 2):
        ls = slice(p * LANES, (p + 1) * LANES)
        q_pair = fq_ref[:, ls] * (FX_DIM ** -0.5)
        heads = []
        for hh in range(2):
            h = 2 * p + hh
            qm = jnp.where(_head_mask(hh), q_pair, 0.0).astype(BF16)
            tot_h = total[h:h + 1, :]
            dq = tot_h + dn[:, h:h + 1]
            carry = (jnp.full((ts, 1), -jnp.inf, F32), jnp.zeros((ts, 1), F32),
                     jnp.zeros((ts, LANES), F32))
            for ci in range(nchunks):
                cs = slice(ci * chunk, (ci + 1) * chunk)
                s = _nt(qm, pk_ref[0, cs, ls].astype(BF16)) + dq - d_past[ci][h:h + 1, :]
                carry = _softmax_update(s, pv_ref[0, cs, ls].astype(BF16), *carry)
            s = _nt(qm, kn_ref[:, ls].astype(BF16)) + dq - (tot_h + dt_new[h:h + 1, :])
            m, l, acc = _softmax_update(jnp.where(visible, s, -jnp.inf),
                                        vn_ref[:, ls].astype(BF16), *carry)
            heads.append(acc / l)
        o_fx = jnp.where(_head_mask(0), heads[0], heads[1]) * fgate_ref[:, ls]
        o_ref[:, ls] = o_fx.astype(o_ref.dtype)

        mq_pair = mq_ref[:, ls]
        om = [_cross(mq_pair, mk_ref[0, :, ls], mv_ref[0, :, ls], hh) for hh in range(2)]
        o_mem = jnp.where(_head_mask(0), om[0], om[1]) * mgate_ref[:, ls]
        lm = slice(FX_WIDTH + p * LANES, FX_WIDTH + (p + 1) * LANES)
        o_ref[:, lm] = o_mem.astype(o_ref.dtype)


def _attn_sample(act, k_new, v_new, d_nat, d_t, past_k, past_v, past_lf_t, mk, mv, *, batch, ts):
    m = act.shape[0]
    past = past_k.shape[1]
    n_mem = mk.shape[1]
    chunk = min(past, 256)
    ablk = lambda c0: pl.BlockSpec((ts, FX_WIDTH), lambda b, c0=c0: (b, c0 // FX_WIDTH))
    return pl.pallas_call(
        functools.partial(_attn_sample_kernel, chunk=chunk),
        grid=(batch,),
        in_specs=[
            ablk(C_FQ), ablk(C_FGATE), ablk(C_MQ), ablk(C_MGATE),
            pl.BlockSpec((m, FX_WIDTH), lambda b: (0, 0)),
            pl.BlockSpec((m, FX_WIDTH), lambda b: (0, 0)),
            pl.BlockSpec((ts, LANES), lambda b: (b, 0)),
            pl.BlockSpec((1, SUBLANES, m), lambda b: (0, 0, 0)),
            pl.BlockSpec((1, past, FX_WIDTH), lambda b: (b, 0, 0)),
            pl.BlockSpec((1, past, FX_WIDTH), lambda b: (b, 0, 0)),
            pl.BlockSpec((1, SUBLANES, past), lambda b: (b, 0, 0)),
            pl.BlockSpec((1, n_mem, MEM_WIDTH), lambda b: (b, 0, 0)),
            pl.BlockSpec((1, n_mem, MEM_WIDTH), lambda b: (b, 0, 0)),
        ],
        out_specs=pl.BlockSpec((ts, FX_WIDTH + MEM_WIDTH), lambda b: (b, 0)),
        out_shape=jax.ShapeDtypeStruct((m, FX_WIDTH + MEM_WIDTH), BF16),
        compiler_params=pltpu.CompilerParams(
            dimension_semantics=("arbitrary",), vmem_limit_bytes=VMEM_LIMIT),
        name="attn_sample",
    )(act, act, act, act, k_new, v_new, d_nat, d_t, past_k, past_v, past_lf_t, mk, mv)


def _memkv_kernel(x_ref, w_ref, k_ref, v_ref):
    kv = _mm(x_ref[...].astype(BF16), w_ref[...])
    k_ref[...] = kv[:, :MEM_WIDTH]
    v_ref[...] = kv[:, MEM_WIDTH:]


def _memkv(mem, w, *, tm):
    m, d = mem.shape
    return pl.pallas_call(
        _memkv_kernel,
        grid=(m // tm,),
        in_specs=[pl.BlockSpec((tm, d), lambda i: (i, 0)),
                  pl.BlockSpec(w.shape, lambda i: (0, 0))],
        out_specs=[pl.BlockSpec((tm, MEM_WIDTH), lambda i: (i, 0)),
                   pl.BlockSpec((tm, MEM_WIDTH), lambda i: (i, 0))],
        out_shape=[jax.ShapeDtypeStruct((m, MEM_WIDTH), F32),
                   jax.ShapeDtypeStruct((m, MEM_WIDTH), F32)],
        compiler_params=pltpu.CompilerParams(
            dimension_semantics=("arbitrary",), vmem_limit_bytes=VMEM_LIMIT),
        name="memkv",
    )(mem, w)


def _out_kernel(x_ref, hg_ref, at_ref, w_ref, g_ref, b_ref, y_ref, *, alpha):
    acc = _mm(hg_ref[...], w_ref[0:HG_WIDTH, :]) + _mm(at_ref[...], w_ref[HG_WIDTH:, :])
    z = alpha * x_ref[...] + acc
    mu = jnp.mean(z, axis=-1, keepdims=True)
    zc = z - mu
    var = jnp.mean(zc * zc, axis=-1, keepdims=True)
    y_ref[...] = zc * lax.rsqrt(var + LN_EPS) * g_ref[...] + b_ref[...]


def _out(x, mix_hg, mix_at, w, g, b, *, alpha, tm):
    m, d = x.shape
    return pl.pallas_call(
        functools.partial(_out_kernel, alpha=alpha),
        grid=(m // tm,),
        in_specs=[
            pl.BlockSpec((tm, d), lambda i: (i, 0)),
            pl.BlockSpec((tm, HG_WIDTH), lambda i: (i, 0)),
            pl.BlockSpec((tm, FX_WIDTH + MEM_WIDTH), lambda i: (i, 0)),
            pl.BlockSpec(w.shape, lambda i: (0, 0)),
            pl.BlockSpec((1, d), lambda i: (0, 0)),
            pl.BlockSpec((1, d), lambda i: (0, 0)),
        ],
        out_specs=pl.BlockSpec((tm, d), lambda i: (i, 0)),
        out_shape=jax.ShapeDtypeStruct((m, d), F32),
        compiler_params=pltpu.CompilerParams(
            dimension_semantics=("arbitrary",), vmem_limit_bytes=VMEM_LIMIT),
        name="out",
    )(x, mix_hg, mix_at, w, g, b)


def _pick_tile(n, target):
    t = min(n, target)
    while n % t:
        t //= 2
    return t


def kernel(x_prompt, x_sample, mem_prompt, state_hgrn, cache_fox_k, cache_fox_v, cache_fox_logf,
           cache_mem_k, cache_mem_v, w_in, b_fox_forget, hgrn_lower_bounds, hgrn_norm_g,
           w_mem_kv, w_out, ln_g, ln_b):
    depth, d_model, _ = w_in.shape
    bp, seq, _ = x_prompt.shape
    bs, dseq, _ = x_sample.shape
    past = cache_fox_k.shape[2]
    n_mem = mem_prompt.shape[1]
    alpha = (2 * depth) ** 0.25

    fl0 = 4 * HG_WIDTH + 4 * FX_WIDTH
    w_main = jnp.concatenate([w_in[:, :, :fl0], w_in[:, :, fl0 + FX_HEADS:]], axis=-1).astype(BF16)
    w_flc = w_in[:, :, fl0:fl0 + FX_HEADS]
    w_fl = jnp.pad(w_flc, ((0, 0), (0, 0), (0, LANES - FX_HEADS))).astype(BF16)
    w_flt = jnp.pad(jnp.swapaxes(w_flc, 1, 2), ((0, 0), (0, SUBLANES - FX_HEADS), (0, 0))).astype(BF16)
    b_fl = jnp.pad(b_fox_forget, ((0, 0), (0, LANES - FX_HEADS)))[:, None, :]
    b_flt = jnp.pad(b_fox_forget, ((0, 0), (0, SUBLANES - FX_HEADS)))[:, :, None]
    lbs = hgrn_lower_bounds.astype(F32)
    w_memb = w_mem_kv.astype(BF16)
    w_outb = w_out.astype(BF16)

    past_k = cache_fox_k.reshape(depth, bs, past, FX_WIDTH)
    past_v = cache_fox_v.reshape(depth, bs, past, FX_WIDTH)
    past_lf_t = jnp.pad(jnp.swapaxes(cache_fox_logf, 2, 3),
                        ((0, 0), (0, 0), (0, SUBLANES - FX_HEADS), (0, 0)))
    smem_k = cache_mem_k.reshape(depth, bs, n_mem, MEM_WIDTH)
    smem_v = cache_mem_v.reshape(depth, bs, n_mem, MEM_WIDTH)

    mp, ms = bp * seq, bs * dseq
    tb = _pick_tile(seq, 256)
    tt = _pick_tile(seq, 2048)
    yp = x_prompt.reshape(mp, d_model)
    ys = x_sample.reshape(ms, d_model)
    mem2 = mem_prompt.reshape(bp * n_mem, d_model)

    p_hg, p_k, p_v, p_lf, p_mk, p_mv = [], [], [], [], [], []
    s_hg, s_k, s_v, s_lf = [], [], [], []
    for l in range(depth):
        wl = (w_main[l], w_fl[l], w_flt[l], b_fl[l], b_flt[l], lbs)
        ng = hgrn_norm_g[l][None, :]
        g_ln, b_ln = ln_g[l][None, :], ln_b[l][None, :]

        mk, mv = _memkv(mem2, w_memb[l], tm=_pick_tile(bp * n_mem, 512))
        act, k, v, lf, d_nat, d_t = _proj(yp, *wl, layer=l, seq_len=seq, tm=tb)
        mix_hg, hg = _hgrn(act, ng, None, batch=bp, seq_len=seq, tt=tt)
        mix_at = _attn_prompt(act, k, v, d_nat, mk, mv, batch=bp, seq_len=seq, tb=tb)
        yp = _out(yp, mix_hg, mix_at, w_outb[l], g_ln, b_ln, alpha=alpha, tm=tb)
        p_hg.append(hg); p_k.append(k); p_v.append(v); p_lf.append(lf); p_mk.append(mk); p_mv.append(mv)

        act, k, v, lf, d_nat, d_t = _proj(ys, *wl, layer=l, seq_len=dseq, tm=ms)
        mix_hg, hg = _hgrn(act, ng, state_hgrn[l], batch=bs, seq_len=dseq, tt=dseq)
        mix_at = _attn_sample(act, k, v, d_nat, d_t, past_k[l], past_v[l], past_lf_t[l],
                              smem_k[l], smem_v[l], batch=bs, ts=dseq)
        ys = _out(ys, mix_hg, mix_at, w_outb[l], g_ln, b_ln, alpha=alpha, tm=ms)
        s_hg.append(hg); s_k.append(k); s_v.append(v); s_lf.append(lf)

    st = jnp.stack
    return (yp.reshape(bp, seq, d_model), ys.reshape(bs, dseq, d_model),
            st(p_hg),
            st(p_k).reshape(depth, bp, seq, FX_HEADS, FX_DIM),
            st(p_v).reshape(depth, bp, seq, FX_HEADS, FX_DIM),
            st(p_lf).reshape(depth, bp, seq, FX_HEADS),
            st(p_mk).reshape(depth, bp, n_mem, MEM_HEADS, MEM_DIM),
            st(p_mv).reshape(depth, bp, n_mem, MEM_HEADS, MEM_DIM),
            st(s_hg),
            st(s_k).reshape(depth, bs, dseq, FX_HEADS, FX_DIM),
            st(s_v).reshape(depth, bs, dseq, FX_HEADS, FX_DIM),
            st(s_lf).reshape(depth, bs, dseq, FX_HEADS))
```

```python
import functools

import jax
import jax.numpy as jnp
from jax import lax
from jax.experimental import pallas as pl
from jax.experimental.pallas import tpu as pltpu

F32 = jnp.float32
BF16 = jnp.bfloat16

HG_HEADS = 4
HG_DIM = 128
HG_WIDTH = HG_HEADS * HG_DIM
FX_HEADS = 4
FX_DIM = 64
FX_WIDTH = FX_HEADS * FX_DIM
MEM_HEADS = 4
MEM_DIM = 64
MEM_WIDTH = MEM_HEADS * MEM_DIM
HG_GROUP = 128
LN_EPS = 1e-5
RMS_EPS = 1e-6
LANES = 128
SUBLANES = 8
VMEM_LIMIT = 56 * 1024 * 1024
AUG = 3

C_HQ, C_HK, C_HV, C_HG, C_HGATE = 0, 512, 1024, 1536, 2048
C_FQ, C_FGATE, C_MQ, C_MGATE = 2560, 2816, 3072, 3328
ACT_WIDTH = 3584
W_HQ, W_HF, W_HI, W_HGATE, W_FQ, W_FK, W_FV, W_FGATE, W_MQ, W_MGATE = (
    0, 512, 1024, 1536, 2048, 2304, 2560, 2816, 3072, 3328)


def _nt(a, b, precision=None):
    return lax.dot_general(a, b, (((1,), (1,)), ((), ())), precision=precision,
                           preferred_element_type=F32)


def _mm(a, b, precision=None):
    return jnp.dot(a, b, precision=precision, preferred_element_type=F32)


def _silu(x):
    return x * jax.nn.sigmoid(x)


def _log_sigmoid(x):
    return jnp.minimum(x, 0.0) - jnp.log(1.0 + jnp.exp(-jnp.abs(x)))


def _head_mask(hh):
    lane = lax.broadcasted_iota(jnp.int32, (1, LANES), 1)
    return (lane // FX_DIM) == hh


def _pair_lanes(h):
    return slice((h // 2) * LANES, (h // 2 + 1) * LANES)


def _split3(x):
    hi = x.astype(BF16).astype(F32)
    r = x - hi
    mid = r.astype(BF16).astype(F32)
    return hi, mid, r - mid


def _aug_operand(x_pair, d, hh, is_query):
    lane = lax.broadcasted_iota(jnp.int32, (1, LANES), 1)
    a = lane - (1 - hh) * FX_DIM
    hi, mid, lo = _split3(d if is_query else -d)
    p0, o0 = (0, AUG) if is_query else (AUG, 0)
    extra = jnp.where(a == p0, hi, jnp.where(a == p0 + 1, mid, jnp.where(a == p0 + 2, lo,
                      jnp.where((a >= o0) & (a < o0 + AUG), 1.0, 0.0))))
    return jnp.where(_head_mask(hh), x_pair, extra).astype(BF16)


def _proj_kernel(x_ref, w_ref, wfl_ref, wflt_ref, bfl_ref, bflt_ref, lbs_ref, *refs,
                 layer, seg, tiles_per_seq, prompt):
    if prompt:
        act_ref, ka_ref, kt_ref, vt_ref, lft_ref, dn_ref, cn_ref, kv_ref = refs
    else:
        act_ref, k_ref, v_ref, lft_ref, dn_ref, dt_ref = refs
    tm = x_ref.shape[0]
    xb = x_ref[...].astype(BF16)

    def proj(c0, width):
        return _mm(xb, w_ref[:, c0:c0 + width])

    lbs = lbs_ref[...]
    e = jnp.exp(lbs - jnp.max(lbs, axis=0, keepdims=True))
    sm = e / jnp.sum(e, axis=0, keepdims=True)
    lb = jnp.sum(sm[:layer + 1], axis=0, keepdims=True) - sm[0:1]

    act_ref[:, C_HQ:C_HQ + HG_WIDTH] = _silu(proj(W_HQ, HG_WIDTH))
    forget = lb + (1.0 - lb) * jax.nn.sigmoid(proj(W_HF, HG_WIDTH))
    act_ref[:, C_HK:C_HK + HG_WIDTH] = 1.0 - forget
    act_ref[:, C_HG:C_HG + HG_WIDTH] = jnp.log(forget)
    act_ref[:, C_HV:C_HV + HG_WIDTH] = proj(W_HI, HG_WIDTH)
    act_ref[:, C_HGATE:C_HGATE + HG_WIDTH] = _silu(proj(W_HGATE, HG_WIDTH))
    act_ref[:, C_FQ:C_FQ + FX_WIDTH] = proj(W_FQ, FX_WIDTH)
    act_ref[:, C_FGATE:C_FGATE + FX_WIDTH] = _silu(proj(W_FGATE, FX_WIDTH))
    act_ref[:, C_MQ:C_MQ + MEM_WIDTH] = proj(W_MQ, MEM_WIDTH)
    act_ref[:, C_MGATE:C_MGATE + MEM_WIDTH] = _silu(proj(W_MGATE, MEM_WIDTH))

    lf_nat = _log_sigmoid(_mm(xb, wfl_ref[...]) + bfl_ref[...])
    lf_t = _log_sigmoid(_nt(wflt_ref[...], xb) + bflt_ref[...])

    r = lax.broadcasted_iota(jnp.int32, (tm, tm), 0)
    c = lax.broadcasted_iota(jnp.int32, (tm, tm), 1)
    same = (r // seg) == (c // seg)
    lower = jnp.where((c <= r) & same, 1.0, 0.0).astype(F32)
    d_nat = _mm(lower, lf_nat, precision=lax.Precision.HIGHEST)

    k = proj(W_FK, FX_WIDTH)
    v = proj(W_FV, FX_WIDTH)
    if prompt:
        if tiles_per_seq > 1:
            @pl.when(pl.program_id(0) % tiles_per_seq == 0)
            def _():
                cn_ref[...] = jnp.zeros_like(cn_ref)
            d_nat = d_nat + cn_ref[0:1, :]
            cn_ref[...] = jnp.broadcast_to(d_nat[tm - 1:tm, :], cn_ref.shape)
        lft_ref[0] = lf_t[:FX_HEADS]
        kv_ref[:, 0:FX_WIDTH] = k
        kv_ref[:, FX_WIDTH:] = v
        kt_ref[0] = kv_ref[:, 0:FX_WIDTH].T
        vt_ref[0] = kv_ref[:, FX_WIDTH:].T
        for h in range(FX_HEADS):
            ka_ref[:, h * LANES:(h + 1) * LANES] = _aug_operand(
                k[:, _pair_lanes(h)], d_nat[:, h:h + 1], h % 2, False)
    else:
        upper = jnp.where((r <= c) & same, 1.0, 0.0).astype(F32)
        dt_ref[...] = _mm(lf_t, upper, precision=lax.Precision.HIGHEST)
        lft_ref[...] = lf_t
        k_ref[...] = k
        v_ref[...] = v
    dn_ref[...] = d_nat


def _proj(x, w_main, w_fl, w_flt, b_fl, b_flt, lbs, *, layer, batch, seq_len, tm, prompt):
    m, d = x.shape
    ntiles = m // tm
    seg = min(seq_len, tm)
    tps = max(seq_len // tm, 1)
    assert prompt or ntiles == 1
    kern = functools.partial(_proj_kernel, layer=layer, seg=seg, tiles_per_seq=tps, prompt=prompt)
    full = lambda shape: pl.BlockSpec(shape, lambda i: (0,) * len(shape))
    rows = lambda width: pl.BlockSpec((tm, width), lambda i: (i, 0))
    feat = lambda nfeat: pl.BlockSpec((1, nfeat, tm), lambda i: (i // tps, 0, i % tps))
    if prompt:
        out_specs = [rows(ACT_WIDTH), rows(FX_HEADS * LANES), feat(FX_WIDTH), feat(FX_WIDTH),
                     feat(FX_HEADS), rows(LANES)]
        out_shape = [jax.ShapeDtypeStruct((m, ACT_WIDTH), F32),
                     jax.ShapeDtypeStruct((m, FX_HEADS * LANES), BF16),
                     jax.ShapeDtypeStruct((batch, FX_WIDTH, seq_len), F32),
                     jax.ShapeDtypeStruct((batch, FX_WIDTH, seq_len), F32),
                     jax.ShapeDtypeStruct((batch, FX_HEADS, seq_len), F32),
                     jax.ShapeDtypeStruct((m, LANES), F32)]
        scratch = [pltpu.VMEM((SUBLANES, LANES), F32), pltpu.VMEM((tm, 2 * FX_WIDTH), F32)]
    else:
        out_specs = [rows(ACT_WIDTH), rows(FX_WIDTH), rows(FX_WIDTH), full((SUBLANES, m)),
                     rows(LANES), full((SUBLANES, m))]
        out_shape = [jax.ShapeDtypeStruct((m, ACT_WIDTH), F32),
                     jax.ShapeDtypeStruct((m, FX_WIDTH), F32),
                     jax.ShapeDtypeStruct((m, FX_WIDTH), F32),
                     jax.ShapeDtypeStruct((SUBLANES, m), F32),
                     jax.ShapeDtypeStruct((m, LANES), F32),
                     jax.ShapeDtypeStruct((SUBLANES, m), F32)]
        scratch = []
    return pl.pallas_call(
        kern,
        grid=(ntiles,),
        in_specs=[
            pl.BlockSpec((tm, d), lambda i: (i, 0)),
            full(w_main.shape), full(w_fl.shape), full(w_flt.shape),
            full(b_fl.shape), full(b_flt.shape), full(lbs.shape),
        ],
        out_specs=out_specs,
        out_shape=out_shape,
        scratch_shapes=scratch,
        compiler_params=pltpu.CompilerParams(
            dimension_semantics=("arbitrary",), vmem_limit_bytes=VMEM_LIMIT),
        name="proj_prompt" if prompt else "proj_sample",
    )(x, w_main, w_fl, w_flt, b_fl, b_flt, lbs)


def _cumsum_rows(x, rows):
    n = x.shape[0]
    d = 1
    while d < n:
        x = x + jnp.where(rows >= d, pltpu.roll(x, d, axis=0), 0.0)
        d *= 2
    return x


def _half_ref(gc, m, rows):
    n, w = gc.shape
    if 2 * m >= SUBLANES:
        g3 = gc.reshape(n // (2 * m), 2 * m, w)
        return jnp.broadcast_to(g3[:, m - 1:m, :], g3.shape).reshape(n, w)
    pos = rows % (2 * m)
    if m == 2:
        return jnp.where(pos == 0, pltpu.roll(gc, n - 1, axis=0),
                         jnp.where(pos == 1, gc,
                                   jnp.where(pos == 2, pltpu.roll(gc, 1, axis=0),
                                             pltpu.roll(gc, 2, axis=0))))
    return jnp.where(pos == 0, gc, pltpu.roll(gc, 1, axis=0))


def _hgrn_group(q, k, v, g, st):
    n = q.shape[0]
    rows = lax.broadcasted_iota(jnp.int32, (n, HG_DIM), 0)
    rr = lax.broadcasted_iota(jnp.int32, (n, n), 0)
    cc = lax.broadcasted_iota(jnp.int32, (n, n), 1)
    diff_bits = rr ^ cc
    gc = _cumsum_rows(g, rows)
    o = _nt((q * jnp.exp(gc)).astype(BF16), st.astype(BF16))
    a = jnp.zeros((n, n), F32)
    m = n // 2
    while m >= 1:
        ref = _half_ref(gc, m, rows)
        first = (rows % (2 * m)) < m
        f = jnp.exp(jnp.where(first, ref - gc, gc - ref))
        a_m = _nt((q * f).astype(BF16), (k * f).astype(BF16))
        a = jnp.where((diff_bits >= m) & (diff_bits < 2 * m), a_m, a)
        m //= 2
    a = jnp.where(rr > cc, a, 0.0)
    o = o + _mm(a.astype(BF16), v.astype(BF16)) + jnp.sum(q * k, axis=-1, keepdims=True) * v
    g_last = gc[n - 1:n, :]
    k_end = k * jnp.exp(g_last - gc)
    upd = lax.dot_general(v.astype(BF16), k_end.astype(BF16), (((0,), (0,)), ((), ())),
                          preferred_element_type=F32)
    return o, st * jnp.exp(g_last) + upd


def _hgrn_kernel(*refs, group, has_init):
    if has_init:
        q_ref, k_ref, v_ref, g_ref, gate_ref, ng_ref, s0_ref, o_ref, so_ref, st_ref = refs
    else:
        q_ref, k_ref, v_ref, g_ref, gate_ref, ng_ref, o_ref, so_ref, st_ref = refs
    t = pl.program_id(2)

    @pl.when(t == 0)
    def _():
        if has_init:
            st_ref[...] = s0_ref[0, 0, 0].T
        else:
            st_ref[...] = jnp.zeros_like(st_ref)

    ng = ng_ref[...]

    def step(i, st):
        sl = pl.ds(pl.multiple_of(i * group, group), group)
        o, st = _hgrn_group(q_ref[sl, :], k_ref[sl, :], v_ref[sl, :], g_ref[sl, :], st)
        o = o * lax.rsqrt(jnp.mean(o * o, axis=-1, keepdims=True) + RMS_EPS) * ng
        o_ref[sl, :] = (o * gate_ref[sl, :]).astype(o_ref.dtype)
        return st

    st_ref[...] = lax.fori_loop(0, q_ref.shape[0] // group, step, st_ref[...])

    @pl.when(t == pl.num_programs(2) - 1)
    def _():
        so_ref[0, 0] = st_ref[...].T


def _hgrn(act, norm_g, s0, *, layer, batch, seq_len, tt):
    m = act.shape[0]
    nt = seq_len // tt
    has_init = s0 is not None
    group = min(tt, HG_GROUP)
    blk = lambda cb: pl.BlockSpec((tt, HG_DIM), lambda b, h, t, cb=cb: (b * nt + t, cb // HG_DIM + h))
    in_specs = [blk(C_HQ), blk(C_HK), blk(C_HV), blk(C_HG), blk(C_HGATE),
                pl.BlockSpec((1, HG_DIM), lambda b, h, t: (0, 0))]
    args = [act, act, act, act, act, norm_g]
    if has_init:
        in_specs.append(pl.BlockSpec((1, 1, 1, HG_DIM, HG_DIM), lambda b, h, t: (layer, b, h, 0, 0)))
        args.append(s0)
    return pl.pallas_call(
        functools.partial(_hgrn_kernel, group=group, has_init=has_init),
        grid=(batch, HG_HEADS, nt),
        in_specs=in_specs,
        out_specs=[
            pl.BlockSpec((tt, HG_DIM), lambda b, h, t: (b * nt + t, h)),
            pl.BlockSpec((1, 1, HG_DIM, HG_DIM), lambda b, h, t: (b, h, 0, 0)),
        ],
        out_shape=[
            jax.ShapeDtypeStruct((m, HG_WIDTH), BF16),
            jax.ShapeDtypeStruct((batch, HG_HEADS, HG_DIM, HG_DIM), F32),
        ],
        scratch_shapes=[pltpu.VMEM((HG_DIM, HG_DIM), F32)],
        compiler_params=pltpu.CompilerParams(
            dimension_semantics=("arbitrary", "arbitrary", "arbitrary"), vmem_limit_bytes=VMEM_LIMIT),
        name="hgrn",
    )(*args)


def _softmax_update_t(st, vt, m, l, acc):
    m_new = jnp.maximum(m, jnp.max(st, axis=0, keepdims=True))
    alpha = jnp.exp(m - m_new)
    p = jnp.exp(st - m_new)
    l = alpha * l + jnp.sum(p, axis=0, keepdims=True)
    acc = alpha * acc + _mm(vt, p.astype(BF16))
    return m_new, l, acc


def _attn_prompt_kernel(fq_ref, fgate_ref, mq_ref, mgate_ref, ka_ref, vt_ref, dn_ref, mk_ref, mvt_ref,
                        o_ref, vtb_ref, mvtb_ref, *, tq, tk):
    seq = fq_ref.shape[0]
    head_rows = lambda h: slice((h % 2) * FX_DIM, (h % 2 + 1) * FX_DIM)

    for j in range(seq // tk):
        for p in range(FX_HEADS // 2):
            vtb_ref[j, p] = vt_ref[0, p * LANES:(p + 1) * LANES, j * tk:(j + 1) * tk].astype(BF16)
    mvtb_ref[...] = mvt_ref[0].astype(BF16)

    key = lax.broadcasted_iota(jnp.int32, (tk, tq), 0)
    qry = lax.broadcasted_iota(jnp.int32, (tk, tq), 1)

    def qtile(i, c):
        rows = pl.ds(pl.multiple_of(i * tq, tq), tq)
        dnq = dn_ref[rows, :]
        last = (i * tq) // tk
        visible = last * tk + key <= i * tq + qry
        for p in range(FX_HEADS // 2):
            ls = _pair_lanes(2 * p)
            q_pair = fq_ref[rows, ls] * (FX_DIM ** -0.5)
            qa = [_aug_operand(q_pair, dnq[:, 2 * p + hh:2 * p + hh + 1], hh, True) for hh in range(2)]

            def scores(j, p=p, qa=qa):
                kr = pl.ds(pl.multiple_of(j * tk, tk), tk)
                return tuple(_nt(ka_ref[kr, (2 * p + hh) * LANES:(2 * p + hh + 1) * LANES], qa[hh])
                             for hh in range(2))

            def update(j, sts, stats, p=p):
                return tuple(_softmax_update_t(sts[hh], vtb_ref[j, p, head_rows(hh), :], *stats[hh])
                             for hh in range(2))

            def body(j, carry, scores=scores, update=update):
                sts, stats = carry
                nxt = scores(j + 1)
                return nxt, update(j, sts, stats)

            init = tuple((jnp.full((1, tq), -jnp.inf, F32), jnp.zeros((1, tq), F32),
                          jnp.zeros((FX_DIM, tq), F32)) for _ in range(2))
            sts, stats = lax.fori_loop(0, last, body, (scores(0), init))
            sts = tuple(jnp.where(visible, st, -jnp.inf) for st in sts)
            (_, l0, a0), (_, l1, a1) = update(last, sts, stats)
            o_t = jnp.concatenate([a0 / l0, a1 / l1], axis=0)
            o_ref[rows, ls] = (o_t.T * fgate_ref[rows, ls]).astype(o_ref.dtype)

            mq_pair = mq_ref[rows, ls] * (MEM_DIM ** -0.5)
            halves = []
            for hh in range(2):
                qm = jnp.where(_head_mask(hh), mq_pair, 0.0).astype(BF16)
                st = _nt(mk_ref[:, ls], qm)
                pr = jnp.exp(st - jnp.max(st, axis=0, keepdims=True))
                halves.append(_mm(mvtb_ref[p * LANES + hh * MEM_DIM:p * LANES + (hh + 1) * MEM_DIM, :],
                                  pr.astype(BF16)) / jnp.sum(pr, axis=0, keepdims=True))
            o_t = jnp.concatenate(halves, axis=0)
            lm = slice(FX_WIDTH + p * LANES, FX_WIDTH + (p + 1) * LANES)
            o_ref[rows, lm] = (o_t.T * mgate_ref[rows, ls]).astype(o_ref.dtype)
        return c

    lax.fori_loop(0, seq // tq, qtile, 0)


def _attn_prompt(act, ka, vt, d_nat, mk, mvt, *, batch, seq_len, tq, tk):
    m = act.shape[0]
    n_mem = mk.shape[0] // batch
    ablk = lambda c0: pl.BlockSpec((seq_len, FX_WIDTH), lambda b, c0=c0: (b, c0 // FX_WIDTH))
    return pl.pallas_call(
        functools.partial(_attn_prompt_kernel, tq=tq, tk=tk),
        grid=(batch,),
        in_specs=[
            ablk(C_FQ), ablk(C_FGATE), ablk(C_MQ), ablk(C_MGATE),
            pl.BlockSpec((seq_len, FX_HEADS * LANES), lambda b: (b, 0)),
            pl.BlockSpec((1, FX_WIDTH, seq_len), lambda b: (b, 0, 0)),
            pl.BlockSpec((seq_len, LANES), lambda b: (b, 0)),
            pl.BlockSpec((n_mem, MEM_WIDTH), lambda b: (b, 0)),
            pl.BlockSpec((1, MEM_WIDTH, n_mem), lambda b: (b, 0, 0)),
        ],
        out_specs=pl.BlockSpec((seq_len, FX_WIDTH + MEM_WIDTH), lambda b: (b, 0)),
        out_shape=jax.ShapeDtypeStruct((m, FX_WIDTH + MEM_WIDTH), BF16),
        scratch_shapes=[
            pltpu.VMEM((seq_len // tk, FX_HEADS // 2, LANES, tk), BF16),
            pltpu.VMEM((MEM_WIDTH, n_mem), BF16),
        ],
        compiler_params=pltpu.CompilerParams(
            dimension_semantics=("arbitrary",), vmem_limit_bytes=VMEM_LIMIT),
        name="attn_prompt",
    )(act, act, act, act, ka, vt, d_nat, mk, mvt)


def _stack_heads(x, scale):
    lane = lax.broadcasted_iota(jnp.int32, (1, x.shape[1]), 1)
    return jnp.concatenate([jnp.where(lane // FX_DIM == h, x * scale, 0.0) for h in range(FX_HEADS)],
                           axis=0).astype(BF16)


def _unstack_heads(o, ts):
    lane = lax.broadcasted_iota(jnp.int32, (1, o.shape[1]), 1)
    out = jnp.zeros((ts, o.shape[1]), F32)
    for h in range(FX_HEADS):
        out = jnp.where(lane // FX_DIM == h, o[h * ts:(h + 1) * ts, :], out)
    return out


def _per_head_rows(vals, ts):
    return jnp.concatenate([jnp.broadcast_to(v, (ts, v.shape[1])) for v in vals], axis=0)


def _softmax_update(s, pv_fn, m, l, acc):
    m_new = jnp.maximum(m, jnp.max(s, axis=-1, keepdims=True))
    alpha = jnp.exp(m - m_new)
    p = jnp.exp(s - m_new)
    l = alpha * l + jnp.sum(p, axis=-1, keepdims=True)
    acc = alpha * acc + pv_fn(p.astype(BF16))
    return m_new, l, acc


def _attn_sample_kernel(fq_ref, fgate_ref, mq_ref, mgate_ref, kn_ref, vn_ref, dn_ref, dt_ref,
                        pkt_ref, pvt_ref, plf_ref, mkt_ref, mvt_ref, o_ref, *, chunk):
    ts = fq_ref.shape[0]
    n_new = kn_ref.shape[0]
    past = pkt_ref.shape[3]
    nchunks = past // chunk
    b = pl.program_id(0)
    hrows = FX_HEADS * ts

    r = lax.broadcasted_iota(jnp.int32, (chunk, chunk), 0)
    c = lax.broadcasted_iota(jnp.int32, (chunk, chunk), 1)
    upper = jnp.where(r <= c, 1.0, 0.0).astype(F32)
    total = jnp.zeros((FX_HEADS, 1), F32)
    d_past = []
    for ci in range(nchunks):
        d = _mm(plf_ref[0, 0, :, ci * chunk:(ci + 1) * chunk], upper,
                precision=lax.Precision.HIGHEST) + total
        d_past.append(d)
        total = d[:, chunk - 1:chunk]

    dn = dn_ref[...]
    dt_new = dt_ref[...]
    tot = [total[h:h + 1, :] for h in range(FX_HEADS)]
    dq = jnp.concatenate([tot[h] + dn[:, h:h + 1] for h in range(FX_HEADS)], axis=0)
    q4 = _stack_heads(fq_ref[...], FX_DIM ** -0.5)

    carry = (jnp.full((hrows, 1), -jnp.inf, F32), jnp.zeros((hrows, 1), F32),
             jnp.zeros((hrows, FX_WIDTH), F32))
    for ci in range(nchunks):
        cs = slice(ci * chunk, (ci + 1) * chunk)
        dk = _per_head_rows([d_past[ci][h:h + 1, :] for h in range(FX_HEADS)], ts)
        s = _mm(q4, pkt_ref[0, 0, :, cs].astype(BF16)) + dq - dk
        vt = pvt_ref[0, 0, :, cs].astype(BF16)
        carry = _softmax_update(s, lambda p, vt=vt: _nt(p, vt), *carry)

    row = lax.broadcasted_iota(jnp.int32, (hrows, n_new), 0)
    col = lax.broadcasted_iota(jnp.int32, (hrows, n_new), 1)
    visible = (col // ts == b) & (col % ts <= row % ts)
    dk = _per_head_rows([tot[h] + dt_new[h:h + 1, :] for h in range(FX_HEADS)], ts)
    s = _nt(q4, kn_ref[...].astype(BF16)) + dq - dk
    vn = vn_ref[...].astype(BF16)
    _, l, acc = _softmax_update(jnp.where(visible, s, -jnp.inf), lambda p: _mm(p, vn), *carry)
    o_fx = _unstack_heads(acc / l, ts) * fgate_ref[...]
    o_ref[:, 0:FX_WIDTH] = o_fx.astype(o_ref.dtype)

    mq4 = _stack_heads(mq_ref[...], MEM_DIM ** -0.5)
    s = _mm(mq4, mkt_ref[0, 0].astype(BF16))
    p = jnp.exp(s - jnp.max(s, axis=-1, keepdims=True))
    o = _nt(p.astype(BF16), mvt_ref[0, 0].astype(BF16)) / jnp.sum(p, axis=-1, keepdims=True)
    o_mem = _unstack_heads(o, ts) * mgate_ref[...]
    o_ref[:, FX_WIDTH:] = o_mem.astype(o_ref.dtype)


def _attn_sample(act, k_new, v_new, d_nat, d_t, past_kt, past_vt, past_lft, mem_kt, mem_vt,
                 *, layer, batch, ts):
    m = act.shape[0]
    past = past_kt.shape[3]
    n_mem = mem_kt.shape[3]
    chunk = min(past, 512)
    ablk = lambda c0: pl.BlockSpec((ts, FX_WIDTH), lambda b, c0=c0: (b, c0 // FX_WIDTH))
    cache = lambda nfeat, n: pl.BlockSpec((1, 1, nfeat, n), lambda b: (layer, b, 0, 0))
    return pl.pallas_call(
        functools.partial(_attn_sample_kernel, chunk=chunk),
        grid=(batch,),
        in_specs=[
            ablk(C_FQ), ablk(C_FGATE), ablk(C_MQ), ablk(C_MGATE),
            pl.BlockSpec((m, FX_WIDTH), lambda b: (0, 0)),
            pl.BlockSpec((m, FX_WIDTH), lambda b: (0, 0)),
            pl.BlockSpec((ts, LANES), lambda b: (b, 0)),
            pl.BlockSpec((SUBLANES, m), lambda b: (0, 0)),
            cache(FX_WIDTH, past), cache(FX_WIDTH, past), cache(FX_HEADS, past),
            cache(MEM_WIDTH, n_mem), cache(MEM_WIDTH, n_mem),
        ],
        out_specs=pl.BlockSpec((ts, FX_WIDTH + MEM_WIDTH), lambda b: (b, 0)),
        out_shape=jax.ShapeDtypeStruct((m, FX_WIDTH + MEM_WIDTH), BF16),
        compiler_params=pltpu.CompilerParams(
            dimension_semantics=("arbitrary",), vmem_limit_bytes=VMEM_LIMIT),
        name="attn_sample",
    )(act, act, act, act, k_new, v_new, d_nat, d_t, past_kt, past_vt, past_lft, mem_kt, mem_vt)


def _memkv_kernel(x_ref, w_ref, kb_ref, kt_ref, vt_ref):
    kv = _mm(x_ref[...].astype(BF16), w_ref[...])
    kb_ref[...] = kv[:, :MEM_WIDTH].astype(BF16)
    kt_ref[0] = kv[:, :MEM_WIDTH].T
    vt_ref[0] = kv[:, MEM_WIDTH:].T


def _memkv(mem, w, *, batch):
    m, d = mem.shape
    n_mem = m // batch
    return pl.pallas_call(
        _memkv_kernel,
        grid=(batch,),
        in_specs=[pl.BlockSpec((n_mem, d), lambda i: (i, 0)),
                  pl.BlockSpec(w.shape, lambda i: (0, 0))],
        out_specs=[pl.BlockSpec((n_mem, MEM_WIDTH), lambda i: (i, 0)),
                   pl.BlockSpec((1, MEM_WIDTH, n_mem), lambda i: (i, 0, 0)),
                   pl.BlockSpec((1, MEM_WIDTH, n_mem), lambda i: (i, 0, 0))],
        out_shape=[jax.ShapeDtypeStruct((m, MEM_WIDTH), BF16),
                   jax.ShapeDtypeStruct((batch, MEM_WIDTH, n_mem), F32),
                   jax.ShapeDtypeStruct((batch, MEM_WIDTH, n_mem), F32)],
        compiler_params=pltpu.CompilerParams(
            dimension_semantics=("arbitrary",), vmem_limit_bytes=VMEM_LIMIT),
        name="memkv",
    )(mem, w)


def _out_kernel(x_ref, hg_ref, at_ref, w_ref, g_ref, b_ref, y_ref, *, alpha):
    acc = _mm(hg_ref[...], w_ref[0:HG_WIDTH, :]) + _mm(at_ref[...], w_ref[HG_WIDTH:, :])
    z = alpha * x_ref[...] + acc
    mu = jnp.mean(z, axis=-1, keepdims=True)
    zc = z - mu
    var = jnp.mean(zc * zc, axis=-1, keepdims=True)
    y_ref[...] = zc * lax.rsqrt(var + LN_EPS) * g_ref[...] + b_ref[...]


def _out(x, mix_hg, mix_at, w, g, b, *, alpha, tm):
    m, d = x.shape
    return pl.pallas_call(
        functools.partial(_out_kernel, alpha=alpha),
        grid=(m // tm,),
        in_specs=[
            pl.BlockSpec((tm, d), lambda i: (i, 0)),
            pl.BlockSpec((tm, HG_WIDTH), lambda i: (i, 0)),
            pl.BlockSpec((tm, FX_WIDTH + MEM_WIDTH), lambda i: (i, 0)),
            pl.BlockSpec(w.shape, lambda i: (0, 0)),
            pl.BlockSpec((1, d), lambda i: (0, 0)),
            pl.BlockSpec((1, d), lambda i: (0, 0)),
        ],
        out_specs=pl.BlockSpec((tm, d), lambda i: (i, 0)),
        out_shape=jax.ShapeDtypeStruct((m, d), F32),
        compiler_params=pltpu.CompilerParams(
            dimension_semantics=("arbitrary",), vmem_limit_bytes=VMEM_LIMIT),
        name="out",
    )(x, mix_hg, mix_at, w, g, b)


def _pick_tile(n, target):
    t = min(n, target)
    while n % t:
        t //= 2
    return t


def _feature_major(x):
    lead = x.shape[:-3]
    n, heads, dim = x.shape[-3:]
    nl = len(lead)
    perm = tuple(range(nl)) + (nl + 1, nl + 2, nl)
    return jnp.transpose(x, perm).reshape(lead + (heads * dim, n))


def _token_major(x, heads):
    lead = x.shape[:-2]
    width, n = x.shape[-2:]
    nl = len(lead)
    x = x.reshape(lead + (heads, width // heads, n))
    return jnp.transpose(x, tuple(range(nl)) + (nl + 2, nl, nl + 1))


def kernel(x_prompt, x_sample, mem_prompt, state_hgrn, cache_fox_k, cache_fox_v, cache_fox_logf,
           cache_mem_k, cache_mem_v, w_in, b_fox_forget, hgrn_lower_bounds, hgrn_norm_g,
           w_mem_kv, w_out, ln_g, ln_b):
    depth, d_model, _ = w_in.shape
    bp, seq, _ = x_prompt.shape
    bs, dseq, _ = x_sample.shape
    n_mem = mem_prompt.shape[1]
    alpha = (2 * depth) ** 0.25

    fl0 = 4 * HG_WIDTH + 4 * FX_WIDTH
    w_main = jnp.concatenate([w_in[:, :, :fl0], w_in[:, :, fl0 + FX_HEADS:]], axis=-1).astype(BF16)
    w_flc = w_in[:, :, fl0:fl0 + FX_HEADS]
    w_fl = jnp.pad(w_flc, ((0, 0), (0, 0), (0, LANES - FX_HEADS))).astype(BF16)
    w_flt = jnp.pad(jnp.swapaxes(w_flc, 1, 2), ((0, 0), (0, SUBLANES - FX_HEADS), (0, 0))).astype(BF16)
    b_fl = jnp.pad(b_fox_forget, ((0, 0), (0, LANES - FX_HEADS)))[:, None, :]
    b_flt = jnp.pad(b_fox_forget, ((0, 0), (0, SUBLANES - FX_HEADS)))[:, :, None]
    lbs = hgrn_lower_bounds.astype(F32)
    w_memb = w_mem_kv.astype(BF16)
    w_outb = w_out.astype(BF16)

    past_kt = _feature_major(cache_fox_k)
    past_vt = _feature_major(cache_fox_v)
    past_lft = jnp.swapaxes(cache_fox_logf, 2, 3)
    mem_kt = _feature_major(cache_mem_k)
    mem_vt = _feature_major(cache_mem_v)

    mp, ms = bp * seq, bs * dseq
    tb = _pick_tile(seq, 256)
    yp = x_prompt.reshape(mp, d_model)
    ys = x_sample.reshape(ms, d_model)
    mem2 = mem_prompt.reshape(bp * n_mem, d_model)

    p_hg, p_kt, p_vt, p_lft, p_mkt, p_mvt = [], [], [], [], [], []
    s_hg, s_k, s_v, s_lft = [], [], [], []
    for l in range(depth):
        wl = (w_main[l], w_fl[l], w_flt[l], b_fl[l], b_flt[l], lbs)
        ng = hgrn_norm_g[l][None, :]
        g_ln, b_ln = ln_g[l][None, :], ln_b[l][None, :]

        mkb, mkt, mvt = _memkv(mem2, w_memb[l], batch=bp)
        act, ka, kt, vt, lft, d_nat = _proj(yp, *wl, layer=l, batch=bp, seq_len=seq, tm=tb, prompt=True)
        mix_hg, hg = _hgrn(act, ng, None, layer=l, batch=bp, seq_len=seq, tt=_pick_tile(seq, 2048))
        mix_at = _attn_prompt(act, ka, vt, d_nat, mkb, mvt, batch=bp, seq_len=seq,
                              tq=_pick_tile(seq, 128), tk=_pick_tile(seq, 256))
        yp = _out(yp, mix_hg, mix_at, w_outb[l], g_ln, b_ln, alpha=alpha, tm=tb)
        p_hg.append(hg); p_kt.append(kt); p_vt.append(vt); p_lft.append(lft)
        p_mkt.append(mkt); p_mvt.append(mvt)

        act, k, v, lft, d_nat, d_t = _proj(ys, *wl, layer=l, batch=bs, seq_len=dseq, tm=ms, prompt=False)
        mix_hg, hg = _hgrn(act, ng, state_hgrn, layer=l, batch=bs, seq_len=dseq, tt=dseq)
        mix_at = _attn_sample(act, k, v, d_nat, d_t, past_kt, past_vt, past_lft, mem_kt, mem_vt,
                              layer=l, batch=bs, ts=dseq)
        ys = _out(ys, mix_hg, mix_at, w_outb[l], g_ln, b_ln, alpha=alpha, tm=ms)
        s_hg.append(hg); s_k.append(k); s_v.append(v); s_lft.append(lft)

    st = jnp.stack
    s_lf = st(s_lft)[:, :FX_HEADS, :].reshape(depth, FX_HEADS, bs, dseq)
    return (yp.reshape(bp, seq, d_model), ys.reshape(bs, dseq, d_model),
            st(p_hg),
            _token_major(st(p_kt), FX_HEADS),
            _token_major(st(p_vt), FX_HEADS),
            jnp.swapaxes(st(p_lft), 2, 3),
            _token_major(st(p_mkt), MEM_HEADS),
            _token_major(st(p_mvt), MEM_HEADS),
            st(s_hg),
            st(s_k).reshape(depth, bs, dseq, FX_HEADS, FX_DIM),
            st(s_v).reshape(depth, bs, dseq, FX_HEADS, FX_DIM),
            jnp.transpose(s_lf, (0, 2, 3, 1)))
```

```python
import functools

import jax
import jax.numpy as jnp
from jax import lax
from jax.experimental import pallas as pl
from jax.experimental.pallas import tpu as pltpu

F32 = jnp.float32
BF16 = jnp.bfloat16

HG_HEADS = 4
HG_DIM = 128
HG_WIDTH = HG_HEADS * HG_DIM
FX_HEADS = 4
FX_DIM = 64
FX_WIDTH = FX_HEADS * FX_DIM
MEM_HEADS = 4
MEM_DIM = 64
MEM_WIDTH = MEM_HEADS * MEM_DIM
HG_GROUP = 128
LN_EPS = 1e-5
RMS_EPS = 1e-6
LANES = 128
SUBLANES = 8
VMEM_LIMIT = 56 * 1024 * 1024
AUG = 3

C_HQ, C_HK, C_HV, C_HG, C_HGATE = 0, 512, 1024, 1536, 2048
C_FQ, C_FGATE, C_MQ, C_MGATE = 2560, 2816, 3072, 3328
ACT_WIDTH = 3584
W_HQ, W_HF, W_HI, W_HGATE, W_FQ, W_FK, W_FV, W_FGATE, W_MQ, W_MGATE = (
    0, 512, 1024, 1536, 2048, 2304, 2560, 2816, 3072, 3328)


def _nt(a, b, precision=None):
    return lax.dot_general(a, b, (((1,), (1,)), ((), ())), precision=precision,
                           preferred_element_type=F32)


def _mm(a, b, precision=None):
    return jnp.dot(a, b, precision=precision, preferred_element_type=F32)


def _silu(x):
    return x * jax.nn.sigmoid(x)


def _log_sigmoid(x):
    return jnp.minimum(x, 0.0) - jnp.log(1.0 + jnp.exp(-jnp.abs(x)))


def _head_mask(hh):
    lane = lax.broadcasted_iota(jnp.int32, (1, LANES), 1)
    return (lane // FX_DIM) == hh


def _pair_lanes(h):
    return slice((h // 2) * LANES, (h // 2 + 1) * LANES)


def _split3(x):
    hi = x.astype(BF16).astype(F32)
    r = x - hi
    mid = r.astype(BF16).astype(F32)
    return hi, mid, r - mid


def _aug_operand(x_pair, d, hh, is_query):
    lane = lax.broadcasted_iota(jnp.int32, (1, LANES), 1)
    a = lane - (1 - hh) * FX_DIM
    hi, mid, lo = _split3(d if is_query else -d)
    p0, o0 = (0, AUG) if is_query else (AUG, 0)
    extra = jnp.where(a == p0, hi, jnp.where(a == p0 + 1, mid, jnp.where(a == p0 + 2, lo,
                      jnp.where((a >= o0) & (a < o0 + AUG), 1.0, 0.0))))
    return jnp.where(_head_mask(hh), x_pair, extra).astype(BF16)


def _proj_kernel(x_ref, w_ref, wfl_ref, wflt_ref, bfl_ref, bflt_ref, lbs_ref, *refs,
                 layer, seg, tiles_per_seq, prompt):
    if prompt:
        act_ref, ka_ref, kt_ref, vt_ref, lft_ref, dn_ref, cn_ref, kv_ref = refs
    else:
        act_ref, k_ref, v_ref, lft_ref, dn_ref, dt_ref = refs
    tm = x_ref.shape[0]
    xb = x_ref[...].astype(BF16)

    def proj(c0, width):
        return _mm(xb, w_ref[:, c0:c0 + width])

    lbs = lbs_ref[...]
    e = jnp.exp(lbs - jnp.max(lbs, axis=0, keepdims=True))
    sm = e / jnp.sum(e, axis=0, keepdims=True)
    lb = jnp.sum(sm[:layer + 1], axis=0, keepdims=True) - sm[0:1]

    act_ref[:, C_HQ:C_HQ + HG_WIDTH] = _silu(proj(W_HQ, HG_WIDTH))
    forget = lb + (1.0 - lb) * jax.nn.sigmoid(proj(W_HF, HG_WIDTH))
    act_ref[:, C_HK:C_HK + HG_WIDTH] = 1.0 - forget
    act_ref[:, C_HG:C_HG + HG_WIDTH] = jnp.log(forget)
    act_ref[:, C_HV:C_HV + HG_WIDTH] = proj(W_HI, HG_WIDTH)
    act_ref[:, C_HGATE:C_HGATE + HG_WIDTH] = _silu(proj(W_HGATE, HG_WIDTH))
    act_ref[:, C_FQ:C_FQ + FX_WIDTH] = proj(W_FQ, FX_WIDTH)
    act_ref[:, C_FGATE:C_FGATE + FX_WIDTH] = _silu(proj(W_FGATE, FX_WIDTH))
    act_ref[:, C_MQ:C_MQ + MEM_WIDTH] = proj(W_MQ, MEM_WIDTH)
    act_ref[:, C_MGATE:C_MGATE + MEM_WIDTH] = _silu(proj(W_MGATE, MEM_WIDTH))

    lf_nat = _log_sigmoid(_mm(xb, wfl_ref[...]) + bfl_ref[...])
    lf_t = _log_sigmoid(_nt(wflt_ref[...], xb) + bflt_ref[...])

    r = lax.broadcasted_iota(jnp.int32, (tm, tm), 0)
    c = lax.broadcasted_iota(jnp.int32, (tm, tm), 1)
    same = (r // seg) == (c // seg)
    lower = jnp.where((c <= r) & same, 1.0, 0.0).astype(F32)
    d_nat = _mm(lower, lf_nat, precision=lax.Precision.HIGHEST)

    k = proj(W_FK, FX_WIDTH)
    v = proj(W_FV, FX_WIDTH)
    if prompt:
        if tiles_per_seq > 1:
            @pl.when(pl.program_id(0) % tiles_per_seq == 0)
            def _():
                cn_ref[...] = jnp.zeros_like(cn_ref)
            d_nat = d_nat + cn_ref[0:1, :]
            cn_ref[...] = jnp.broadcast_to(d_nat[tm - 1:tm, :], cn_ref.shape)
        lft_ref[0] = lf_t[:FX_HEADS]
        kv_ref[:, 0:FX_WIDTH] = k
        kv_ref[:, FX_WIDTH:] = v
        kt_ref[0] = kv_ref[:, 0:FX_WIDTH].T
        vt_ref[0] = kv_ref[:, FX_WIDTH:].T
        for h in range(FX_HEADS):
            ka_ref[:, h * LANES:(h + 1) * LANES] = _aug_operand(
                k[:, _pair_lanes(h)], d_nat[:, h:h + 1], h % 2, False)
    else:
        upper = jnp.where((r <= c) & same, 1.0, 0.0).astype(F32)
        dt_ref[...] = _mm(lf_t, upper, precision=lax.Precision.HIGHEST)
        lft_ref[...] = lf_t
        k_ref[...] = k
        v_ref[...] = v
    dn_ref[...] = d_nat


def _proj(x, w_main, w_fl, w_flt, b_fl, b_flt, lbs, *, layer, batch, seq_len, tm, prompt):
    m, d = x.shape
    ntiles = m // tm
    seg = min(seq_len, tm)
    tps = max(seq_len // tm, 1)
    assert prompt or ntiles == 1
    kern = functools.partial(_proj_kernel, layer=layer, seg=seg, tiles_per_seq=tps, prompt=prompt)
    full = lambda shape: pl.BlockSpec(shape, lambda i: (0,) * len(shape))
    rows = lambda width: pl.BlockSpec((tm, width), lambda i: (i, 0))
    feat = lambda nfeat: pl.BlockSpec((1, nfeat, tm), lambda i: (i // tps, 0, i % tps))
    if prompt:
        out_specs = [rows(ACT_WIDTH), rows(FX_HEADS * LANES), feat(FX_WIDTH), feat(FX_WIDTH),
                     feat(FX_HEADS), rows(LANES)]
        out_shape = [jax.ShapeDtypeStruct((m, ACT_WIDTH), F32),
                     jax.ShapeDtypeStruct((m, FX_HEADS * LANES), BF16),
                     jax.ShapeDtypeStruct((batch, FX_WIDTH, seq_len), F32),
                     jax.ShapeDtypeStruct((batch, FX_WIDTH, seq_len), F32),
                     jax.ShapeDtypeStruct((batch, FX_HEADS, seq_len), F32),
                     jax.ShapeDtypeStruct((m, LANES), F32)]
        scratch = [pltpu.VMEM((SUBLANES, LANES), F32), pltpu.VMEM((tm, 2 * FX_WIDTH), F32)]
    else:
        out_specs = [rows(ACT_WIDTH), rows(FX_WIDTH), rows(FX_WIDTH), full((SUBLANES, m)),
                     rows(LANES), full((SUBLANES, m))]
        out_shape = [jax.ShapeDtypeStruct((m, ACT_WIDTH), F32),
                     jax.ShapeDtypeStruct((m, FX_WIDTH), F32),
                     jax.ShapeDtypeStruct((m, FX_WIDTH), F32),
                     jax.ShapeDtypeStruct((SUBLANES, m), F32),
                     jax.ShapeDtypeStruct((m, LANES), F32),
                     jax.ShapeDtypeStruct((SUBLANES, m), F32)]
        scratch = []
    return pl.pallas_call(
        kern,
        grid=(ntiles,),
        in_specs=[
            pl.BlockSpec((tm, d), lambda i: (i, 0)),
            full(w_main.shape), full(w_fl.shape), full(w_flt.shape),
            full(b_fl.shape), full(b_flt.shape), full(lbs.shape),
        ],
        out_specs=out_specs,
        out_shape=out_shape,
        scratch_shapes=scratch,
        compiler_params=pltpu.CompilerParams(
            dimension_semantics=("arbitrary",), vmem_limit_bytes=VMEM_LIMIT),
        name="proj_prompt" if prompt else "proj_sample",
    )(x, w_main, w_fl, w_flt, b_fl, b_flt, lbs)


def _cumsum_rows(x, rows):
    n = x.shape[0]
    d = 1
    while d < n:
        x = x + jnp.where(rows >= d, pltpu.roll(x, d, axis=0), 0.0)
        d *= 2
    return x


def _half_ref(gc, m, rows):
    n, w = gc.shape
    if 2 * m >= SUBLANES:
        g3 = gc.reshape(n // (2 * m), 2 * m, w)
        return jnp.broadcast_to(g3[:, m - 1:m, :], g3.shape).reshape(n, w)
    pos = rows % (2 * m)
    if m == 2:
        return jnp.where(pos == 0, pltpu.roll(gc, n - 1, axis=0),
                         jnp.where(pos == 1, gc,
                                   jnp.where(pos == 2, pltpu.roll(gc, 1, axis=0),
                                             pltpu.roll(gc, 2, axis=0))))
    return jnp.where(pos == 0, gc, pltpu.roll(gc, 1, axis=0))


def _hgrn_group(q, k, v, g, st):
    n = q.shape[0]
    rows = lax.broadcasted_iota(jnp.int32, (n, HG_DIM), 0)
    rr = lax.broadcasted_iota(jnp.int32, (n, n), 0)
    cc = lax.broadcasted_iota(jnp.int32, (n, n), 1)
    diff_bits = rr ^ cc
    gc = _cumsum_rows(g, rows)
    o = _nt((q * jnp.exp(gc)).astype(BF16), st.astype(BF16))
    a = jnp.zeros((n, n), F32)
    m = n // 2
    while m >= 1:
        ref = _half_ref(gc, m, rows)
        first = (rows % (2 * m)) < m
        f = jnp.exp(jnp.where(first, ref - gc, gc - ref))
        a_m = _nt((q * f).astype(BF16), (k * f).astype(BF16))
        a = jnp.where((diff_bits >= m) & (diff_bits < 2 * m), a_m, a)
        m //= 2
    a = jnp.where(rr > cc, a, 0.0)
    o = o + _mm(a.astype(BF16), v.astype(BF16)) + jnp.sum(q * k, axis=-1, keepdims=True) * v
    g_last = gc[n - 1:n, :]
    k_end = k * jnp.exp(g_last - gc)
    upd = lax.dot_general(v.astype(BF16), k_end.astype(BF16), (((0,), (0,)), ((), ())),
                          preferred_element_type=F32)
    return o, st * jnp.exp(g_last) + upd


def _hgrn_kernel(*refs, group, has_init):
    if has_init:
        q_ref, k_ref, v_ref, g_ref, gate_ref, ng_ref, s0_ref, o_ref, so_ref, st_ref = refs
    else:
        q_ref, k_ref, v_ref, g_ref, gate_ref, ng_ref, o_ref, so_ref, st_ref = refs
    t = pl.program_id(2)

    @pl.when(t == 0)
    def _():
        if has_init:
            st_ref[...] = s0_ref[0, 0, 0].T
        else:
            st_ref[...] = jnp.zeros_like(st_ref)

    ng = ng_ref[...]

    def step(i, st):
        sl = pl.ds(pl.multiple_of(i * group, group), group)
        o, st = _hgrn_group(q_ref[sl, :], k_ref[sl, :], v_ref[sl, :], g_ref[sl, :], st)
        o = o * lax.rsqrt(jnp.mean(o * o, axis=-1, keepdims=True) + RMS_EPS) * ng
        o_ref[sl, :] = (o * gate_ref[sl, :]).astype(o_ref.dtype)
        return st

    st_ref[...] = lax.fori_loop(0, q_ref.shape[0] // group, step, st_ref[...])

    @pl.when(t == pl.num_programs(2) - 1)
    def _():
        so_ref[0, 0] = st_ref[...].T


def _hgrn(act, norm_g, s0, *, layer, batch, seq_len, tt):
    m = act.shape[0]
    nt = seq_len // tt
    has_init = s0 is not None
    group = min(tt, HG_GROUP)
    blk = lambda cb: pl.BlockSpec((tt, HG_DIM), lambda b, h, t, cb=cb: (b * nt + t, cb // HG_DIM + h))
    in_specs = [blk(C_HQ), blk(C_HK), blk(C_HV), blk(C_HG), blk(C_HGATE),
                pl.BlockSpec((1, HG_DIM), lambda b, h, t: (0, 0))]
    args = [act, act, act, act, act, norm_g]
    if has_init:
        in_specs.append(pl.BlockSpec((1, 1, 1, HG_DIM, HG_DIM), lambda b, h, t: (layer, b, h, 0, 0)))
        args.append(s0)
    return pl.pallas_call(
        functools.partial(_hgrn_kernel, group=group, has_init=has_init),
        grid=(batch, HG_HEADS, nt),
        in_specs=in_specs,
        out_specs=[
            pl.BlockSpec((tt, HG_DIM), lambda b, h, t: (b * nt + t, h)),
            pl.BlockSpec((1, 1, HG_DIM, HG_DIM), lambda b, h, t: (b, h, 0, 0)),
        ],
        out_shape=[
            jax.ShapeDtypeStruct((m, HG_WIDTH), BF16),
            jax.ShapeDtypeStruct((batch, HG_HEADS, HG_DIM, HG_DIM), F32),
        ],
        scratch_shapes=[pltpu.VMEM((HG_DIM, HG_DIM), F32)],
        compiler_params=pltpu.CompilerParams(
            dimension_semantics=("arbitrary", "arbitrary", "arbitrary"), vmem_limit_bytes=VMEM_LIMIT),
        name="hgrn",
    )(*args)


def _softmax_t(st, m, l):
    m_new = jnp.maximum(m, jnp.max(st, axis=0, keepdims=True))
    alpha = jnp.exp(m - m_new)
    p = jnp.exp(st - m_new)
    return m_new, alpha * l + jnp.sum(p, axis=0, keepdims=True), alpha, p.astype(BF16)


def _attn_prompt_kernel(fq_ref, fgate_ref, mq_ref, mgate_ref, ka_ref, vt_ref, dn_ref, mk_ref, mvt_ref,
                        o_ref, vtb_ref, mvtb_ref, qa_ref, st_ref, p_ref, acc_ref, *, tq):
    seq = fq_ref.shape[0]
    tk = tq
    heads = range(FX_HEADS)

    for j in range(seq // tk):
        for p in range(FX_HEADS // 2):
            vtb_ref[j, p] = vt_ref[0, p * LANES:(p + 1) * LANES, j * tk:(j + 1) * tk].astype(BF16)
    mvtb_ref[...] = mvt_ref[0].astype(BF16)

    key = lax.broadcasted_iota(jnp.int32, (tk, tq), 0)
    qry = lax.broadcasted_iota(jnp.int32, (tk, tq), 1)

    def scores(j, h):
        kr = pl.ds(pl.multiple_of(j * tk, tk), tk)
        return _nt(ka_ref[kr, h * LANES:(h + 1) * LANES], qa_ref[h])

    def values(j, h):
        return vtb_ref[j, h // 2, (h % 2) * FX_DIM:(h % 2 + 1) * FX_DIM, :]

    def qtile(i, c):
        rows = pl.ds(pl.multiple_of(i * tq, tq), tq)
        dnq = dn_ref[rows, :]
        for h in heads:
            qa_ref[h] = _aug_operand(fq_ref[rows, _pair_lanes(h)] * (FX_DIM ** -0.5),
                                     dnq[:, h:h + 1], h % 2, True)
        for h in heads:
            st_ref[h] = scores(0, h)
            p_ref[h] = jnp.zeros((tk, tq), BF16)
            acc_ref[h] = jnp.zeros((FX_DIM, tq), F32)

        def body(j, carry):
            prev = jnp.maximum(j - 1, 0)
            out = []
            for h in heads:
                m, l, alpha_prev = carry[h]
                pv = _mm(values(prev, h), p_ref[h])
                st = st_ref[h]
                st_ref[h] = scores(j + 1, h)
                m, l, alpha, p = _softmax_t(st, m, l)
                acc_ref[h] = alpha_prev * acc_ref[h] + pv
                p_ref[h] = p
                out.append((m, l, alpha))
            return tuple(out)

        init = tuple((jnp.full((1, tq), -jnp.inf, F32), jnp.zeros((1, tq), F32), jnp.ones((1, tq), F32))
                     for _ in heads)
        carry = lax.fori_loop(0, i, body, init)

        prev = jnp.maximum(i - 1, 0)
        o_heads = []
        for h in heads:
            m, l, alpha_prev = carry[h]
            acc = alpha_prev * acc_ref[h] + _mm(values(prev, h), p_ref[h])
            m, l, alpha, p = _softmax_t(jnp.where(key <= qry, st_ref[h], -jnp.inf), m, l)
            acc = alpha * acc + _mm(values(i, h), p)
            o_heads.append(acc / l)
        for pr in range(FX_HEADS // 2):
            ls = _pair_lanes(2 * pr)
            o_t = jnp.concatenate(o_heads[2 * pr:2 * pr + 2], axis=0)
            o_ref[rows, ls] = (o_t.T * fgate_ref[rows, ls]).astype(o_ref.dtype)

        for pr in range(MEM_HEADS // 2):
            ls = _pair_lanes(2 * pr)
            mq_pair = mq_ref[rows, ls] * (MEM_DIM ** -0.5)
            halves = []
            for hh in range(2):
                qm = jnp.where(_head_mask(hh), mq_pair, 0.0).astype(BF16)
                st = _nt(mk_ref[:, ls], qm)
                pm = jnp.exp(st - jnp.max(st, axis=0, keepdims=True))
                halves.append(_mm(mvtb_ref[pr * LANES + hh * MEM_DIM:pr * LANES + (hh + 1) * MEM_DIM, :],
                                  pm.astype(BF16)) / jnp.sum(pm, axis=0, keepdims=True))
            o_t = jnp.concatenate(halves, axis=0)
            lm = slice(FX_WIDTH + pr * LANES, FX_WIDTH + (pr + 1) * LANES)
            o_ref[rows, lm] = (o_t.T * mgate_ref[rows, ls]).astype(o_ref.dtype)
        return c

    lax.fori_loop(0, seq // tq, qtile, 0)


def _attn_prompt(act, ka, vt, d_nat, mk, mvt, *, batch, seq_len, tq):
    m = act.shape[0]
    n_mem = mk.shape[0] // batch
    tk = tq
    ablk = lambda c0: pl.BlockSpec((seq_len, FX_WIDTH), lambda b, c0=c0: (b, c0 // FX_WIDTH))
    return pl.pallas_call(
        functools.partial(_attn_prompt_kernel, tq=tq),
        grid=(batch,),
        in_specs=[
            ablk(C_FQ), ablk(C_FGATE), ablk(C_MQ), ablk(C_MGATE),
            pl.BlockSpec((seq_len, FX_HEADS * LANES), lambda b: (b, 0)),
            pl.BlockSpec((1, FX_WIDTH, seq_len), lambda b: (b, 0, 0)),
            pl.BlockSpec((seq_len, LANES), lambda b: (b, 0)),
            pl.BlockSpec((n_mem, MEM_WIDTH), lambda b: (b, 0)),
            pl.BlockSpec((1, MEM_WIDTH, n_mem), lambda b: (b, 0, 0)),
        ],
        out_specs=pl.BlockSpec((seq_len, FX_WIDTH + MEM_WIDTH), lambda b: (b, 0)),
        out_shape=jax.ShapeDtypeStruct((m, FX_WIDTH + MEM_WIDTH), BF16),
        scratch_shapes=[
            pltpu.VMEM((seq_len // tk, FX_HEADS // 2, LANES, tk), BF16),
            pltpu.VMEM((MEM_WIDTH, n_mem), BF16),
            pltpu.VMEM((FX_HEADS, tq, LANES), BF16),
            pltpu.VMEM((FX_HEADS, tk, tq), F32),
            pltpu.VMEM((FX_HEADS, tk, tq), BF16),
            pltpu.VMEM((FX_HEADS, FX_DIM, tq), F32),
        ],
        compiler_params=pltpu.CompilerParams(
            dimension_semantics=("arbitrary",), vmem_limit_bytes=VMEM_LIMIT),
        name="attn_prompt",
    )(act, act, act, act, ka, vt, d_nat, mk, mvt)


def _stack_heads(x, scale):
    lane = lax.broadcasted_iota(jnp.int32, (1, x.shape[1]), 1)
    return jnp.concatenate([jnp.where(lane // FX_DIM == h, x * scale, 0.0) for h in range(FX_HEADS)],
                           axis=0).astype(BF16)


def _unstack_heads(o, ts):
    lane = lax.broadcasted_iota(jnp.int32, (1, o.shape[1]), 1)
    out = jnp.zeros((ts, o.shape[1]), F32)
    for h in range(FX_HEADS):
        out = jnp.where(lane // FX_DIM == h, o[h * ts:(h + 1) * ts, :], out)
    return out


def _per_head_rows(vals, ts):
    return jnp.concatenate([jnp.broadcast_to(v, (ts, v.shape[1])) for v in vals], axis=0)


def _softmax_update(s, pv_fn, m, l, acc):
    m_new = jnp.maximum(m, jnp.max(s, axis=-1, keepdims=True))
    alpha = jnp.exp(m - m_new)
    p = jnp.exp(s - m_new)
    l = alpha * l + jnp.sum(p, axis=-1, keepdims=True)
    acc = alpha * acc + pv_fn(p.astype(BF16))
    return m_new, l, acc


def _attn_sample_kernel(fq_ref, fgate_ref, mq_ref, mgate_ref, kn_ref, vn_ref, dn_ref, dt_ref,
                        pkt_ref, pvt_ref, plf_ref, mkt_ref, mvt_ref, o_ref, *, chunk):
    ts = fq_ref.shape[0]
    n_new = kn_ref.shape[0]
    past = pkt_ref.shape[3]
    nchunks = past // chunk
    b = pl.program_id(0)
    hrows = FX_HEADS * ts

    r = lax.broadcasted_iota(jnp.int32, (chunk, chunk), 0)
    c = lax.broadcasted_iota(jnp.int32, (chunk, chunk), 1)
    upper = jnp.where(r <= c, 1.0, 0.0).astype(F32)
    total = jnp.zeros((FX_HEADS, 1), F32)
    d_past = []
    for ci in range(nchunks):
        d = _mm(plf_ref[0, 0, :, ci * chunk:(ci + 1) * chunk], upper,
                precision=lax.Precision.HIGHEST) + total
        d_past.append(d)
        total = d[:, chunk - 1:chunk]

    dn = dn_ref[...]
    dt_new = dt_ref[...]
    tot = [total[h:h + 1, :] for h in range(FX_HEADS)]
    dq = jnp.concatenate([tot[h] + dn[:, h:h + 1] for h in range(FX_HEADS)], axis=0)
    q4 = _stack_heads(fq_ref[...], FX_DIM ** -0.5)

    carry = (jnp.full((hrows, 1), -jnp.inf, F32), jnp.zeros((hrows, 1), F32),
             jnp.zeros((hrows, FX_WIDTH), F32))
    for ci in range(nchunks):
        cs = slice(ci * chunk, (ci + 1) * chunk)
        dk = _per_head_rows([d_past[ci][h:h + 1, :] for h in range(FX_HEADS)], ts)
        s = _mm(q4, pkt_ref[0, 0, :, cs].astype(BF16)) + dq - dk
        vt = pvt_ref[0, 0, :, cs].astype(BF16)
        carry = _softmax_update(s, lambda p, vt=vt: _nt(p, vt), *carry)

    row = lax.broadcasted_iota(jnp.int32, (hrows, n_new), 0)
    col = lax.broadcasted_iota(jnp.int32, (hrows, n_new), 1)
    visible = (col // ts == b) & (col % ts <= row % ts)
    dk = _per_head_rows([tot[h] + dt_new[h:h + 1, :] for h in range(FX_HEADS)], ts)
    s = _nt(q4, kn_ref[...].astype(BF16)) + dq - dk
    vn = vn_ref[...].astype(BF16)
    _, l, acc = _softmax_update(jnp.where(visible, s, -jnp.inf), lambda p: _mm(p, vn), *carry)
    o_fx = _unstack_heads(acc / l, ts) * fgate_ref[...]
    o_ref[:, 0:FX_WIDTH] = o_fx.astype(o_ref.dtype)

    mq4 = _stack_heads(mq_ref[...], MEM_DIM ** -0.5)
    s = _mm(mq4, mkt_ref[0, 0].astype(BF16))
    p = jnp.exp(s - jnp.max(s, axis=-1, keepdims=True))
    o = _nt(p.astype(BF16), mvt_ref[0, 0].astype(BF16)) / jnp.sum(p, axis=-1, keepdims=True)
    o_mem = _unstack_heads(o, ts) * mgate_ref[...]
    o_ref[:, FX_WIDTH:] = o_mem.astype(o_ref.dtype)


def _attn_sample(act, k_new, v_new, d_nat, d_t, past_kt, past_vt, past_lft, mem_kt, mem_vt,
                 *, layer, batch, ts):
    m = act.shape[0]
    past = past_kt.shape[3]
    n_mem = mem_kt.shape[3]
    chunk = min(past, 512)
    ablk = lambda c0: pl.BlockSpec((ts, FX_WIDTH), lambda b, c0=c0: (b, c0 // FX_WIDTH))
    cache = lambda nfeat, n: pl.BlockSpec((1, 1, nfeat, n), lambda b: (layer, b, 0, 0))
    return pl.pallas_call(
        functools.partial(_attn_sample_kernel, chunk=chunk),
        grid=(batch,),
        in_specs=[
            ablk(C_FQ), ablk(C_FGATE), ablk(C_MQ), ablk(C_MGATE),
            pl.BlockSpec((m, FX_WIDTH), lambda b: (0, 0)),
            pl.BlockSpec((m, FX_WIDTH), lambda b: (0, 0)),
            pl.BlockSpec((ts, LANES), lambda b: (b, 0)),
            pl.BlockSpec((SUBLANES, m), lambda b: (0, 0)),
            cache(FX_WIDTH, past), cache(FX_WIDTH, past), cache(FX_HEADS, past),
            cache(MEM_WIDTH, n_mem), cache(MEM_WIDTH, n_mem),
        ],
        out_specs=pl.BlockSpec((ts, FX_WIDTH + MEM_WIDTH), lambda b: (b, 0)),
        out_shape=jax.ShapeDtypeStruct((m, FX_WIDTH + MEM_WIDTH), BF16),
        compiler_params=pltpu.CompilerParams(
            dimension_semantics=("arbitrary",), vmem_limit_bytes=VMEM_LIMIT),
        name="attn_sample",
    )(act, act, act, act, k_new, v_new, d_nat, d_t, past_kt, past_vt, past_lft, mem_kt, mem_vt)


def _memkv_kernel(x_ref, w_ref, kb_ref, kt_ref, vt_ref):
    kv = _mm(x_ref[...].astype(BF16), w_ref[...])
    kb_ref[...] = kv[:, :MEM_WIDTH].astype(BF16)
    kt_ref[0] = kv[:, :MEM_WIDTH].T
    vt_ref[0] = kv[:, MEM_WIDTH:].T


def _memkv(mem, w, *, batch):
    m, d = mem.shape
    n_mem = m // batch
    return pl.pallas_call(
        _memkv_kernel,
        grid=(batch,),
        in_specs=[pl.BlockSpec((n_mem, d), lambda i: (i, 0)),
                  pl.BlockSpec(w.shape, lambda i: (0, 0))],
        out_specs=[pl.BlockSpec((n_mem, MEM_WIDTH), lambda i: (i, 0)),
                   pl.BlockSpec((1, MEM_WIDTH, n_mem), lambda i: (i, 0, 0)),
                   pl.BlockSpec((1, MEM_WIDTH, n_mem), lambda i: (i, 0, 0))],
        out_shape=[jax.ShapeDtypeStruct((m, MEM_WIDTH), BF16),
                   jax.ShapeDtypeStruct((batch, MEM_WIDTH, n_mem), F32),
                   jax.ShapeDtypeStruct((batch, MEM_WIDTH, n_mem), F32)],
        compiler_params=pltpu.CompilerParams(
            dimension_semantics=("arbitrary",), vmem_limit_bytes=VMEM_LIMIT),
        name="memkv",
    )(mem, w)


def _out_kernel(x_ref, hg_ref, at_ref, w_ref, g_ref, b_ref, y_ref, *, alpha):
    acc = _mm(hg_ref[...], w_ref[0:HG_WIDTH, :]) + _mm(at_ref[...], w_ref[HG_WIDTH:, :])
    z = alpha * x_ref[...] + acc
    mu = jnp.mean(z, axis=-1, keepdims=True)
    zc = z - mu
    var = jnp.mean(zc * zc, axis=-1, keepdims=True)
    y_ref[...] = zc * lax.rsqrt(var + LN_EPS) * g_ref[...] + b_ref[...]


def _out(x, mix_hg, mix_at, w, g, b, *, alpha, tm):
    m, d = x.shape
    return pl.pallas_call(
        functools.partial(_out_kernel, alpha=alpha),
        grid=(m // tm,),
        in_specs=[
            pl.BlockSpec((tm, d), lambda i: (i, 0)),
            pl.BlockSpec((tm, HG_WIDTH), lambda i: (i, 0)),
            pl.BlockSpec((tm, FX_WIDTH + MEM_WIDTH), lambda i: (i, 0)),
            pl.BlockSpec(w.shape, lambda i: (0, 0)),
            pl.BlockSpec((1, d), lambda i: (0, 0)),
            pl.BlockSpec((1, d), lambda i: (0, 0)),
        ],
        out_specs=pl.BlockSpec((tm, d), lambda i: (i, 0)),
        out_shape=jax.ShapeDtypeStruct((m, d), F32),
        compiler_params=pltpu.CompilerParams(
            dimension_semantics=("arbitrary",), vmem_limit_bytes=VMEM_LIMIT),
        name="out",
    )(x, mix_hg, mix_at, w, g, b)


def _pick_tile(n, target):
    t = min(n, target)
    while n % t:
        t //= 2
    return t


def _feature_major(x):
    lead = x.shape[:-3]
    n, heads, dim = x.shape[-3:]
    nl = len(lead)
    perm = tuple(range(nl)) + (nl + 1, nl + 2, nl)
    return jnp.transpose(x, perm).reshape(lead + (heads * dim, n))


def _token_major(x, heads):
    lead = x.shape[:-2]
    width, n = x.shape[-2:]
    nl = len(lead)
    x = x.reshape(lead + (heads, width // heads, n))
    return jnp.transpose(x, tuple(range(nl)) + (nl + 2, nl, nl + 1))


def kernel(x_prompt, x_sample, mem_prompt, state_hgrn, cache_fox_k, cache_fox_v, cache_fox_logf,
           cache_mem_k, cache_mem_v, w_in, b_fox_forget, hgrn_lower_bounds, hgrn_norm_g,
           w_mem_kv, w_out, ln_g, ln_b):
    depth, d_model, _ = w_in.shape
    bp, seq, _ = x_prompt.shape
    bs, dseq, _ = x_sample.shape
    n_mem = mem_prompt.shape[1]
    alpha = (2 * depth) ** 0.25

    fl0 = 4 * HG_WIDTH + 4 * FX_WIDTH
    w_main = jnp.concatenate([w_in[:, :, :fl0], w_in[:, :, fl0 + FX_HEADS:]], axis=-1).astype(BF16)
    w_flc = w_in[:, :, fl0:fl0 + FX_HEADS]
    w_fl = jnp.pad(w_flc, ((0, 0), (0, 0), (0, LANES - FX_HEADS))).astype(BF16)
    w_flt = jnp.pad(jnp.swapaxes(w_flc, 1, 2), ((0, 0), (0, SUBLANES - FX_HEADS), (0, 0))).astype(BF16)
    b_fl = jnp.pad(b_fox_forget, ((0, 0), (0, LANES - FX_HEADS)))[:, None, :]
    b_flt = jnp.pad(b_fox_forget, ((0, 0), (0, SUBLANES - FX_HEADS)))[:, :, None]
    lbs = hgrn_lower_bounds.astype(F32)
    w_memb = w_mem_kv.astype(BF16)
    w_outb = w_out.astype(BF16)

    past_kt = _feature_major(cache_fox_k)
    past_vt = _feature_major(cache_fox_v)
    past_lft = jnp.swapaxes(cache_fox_logf, 2, 3)
    mem_kt = _feature_major(cache_mem_k)
    mem_vt = _feature_major(cache_mem_v)

    mp, ms = bp * seq, bs * dseq
    tb = _pick_tile(seq, 256)
    yp = x_prompt.reshape(mp, d_model)
    ys = x_sample.reshape(ms, d_model)
    mem2 = mem_prompt.reshape(bp * n_mem, d_model)

    p_hg, p_kt, p_vt, p_lft, p_mkt, p_mvt = [], [], [], [], [], []
    s_hg, s_k, s_v, s_lft = [], [], [], []
    for l in range(depth):
        wl = (w_main[l], w_fl[l], w_flt[l], b_fl[l], b_flt[l], lbs)
        ng = hgrn_norm_g[l][None, :]
        g_ln, b_ln = ln_g[l][None, :], ln_b[l][None, :]

        mkb, mkt, mvt = _memkv(mem2, w_memb[l], batch=bp)
        act, ka, kt, vt, lft, d_nat = _proj(yp, *wl, layer=l, batch=bp, seq_len=seq, tm=tb, prompt=True)
        mix_hg, hg = _hgrn(act, ng, None, layer=l, batch=bp, seq_len=seq, tt=_pick_tile(seq, 2048))
        mix_at = _attn_prompt(act, ka, vt, d_nat, mkb, mvt, batch=bp, seq_len=seq, tq=tb)
        yp = _out(yp, mix_hg, mix_at, w_outb[l], g_ln, b_ln, alpha=alpha, tm=tb)
        p_hg.append(hg); p_kt.append(kt); p_vt.append(vt); p_lft.append(lft)
        p_mkt.append(mkt); p_mvt.append(mvt)

        act, k, v, lft, d_nat, d_t = _proj(ys, *wl, layer=l, batch=bs, seq_len=dseq, tm=ms, prompt=False)
        mix_hg, hg = _hgrn(act, ng, state_hgrn, layer=l, batch=bs, seq_len=dseq, tt=dseq)
        mix_at = _attn_sample(act, k, v, d_nat, d_t, past_kt, past_vt, past_lft, mem_kt, mem_vt,
                              layer=l, batch=bs, ts=dseq)
        ys = _out(ys, mix_hg, mix_at, w_outb[l], g_ln, b_ln, alpha=alpha, tm=ms)
        s_hg.append(hg); s_k.append(k); s_v.append(v); s_lft.append(lft)

    st = jnp.stack
    s_lf = st(s_lft)[:, :FX_HEADS, :].reshape(depth, FX_HEADS, bs, dseq)
    return (yp.reshape(bp, seq, d_model), ys.reshape(bs, dseq, d_model),
            st(p_hg),
            _token_major(st(p_kt), FX_HEADS),
            _token_major(st(p_vt), FX_HEADS),
            jnp.swapaxes(st(p_lft), 2, 3),
            _token_major(st(p_mkt), MEM_HEADS),
            _token_major(st(p_mvt), MEM_HEADS),
            st(s_hg),
            st(s_k).reshape(depth, bs, dseq, FX_HEADS, FX_DIM),
            st(s_v).reshape(depth, bs, dseq, FX_HEADS, FX_DIM),
            jnp.transpose(s_lf, (0, 2, 3, 1)))
```

```python
import functools

import jax
import jax.numpy as jnp
from jax import lax
from jax.experimental import pallas as pl
from jax.experimental.pallas import tpu as pltpu

F32 = jnp.float32
BF16 = jnp.bfloat16

HG_HEADS = 4
HG_DIM = 128
HG_WIDTH = HG_HEADS * HG_DIM
FX_HEADS = 4
FX_DIM = 64
FX_WIDTH = FX_HEADS * FX_DIM
MEM_HEADS = 4
MEM_DIM = 64
MEM_WIDTH = MEM_HEADS * MEM_DIM
HG_GROUP = 128
LN_EPS = 1e-5
RMS_EPS = 1e-6
LANES = 128
SUBLANES = 8
VMEM_LIMIT = 56 * 1024 * 1024
AUG = 3
LOG2E = 1.4426950408889634

C_HQ, C_HK, C_HV, C_HG, C_HGATE = 0, 512, 1024, 1536, 2048
C_FQ, C_FGATE, C_MQ, C_MGATE = 2560, 2816, 3072, 3328
ACT_WIDTH = 3584
W_HQ, W_HF, W_HI, W_HGATE, W_FQ, W_FK, W_FV, W_FGATE, W_MQ, W_MGATE = (
    0, 512, 1024, 1536, 2048, 2304, 2560, 2816, 3072, 3328)


def _nt(a, b, precision=None):
    return lax.dot_general(a, b, (((1,), (1,)), ((), ())), precision=precision,
                           preferred_element_type=F32)


def _mm(a, b, precision=None):
    return jnp.dot(a, b, precision=precision, preferred_element_type=F32)


def _silu(x):
    return x * jax.nn.sigmoid(x)


def _log_sigmoid(x):
    return jnp.minimum(x, 0.0) - jnp.log(1.0 + jnp.exp(-jnp.abs(x)))


def _head_mask(hh):
    lane = lax.broadcasted_iota(jnp.int32, (1, LANES), 1)
    return (lane // FX_DIM) == hh


def _pair_lanes(h):
    return slice((h // 2) * LANES, (h // 2 + 1) * LANES)


def _split3(x):
    hi = x.astype(BF16).astype(F32)
    r = x - hi
    mid = r.astype(BF16).astype(F32)
    return hi, mid, r - mid


def _aug_operand(x_pair, d, hh, is_query):
    lane = lax.broadcasted_iota(jnp.int32, (1, LANES), 1)
    a = lane - (1 - hh) * FX_DIM
    hi, mid, lo = _split3(d if is_query else -d)
    p0, o0 = (0, AUG) if is_query else (AUG, 0)
    extra = jnp.where(a == p0, hi, jnp.where(a == p0 + 1, mid, jnp.where(a == p0 + 2, lo,
                      jnp.where((a >= o0) & (a < o0 + AUG), 1.0, 0.0))))
    return jnp.where(_head_mask(hh), x_pair, extra).astype(BF16)


def _proj_kernel(x_ref, w_ref, wfl_ref, wflt_ref, bfl_ref, bflt_ref, lbs_ref, *refs,
                 layer, seg, tiles_per_seq, prompt):
    if prompt:
        act_ref, ka_ref, kt_ref, vt_ref, lft_ref, dn_ref, cn_ref, kv_ref = refs
    else:
        act_ref, k_ref, v_ref, lft_ref, dn_ref, dt_ref = refs
    tm = x_ref.shape[0]
    xb = x_ref[...].astype(BF16)

    def proj(c0, width):
        return _mm(xb, w_ref[:, c0:c0 + width])

    lbs = lbs_ref[...]
    e = jnp.exp(lbs - jnp.max(lbs, axis=0, keepdims=True))
    sm = e / jnp.sum(e, axis=0, keepdims=True)
    lb = jnp.sum(sm[:layer + 1], axis=0, keepdims=True) - sm[0:1]

    act_ref[:, C_HQ:C_HQ + HG_WIDTH] = _silu(proj(W_HQ, HG_WIDTH))
    forget = lb + (1.0 - lb) * jax.nn.sigmoid(proj(W_HF, HG_WIDTH))
    act_ref[:, C_HK:C_HK + HG_WIDTH] = 1.0 - forget
    act_ref[:, C_HG:C_HG + HG_WIDTH] = jnp.log2(forget)
    act_ref[:, C_HV:C_HV + HG_WIDTH] = proj(W_HI, HG_WIDTH)
    act_ref[:, C_HGATE:C_HGATE + HG_WIDTH] = _silu(proj(W_HGATE, HG_WIDTH))
    act_ref[:, C_FQ:C_FQ + FX_WIDTH] = proj(W_FQ, FX_WIDTH)
    act_ref[:, C_FGATE:C_FGATE + FX_WIDTH] = _silu(proj(W_FGATE, FX_WIDTH))
    act_ref[:, C_MQ:C_MQ + MEM_WIDTH] = proj(W_MQ, MEM_WIDTH)
    act_ref[:, C_MGATE:C_MGATE + MEM_WIDTH] = _silu(proj(W_MGATE, MEM_WIDTH))

    lf_nat = _log_sigmoid(_mm(xb, wfl_ref[...]) + bfl_ref[...])
    lf_t = _log_sigmoid(_nt(wflt_ref[...], xb) + bflt_ref[...])

    r = lax.broadcasted_iota(jnp.int32, (tm, tm), 0)
    c = lax.broadcasted_iota(jnp.int32, (tm, tm), 1)
    same = (r // seg) == (c // seg)
    lower = jnp.where((c <= r) & same, 1.0, 0.0).astype(F32)
    d_nat = _mm(lower, lf_nat, precision=lax.Precision.HIGHEST)

    k = proj(W_FK, FX_WIDTH)
    v = proj(W_FV, FX_WIDTH)
    if prompt:
        if tiles_per_seq > 1:
            @pl.when(pl.program_id(0) % tiles_per_seq == 0)
            def _():
                cn_ref[...] = jnp.zeros_like(cn_ref)
            d_nat = d_nat + cn_ref[0:1, :]
            cn_ref[...] = jnp.broadcast_to(d_nat[tm - 1:tm, :], cn_ref.shape)
        lft_ref[0] = lf_t[:FX_HEADS]
        kv_ref[:, 0:FX_WIDTH] = k
        kv_ref[:, FX_WIDTH:] = v
        kt_ref[0] = kv_ref[:, 0:FX_WIDTH].T
        vt_ref[0] = kv_ref[:, FX_WIDTH:].T
        for h in range(FX_HEADS):
            ka_ref[:, h * LANES:(h + 1) * LANES] = _aug_operand(
                k[:, _pair_lanes(h)], d_nat[:, h:h + 1], h % 2, False)
    else:
        upper = jnp.where((r <= c) & same, 1.0, 0.0).astype(F32)
        dt_ref[...] = _mm(lf_t, upper, precision=lax.Precision.HIGHEST)
        lft_ref[...] = lf_t
        k_ref[...] = k
        v_ref[...] = v
    dn_ref[...] = d_nat


def _proj(x, w_main, w_fl, w_flt, b_fl, b_flt, lbs, *, layer, batch, seq_len, tm, prompt):
    m, d = x.shape
    ntiles = m // tm
    seg = min(seq_len, tm)
    tps = max(seq_len // tm, 1)
    assert prompt or ntiles == 1
    kern = functools.partial(_proj_kernel, layer=layer, seg=seg, tiles_per_seq=tps, prompt=prompt)
    full = lambda shape: pl.BlockSpec(shape, lambda i: (0,) * len(shape))
    rows = lambda width: pl.BlockSpec((tm, width), lambda i: (i, 0))
    feat = lambda nfeat: pl.BlockSpec((1, nfeat, tm), lambda i: (i // tps, 0, i % tps))
    if prompt:
        out_specs = [rows(ACT_WIDTH), rows(FX_HEADS * LANES), feat(FX_WIDTH), feat(FX_WIDTH),
                     feat(FX_HEADS), rows(LANES)]
        out_shape = [jax.ShapeDtypeStruct((m, ACT_WIDTH), F32),
                     jax.ShapeDtypeStruct((m, FX_HEADS * LANES), BF16),
                     jax.ShapeDtypeStruct((batch, FX_WIDTH, seq_len), F32),
                     jax.ShapeDtypeStruct((batch, FX_WIDTH, seq_len), F32),
                     jax.ShapeDtypeStruct((batch, FX_HEADS, seq_len), F32),
                     jax.ShapeDtypeStruct((m, LANES), F32)]
        scratch = [pltpu.VMEM((SUBLANES, LANES), F32), pltpu.VMEM((tm, 2 * FX_WIDTH), F32)]
    else:
        out_specs = [rows(ACT_WIDTH), rows(FX_WIDTH), rows(FX_WIDTH), full((SUBLANES, m)),
                     rows(LANES), full((SUBLANES, m))]
        out_shape = [jax.ShapeDtypeStruct((m, ACT_WIDTH), F32),
                     jax.ShapeDtypeStruct((m, FX_WIDTH), F32),
                     jax.ShapeDtypeStruct((m, FX_WIDTH), F32),
                     jax.ShapeDtypeStruct((SUBLANES, m), F32),
                     jax.ShapeDtypeStruct((m, LANES), F32),
                     jax.ShapeDtypeStruct((SUBLANES, m), F32)]
        scratch = []
    return pl.pallas_call(
        kern,
        grid=(ntiles,),
        in_specs=[
            pl.BlockSpec((tm, d), lambda i: (i, 0)),
            full(w_main.shape), full(w_fl.shape), full(w_flt.shape),
            full(b_fl.shape), full(b_flt.shape), full(lbs.shape),
        ],
        out_specs=out_specs,
        out_shape=out_shape,
        scratch_shapes=scratch,
        compiler_params=pltpu.CompilerParams(
            dimension_semantics=("arbitrary",), vmem_limit_bytes=VMEM_LIMIT),
        name="proj_prompt" if prompt else "proj_sample",
    )(x, w_main, w_fl, w_flt, b_fl, b_flt, lbs)


def _half_ref(gc, m, rows):
    n, w = gc.shape
    if 2 * m >= SUBLANES:
        g3 = gc.reshape(n // (2 * m), 2 * m, w)
        return jnp.broadcast_to(g3[:, m - 1:m, :], g3.shape).reshape(n, w)
    pos = rows % (2 * m)
    if m == 2:
        return jnp.where(pos == 0, pltpu.roll(gc, n - 1, axis=0),
                         jnp.where(pos == 1, gc,
                                   jnp.where(pos == 2, pltpu.roll(gc, 1, axis=0),
                                             pltpu.roll(gc, 2, axis=0))))
    return jnp.where(pos == 0, gc, pltpu.roll(gc, 1, axis=0))


def _level_exponent(g2, m, rows):
    n = g2.shape[0]
    if m >= SUBLANES:
        pieces = []
        for r0 in range(0, n, 2 * m):
            first, second = g2[r0:r0 + m], g2[r0 + m:r0 + 2 * m]
            ref = first[m - 1:m]
            pieces += [ref - first, second - ref]
        return jnp.concatenate(pieces, axis=0)
    ref = _half_ref(g2, m, rows)
    return jnp.where((rows % (2 * m)) < m, ref - g2, g2 - ref)


def _pair_levels(n):
    rr = lax.broadcasted_iota(jnp.int32, (n, n), 0)
    cc = lax.broadcasted_iota(jnp.int32, (n, n), 1)
    xf = (rr ^ cc).astype(F32)
    level = (lax.bitcast_convert_type(xf, jnp.int32) >> 23) - 127
    return jnp.where(rr > cc, level, -1)


def _cumsum_rows(gs, lower):
    w = gs[0].shape[1]
    pieces = [p.astype(BF16) for g in gs for p in _split3(g)]
    s = _mm(lower, jnp.concatenate(pieces, axis=1))
    return [s[:, (3 * i) * w:(3 * i + 1) * w] + s[:, (3 * i + 1) * w:(3 * i + 2) * w]
            + s[:, (3 * i + 2) * w:(3 * i + 3) * w] for i in range(len(gs))]


def _hgrn_intra(q, k, g2, levels):
    n = q.shape[0]
    rows = lax.broadcasted_iota(jnp.int32, (n, HG_DIM), 0)
    a = jnp.zeros((n, n), F32)
    m, bit = n // 2, n.bit_length() - 2
    while m >= 1:
        f = jnp.exp2(_level_exponent(g2, m, rows))
        a_m = _nt((q * f).astype(BF16), (k * f).astype(BF16))
        a = jnp.where(levels == bit, a_m, a)
        m, bit = m // 2, bit - 1
    return a


def _hgrn_kernel(*refs, group, has_init):
    if has_init:
        q_ref, k_ref, v_ref, g_ref, gate_ref, ng_ref, s0_ref, o_ref, so_ref = refs[:9]
    else:
        q_ref, k_ref, v_ref, g_ref, gate_ref, ng_ref, o_ref, so_ref = refs[:8]
    st_ref, a_ref, qe_ref, ke_ref, vb_ref, dg_ref, dec_ref = refs[-7:]
    t = pl.program_id(2)
    nheads = st_ref.shape[0]
    ngroups = q_ref.shape[0] // group
    lanes = [slice(hh * HG_DIM, (hh + 1) * HG_DIM) for hh in range(nheads)]

    @pl.when(t == 0)
    def _():
        for hh in range(nheads):
            st_ref[hh] = s0_ref[0, 0, hh].T if has_init else jnp.zeros((HG_DIM, HG_DIM), F32)

    ng = ng_ref[...]
    levels = _pair_levels(group)
    rr = lax.broadcasted_iota(jnp.int32, (group, group), 0)
    cc = lax.broadcasted_iota(jnp.int32, (group, group), 1)
    lower = jnp.where(rr >= cc, 1.0, 0.0).astype(BF16)

    def prepare(i):
        sl = pl.ds(pl.multiple_of(i * group, group), group)
        g2s = _cumsum_rows([g_ref[sl, ls] for ls in lanes], lower)
        for hh, ls in enumerate(lanes):
            q, k, v, g2 = q_ref[sl, ls], k_ref[sl, ls], v_ref[sl, ls], g2s[hh]
            g_last = g2[group - 1:group, :]
            a_ref[hh] = _hgrn_intra(q, k, g2, levels).astype(BF16)
            qe_ref[hh] = (q * jnp.exp2(g2)).astype(BF16)
            ke_ref[hh] = (k * jnp.exp2(g_last - g2)).astype(BF16)
            vb_ref[hh] = v.astype(BF16)
            dg_ref[hh] = jnp.sum(q * k, axis=-1, keepdims=True) * v
            dec_ref[hh] = jnp.broadcast_to(jnp.exp2(g_last), (SUBLANES, HG_DIM))

    def finish(i):
        sl = pl.ds(pl.multiple_of(i * group, group), group)
        for hh, ls in enumerate(lanes):
            st = st_ref[hh]
            vb = vb_ref[hh]
            o = _nt(qe_ref[hh], st.astype(BF16)) + _mm(a_ref[hh], vb) + dg_ref[hh]
            upd = lax.dot_general(vb, ke_ref[hh], (((0,), (0,)), ((), ())), preferred_element_type=F32)
            st_ref[hh] = st * dec_ref[hh, 0:1, :] + upd
            o = o * lax.rsqrt(jnp.mean(o * o, axis=-1, keepdims=True) + RMS_EPS) * ng
            o_ref[sl, ls] = (o * gate_ref[sl, ls]).astype(o_ref.dtype)

    prepare(0)

    def step(i, c):
        finish(i)
        prepare(i + 1)
        return c

    lax.fori_loop(0, ngroups - 1, step, 0)
    finish(ngroups - 1)

    @pl.when(t == pl.num_programs(2) - 1)
    def _():
        for hh in range(nheads):
            so_ref[0, hh] = st_ref[hh].T


def _hgrn(act, norm_g, s0, *, layer, batch, seq_len, tt, nheads):
    m = act.shape[0]
    nt = seq_len // tt
    has_init = s0 is not None
    group = min(tt, HG_GROUP)
    width = nheads * HG_DIM
    blk = lambda cb: pl.BlockSpec((tt, width), lambda b, h, t, cb=cb: (b * nt + t, cb // width + h))
    in_specs = [blk(C_HQ), blk(C_HK), blk(C_HV), blk(C_HG), blk(C_HGATE),
                pl.BlockSpec((1, HG_DIM), lambda b, h, t: (0, 0))]
    args = [act, act, act, act, act, norm_g]
    if has_init:
        in_specs.append(pl.BlockSpec((1, 1, nheads, HG_DIM, HG_DIM), lambda b, h, t: (layer, b, h, 0, 0)))
        args.append(s0)
    return pl.pallas_call(
        functools.partial(_hgrn_kernel, group=group, has_init=has_init),
        grid=(batch, HG_HEADS // nheads, nt),
        in_specs=in_specs,
        out_specs=[
            pl.BlockSpec((tt, width), lambda b, h, t: (b * nt + t, h)),
            pl.BlockSpec((1, nheads, HG_DIM, HG_DIM), lambda b, h, t: (b, h, 0, 0)),
        ],
        out_shape=[
            jax.ShapeDtypeStruct((m, HG_WIDTH), BF16),
            jax.ShapeDtypeStruct((batch, HG_HEADS, HG_DIM, HG_DIM), F32),
        ],
        scratch_shapes=[
            pltpu.VMEM((nheads, HG_DIM, HG_DIM), F32),
            pltpu.VMEM((nheads, group, group), BF16),
            pltpu.VMEM((nheads, group, HG_DIM), BF16),
            pltpu.VMEM((nheads, group, HG_DIM), BF16),
            pltpu.VMEM((nheads, group, HG_DIM), BF16),
            pltpu.VMEM((nheads, group, HG_DIM), F32),
            pltpu.VMEM((nheads, SUBLANES, HG_DIM), F32),
        ],
        compiler_params=pltpu.CompilerParams(
            dimension_semantics=("arbitrary", "arbitrary", "arbitrary"), vmem_limit_bytes=VMEM_LIMIT),
        name="hgrn",
    )(*args)


def _softmax_t(st, m, l):
    m_new = jnp.maximum(m, jnp.max(st, axis=0, keepdims=True))
    alpha = jnp.exp(m - m_new)
    p = jnp.exp(st - m_new)
    return m_new, alpha * l + jnp.sum(p, axis=0, keepdims=True), alpha, p.astype(BF16)


def _attn_prompt_kernel(fq_ref, fgate_ref, mq_ref, mgate_ref, ka_ref, vt_ref, dn_ref, mk_ref, mvt_ref,
                        o_ref, vtb_ref, mvtb_ref, qa_ref, st_ref, p_ref, acc_ref, *, tq):
    seq = fq_ref.shape[0]
    tk = tq
    heads = range(FX_HEADS)

    for j in range(seq // tk):
        for p in range(FX_HEADS // 2):
            vtb_ref[j, p] = vt_ref[0, p * LANES:(p + 1) * LANES, j * tk:(j + 1) * tk].astype(BF16)
    mvtb_ref[...] = mvt_ref[0].astype(BF16)

    key = lax.broadcasted_iota(jnp.int32, (tk, tq), 0)
    qry = lax.broadcasted_iota(jnp.int32, (tk, tq), 1)

    def scores(j, h):
        kr = pl.ds(pl.multiple_of(j * tk, tk), tk)
        return _nt(ka_ref[kr, h * LANES:(h + 1) * LANES], qa_ref[h])

    def values(j, h):
        return vtb_ref[j, h // 2, (h % 2) * FX_DIM:(h % 2 + 1) * FX_DIM, :]

    def qtile(i, c):
        rows = pl.ds(pl.multiple_of(i * tq, tq), tq)
        dnq = dn_ref[rows, :]
        for h in heads:
            qa_ref[h] = _aug_operand(fq_ref[rows, _pair_lanes(h)] * (FX_DIM ** -0.5),
                                     dnq[:, h:h + 1], h % 2, True)
        for h in heads:
            st_ref[h] = scores(0, h)
            p_ref[h] = jnp.zeros((tk, tq), BF16)
            acc_ref[h] = jnp.zeros((FX_DIM, tq), F32)

        def body(j, carry):
            prev = jnp.maximum(j - 1, 0)
            out = []
            for h in heads:
                m, l, alpha_prev = carry[h]
                pv = _mm(values(prev, h), p_ref[h])
                st = st_ref[h]
                st_ref[h] = scores(j + 1, h)
                m, l, alpha, p = _softmax_t(st, m, l)
                acc_ref[h] = alpha_prev * acc_ref[h] + pv
                p_ref[h] = p
                out.append((m, l, alpha))
            return tuple(out)

        init = tuple((jnp.full((1, tq), -jnp.inf, F32), jnp.zeros((1, tq), F32), jnp.ones((1, tq), F32))
                     for _ in heads)
        carry = lax.fori_loop(0, i, body, init)

        prev = jnp.maximum(i - 1, 0)
        o_heads = []
        for h in heads:
            m, l, alpha_prev = carry[h]
            acc = alpha_prev * acc_ref[h] + _mm(values(prev, h), p_ref[h])
            m, l, alpha, p = _softmax_t(jnp.where(key <= qry, st_ref[h], -jnp.inf), m, l)
            acc = alpha * acc + _mm(values(i, h), p)
            o_heads.append(acc / l)
        for pr in range(FX_HEADS // 2):
            ls = _pair_lanes(2 * pr)
            o_t = jnp.concatenate(o_heads[2 * pr:2 * pr + 2], axis=0)
            o_ref[rows, ls] = (o_t.T * fgate_ref[rows, ls]).astype(o_ref.dtype)

        for pr in range(MEM_HEADS // 2):
            ls = _pair_lanes(2 * pr)
            mq_pair = mq_ref[rows, ls] * (MEM_DIM ** -0.5)
            halves = []
            for hh in range(2):
                qm = jnp.where(_head_mask(hh), mq_pair, 0.0).astype(BF16)
                st = _nt(mk_ref[:, ls], qm)
                pm = jnp.exp(st - jnp.max(st, axis=0, keepdims=True))
                halves.append(_mm(mvtb_ref[pr * LANES + hh * MEM_DIM:pr * LANES + (hh + 1) * MEM_DIM, :],
                                  pm.astype(BF16)) / jnp.sum(pm, axis=0, keepdims=True))
            o_t = jnp.concatenate(halves, axis=0)
            lm = slice(FX_WIDTH + pr * LANES, FX_WIDTH + (pr + 1) * LANES)
            o_ref[rows, lm] = (o_t.T * mgate_ref[rows, ls]).astype(o_ref.dtype)
        return c

    lax.fori_loop(0, seq // tq, qtile, 0)


def _attn_prompt(act, ka, vt, d_nat, mk, mvt, *, batch, seq_len, tq):
    m = act.shape[0]
    n_mem = mk.shape[0] // batch
    tk = tq
    ablk = lambda c0: pl.BlockSpec((seq_len, FX_WIDTH), lambda b, c0=c0: (b, c0 // FX_WIDTH))
    return pl.pallas_call(
        functools.partial(_attn_prompt_kernel, tq=tq),
        grid=(batch,),
        in_specs=[
            ablk(C_FQ), ablk(C_FGATE), ablk(C_MQ), ablk(C_MGATE),
            pl.BlockSpec((seq_len, FX_HEADS * LANES), lambda b: (b, 0)),
            pl.BlockSpec((1, FX_WIDTH, seq_len), lambda b: (b, 0, 0)),
            pl.BlockSpec((seq_len, LANES), lambda b: (b, 0)),
            pl.BlockSpec((n_mem, MEM_WIDTH), lambda b: (b, 0)),
            pl.BlockSpec((1, MEM_WIDTH, n_mem), lambda b: (b, 0, 0)),
        ],
        out_specs=pl.BlockSpec((seq_len, FX_WIDTH + MEM_WIDTH), lambda b: (b, 0)),
        out_shape=jax.ShapeDtypeStruct((m, FX_WIDTH + MEM_WIDTH), BF16),
        scratch_shapes=[
            pltpu.VMEM((seq_len // tk, FX_HEADS // 2, LANES, tk), BF16),
            pltpu.VMEM((MEM_WIDTH, n_mem), BF16),
            pltpu.VMEM((FX_HEADS, tq, LANES), BF16),
            pltpu.VMEM((FX_HEADS, tk, tq), F32),
            pltpu.VMEM((FX_HEADS, tk, tq), BF16),
            pltpu.VMEM((FX_HEADS, FX_DIM, tq), F32),
        ],
        compiler_params=pltpu.CompilerParams(
            dimension_semantics=("arbitrary",), vmem_limit_bytes=VMEM_LIMIT),
        name="attn_prompt",
    )(act, act, act, act, ka, vt, d_nat, mk, mvt)


def _stack_heads(x, scale):
    lane = lax.broadcasted_iota(jnp.int32, (1, x.shape[1]), 1)
    return jnp.concatenate([jnp.where(lane // FX_DIM == h, x * scale, 0.0) for h in range(FX_HEADS)],
                           axis=0).astype(BF16)


def _unstack_heads(o, ts):
    lane = lax.broadcasted_iota(jnp.int32, (1, o.shape[1]), 1)
    out = jnp.zeros((ts, o.shape[1]), F32)
    for h in range(FX_HEADS):
        out = jnp.where(lane // FX_DIM == h, o[h * ts:(h + 1) * ts, :], out)
    return out


def _per_head_rows(vals, ts):
    return jnp.concatenate([jnp.broadcast_to(v, (ts, v.shape[1])) for v in vals], axis=0)


def _softmax_update(s, pv_fn, m, l, acc):
    m_new = jnp.maximum(m, jnp.max(s, axis=-1, keepdims=True))
    alpha = jnp.exp(m - m_new)
    p = jnp.exp(s - m_new)
    l = alpha * l + jnp.sum(p, axis=-1, keepdims=True)
    acc = alpha * acc + pv_fn(p.astype(BF16))
    return m_new, l, acc


def _attn_sample_kernel(fq_ref, fgate_ref, mq_ref, mgate_ref, kn_ref, vn_ref, dn_ref, dt_ref,
                        pkt_ref, pvt_ref, plf_ref, mkt_ref, mvt_ref, o_ref, *, chunk):
    ts = fq_ref.shape[0]
    n_new = kn_ref.shape[0]
    past = pkt_ref.shape[3]
    nchunks = past // chunk
    b = pl.program_id(0)
    hrows = FX_HEADS * ts

    r = lax.broadcasted_iota(jnp.int32, (chunk, chunk), 0)
    c = lax.broadcasted_iota(jnp.int32, (chunk, chunk), 1)
    upper = jnp.where(r <= c, 1.0, 0.0).astype(F32)
    total = jnp.zeros((FX_HEADS, 1), F32)
    d_past = []
    for ci in range(nchunks):
        d = _mm(plf_ref[0, 0, :, ci * chunk:(ci + 1) * chunk], upper,
                precision=lax.Precision.HIGHEST) + total
        d_past.append(d)
        total = d[:, chunk - 1:chunk]

    dn = dn_ref[...]
    dt_new = dt_ref[...]
    tot = [total[h:h + 1, :] for h in range(FX_HEADS)]
    dq = jnp.concatenate([tot[h] + dn[:, h:h + 1] for h in range(FX_HEADS)], axis=0)
    q4 = _stack_heads(fq_ref[...], FX_DIM ** -0.5)

    carry = (jnp.full((hrows, 1), -jnp.inf, F32), jnp.zeros((hrows, 1), F32),
             jnp.zeros((hrows, FX_WIDTH), F32))
    for ci in range(nchunks):
        cs = slice(ci * chunk, (ci + 1) * chunk)
        dk = _per_head_rows([d_past[ci][h:h + 1, :] for h in range(FX_HEADS)], ts)
        s = _mm(q4, pkt_ref[0, 0, :, cs].astype(BF16)) + dq - dk
        vt = pvt_ref[0, 0, :, cs].astype(BF16)
        carry = _softmax_update(s, lambda p, vt=vt: _nt(p, vt), *carry)

    row = lax.broadcasted_iota(jnp.int32, (hrows, n_new), 0)
    col = lax.broadcasted_iota(jnp.int32, (hrows, n_new), 1)
    visible = (col // ts == b) & (col % ts <= row % ts)
    dk = _per_head_rows([tot[h] + dt_new[h:h + 1, :] for h in range(FX_HEADS)], ts)
    s = _nt(q4, kn_ref[...].astype(BF16)) + dq - dk
    vn = vn_ref[...].astype(BF16)
    _, l, acc = _softmax_update(jnp.where(visible, s, -jnp.inf), lambda p: _mm(p, vn), *carry)
    o_fx = _unstack_heads(acc / l, ts) * fgate_ref[...]
    o_ref[:, 0:FX_WIDTH] = o_fx.astype(o_ref.dtype)

    mq4 = _stack_heads(mq_ref[...], MEM_DIM ** -0.5)
    s = _mm(mq4, mkt_ref[0, 0].astype(BF16))
    p = jnp.exp(s - jnp.max(s, axis=-1, keepdims=True))
    o = _nt(p.astype(BF16), mvt_ref[0, 0].astype(BF16)) / jnp.sum(p, axis=-1, keepdims=True)
    o_mem = _unstack_heads(o, ts) * mgate_ref[...]
    o_ref[:, FX_WIDTH:] = o_mem.astype(o_ref.dtype)


def _attn_sample(act, k_new, v_new, d_nat, d_t, past_kt, past_vt, past_lft, mem_kt, mem_vt,
                 *, layer, batch, ts):
    m = act.shape[0]
    past = past_kt.shape[3]
    n_mem = mem_kt.shape[3]
    chunk = min(past, 512)
    ablk = lambda c0: pl.BlockSpec((ts, FX_WIDTH), lambda b, c0=c0: (b, c0 // FX_WIDTH))
    cache = lambda nfeat, n: pl.BlockSpec((1, 1, nfeat, n), lambda b: (layer, b, 0, 0))
    return pl.pallas_call(
        functools.partial(_attn_sample_kernel, chunk=chunk),
        grid=(batch,),
        in_specs=[
            ablk(C_FQ), ablk(C_FGATE), ablk(C_MQ), ablk(C_MGATE),
            pl.BlockSpec((m, FX_WIDTH), lambda b: (0, 0)),
            pl.BlockSpec((m, FX_WIDTH), lambda b: (0, 0)),
            pl.BlockSpec((ts, LANES), lambda b: (b, 0)),
            pl.BlockSpec((SUBLANES, m), lambda b: (0, 0)),
            cache(FX_WIDTH, past), cache(FX_WIDTH, past), cache(FX_HEADS, past),
            cache(MEM_WIDTH, n_mem), cache(MEM_WIDTH, n_mem),
        ],
        out_specs=pl.BlockSpec((ts, FX_WIDTH + MEM_WIDTH), lambda b: (b, 0)),
        out_shape=jax.ShapeDtypeStruct((m, FX_WIDTH + MEM_WIDTH), BF16),
        compiler_params=pltpu.CompilerParams(
            dimension_semantics=("arbitrary",), vmem_limit_bytes=VMEM_LIMIT),
        name="attn_sample",
    )(act, act, act, act, k_new, v_new, d_nat, d_t, past_kt, past_vt, past_lft, mem_kt, mem_vt)


def _memkv_kernel(x_ref, w_ref, kb_ref, kt_ref, vt_ref):
    kv = _mm(x_ref[...].astype(BF16), w_ref[...])
    kb_ref[...] = kv[:, :MEM_WIDTH].astype(BF16)
    kt_ref[0] = kv[:, :MEM_WIDTH].T
    vt_ref[0] = kv[:, MEM_WIDTH:].T


def _memkv(mem, w, *, batch):
    m, d = mem.shape
    n_mem = m // batch
    return pl.pallas_call(
        _memkv_kernel,
        grid=(batch,),
        in_specs=[pl.BlockSpec((n_mem, d), lambda i: (i, 0)),
                  pl.BlockSpec(w.shape, lambda i: (0, 0))],
        out_specs=[pl.BlockSpec((n_mem, MEM_WIDTH), lambda i: (i, 0)),
                   pl.BlockSpec((1, MEM_WIDTH, n_mem), lambda i: (i, 0, 0)),
                   pl.BlockSpec((1, MEM_WIDTH, n_mem), lambda i: (i, 0, 0))],
        out_shape=[jax.ShapeDtypeStruct((m, MEM_WIDTH), BF16),
                   jax.ShapeDtypeStruct((batch, MEM_WIDTH, n_mem), F32),
                   jax.ShapeDtypeStruct((batch, MEM_WIDTH, n_mem), F32)],
        compiler_params=pltpu.CompilerParams(
            dimension_semantics=("arbitrary",), vmem_limit_bytes=VMEM_LIMIT),
        name="memkv",
    )(mem, w)


def _out_kernel(x_ref, hg_ref, at_ref, w_ref, g_ref, b_ref, y_ref, *, alpha):
    acc = _mm(hg_ref[...], w_ref[0:HG_WIDTH, :]) + _mm(at_ref[...], w_ref[HG_WIDTH:, :])
    z = alpha * x_ref[...] + acc
    mu = jnp.mean(z, axis=-1, keepdims=True)
    zc = z - mu
    var = jnp.mean(zc * zc, axis=-1, keepdims=True)
    y_ref[...] = zc * lax.rsqrt(var + LN_EPS) * g_ref[...] + b_ref[...]


def _out(x, mix_hg, mix_at, w, g, b, *, alpha, tm):
    m, d = x.shape
    return pl.pallas_call(
        functools.partial(_out_kernel, alpha=alpha),
        grid=(m // tm,),
        in_specs=[
            pl.BlockSpec((tm, d), lambda i: (i, 0)),
            pl.BlockSpec((tm, HG_WIDTH), lambda i: (i, 0)),
            pl.BlockSpec((tm, FX_WIDTH + MEM_WIDTH), lambda i: (i, 0)),
            pl.BlockSpec(w.shape, lambda i: (0, 0)),
            pl.BlockSpec((1, d), lambda i: (0, 0)),
            pl.BlockSpec((1, d), lambda i: (0, 0)),
        ],
        out_specs=pl.BlockSpec((tm, d), lambda i: (i, 0)),
        out_shape=jax.ShapeDtypeStruct((m, d), F32),
        compiler_params=pltpu.CompilerParams(
            dimension_semantics=("arbitrary",), vmem_limit_bytes=VMEM_LIMIT),
        name="out",
    )(x, mix_hg, mix_at, w, g, b)


def _pick_tile(n, target):
    t = min(n, target)
    while n % t:
        t //= 2
    return t


def _feature_major(x):
    lead = x.shape[:-3]
    n, heads, dim = x.shape[-3:]
    nl = len(lead)
    perm = tuple(range(nl)) + (nl + 1, nl + 2, nl)
    return jnp.transpose(x, perm).reshape(lead + (heads * dim, n))


def _token_major(x, heads):
    lead = x.shape[:-2]
    width, n = x.shape[-2:]
    nl = len(lead)
    x = x.reshape(lead + (heads, width // heads, n))
    return jnp.transpose(x, tuple(range(nl)) + (nl + 2, nl, nl + 1))


def kernel(x_prompt, x_sample, mem_prompt, state_hgrn, cache_fox_k, cache_fox_v, cache_fox_logf,
           cache_mem_k, cache_mem_v, w_in, b_fox_forget, hgrn_lower_bounds, hgrn_norm_g,
           w_mem_kv, w_out, ln_g, ln_b):
    depth, d_model, _ = w_in.shape
    bp, seq, _ = x_prompt.shape
    bs, dseq, _ = x_sample.shape
    n_mem = mem_prompt.shape[1]
    alpha = (2 * depth) ** 0.25

    fl0 = 4 * HG_WIDTH + 4 * FX_WIDTH
    w_main = jnp.concatenate([w_in[:, :, :fl0], w_in[:, :, fl0 + FX_HEADS:]], axis=-1).astype(BF16)
    w_flc = w_in[:, :, fl0:fl0 + FX_HEADS]
    w_fl = jnp.pad(w_flc, ((0, 0), (0, 0), (0, LANES - FX_HEADS))).astype(BF16)
    w_flt = jnp.pad(jnp.swapaxes(w_flc, 1, 2), ((0, 0), (0, SUBLANES - FX_HEADS), (0, 0))).astype(BF16)
    b_fl = jnp.pad(b_fox_forget, ((0, 0), (0, LANES - FX_HEADS)))[:, None, :]
    b_flt = jnp.pad(b_fox_forget, ((0, 0), (0, SUBLANES - FX_HEADS)))[:, :, None]
    lbs = hgrn_lower_bounds.astype(F32)
    w_memb = w_mem_kv.astype(BF16)
    w_outb = w_out.astype(BF16)

    past_kt = _feature_major(cache_fox_k)
    past_vt = _feature_major(cache_fox_v)
    past_lft = jnp.swapaxes(cache_fox_logf, 2, 3)
    mem_kt = _feature_major(cache_mem_k)
    mem_vt = _feature_major(cache_mem_v)

    mp, ms = bp * seq, bs * dseq
    tb = _pick_tile(seq, 256)
    yp = x_prompt.reshape(mp, d_model)
    ys = x_sample.reshape(ms, d_model)
    mem2 = mem_prompt.reshape(bp * n_mem, d_model)

    p_hg, p_kt, p_vt, p_lft, p_mkt, p_mvt = [], [], [], [], [], []
    s_hg, s_k, s_v, s_lft = [], [], [], []
    for l in range(depth):
        wl = (w_main[l], w_fl[l], w_flt[l], b_fl[l], b_flt[l], lbs)
        ng = hgrn_norm_g[l][None, :]
        g_ln, b_ln = ln_g[l][None, :], ln_b[l][None, :]

        mkb, mkt, mvt = _memkv(mem2, w_memb[l], batch=bp)
        act, ka, kt, vt, lft, d_nat = _proj(yp, *wl, layer=l, batch=bp, seq_len=seq, tm=tb, prompt=True)
        mix_hg, hg = _hgrn(act, ng, None, layer=l, batch=bp, seq_len=seq, tt=_pick_tile(seq, 2048),
                           nheads=2)
        mix_at = _attn_prompt(act, ka, vt, d_nat, mkb, mvt, batch=bp, seq_len=seq, tq=tb)
        yp = _out(yp, mix_hg, mix_at, w_outb[l], g_ln, b_ln, alpha=alpha, tm=tb)
        p_hg.append(hg); p_kt.append(kt); p_vt.append(vt); p_lft.append(lft)
        p_mkt.append(mkt); p_mvt.append(mvt)

        act, k, v, lft, d_nat, d_t = _proj(ys, *wl, layer=l, batch=bs, seq_len=dseq, tm=ms, prompt=False)
        mix_hg, hg = _hgrn(act, ng, state_hgrn, layer=l, batch=bs, seq_len=dseq, tt=dseq, nheads=2)
        mix_at = _attn_sample(act, k, v, d_nat, d_t, past_kt, past_vt, past_lft, mem_kt, mem_vt,
                              layer=l, batch=bs, ts=dseq)
        ys = _out(ys, mix_hg, mix_at, w_outb[l], g_ln, b_ln, alpha=alpha, tm=ms)
        s_hg.append(hg); s_k.append(k); s_v.append(v); s_lft.append(lft)

    st = jnp.stack
    s_lf = st(s_lft)[:, :FX_HEADS, :].reshape(depth, FX_HEADS, bs, dseq)
    return (yp.reshape(bp, seq, d_model), ys.reshape(bs, dseq, d_model),
            st(p_hg),
            _token_major(st(p_kt), FX_HEADS),
            _token_major(st(p_vt), FX_HEADS),
            jnp.swapaxes(st(p_lft), 2, 3),
            _token_major(st(p_mkt), MEM_HEADS),
            _token_major(st(p_mvt), MEM_HEADS),
            st(s_hg),
            st(s_k).reshape(depth, bs, dseq, FX_HEADS, FX_DIM),
            st(s_v).reshape(depth, bs, dseq, FX_HEADS, FX_DIM),
            jnp.transpose(s_lf, (0, 2, 3, 1)))
```

```python
import functools

import jax
import jax.numpy as jnp
from jax import lax
from jax.experimental import pallas as pl
from jax.experimental.pallas import tpu as pltpu

F32 = jnp.float32
BF16 = jnp.bfloat16

HG_HEADS = 4
HG_DIM = 128
HG_WIDTH = HG_HEADS * HG_DIM
FX_HEADS = 4
FX_DIM = 64
FX_WIDTH = FX_HEADS * FX_DIM
MEM_HEADS = 4
MEM_DIM = 64
MEM_WIDTH = MEM_HEADS * MEM_DIM
HG_GROUP = 128
LN_EPS = 1e-5
RMS_EPS = 1e-6
LANES = 128
SUBLANES = 8
VMEM_LIMIT = 56 * 1024 * 1024
AUG = 3
LOG2E = 1.4426950408889634

C_HQ, C_HK, C_HV, C_HG, C_HGATE = 0, 512, 1024, 1536, 2048
C_FQ, C_FGATE, C_MQ, C_MGATE = 2560, 2816, 3072, 3328
ACT_WIDTH = 3584
W_HQ, W_HF, W_HI, W_HGATE, W_FQ, W_FK, W_FV, W_FGATE, W_MQ, W_MGATE = (
    0, 512, 1024, 1536, 2048, 2304, 2560, 2816, 3072, 3328)


def _nt(a, b, precision=None):
    return lax.dot_general(a, b, (((1,), (1,)), ((), ())), precision=precision,
                           preferred_element_type=F32)


def _mm(a, b, precision=None):
    return jnp.dot(a, b, precision=precision, preferred_element_type=F32)


def _silu(x):
    return x * jax.nn.sigmoid(x)


def _log_sigmoid(x):
    return jnp.minimum(x, 0.0) - jnp.log(1.0 + jnp.exp(-jnp.abs(x)))


def _head_mask(hh):
    lane = lax.broadcasted_iota(jnp.int32, (1, LANES), 1)
    return (lane // FX_DIM) == hh


def _pair_lanes(h):
    return slice((h // 2) * LANES, (h // 2 + 1) * LANES)


def _split3(x):
    hi = x.astype(BF16).astype(F32)
    r = x - hi
    mid = r.astype(BF16).astype(F32)
    return hi, mid, r - mid


def _aug_operand(x_pair, d, hh, is_query):
    lane = lax.broadcasted_iota(jnp.int32, (1, LANES), 1)
    a = lane - (1 - hh) * FX_DIM
    hi, mid, lo = _split3(d if is_query else -d)
    p0, o0 = (0, AUG) if is_query else (AUG, 0)
    extra = jnp.where(a == p0, hi, jnp.where(a == p0 + 1, mid, jnp.where(a == p0 + 2, lo,
                      jnp.where((a >= o0) & (a < o0 + AUG), 1.0, 0.0))))
    return jnp.where(_head_mask(hh), x_pair, extra).astype(BF16)


def _proj_kernel(x_ref, w_ref, wfl_ref, wflt_ref, bfl_ref, bflt_ref, lbs_ref, *refs,
                 layer, seg, tiles_per_seq, prompt):
    if prompt:
        act_ref, ka_ref, kt_ref, vt_ref, lft_ref, dn_ref, cn_ref, kv_ref = refs
    else:
        act_ref, k_ref, v_ref, lft_ref, dn_ref, dt_ref = refs
    tm = x_ref.shape[0]
    xb = x_ref[...].astype(BF16)

    def proj(c0, width):
        return _mm(xb, w_ref[:, c0:c0 + width])

    lbs = lbs_ref[...]
    e = jnp.exp(lbs - jnp.max(lbs, axis=0, keepdims=True))
    sm = e / jnp.sum(e, axis=0, keepdims=True)
    lb = jnp.sum(sm[:layer + 1], axis=0, keepdims=True) - sm[0:1]

    lf_nat = _log_sigmoid(_mm(xb, wfl_ref[...]) + bfl_ref[...])
    lf_t = _log_sigmoid(_nt(wflt_ref[...], xb) + bflt_ref[...])

    r = lax.broadcasted_iota(jnp.int32, (tm, tm), 0)
    c = lax.broadcasted_iota(jnp.int32, (tm, tm), 1)
    same = (r // seg) == (c // seg)
    lower = jnp.where((c <= r) & same, 1.0, 0.0).astype(BF16)
    d_nat = _cumsum_rows([lf_nat], lower)[0]

    k = proj(W_FK, FX_WIDTH)
    v = proj(W_FV, FX_WIDTH)
    _proj_fox_outputs(k, v, lf_t, d_nat, r, c, same, refs, tiles_per_seq=tiles_per_seq, prompt=prompt)

    act_ref[:, C_HQ:C_HQ + HG_WIDTH] = _silu(proj(W_HQ, HG_WIDTH))
    forget = lb + (1.0 - lb) * jax.nn.sigmoid(proj(W_HF, HG_WIDTH))
    act_ref[:, C_HK:C_HK + HG_WIDTH] = 1.0 - forget
    act_ref[:, C_HG:C_HG + HG_WIDTH] = jnp.log2(forget)
    act_ref[:, C_HV:C_HV + HG_WIDTH] = proj(W_HI, HG_WIDTH)
    act_ref[:, C_HGATE:C_HGATE + HG_WIDTH] = _silu(proj(W_HGATE, HG_WIDTH))
    act_ref[:, C_FQ:C_FQ + FX_WIDTH] = proj(W_FQ, FX_WIDTH)
    act_ref[:, C_FGATE:C_FGATE + FX_WIDTH] = _silu(proj(W_FGATE, FX_WIDTH))
    act_ref[:, C_MQ:C_MQ + MEM_WIDTH] = proj(W_MQ, MEM_WIDTH)
    act_ref[:, C_MGATE:C_MGATE + MEM_WIDTH] = _silu(proj(W_MGATE, MEM_WIDTH))


def _proj_fox_outputs(k, v, lf_t, d_nat, r, c, same, refs, *, tiles_per_seq, prompt):
    tm = k.shape[0]
    if prompt:
        act_ref, ka_ref, kt_ref, vt_ref, lft_ref, dn_ref, cn_ref, kv_ref = refs
    else:
        act_ref, k_ref, v_ref, lft_ref, dn_ref, dt_ref = refs
    if prompt:
        if tiles_per_seq > 1:
            @pl.when(pl.program_id(0) % tiles_per_seq == 0)
            def _():
                cn_ref[...] = jnp.zeros_like(cn_ref)
            d_nat = d_nat + cn_ref[0:1, :]
            cn_ref[...] = jnp.broadcast_to(d_nat[tm - 1:tm, :], cn_ref.shape)
        lft_ref[0] = lf_t[:FX_HEADS]
        kv_ref[:, 0:FX_WIDTH] = k
        kv_ref[:, FX_WIDTH:] = v
        kt_ref[0] = kv_ref[:, 0:FX_WIDTH].T
        vt_ref[0] = kv_ref[:, FX_WIDTH:].T
        for h in range(FX_HEADS):
            ka_ref[:, h * LANES:(h + 1) * LANES] = _aug_operand(
                k[:, _pair_lanes(h)], d_nat[:, h:h + 1], h % 2, False)
    else:
        upper = jnp.where((r <= c) & same, 1.0, 0.0).astype(F32)
        dt_ref[...] = _mm(lf_t, upper, precision=lax.Precision.HIGHEST)
        lft_ref[...] = lf_t
        k_ref[...] = k
        v_ref[...] = v
    dn_ref[...] = d_nat


def _proj(x, w_main, w_fl, w_flt, b_fl, b_flt, lbs, *, layer, batch, seq_len, tm, prompt):
    m, d = x.shape
    ntiles = m // tm
    seg = min(seq_len, tm)
    tps = max(seq_len // tm, 1)
    assert prompt or ntiles == 1
    kern = functools.partial(_proj_kernel, layer=layer, seg=seg, tiles_per_seq=tps, prompt=prompt)
    full = lambda shape: pl.BlockSpec(shape, lambda i: (0,) * len(shape))
    rows = lambda width: pl.BlockSpec((tm, width), lambda i: (i, 0))
    feat = lambda nfeat: pl.BlockSpec((1, nfeat, tm), lambda i: (i // tps, 0, i % tps))
    if prompt:
        out_specs = [rows(ACT_WIDTH), rows(FX_HEADS * LANES), feat(FX_WIDTH), feat(FX_WIDTH),
                     feat(FX_HEADS), rows(LANES)]
        out_shape = [jax.ShapeDtypeStruct((m, ACT_WIDTH), F32),
                     jax.ShapeDtypeStruct((m, FX_HEADS * LANES), BF16),
                     jax.ShapeDtypeStruct((batch, FX_WIDTH, seq_len), F32),
                     jax.ShapeDtypeStruct((batch, FX_WIDTH, seq_len), F32),
                     jax.ShapeDtypeStruct((batch, FX_HEADS, seq_len), F32),
                     jax.ShapeDtypeStruct((m, LANES), F32)]
        scratch = [pltpu.VMEM((SUBLANES, LANES), F32), pltpu.VMEM((tm, 2 * FX_WIDTH), F32)]
    else:
        out_specs = [rows(ACT_WIDTH), rows(FX_WIDTH), rows(FX_WIDTH), full((SUBLANES, m)),
                     rows(LANES), full((SUBLANES, m))]
        out_shape = [jax.ShapeDtypeStruct((m, ACT_WIDTH), F32),
                     jax.ShapeDtypeStruct((m, FX_WIDTH), F32),
                     jax.ShapeDtypeStruct((m, FX_WIDTH), F32),
                     jax.ShapeDtypeStruct((SUBLANES, m), F32),
                     jax.ShapeDtypeStruct((m, LANES), F32),
                     jax.ShapeDtypeStruct((SUBLANES, m), F32)]
        scratch = []
    return pl.pallas_call(
        kern,
        grid=(ntiles,),
        in_specs=[
            pl.BlockSpec((tm, d), lambda i: (i, 0)),
            full(w_main.shape), full(w_fl.shape), full(w_flt.shape),
            full(b_fl.shape), full(b_flt.shape), full(lbs.shape),
        ],
        out_specs=out_specs,
        out_shape=out_shape,
        scratch_shapes=scratch,
        compiler_params=pltpu.CompilerParams(
            dimension_semantics=("arbitrary",), vmem_limit_bytes=VMEM_LIMIT),
        name="proj_prompt" if prompt else "proj_sample",
    )(x, w_main, w_fl, w_flt, b_fl, b_flt, lbs)


def _half_ref(gc, m, rows):
    n, w = gc.shape
    if 2 * m >= SUBLANES:
        g3 = gc.reshape(n // (2 * m), 2 * m, w)
        return jnp.broadcast_to(g3[:, m - 1:m, :], g3.shape).reshape(n, w)
    pos = rows % (2 * m)
    if m == 2:
        return jnp.where(pos == 0, pltpu.roll(gc, n - 1, axis=0),
                         jnp.where(pos == 1, gc,
                                   jnp.where(pos == 2, pltpu.roll(gc, 1, axis=0),
                                             pltpu.roll(gc, 2, axis=0))))
    return jnp.where(pos == 0, gc, pltpu.roll(gc, 1, axis=0))


def _level_exponent(g2, m, rows):
    n = g2.shape[0]
    if m >= SUBLANES:
        pieces = []
        for r0 in range(0, n, 2 * m):
            first, second = g2[r0:r0 + m], g2[r0 + m:r0 + 2 * m]
            ref = first[m - 1:m]
            pieces += [ref - first, second - ref]
        return jnp.concatenate(pieces, axis=0)
    ref = _half_ref(g2, m, rows)
    return jnp.where((rows % (2 * m)) < m, ref - g2, g2 - ref)


def _pair_levels(n):
    rr = lax.broadcasted_iota(jnp.int32, (n, n), 0)
    cc = lax.broadcasted_iota(jnp.int32, (n, n), 1)
    xf = (rr ^ cc).astype(F32)
    level = (lax.bitcast_convert_type(xf, jnp.int32) >> 23) - 127
    return jnp.where(rr > cc, level, -1)


def _cumsum_rows(gs, lower):
    w = gs[0].shape[1]
    pieces = [p.astype(BF16) for g in gs for p in _split3(g)]
    s = _mm(lower, jnp.concatenate(pieces, axis=1))
    return [s[:, (3 * i) * w:(3 * i + 1) * w] + s[:, (3 * i + 1) * w:(3 * i + 2) * w]
            + s[:, (3 * i + 2) * w:(3 * i + 3) * w] for i in range(len(gs))]


def _hgrn_intra(q, k, g2, levels):
    n = q.shape[0]
    rows = lax.broadcasted_iota(jnp.int32, (n, HG_DIM), 0)
    a = jnp.zeros((n, n), F32)
    m, bit = n // 2, n.bit_length() - 2
    while m >= 1:
        f = jnp.exp2(_level_exponent(g2, m, rows))
        a_m = _nt((q * f).astype(BF16), (k * f).astype(BF16))
        a = jnp.where(levels == bit, a_m, a)
        m, bit = m // 2, bit - 1
    return a


def _hgrn_kernel(*refs, group, has_init):
    if has_init:
        q_ref, k_ref, v_ref, g_ref, gate_ref, ng_ref, s0_ref, o_ref, so_ref = refs[:9]
    else:
        q_ref, k_ref, v_ref, g_ref, gate_ref, ng_ref, o_ref, so_ref = refs[:8]
    st_ref, a_ref, qe_ref, ke_ref, vb_ref, dg_ref, dec_ref = refs[-7:]
    t = pl.program_id(2)
    nheads = st_ref.shape[0]
    ngroups = q_ref.shape[0] // group
    lanes = [slice(hh * HG_DIM, (hh + 1) * HG_DIM) for hh in range(nheads)]

    @pl.when(t == 0)
    def _():
        for hh in range(nheads):
            st_ref[hh] = s0_ref[0, 0, hh].T if has_init else jnp.zeros((HG_DIM, HG_DIM), F32)

    ng = ng_ref[...]
    levels = _pair_levels(group)
    rr = lax.broadcasted_iota(jnp.int32, (group, group), 0)
    cc = lax.broadcasted_iota(jnp.int32, (group, group), 1)
    lower = jnp.where(rr >= cc, 1.0, 0.0).astype(BF16)

    def prepare(i):
        sl = pl.ds(pl.multiple_of(i * group, group), group)
        g2s = _cumsum_rows([g_ref[sl, ls] for ls in lanes], lower)
        for hh, ls in enumerate(lanes):
            q, k, v, g2 = q_ref[sl, ls], k_ref[sl, ls], v_ref[sl, ls], g2s[hh]
            g_last = g2[group - 1:group, :]
            a_ref[hh] = _hgrn_intra(q, k, g2, levels).astype(BF16)
            qe_ref[hh] = (q * jnp.exp2(g2)).astype(BF16)
            ke_ref[hh] = (k * jnp.exp2(g_last - g2)).astype(BF16)
            vb_ref[hh] = v.astype(BF16)
            dg_ref[hh] = jnp.sum(q * k, axis=-1, keepdims=True) * v
            dec_ref[hh] = jnp.broadcast_to(jnp.exp2(g_last), (SUBLANES, HG_DIM))

    def finish(i):
        sl = pl.ds(pl.multiple_of(i * group, group), group)
        for hh, ls in enumerate(lanes):
            st = st_ref[hh]
            vb = vb_ref[hh]
            o = _nt(qe_ref[hh], st.astype(BF16)) + _mm(a_ref[hh], vb) + dg_ref[hh]
            upd = lax.dot_general(vb, ke_ref[hh], (((0,), (0,)), ((), ())), preferred_element_type=F32)
            st_ref[hh] = st * dec_ref[hh, 0:1, :] + upd
            o = o * lax.rsqrt(jnp.mean(o * o, axis=-1, keepdims=True) + RMS_EPS) * ng
            o_ref[sl, ls] = (o * gate_ref[sl, ls]).astype(o_ref.dtype)

    prepare(0)

    def step(i, c):
        finish(i)
        prepare(i + 1)
        return c

    lax.fori_loop(0, ngroups - 1, step, 0)
    finish(ngroups - 1)

    @pl.when(t == pl.num_programs(2) - 1)
    def _():
        for hh in range(nheads):
            so_ref[0, hh] = st_ref[hh].T


def _hgrn(act, norm_g, s0, *, layer, batch, seq_len, tt, nheads):
    m = act.shape[0]
    nt = seq_len // tt
    has_init = s0 is not None
    group = min(tt, HG_GROUP)
    width = nheads * HG_DIM
    blk = lambda cb: pl.BlockSpec((tt, width), lambda b, h, t, cb=cb: (b * nt + t, cb // width + h))
    in_specs = [blk(C_HQ), blk(C_HK), blk(C_HV), blk(C_HG), blk(C_HGATE),
                pl.BlockSpec((1, HG_DIM), lambda b, h, t: (0, 0))]
    args = [act, act, act, act, act, norm_g]
    if has_init:
        in_specs.append(pl.BlockSpec((1, 1, nheads, HG_DIM, HG_DIM), lambda b, h, t: (layer, b, h, 0, 0)))
        args.append(s0)
    return pl.pallas_call(
        functools.partial(_hgrn_kernel, group=group, has_init=has_init),
        grid=(batch, HG_HEADS // nheads, nt),
        in_specs=in_specs,
        out_specs=[
            pl.BlockSpec((tt, width), lambda b, h, t: (b * nt + t, h)),
            pl.BlockSpec((1, nheads, HG_DIM, HG_DIM), lambda b, h, t: (b, h, 0, 0)),
        ],
        out_shape=[
            jax.ShapeDtypeStruct((m, HG_WIDTH), BF16),
            jax.ShapeDtypeStruct((batch, HG_HEADS, HG_DIM, HG_DIM), F32),
        ],
        scratch_shapes=[
            pltpu.VMEM((nheads, HG_DIM, HG_DIM), F32),
            pltpu.VMEM((nheads, group, group), BF16),
            pltpu.VMEM((nheads, group, HG_DIM), BF16),
            pltpu.VMEM((nheads, group, HG_DIM), BF16),
            pltpu.VMEM((nheads, group, HG_DIM), BF16),
            pltpu.VMEM((nheads, group, HG_DIM), F32),
            pltpu.VMEM((nheads, SUBLANES, HG_DIM), F32),
        ],
        compiler_params=pltpu.CompilerParams(
            dimension_semantics=("arbitrary", "arbitrary", "arbitrary"), vmem_limit_bytes=VMEM_LIMIT),
        name="hgrn",
    )(*args)


def _softmax_t(st, m, l):
    m_new = jnp.maximum(m, jnp.max(st, axis=0, keepdims=True))
    alpha = jnp.exp(m - m_new)
    p = jnp.exp(st - m_new)
    return m_new, alpha * l + jnp.sum(p, axis=0, keepdims=True), alpha, p.astype(BF16)


def _attn_prompt_kernel(fq_ref, fgate_ref, mq_ref, mgate_ref, ka_ref, vt_ref, dn_ref, mk_ref, mvt_ref,
                        o_ref, vtb_ref, mvtb_ref, qa_ref, st_ref, p_ref, acc_ref, *, tq):
    seq = fq_ref.shape[0]
    tk = tq
    heads = range(FX_HEADS)

    for j in range(seq // tk):
        for p in range(FX_HEADS // 2):
            vtb_ref[j, p] = vt_ref[0, p * LANES:(p + 1) * LANES, j * tk:(j + 1) * tk].astype(BF16)
    mvtb_ref[...] = mvt_ref[0].astype(BF16)

    key = lax.broadcasted_iota(jnp.int32, (tk, tq), 0)
    qry = lax.broadcasted_iota(jnp.int32, (tk, tq), 1)

    def scores(j, h):
        kr = pl.ds(pl.multiple_of(j * tk, tk), tk)
        return _nt(ka_ref[kr, h * LANES:(h + 1) * LANES], qa_ref[h])

    def values(j, h):
        return vtb_ref[j, h // 2, (h % 2) * FX_DIM:(h % 2 + 1) * FX_DIM, :]

    def qtile(i, c):
        rows = pl.ds(pl.multiple_of(i * tq, tq), tq)
        dnq = dn_ref[rows, :]
        for h in heads:
            qa_ref[h] = _aug_operand(fq_ref[rows, _pair_lanes(h)] * (FX_DIM ** -0.5),
                                     dnq[:, h:h + 1], h % 2, True)
        for h in heads:
            st_ref[h] = scores(0, h)
            p_ref[h] = jnp.zeros((tk, tq), BF16)
            acc_ref[h] = jnp.zeros((FX_DIM, tq), F32)

        def body(j, carry):
            prev = jnp.maximum(j - 1, 0)
            out = []
            for h in heads:
                m, l, alpha_prev = carry[h]
                pv = _mm(values(prev, h), p_ref[h])
                st = st_ref[h]
                st_ref[h] = scores(j + 1, h)
                m, l, alpha, p = _softmax_t(st, m, l)
                acc_ref[h] = alpha_prev * acc_ref[h] + pv
                p_ref[h] = p
                out.append((m, l, alpha))
            return tuple(out)

        init = tuple((jnp.full((1, tq), -jnp.inf, F32), jnp.zeros((1, tq), F32), jnp.ones((1, tq), F32))
                     for _ in heads)
        carry = lax.fori_loop(0, i, body, init)

        prev = jnp.maximum(i - 1, 0)
        o_heads = []
        for h in heads:
            m, l, alpha_prev = carry[h]
            acc = alpha_prev * acc_ref[h] + _mm(values(prev, h), p_ref[h])
            m, l, alpha, p = _softmax_t(jnp.where(key <= qry, st_ref[h], -jnp.inf), m, l)
            acc = alpha * acc + _mm(values(i, h), p)
            o_heads.append(acc / l)
        for pr in range(FX_HEADS // 2):
            ls = _pair_lanes(2 * pr)
            o_t = jnp.concatenate(o_heads[2 * pr:2 * pr + 2], axis=0)
            o_ref[rows, ls] = (o_t.T * fgate_ref[rows, ls]).astype(o_ref.dtype)

        for pr in range(MEM_HEADS // 2):
            ls = _pair_lanes(2 * pr)
            mq_pair = mq_ref[rows, ls] * (MEM_DIM ** -0.5)
            halves = []
            for hh in range(2):
                qm = jnp.where(_head_mask(hh), mq_pair, 0.0).astype(BF16)
                st = _nt(mk_ref[:, ls], qm)
                pm = jnp.exp(st - jnp.max(st, axis=0, keepdims=True))
                halves.append(_mm(mvtb_ref[pr * LANES + hh * MEM_DIM:pr * LANES + (hh + 1) * MEM_DIM, :],
                                  pm.astype(BF16)) / jnp.sum(pm, axis=0, keepdims=True))
            o_t = jnp.concatenate(halves, axis=0)
            lm = slice(FX_WIDTH + pr * LANES, FX_WIDTH + (pr + 1) * LANES)
            o_ref[rows, lm] = (o_t.T * mgate_ref[rows, ls]).astype(o_ref.dtype)
        return c

    lax.fori_loop(0, seq // tq, qtile, 0)


def _attn_prompt(act, ka, vt, d_nat, mk, mvt, *, batch, seq_len, tq):
    m = act.shape[0]
    n_mem = mk.shape[0] // batch
    tk = tq
    ablk = lambda c0: pl.BlockSpec((seq_len, FX_WIDTH), lambda b, c0=c0: (b, c0 // FX_WIDTH))
    return pl.pallas_call(
        functools.partial(_attn_prompt_kernel, tq=tq),
        grid=(batch,),
        in_specs=[
            ablk(C_FQ), ablk(C_FGATE), ablk(C_MQ), ablk(C_MGATE),
            pl.BlockSpec((seq_len, FX_HEADS * LANES), lambda b: (b, 0)),
            pl.BlockSpec((1, FX_WIDTH, seq_len), lambda b: (b, 0, 0)),
            pl.BlockSpec((seq_len, LANES), lambda b: (b, 0)),
            pl.BlockSpec((n_mem, MEM_WIDTH), lambda b: (b, 0)),
            pl.BlockSpec((1, MEM_WIDTH, n_mem), lambda b: (b, 0, 0)),
        ],
        out_specs=pl.BlockSpec((seq_len, FX_WIDTH + MEM_WIDTH), lambda b: (b, 0)),
        out_shape=jax.ShapeDtypeStruct((m, FX_WIDTH + MEM_WIDTH), BF16),
        scratch_shapes=[
            pltpu.VMEM((seq_len // tk, FX_HEADS // 2, LANES, tk), BF16),
            pltpu.VMEM((MEM_WIDTH, n_mem), BF16),
            pltpu.VMEM((FX_HEADS, tq, LANES), BF16),
            pltpu.VMEM((FX_HEADS, tk, tq), F32),
            pltpu.VMEM((FX_HEADS, tk, tq), BF16),
            pltpu.VMEM((FX_HEADS, FX_DIM, tq), F32),
        ],
        compiler_params=pltpu.CompilerParams(
            dimension_semantics=("arbitrary",), vmem_limit_bytes=VMEM_LIMIT),
        name="attn_prompt",
    )(act, act, act, act, ka, vt, d_nat, mk, mvt)


def _stack_heads(x, scale):
    lane = lax.broadcasted_iota(jnp.int32, (1, x.shape[1]), 1)
    return jnp.concatenate([jnp.where(lane // FX_DIM == h, x * scale, 0.0) for h in range(FX_HEADS)],
                           axis=0).astype(BF16)


def _unstack_heads(o, ts):
    lane = lax.broadcasted_iota(jnp.int32, (1, o.shape[1]), 1)
    out = jnp.zeros((ts, o.shape[1]), F32)
    for h in range(FX_HEADS):
        out = jnp.where(lane // FX_DIM == h, o[h * ts:(h + 1) * ts, :], out)
    return out


def _per_head_rows(vals, ts):
    return jnp.concatenate([jnp.broadcast_to(v, (ts, v.shape[1])) for v in vals], axis=0)


def _softmax_update(s, pv_fn, m, l, acc):
    m_new = jnp.maximum(m, jnp.max(s, axis=-1, keepdims=True))
    alpha = jnp.exp(m - m_new)
    p = jnp.exp(s - m_new)
    l = alpha * l + jnp.sum(p, axis=-1, keepdims=True)
    acc = alpha * acc + pv_fn(p.astype(BF16))
    return m_new, l, acc


def _attn_sample_kernel(fq_ref, fgate_ref, mq_ref, mgate_ref, kn_ref, vn_ref, dn_ref, dt_ref,
                        pkt_ref, pvt_ref, plf_ref, mkt_ref, mvt_ref, o_ref, *, chunk):
    ts = fq_ref.shape[0]
    n_new = kn_ref.shape[0]
    past = pkt_ref.shape[3]
    nchunks = past // chunk
    b = pl.program_id(0)
    hrows = FX_HEADS * ts

    r = lax.broadcasted_iota(jnp.int32, (chunk, chunk), 0)
    c = lax.broadcasted_iota(jnp.int32, (chunk, chunk), 1)
    upper = jnp.where(r <= c, 1.0, 0.0).astype(F32)
    total = jnp.zeros((FX_HEADS, 1), F32)
    d_past = []
    for ci in range(nchunks):
        d = _mm(plf_ref[0, 0, :, ci * chunk:(ci + 1) * chunk], upper,
                precision=lax.Precision.HIGHEST) + total
        d_past.append(d)
        total = d[:, chunk - 1:chunk]

    dn = dn_ref[...]
    dt_new = dt_ref[...]
    tot = [total[h:h + 1, :] for h in range(FX_HEADS)]
    dq = jnp.concatenate([tot[h] + dn[:, h:h + 1] for h in range(FX_HEADS)], axis=0)
    q4 = _stack_heads(fq_ref[...], FX_DIM ** -0.5)

    carry = (jnp.full((hrows, 1), -jnp.inf, F32), jnp.zeros((hrows, 1), F32),
             jnp.zeros((hrows, FX_WIDTH), F32))
    for ci in range(nchunks):
        cs = slice(ci * chunk, (ci + 1) * chunk)
        dk = _per_head_rows([d_past[ci][h:h + 1, :] for h in range(FX_HEADS)], ts)
        s = _mm(q4, pkt_ref[0, 0, :, cs].astype(BF16)) + dq - dk
        vt = pvt_ref[0, 0, :, cs].astype(BF16)
        carry = _softmax_update(s, lambda p, vt=vt: _nt(p, vt), *carry)

    row = lax.broadcasted_iota(jnp.int32, (hrows, n_new), 0)
    col = lax.broadcasted_iota(jnp.int32, (hrows, n_new), 1)
    visible = (col // ts == b) & (col % ts <= row % ts)
    dk = _per_head_rows([tot[h] + dt_new[h:h + 1, :] for h in range(FX_HEADS)], ts)
    s = _nt(q4, kn_ref[...].astype(BF16)) + dq - dk
    vn = vn_ref[...].astype(BF16)
    _, l, acc = _softmax_update(jnp.where(visible, s, -jnp.inf), lambda p: _mm(p, vn), *carry)
    o_fx = _unstack_heads(acc / l, ts) * fgate_ref[...]
    o_ref[:, 0:FX_WIDTH] = o_fx.astype(o_ref.dtype)

    mq4 = _stack_heads(mq_ref[...], MEM_DIM ** -0.5)
    s = _mm(mq4, mkt_ref[0, 0].astype(BF16))
    p = jnp.exp(s - jnp.max(s, axis=-1, keepdims=True))
    o = _nt(p.astype(BF16), mvt_ref[0, 0].astype(BF16)) / jnp.sum(p, axis=-1, keepdims=True)
    o_mem = _unstack_heads(o, ts) * mgate_ref[...]
    o_ref[:, FX_WIDTH:] = o_mem.astype(o_ref.dtype)


def _attn_sample(act, k_new, v_new, d_nat, d_t, past_kt, past_vt, past_lft, mem_kt, mem_vt,
                 *, layer, batch, ts):
    m = act.shape[0]
    past = past_kt.shape[3]
    n_mem = mem_kt.shape[3]
    chunk = min(past, 512)
    ablk = lambda c0: pl.BlockSpec((ts, FX_WIDTH), lambda b, c0=c0: (b, c0 // FX_WIDTH))
    cache = lambda nfeat, n: pl.BlockSpec((1, 1, nfeat, n), lambda b: (layer, b, 0, 0))
    return pl.pallas_call(
        functools.partial(_attn_sample_kernel, chunk=chunk),
        grid=(batch,),
        in_specs=[
            ablk(C_FQ), ablk(C_FGATE), ablk(C_MQ), ablk(C_MGATE),
            pl.BlockSpec((m, FX_WIDTH), lambda b: (0, 0)),
            pl.BlockSpec((m, FX_WIDTH), lambda b: (0, 0)),
            pl.BlockSpec((ts, LANES), lambda b: (b, 0)),
            pl.BlockSpec((SUBLANES, m), lambda b: (0, 0)),
            cache(FX_WIDTH, past), cache(FX_WIDTH, past), cache(FX_HEADS, past),
            cache(MEM_WIDTH, n_mem), cache(MEM_WIDTH, n_mem),
        ],
        out_specs=pl.BlockSpec((ts, FX_WIDTH + MEM_WIDTH), lambda b: (b, 0)),
        out_shape=jax.ShapeDtypeStruct((m, FX_WIDTH + MEM_WIDTH), BF16),
        compiler_params=pltpu.CompilerParams(
            dimension_semantics=("arbitrary",), vmem_limit_bytes=VMEM_LIMIT),
        name="attn_sample",
    )(act, act, act, act, k_new, v_new, d_nat, d_t, past_kt, past_vt, past_lft, mem_kt, mem_vt)


def _memkv_kernel(x_ref, w_ref, kb_ref, kt_ref, vt_ref):
    kv = _mm(x_ref[...].astype(BF16), w_ref[...])
    kb_ref[...] = kv[:, :MEM_WIDTH].astype(BF16)
    kt_ref[0] = kv[:, :MEM_WIDTH].T
    vt_ref[0] = kv[:, MEM_WIDTH:].T


def _memkv(mem, w, *, batch):
    m, d = mem.shape
    n_mem = m // batch
    return pl.pallas_call(
        _memkv_kernel,
        grid=(batch,),
        in_specs=[pl.BlockSpec((n_mem, d), lambda i: (i, 0)),
                  pl.BlockSpec(w.shape, lambda i: (0, 0))],
        out_specs=[pl.BlockSpec((n_mem, MEM_WIDTH), lambda i: (i, 0)),
                   pl.BlockSpec((1, MEM_WIDTH, n_mem), lambda i: (i, 0, 0)),
                   pl.BlockSpec((1, MEM_WIDTH, n_mem), lambda i: (i, 0, 0))],
        out_shape=[jax.ShapeDtypeStruct((m, MEM_WIDTH), BF16),
                   jax.ShapeDtypeStruct((batch, MEM_WIDTH, n_mem), F32),
                   jax.ShapeDtypeStruct((batch, MEM_WIDTH, n_mem), F32)],
        compiler_params=pltpu.CompilerParams(
            dimension_semantics=("arbitrary",), vmem_limit_bytes=VMEM_LIMIT),
        name="memkv",
    )(mem, w)


def _out_kernel(x_ref, hg_ref, at_ref, w_ref, g_ref, b_ref, y_ref, *, alpha):
    acc = _mm(hg_ref[...], w_ref[0:HG_WIDTH, :]) + _mm(at_ref[...], w_ref[HG_WIDTH:, :])
    z = alpha * x_ref[...] + acc
    mu = jnp.mean(z, axis=-1, keepdims=True)
    zc = z - mu
    var = jnp.mean(zc * zc, axis=-1, keepdims=True)
    y_ref[...] = zc * lax.rsqrt(var + LN_EPS) * g_ref[...] + b_ref[...]


def _out(x, mix_hg, mix_at, w, g, b, *, alpha, tm):
    m, d = x.shape
    return pl.pallas_call(
        functools.partial(_out_kernel, alpha=alpha),
        grid=(m // tm,),
        in_specs=[
            pl.BlockSpec((tm, d), lambda i: (i, 0)),
            pl.BlockSpec((tm, HG_WIDTH), lambda i: (i, 0)),
            pl.BlockSpec((tm, FX_WIDTH + MEM_WIDTH), lambda i: (i, 0)),
            pl.BlockSpec(w.shape, lambda i: (0, 0)),
            pl.BlockSpec((1, d), lambda i: (0, 0)),
            pl.BlockSpec((1, d), lambda i: (0, 0)),
        ],
        out_specs=pl.BlockSpec((tm, d), lambda i: (i, 0)),
        out_shape=jax.ShapeDtypeStruct((m, d), F32),
        compiler_params=pltpu.CompilerParams(
            dimension_semantics=("arbitrary",), vmem_limit_bytes=VMEM_LIMIT),
        name="out",
    )(x, mix_hg, mix_at, w, g, b)


def _pick_tile(n, target):
    t = min(n, target)
    while n % t:
        t //= 2
    return t


def _feature_major(x):
    lead = x.shape[:-3]
    n, heads, dim = x.shape[-3:]
    nl = len(lead)
    perm = tuple(range(nl)) + (nl + 1, nl + 2, nl)
    return jnp.transpose(x, perm).reshape(lead + (heads * dim, n))


def _token_major(x, heads):
    lead = x.shape[:-2]
    width, n = x.shape[-2:]
    nl = len(lead)
    x = x.reshape(lead + (heads, width // heads, n))
    return jnp.transpose(x, tuple(range(nl)) + (nl + 2, nl, nl + 1))


def kernel(x_prompt, x_sample, mem_prompt, state_hgrn, cache_fox_k, cache_fox_v, cache_fox_logf,
           cache_mem_k, cache_mem_v, w_in, b_fox_forget, hgrn_lower_bounds, hgrn_norm_g,
           w_mem_kv, w_out, ln_g, ln_b):
    depth, d_model, _ = w_in.shape
    bp, seq, _ = x_prompt.shape
    bs, dseq, _ = x_sample.shape
    n_mem = mem_prompt.shape[1]
    alpha = (2 * depth) ** 0.25

    fl0 = 4 * HG_WIDTH + 4 * FX_WIDTH
    w_main = jnp.concatenate([w_in[:, :, :fl0], w_in[:, :, fl0 + FX_HEADS:]], axis=-1).astype(BF16)
    w_flc = w_in[:, :, fl0:fl0 + FX_HEADS]
    w_fl = jnp.pad(w_flc, ((0, 0), (0, 0), (0, LANES - FX_HEADS))).astype(BF16)
    w_flt = jnp.pad(jnp.swapaxes(w_flc, 1, 2), ((0, 0), (0, SUBLANES - FX_HEADS), (0, 0))).astype(BF16)
    b_fl = jnp.pad(b_fox_forget, ((0, 0), (0, LANES - FX_HEADS)))[:, None, :]
    b_flt = jnp.pad(b_fox_forget, ((0, 0), (0, SUBLANES - FX_HEADS)))[:, :, None]
    lbs = hgrn_lower_bounds.astype(F32)
    w_memb = w_mem_kv.astype(BF16)
    w_outb = w_out.astype(BF16)

    past_kt = _feature_major(cache_fox_k)
    past_vt = _feature_major(cache_fox_v)
    past_lft = jnp.swapaxes(cache_fox_logf, 2, 3)
    mem_kt = _feature_major(cache_mem_k)
    mem_vt = _feature_major(cache_mem_v)

    mp, ms = bp * seq, bs * dseq
    tb = _pick_tile(seq, 256)
    yp = x_prompt.reshape(mp, d_model)
    ys = x_sample.reshape(ms, d_model)
    mem2 = mem_prompt.reshape(bp * n_mem, d_model)

    p_hg, p_kt, p_vt, p_lft, p_mkt, p_mvt = [], [], [], [], [], []
    s_hg, s_k, s_v, s_lft = [], [], [], []
    for l in range(depth):
        wl = (w_main[l], w_fl[l], w_flt[l], b_fl[l], b_flt[l], lbs)
        ng = hgrn_norm_g[l][None, :]
        g_ln, b_ln = ln_g[l][None, :], ln_b[l][None, :]

        mkb, mkt, mvt = _memkv(mem2, w_memb[l], batch=bp)
        act, ka, kt, vt, lft, d_nat = _proj(yp, *wl, layer=l, batch=bp, seq_len=seq,
                                            tm=_pick_tile(seq, 512), prompt=True)
        mix_hg, hg = _hgrn(act, ng, None, layer=l, batch=bp, seq_len=seq, tt=_pick_tile(seq, 2048),
                           nheads=2)
        mix_at = _attn_prompt(act, ka, vt, d_nat, mkb, mvt, batch=bp, seq_len=seq, tq=tb)
        yp = _out(yp, mix_hg, mix_at, w_outb[l], g_ln, b_ln, alpha=alpha, tm=_pick_tile(mp, 1024))
        p_hg.append(hg); p_kt.append(kt); p_vt.append(vt); p_lft.append(lft)
        p_mkt.append(mkt); p_mvt.append(mvt)

        act, k, v, lft, d_nat, d_t = _proj(ys, *wl, layer=l, batch=bs, seq_len=dseq, tm=ms, prompt=False)
        mix_hg, hg = _hgrn(act, ng, state_hgrn, layer=l, batch=bs, seq_len=dseq, tt=dseq, nheads=2)
        mix_at = _attn_sample(act, k, v, d_nat, d_t, past_kt, past_vt, past_lft, mem_kt, mem_vt,
                              layer=l, batch=bs, ts=dseq)
        ys = _out(ys, mix_hg, mix_at, w_outb[l], g_ln, b_ln, alpha=alpha, tm=ms)
        s_hg.append(hg); s_k.append(k); s_v.append(v); s_lft.append(lft)

    st = jnp.stack
    s_lf = st(s_lft)[:, :FX_HEADS, :].reshape(depth, FX_HEADS, bs, dseq)
    return (yp.reshape(bp, seq, d_model), ys.reshape(bs, dseq, d_model),
            st(p_hg),
            _token_major(st(p_kt), FX_HEADS),
            _token_major(st(p_vt), FX_HEADS),
            jnp.swapaxes(st(p_lft), 2, 3),
            _token_major(st(p_mkt), MEM_HEADS),
            _token_major(st(p_mvt), MEM_HEADS),
            st(s_hg),
            st(s_k).reshape(depth, bs, dseq, FX_HEADS, FX_DIM),
            st(s_v).reshape(depth, bs, dseq, FX_HEADS, FX_DIM),
            jnp.transpose(s_lf, (0, 2, 3, 1)))
```

```python
import functools

import jax
import jax.numpy as jnp
from jax import lax
from jax.experimental import pallas as pl
from jax.experimental.pallas import tpu as pltpu

F32 = jnp.float32
BF16 = jnp.bfloat16

HG_HEADS = 4
HG_DIM = 128
HG_WIDTH = HG_HEADS * HG_DIM
FX_HEADS = 4
FX_DIM = 64
FX_WIDTH = FX_HEADS * FX_DIM
MEM_HEADS = 4
MEM_DIM = 64
MEM_WIDTH = MEM_HEADS * MEM_DIM
HG_GROUP = 128
LN_EPS = 1e-5
RMS_EPS = 1e-6
LANES = 128
SUBLANES = 8
VMEM_LIMIT = 56 * 1024 * 1024
AUG = 3
LOG2E = 1.4426950408889634

C_HQ, C_HK, C_HV, C_HG, C_HGATE = 0, 512, 1024, 1536, 2048
ACT_PROMPT = dict(FGATE=2560, MQ=2816, MGATE=3072, WIDTH=3328)
ACT_SAMPLE = dict(FQ=2560, FGATE=2816, MQ=3072, MGATE=3328, WIDTH=3584)
W_HQ, W_HF, W_HI, W_HGATE, W_FQ, W_FK, W_FV, W_FGATE, W_MQ, W_MGATE = (
    0, 512, 1024, 1536, 2048, 2304, 2560, 2816, 3072, 3328)


def _nt(a, b, precision=None):
    return lax.dot_general(a, b, (((1,), (1,)), ((), ())), precision=precision,
                           preferred_element_type=F32)


def _mm(a, b, precision=None):
    return jnp.dot(a, b, precision=precision, preferred_element_type=F32)


def _silu(x):
    return x * jax.nn.sigmoid(x)


def _log_sigmoid(x):
    return jnp.minimum(x, 0.0) - jnp.log(1.0 + jnp.exp(-jnp.abs(x)))


def _head_mask(hh):
    lane = lax.broadcasted_iota(jnp.int32, (1, LANES), 1)
    return (lane // FX_DIM) == hh


def _pair_lanes(h):
    return slice((h // 2) * LANES, (h // 2 + 1) * LANES)


def _split3(x):
    hi = x.astype(BF16).astype(F32)
    r = x - hi
    mid = r.astype(BF16).astype(F32)
    return hi, mid, r - mid


def _aug_operand(x_pair, d, hh, is_query):
    lane = lax.broadcasted_iota(jnp.int32, (1, LANES), 1)
    a = lane - (1 - hh) * FX_DIM
    hi, mid, lo = _split3(d if is_query else -d)
    p0, o0 = (0, AUG) if is_query else (AUG, 0)
    extra = jnp.where(a == p0, hi, jnp.where(a == p0 + 1, mid, jnp.where(a == p0 + 2, lo,
                      jnp.where((a >= o0) & (a < o0 + AUG), 1.0, 0.0))))
    return jnp.where(_head_mask(hh), x_pair, extra).astype(BF16)


def _proj_kernel(x_ref, w_ref, wfl_ref, wflt_ref, bfl_ref, bflt_ref, lbs_ref, *refs,
                 layer, seg, tiles_per_seq, prompt):
    act_ref = refs[0]
    cols = ACT_PROMPT if prompt else ACT_SAMPLE
    tm = x_ref.shape[0]
    xb = x_ref[...].astype(BF16)

    def proj(c0, width):
        return _mm(xb, w_ref[:, c0:c0 + width])

    lbs = lbs_ref[...]
    e = jnp.exp(lbs - jnp.max(lbs, axis=0, keepdims=True))
    sm = e / jnp.sum(e, axis=0, keepdims=True)
    lb = jnp.sum(sm[:layer + 1], axis=0, keepdims=True) - sm[0:1]

    lf_nat = _log_sigmoid(_mm(xb, wfl_ref[...]) + bfl_ref[...])
    lf_t = _log_sigmoid(_nt(wflt_ref[...], xb) + bflt_ref[...])

    r = lax.broadcasted_iota(jnp.int32, (tm, tm), 0)
    c = lax.broadcasted_iota(jnp.int32, (tm, tm), 1)
    same = (r // seg) == (c // seg)
    lower = jnp.where((c <= r) & same, 1.0, 0.0).astype(BF16)
    d_nat = _cumsum_rows([lf_nat], lower)[0]

    fq = proj(W_FQ, FX_WIDTH)
    k = proj(W_FK, FX_WIDTH)
    v = proj(W_FV, FX_WIDTH)
    if prompt:
        _, qa_ref, ka_ref, kt_ref, vt_ref, lft_ref, cn_ref, kv_ref = refs
        if tiles_per_seq > 1:
            @pl.when(pl.program_id(0) % tiles_per_seq == 0)
            def _():
                cn_ref[...] = jnp.zeros_like(cn_ref)
            d_nat = d_nat + cn_ref[0:1, :]
            cn_ref[...] = jnp.broadcast_to(d_nat[tm - 1:tm, :], cn_ref.shape)
        lft_ref[0] = lf_t[:FX_HEADS]
        kv_ref[:, 0:FX_WIDTH] = k
        kv_ref[:, FX_WIDTH:] = v
        kt_ref[0] = kv_ref[:, 0:FX_WIDTH].T
        vt_ref[0] = kv_ref[:, FX_WIDTH:].T
        d2 = d_nat * LOG2E
        for h in range(FX_HEADS):
            hl = slice(h * LANES, (h + 1) * LANES)
            qa_ref[:, hl] = _aug_operand(fq[:, _pair_lanes(h)] * (FX_DIM ** -0.5 * LOG2E),
                                         d2[:, h:h + 1], h % 2, True)
            ka_ref[:, hl] = _aug_operand(k[:, _pair_lanes(h)], d2[:, h:h + 1], h % 2, False)
    else:
        _, k_ref, v_ref, lft_ref, dn_ref, dt_ref = refs
        upper = jnp.where((r <= c) & same, 1.0, 0.0).astype(F32)
        dt_ref[...] = _mm(lf_t, upper, precision=lax.Precision.HIGHEST)
        lft_ref[...] = lf_t
        k_ref[...] = k
        v_ref[...] = v
        dn_ref[...] = d_nat
        act_ref[:, cols["FQ"]:cols["FQ"] + FX_WIDTH] = fq

    act_ref[:, C_HQ:C_HQ + HG_WIDTH] = _silu(proj(W_HQ, HG_WIDTH))
    forget = lb + (1.0 - lb) * jax.nn.sigmoid(proj(W_HF, HG_WIDTH))
    act_ref[:, C_HK:C_HK + HG_WIDTH] = 1.0 - forget
    act_ref[:, C_HG:C_HG + HG_WIDTH] = jnp.log2(forget)
    act_ref[:, C_HV:C_HV + HG_WIDTH] = proj(W_HI, HG_WIDTH)
    act_ref[:, C_HGATE:C_HGATE + HG_WIDTH] = _silu(proj(W_HGATE, HG_WIDTH))
    act_ref[:, cols["FGATE"]:cols["FGATE"] + FX_WIDTH] = _silu(proj(W_FGATE, FX_WIDTH))
    act_ref[:, cols["MQ"]:cols["MQ"] + MEM_WIDTH] = proj(W_MQ, MEM_WIDTH)
    act_ref[:, cols["MGATE"]:cols["MGATE"] + MEM_WIDTH] = _silu(proj(W_MGATE, MEM_WIDTH))


def _proj(x, w_main, w_fl, w_flt, b_fl, b_flt, lbs, *, layer, batch, seq_len, tm, prompt):
    m, d = x.shape
    ntiles = m // tm
    seg = min(seq_len, tm)
    tps = max(seq_len // tm, 1)
    assert prompt or ntiles == 1
    kern = functools.partial(_proj_kernel, layer=layer, seg=seg, tiles_per_seq=tps, prompt=prompt)
    full = lambda shape: pl.BlockSpec(shape, lambda i: (0,) * len(shape))
    rows = lambda width: pl.BlockSpec((tm, width), lambda i: (i, 0))
    feat = lambda nfeat: pl.BlockSpec((1, nfeat, tm), lambda i: (i // tps, 0, i % tps))
    if prompt:
        out_specs = [rows(ACT_PROMPT["WIDTH"]), rows(FX_HEADS * LANES), rows(FX_HEADS * LANES),
                     feat(FX_WIDTH), feat(FX_WIDTH), feat(FX_HEADS)]
        out_shape = [jax.ShapeDtypeStruct((m, ACT_PROMPT["WIDTH"]), F32),
                     jax.ShapeDtypeStruct((m, FX_HEADS * LANES), BF16),
                     jax.ShapeDtypeStruct((m, FX_HEADS * LANES), BF16),
                     jax.ShapeDtypeStruct((batch, FX_WIDTH, seq_len), F32),
                     jax.ShapeDtypeStruct((batch, FX_WIDTH, seq_len), F32),
                     jax.ShapeDtypeStruct((batch, FX_HEADS, seq_len), F32)]
        scratch = [pltpu.VMEM((SUBLANES, LANES), F32), pltpu.VMEM((tm, 2 * FX_WIDTH), F32)]
    else:
        out_specs = [rows(ACT_SAMPLE["WIDTH"]), rows(FX_WIDTH), rows(FX_WIDTH), full((SUBLANES, m)),
                     rows(LANES), full((SUBLANES, m))]
        out_shape = [jax.ShapeDtypeStruct((m, ACT_SAMPLE["WIDTH"]), F32),
                     jax.ShapeDtypeStruct((m, FX_WIDTH), F32),
                     jax.ShapeDtypeStruct((m, FX_WIDTH), F32),
                     jax.ShapeDtypeStruct((SUBLANES, m), F32),
                     jax.ShapeDtypeStruct((m, LANES), F32),
                     jax.ShapeDtypeStruct((SUBLANES, m), F32)]
        scratch = []
    return pl.pallas_call(
        kern,
        grid=(ntiles,),
        in_specs=[
            pl.BlockSpec((tm, d), lambda i: (i, 0)),
            full(w_main.shape), full(w_fl.shape), full(w_flt.shape),
            full(b_fl.shape), full(b_flt.shape), full(lbs.shape),
        ],
        out_specs=out_specs,
        out_shape=out_shape,
        scratch_shapes=scratch,
        compiler_params=pltpu.CompilerParams(
            dimension_semantics=("arbitrary",), vmem_limit_bytes=VMEM_LIMIT),
        name="proj_prompt" if prompt else "proj_sample",
    )(x, w_main, w_fl, w_flt, b_fl, b_flt, lbs)


def _half_ref(gc, m, rows):
    n, w = gc.shape
    if 2 * m >= SUBLANES:
        g3 = gc.reshape(n // (2 * m), 2 * m, w)
        return jnp.broadcast_to(g3[:, m - 1:m, :], g3.shape).reshape(n, w)
    pos = rows % (2 * m)
    if m == 2:
        return jnp.where(pos == 0, pltpu.roll(gc, n - 1, axis=0),
                         jnp.where(pos == 1, gc,
                                   jnp.where(pos == 2, pltpu.roll(gc, 1, axis=0),
                                             pltpu.roll(gc, 2, axis=0))))
    return jnp.where(pos == 0, gc, pltpu.roll(gc, 1, axis=0))


def _level_exponent(g2, m, rows):
    n = g2.shape[0]
    if m >= SUBLANES:
        pieces = []
        for r0 in range(0, n, 2 * m):
            first, second = g2[r0:r0 + m], g2[r0 + m:r0 + 2 * m]
            ref = first[m - 1:m]
            pieces += [ref - first, second - ref]
        return jnp.concatenate(pieces, axis=0)
    ref = _half_ref(g2, m, rows)
    return jnp.where((rows % (2 * m)) < m, ref - g2, g2 - ref)


def _pair_levels(n):
    rr = lax.broadcasted_iota(jnp.int32, (n, n), 0)
    cc = lax.broadcasted_iota(jnp.int32, (n, n), 1)
    xf = (rr ^ cc).astype(F32)
    level = (lax.bitcast_convert_type(xf, jnp.int32) >> 23) - 127
    return jnp.where(rr > cc, level, -1)


def _cumsum_rows(gs, lower):
    w = gs[0].shape[1]
    pieces = [p.astype(BF16) for g in gs for p in _split3(g)]
    s = _mm(lower, jnp.concatenate(pieces, axis=1))
    return [s[:, (3 * i) * w:(3 * i + 1) * w] + s[:, (3 * i + 1) * w:(3 * i + 2) * w]
            + s[:, (3 * i + 2) * w:(3 * i + 3) * w] for i in range(len(gs))]


def _hgrn_intra(q, k, g2, levels):
    n = q.shape[0]
    rows = lax.broadcasted_iota(jnp.int32, (n, HG_DIM), 0)
    a = jnp.zeros((n, n), F32)
    m, bit = n // 2, n.bit_length() - 2
    while m >= 1:
        f = jnp.exp2(_level_exponent(g2, m, rows))
        a_m = _nt((q * f).astype(BF16), (k * f).astype(BF16))
        a = jnp.where(levels == bit, a_m, a)
        m, bit = m // 2, bit - 1
    return a


def _hgrn_kernel(*refs, group, has_init):
    if has_init:
        q_ref, k_ref, v_ref, g_ref, gate_ref, ng_ref, s0_ref, o_ref, so_ref = refs[:9]
    else:
        q_ref, k_ref, v_ref, g_ref, gate_ref, ng_ref, o_ref, so_ref = refs[:8]
    st_ref, a_ref, qe_ref, ke_ref, vb_ref, dg_ref, dec_ref = refs[-7:]
    t = pl.program_id(2)
    nheads = st_ref.shape[0]
    ngroups = q_ref.shape[0] // group
    lanes = [slice(hh * HG_DIM, (hh + 1) * HG_DIM) for hh in range(nheads)]

    @pl.when(t == 0)
    def _():
        for hh in range(nheads):
            st_ref[hh] = s0_ref[0, 0, hh].T if has_init else jnp.zeros((HG_DIM, HG_DIM), F32)

    ng = ng_ref[...]
    levels = _pair_levels(group)
    rr = lax.broadcasted_iota(jnp.int32, (group, group), 0)
    cc = lax.broadcasted_iota(jnp.int32, (group, group), 1)
    lower = jnp.where(rr >= cc, 1.0, 0.0).astype(BF16)

    def prepare(i):
        sl = pl.ds(pl.multiple_of(i * group, group), group)
        g2s = _cumsum_rows([g_ref[sl, ls] for ls in lanes], lower)
        for hh, ls in enumerate(lanes):
            q, k, v, g2 = q_ref[sl, ls], k_ref[sl, ls], v_ref[sl, ls], g2s[hh]
            g_last = g2[group - 1:group, :]
            a_ref[hh] = _hgrn_intra(q, k, g2, levels).astype(BF16)
            qe_ref[hh] = (q * jnp.exp2(g2)).astype(BF16)
            ke_ref[hh] = (k * jnp.exp2(g_last - g2)).astype(BF16)
            vb_ref[hh] = v.astype(BF16)
            dg_ref[hh] = jnp.sum(q * k, axis=-1, keepdims=True) * v
            dec_ref[hh] = jnp.broadcast_to(jnp.exp2(g_last), (SUBLANES, HG_DIM))

    def finish(i):
        sl = pl.ds(pl.multiple_of(i * group, group), group)
        for hh, ls in enumerate(lanes):
            st = st_ref[hh]
            vb = vb_ref[hh]
            o = _nt(qe_ref[hh], st.astype(BF16)) + _mm(a_ref[hh], vb) + dg_ref[hh]
            upd = lax.dot_general(vb, ke_ref[hh], (((0,), (0,)), ((), ())), preferred_element_type=F32)
            st_ref[hh] = st * dec_ref[hh, 0:1, :] + upd
            o = o * lax.rsqrt(jnp.mean(o * o, axis=-1, keepdims=True) + RMS_EPS) * ng
            o_ref[sl, ls] = (o * gate_ref[sl, ls]).astype(o_ref.dtype)

    prepare(0)

    def step(i, c):
        finish(i)
        prepare(i + 1)
        return c

    lax.fori_loop(0, ngroups - 1, step, 0)
    finish(ngroups - 1)

    @pl.when(t == pl.num_programs(2) - 1)
    def _():
        for hh in range(nheads):
            so_ref[0, hh] = st_ref[hh].T


def _hgrn(act, norm_g, s0, *, layer, batch, seq_len, tt, nheads):
    m = act.shape[0]
    nt = seq_len // tt
    has_init = s0 is not None
    group = min(tt, HG_GROUP)
    width = nheads * HG_DIM
    blk = lambda cb: pl.BlockSpec((tt, width), lambda b, h, t, cb=cb: (b * nt + t, cb // width + h))
    in_specs = [blk(C_HQ), blk(C_HK), blk(C_HV), blk(C_HG), blk(C_HGATE),
                pl.BlockSpec((1, HG_DIM), lambda b, h, t: (0, 0))]
    args = [act, act, act, act, act, norm_g]
    if has_init:
        in_specs.append(pl.BlockSpec((1, 1, nheads, HG_DIM, HG_DIM), lambda b, h, t: (layer, b, h, 0, 0)))
        args.append(s0)
    return pl.pallas_call(
        functools.partial(_hgrn_kernel, group=group, has_init=has_init),
        grid=(batch, HG_HEADS // nheads, nt),
        in_specs=in_specs,
        out_specs=[
            pl.BlockSpec((tt, width), lambda b, h, t: (b * nt + t, h)),
            pl.BlockSpec((1, nheads, HG_DIM, HG_DIM), lambda b, h, t: (b, h, 0, 0)),
        ],
        out_shape=[
            jax.ShapeDtypeStruct((m, HG_WIDTH), BF16),
            jax.ShapeDtypeStruct((batch, HG_HEADS, HG_DIM, HG_DIM), F32),
        ],
        scratch_shapes=[
            pltpu.VMEM((nheads, HG_DIM, HG_DIM), F32),
            pltpu.VMEM((nheads, group, group), BF16),
            pltpu.VMEM((nheads, group, HG_DIM), BF16),
            pltpu.VMEM((nheads, group, HG_DIM), BF16),
            pltpu.VMEM((nheads, group, HG_DIM), BF16),
            pltpu.VMEM((nheads, group, HG_DIM), F32),
            pltpu.VMEM((nheads, SUBLANES, HG_DIM), F32),
        ],
        compiler_params=pltpu.CompilerParams(
            dimension_semantics=("arbitrary", "arbitrary", "arbitrary"), vmem_limit_bytes=VMEM_LIMIT),
        name="hgrn",
    )(*args)


VROWS = FX_DIM + 16


def _softmax_t(st, m):
    m_new = jnp.maximum(m, jnp.max(st, axis=0, keepdims=True))
    return m_new, jnp.exp2(m - m_new), jnp.exp2(st - m_new).astype(BF16)


def _with_ones(vt):
    return jnp.concatenate([vt.astype(BF16), jnp.ones((VROWS - FX_DIM, vt.shape[1]), BF16)], axis=0)


def _normalised(acc):
    return acc[0:FX_DIM] / acc[FX_DIM:FX_DIM + 1]


def _attn_prompt_kernel(qa_ref, fgate_ref, mq_ref, mgate_ref, ka_ref, vt_ref, mk_ref, mvt_ref,
                        o_ref, vtb_ref, mvtb_ref, st_ref, p_ref, acc_ref, *, tq):
    seq = qa_ref.shape[0]
    tk = tq
    ntiles = seq // tq
    heads = range(FX_HEADS)

    for j in range(seq // tk):
        for h in heads:
            vtb_ref[j, h] = _with_ones(vt_ref[0, h * FX_DIM:(h + 1) * FX_DIM, j * tk:(j + 1) * tk])
    for h in range(MEM_HEADS):
        mvtb_ref[h] = _with_ones(mvt_ref[0, h * MEM_DIM:(h + 1) * MEM_DIM, :])

    key = lax.broadcasted_iota(jnp.int32, (tk, tq), 0)
    qry = lax.broadcasted_iota(jnp.int32, (tk, tq), 1)

    def tile_rows(i):
        return pl.ds(pl.multiple_of(i * tq, tq), tq)

    def scores(i, j, h):
        hl = slice(h * LANES, (h + 1) * LANES)
        return _nt(ka_ref[tile_rows(j), hl], qa_ref[tile_rows(i), hl])

    def start(i, s):
        for h in heads:
            st_ref[s, h] = scores(i, 0, h)
            p_ref[s, h] = jnp.zeros((tk, tq), BF16)
            acc_ref[s, h] = jnp.zeros((VROWS, tq), F32)

    def finish(i, s, carry):
        rows = tile_rows(i)
        prev = jnp.maximum(i - 1, 0)
        o_heads = []
        for h in heads:
            m, alpha_prev = carry[h]
            acc = alpha_prev * acc_ref[s, h] + _mm(vtb_ref[prev, h], p_ref[s, h])
            m, alpha, p = _softmax_t(jnp.where(key <= qry, st_ref[s, h], -jnp.inf), m)
            acc = alpha * acc + _mm(vtb_ref[i, h], p)
            o_heads.append(_normalised(acc))
        for pr in range(FX_HEADS // 2):
            ls = _pair_lanes(2 * pr)
            o_t = jnp.concatenate(o_heads[2 * pr:2 * pr + 2], axis=0)
            o_ref[rows, ls] = (o_t.T * fgate_ref[rows, ls]).astype(o_ref.dtype)

        for pr in range(MEM_HEADS // 2):
            ls = _pair_lanes(2 * pr)
            mq_pair = mq_ref[rows, ls] * (MEM_DIM ** -0.5 * LOG2E)
            halves = []
            for hh in range(2):
                qm = jnp.where(_head_mask(hh), mq_pair, 0.0).astype(BF16)
                st = _nt(mk_ref[:, ls], qm)
                pm = jnp.exp2(st - jnp.max(st, axis=0, keepdims=True))
                halves.append(_normalised(_mm(mvtb_ref[2 * pr + hh], pm.astype(BF16))))
            o_t = jnp.concatenate(halves, axis=0)
            lm = slice(FX_WIDTH + pr * LANES, FX_WIDTH + (pr + 1) * LANES)
            o_ref[rows, lm] = (o_t.T * mgate_ref[rows, ls]).astype(o_ref.dtype)

    def qtile(i, s):
        def body(j, carry):
            prev = jnp.maximum(j - 1, 0)
            out = []
            for h in heads:
                m, alpha_prev = carry[h]
                pv = _mm(vtb_ref[prev, h], p_ref[s, h])
                st = st_ref[s, h]
                st_ref[s, h] = scores(i, j + 1, h)
                m, alpha, p = _softmax_t(st, m)
                acc_ref[s, h] = alpha_prev * acc_ref[s, h] + pv
                p_ref[s, h] = p
                out.append((m, alpha))
            return tuple(out)

        init = tuple((jnp.full((1, tq), -jnp.inf, F32), jnp.ones((1, tq), F32)) for _ in heads)
        carry = lax.fori_loop(0, i, body, init)
        finish(i, s, carry)
        start(jnp.minimum(i + 1, ntiles - 1), 1 - s)

    start(0, 0)

    def tile_pair(ip, c):
        qtile(2 * ip, 0)
        qtile(2 * ip + 1, 1)
        return c

    lax.fori_loop(0, ntiles // 2, tile_pair, 0)


def _attn_prompt(act, qa, ka, vt, mk, mvt, *, batch, seq_len, tq):
    m = act.shape[0]
    n_mem = mk.shape[0] // batch
    tk = tq
    assert (seq_len // tq) % 2 == 0, "query tiles are processed in pairs"
    ablk = lambda c0: pl.BlockSpec((seq_len, FX_WIDTH), lambda b, c0=c0: (b, c0 // FX_WIDTH))
    return pl.pallas_call(
        functools.partial(_attn_prompt_kernel, tq=tq),
        grid=(batch,),
        in_specs=[
            pl.BlockSpec((seq_len, FX_HEADS * LANES), lambda b: (b, 0)),
            ablk(ACT_PROMPT["FGATE"]), ablk(ACT_PROMPT["MQ"]), ablk(ACT_PROMPT["MGATE"]),
            pl.BlockSpec((seq_len, FX_HEADS * LANES), lambda b: (b, 0)),
            pl.BlockSpec((1, FX_WIDTH, seq_len), lambda b: (b, 0, 0)),
            pl.BlockSpec((n_mem, MEM_WIDTH), lambda b: (b, 0)),
            pl.BlockSpec((1, MEM_WIDTH, n_mem), lambda b: (b, 0, 0)),
        ],
        out_specs=pl.BlockSpec((seq_len, FX_WIDTH + MEM_WIDTH), lambda b: (b, 0)),
        out_shape=jax.ShapeDtypeStruct((m, FX_WIDTH + MEM_WIDTH), BF16),
        scratch_shapes=[
            pltpu.VMEM((seq_len // tk, FX_HEADS, VROWS, tk), BF16),
            pltpu.VMEM((MEM_HEADS, VROWS, n_mem), BF16),
            pltpu.VMEM((2, FX_HEADS, tk, tq), F32),
            pltpu.VMEM((2, FX_HEADS, tk, tq), BF16),
            pltpu.VMEM((2, FX_HEADS, VROWS, tq), F32),
        ],
        compiler_params=pltpu.CompilerParams(
            dimension_semantics=("arbitrary",), vmem_limit_bytes=VMEM_LIMIT),
        name="attn_prompt",
    )(qa, act, act, act, ka, vt, mk, mvt)


def _stack_heads(x, scale):
    lane = lax.broadcasted_iota(jnp.int32, (1, x.shape[1]), 1)
    return jnp.concatenate([jnp.where(lane // FX_DIM == h, x * scale, 0.0) for h in range(FX_HEADS)],
                           axis=0).astype(BF16)


def _unstack_heads(o, ts):
    lane = lax.broadcasted_iota(jnp.int32, (1, o.shape[1]), 1)
    out = jnp.zeros((ts, o.shape[1]), F32)
    for h in range(FX_HEADS):
        out = jnp.where(lane // FX_DIM == h, o[h * ts:(h + 1) * ts, :], out)
    return out


def _per_head_rows(vals, ts):
    return jnp.concatenate([jnp.broadcast_to(v, (ts, v.shape[1])) for v in vals], axis=0)


def _softmax_update(s, pv_fn, m, l, acc):
    m_new = jnp.maximum(m, jnp.max(s, axis=-1, keepdims=True))
    alpha = jnp.exp(m - m_new)
    p = jnp.exp(s - m_new)
    l = alpha * l + jnp.sum(p, axis=-1, keepdims=True)
    acc = alpha * acc + pv_fn(p.astype(BF16))
    return m_new, l, acc


def _attn_sample_kernel(fq_ref, fgate_ref, mq_ref, mgate_ref, kn_ref, vn_ref, dn_ref, dt_ref,
                        pkt_ref, pvt_ref, plf_ref, mkt_ref, mvt_ref, o_ref, *, chunk):
    ts = fq_ref.shape[0]
    n_new = kn_ref.shape[0]
    past = pkt_ref.shape[3]
    nchunks = past // chunk
    b = pl.program_id(0)
    hrows = FX_HEADS * ts

    r = lax.broadcasted_iota(jnp.int32, (chunk, chunk), 0)
    c = lax.broadcasted_iota(jnp.int32, (chunk, chunk), 1)
    upper = jnp.where(r <= c, 1.0, 0.0).astype(F32)
    total = jnp.zeros((FX_HEADS, 1), F32)
    d_past = []
    for ci in range(nchunks):
        d = _mm(plf_ref[0, 0, :, ci * chunk:(ci + 1) * chunk], upper,
                precision=lax.Precision.HIGHEST) + total
        d_past.append(d)
        total = d[:, chunk - 1:chunk]

    dn = dn_ref[...]
    dt_new = dt_ref[...]
    tot = [total[h:h + 1, :] for h in range(FX_HEADS)]
    dq = jnp.concatenate([tot[h] + dn[:, h:h + 1] for h in range(FX_HEADS)], axis=0)
    q4 = _stack_heads(fq_ref[...], FX_DIM ** -0.5)

    carry = (jnp.full((hrows, 1), -jnp.inf, F32), jnp.zeros((hrows, 1), F32),
             jnp.zeros((hrows, FX_WIDTH), F32))
    for ci in range(nchunks):
        cs = slice(ci * chunk, (ci + 1) * chunk)
        dk = _per_head_rows([d_past[ci][h:h + 1, :] for h in range(FX_HEADS)], ts)
        s = _mm(q4, pkt_ref[0, 0, :, cs].astype(BF16)) + dq - dk
        vt = pvt_ref[0, 0, :, cs].astype(BF16)
        carry = _softmax_update(s, lambda p, vt=vt: _nt(p, vt), *carry)

    row = lax.broadcasted_iota(jnp.int32, (hrows, n_new), 0)
    col = lax.broadcasted_iota(jnp.int32, (hrows, n_new), 1)
    visible = (col // ts == b) & (col % ts <= row % ts)
    dk = _per_head_rows([tot[h] + dt_new[h:h + 1, :] for h in range(FX_HEADS)], ts)
    s = _nt(q4, kn_ref[...].astype(BF16)) + dq - dk
    vn = vn_ref[...].astype(BF16)
    _, l, acc = _softmax_update(jnp.where(visible, s, -jnp.inf), lambda p: _mm(p, vn), *carry)
    o_fx = _unstack_heads(acc / l, ts) * fgate_ref[...]
    o_ref[:, 0:FX_WIDTH] = o_fx.astype(o_ref.dtype)

    mq4 = _stack_heads(mq_ref[...], MEM_DIM ** -0.5)
    s = _mm(mq4, mkt_ref[0, 0].astype(BF16))
    p = jnp.exp(s - jnp.max(s, axis=-1, keepdims=True))
    o = _nt(p.astype(BF16), mvt_ref[0, 0].astype(BF16)) / jnp.sum(p, axis=-1, keepdims=True)
    o_mem = _unstack_heads(o, ts) * mgate_ref[...]
    o_ref[:, FX_WIDTH:] = o_mem.astype(o_ref.dtype)


def _attn_sample(act, k_new, v_new, d_nat, d_t, past_kt, past_vt, past_lft, mem_kt, mem_vt,
                 *, layer, batch, ts):
    m = act.shape[0]
    past = past_kt.shape[3]
    n_mem = mem_kt.shape[3]
    chunk = min(past, 512)
    ablk = lambda c0: pl.BlockSpec((ts, FX_WIDTH), lambda b, c0=c0: (b, c0 // FX_WIDTH))
    cache = lambda nfeat, n: pl.BlockSpec((1, 1, nfeat, n), lambda b: (layer, b, 0, 0))
    return pl.pallas_call(
        functools.partial(_attn_sample_kernel, chunk=chunk),
        grid=(batch,),
        in_specs=[
            ablk(ACT_SAMPLE["FQ"]), ablk(ACT_SAMPLE["FGATE"]), ablk(ACT_SAMPLE["MQ"]),
            ablk(ACT_SAMPLE["MGATE"]),
            pl.BlockSpec((m, FX_WIDTH), lambda b: (0, 0)),
            pl.BlockSpec((m, FX_WIDTH), lambda b: (0, 0)),
            pl.BlockSpec((ts, LANES), lambda b: (b, 0)),
            pl.BlockSpec((SUBLANES, m), lambda b: (0, 0)),
            cache(FX_WIDTH, past), cache(FX_WIDTH, past), cache(FX_HEADS, past),
            cache(MEM_WIDTH, n_mem), cache(MEM_WIDTH, n_mem),
        ],
        out_specs=pl.BlockSpec((ts, FX_WIDTH + MEM_WIDTH), lambda b: (b, 0)),
        out_shape=jax.ShapeDtypeStruct((m, FX_WIDTH + MEM_WIDTH), BF16),
        compiler_params=pltpu.CompilerParams(
            dimension_semantics=("arbitrary",), vmem_limit_bytes=VMEM_LIMIT),
        name="attn_sample",
    )(act, act, act, act, k_new, v_new, d_nat, d_t, past_kt, past_vt, past_lft, mem_kt, mem_vt)


def _memkv_kernel(x_ref, w_ref, kb_ref, kt_ref, vt_ref):
    kv = _mm(x_ref[...].astype(BF16), w_ref[...])
    kb_ref[...] = kv[:, :MEM_WIDTH].astype(BF16)
    kt_ref[0] = kv[:, :MEM_WIDTH].T
    vt_ref[0] = kv[:, MEM_WIDTH:].T


def _memkv(mem, w, *, batch):
    m, d = mem.shape
    n_mem = m // batch
    return pl.pallas_call(
        _memkv_kernel,
        grid=(batch,),
        in_specs=[pl.BlockSpec((n_mem, d), lambda i: (i, 0)),
                  pl.BlockSpec(w.shape, lambda i: (0, 0))],
        out_specs=[pl.BlockSpec((n_mem, MEM_WIDTH), lambda i: (i, 0)),
                   pl.BlockSpec((1, MEM_WIDTH, n_mem), lambda i: (i, 0, 0)),
                   pl.BlockSpec((1, MEM_WIDTH, n_mem), lambda i: (i, 0, 0))],
        out_shape=[jax.ShapeDtypeStruct((m, MEM_WIDTH), BF16),
                   jax.ShapeDtypeStruct((batch, MEM_WIDTH, n_mem), F32),
                   jax.ShapeDtypeStruct((batch, MEM_WIDTH, n_mem), F32)],
        compiler_params=pltpu.CompilerParams(
            dimension_semantics=("arbitrary",), vmem_limit_bytes=VMEM_LIMIT),
        name="memkv",
    )(mem, w)


def _out_kernel(x_ref, hg_ref, at_ref, w_ref, g_ref, b_ref, y_ref, *, alpha):
    acc = _mm(hg_ref[...], w_ref[0:HG_WIDTH, :]) + _mm(at_ref[...], w_ref[HG_WIDTH:, :])
    z = alpha * x_ref[...] + acc
    mu = jnp.mean(z, axis=-1, keepdims=True)
    zc = z - mu
    var = jnp.mean(zc * zc, axis=-1, keepdims=True)
    y_ref[...] = zc * lax.rsqrt(var + LN_EPS) * g_ref[...] + b_ref[...]


def _out(x, mix_hg, mix_at, w, g, b, *, alpha, tm):
    m, d = x.shape
    return pl.pallas_call(
        functools.partial(_out_kernel, alpha=alpha),
        grid=(m // tm,),
        in_specs=[
            pl.BlockSpec((tm, d), lambda i: (i, 0)),
            pl.BlockSpec((tm, HG_WIDTH), lambda i: (i, 0)),
            pl.BlockSpec((tm, FX_WIDTH + MEM_WIDTH), lambda i: (i, 0)),
            pl.BlockSpec(w.shape, lambda i: (0, 0)),
            pl.BlockSpec((1, d), lambda i: (0, 0)),
            pl.BlockSpec((1, d), lambda i: (0, 0)),
        ],
        out_specs=pl.BlockSpec((tm, d), lambda i: (i, 0)),
        out_shape=jax.ShapeDtypeStruct((m, d), F32),
        compiler_params=pltpu.CompilerParams(
            dimension_semantics=("arbitrary",), vmem_limit_bytes=VMEM_LIMIT),
        name="out",
    )(x, mix_hg, mix_at, w, g, b)


def _pick_tile(n, target):
    t = min(n, target)
    while n % t:
        t //= 2
    return t


def _feature_major(x):
    lead = x.shape[:-3]
    n, heads, dim = x.shape[-3:]
    nl = len(lead)
    perm = tuple(range(nl)) + (nl + 1, nl + 2, nl)
    return jnp.transpose(x, perm).reshape(lead + (heads * dim, n))


def _token_major(x, heads):
    lead = x.shape[:-2]
    width, n = x.shape[-2:]
    nl = len(lead)
    x = x.reshape(lead + (heads, width // heads, n))
    return jnp.transpose(x, tuple(range(nl)) + (nl + 2, nl, nl + 1))


def kernel(x_prompt, x_sample, mem_prompt, state_hgrn, cache_fox_k, cache_fox_v, cache_fox_logf,
           cache_mem_k, cache_mem_v, w_in, b_fox_forget, hgrn_lower_bounds, hgrn_norm_g,
           w_mem_kv, w_out, ln_g, ln_b):
    depth, d_model, _ = w_in.shape
    bp, seq, _ = x_prompt.shape
    bs, dseq, _ = x_sample.shape
    n_mem = mem_prompt.shape[1]
    alpha = (2 * depth) ** 0.25

    fl0 = 4 * HG_WIDTH + 4 * FX_WIDTH
    w_main = jnp.concatenate([w_in[:, :, :fl0], w_in[:, :, fl0 + FX_HEADS:]], axis=-1).astype(BF16)
    w_flc = w_in[:, :, fl0:fl0 + FX_HEADS]
    w_fl = jnp.pad(w_flc, ((0, 0), (0, 0), (0, LANES - FX_HEADS))).astype(BF16)
    w_flt = jnp.pad(jnp.swapaxes(w_flc, 1, 2), ((0, 0), (0, SUBLANES - FX_HEADS), (0, 0))).astype(BF16)
    b_fl = jnp.pad(b_fox_forget, ((0, 0), (0, LANES - FX_HEADS)))[:, None, :]
    b_flt = jnp.pad(b_fox_forget, ((0, 0), (0, SUBLANES - FX_HEADS)))[:, :, None]
    lbs = hgrn_lower_bounds.astype(F32)
    w_memb = w_mem_kv.astype(BF16)
    w_outb = w_out.astype(BF16)

    past_kt = _feature_major(cache_fox_k)
    past_vt = _feature_major(cache_fox_v)
    past_lft = jnp.swapaxes(cache_fox_logf, 2, 3)
    mem_kt = _feature_major(cache_mem_k)
    mem_vt = _feature_major(cache_mem_v)

    mp, ms = bp * seq, bs * dseq
    tb = _pick_tile(seq, 256)
    yp = x_prompt.reshape(mp, d_model)
    ys = x_sample.reshape(ms, d_model)
    mem2 = mem_prompt.reshape(bp * n_mem, d_model)

    p_hg, p_kt, p_vt, p_lft, p_mkt, p_mvt = [], [], [], [], [], []
    s_hg, s_k, s_v, s_lft = [], [], [], []
    for l in range(depth):
        wl = (w_main[l], w_fl[l], w_flt[l], b_fl[l], b_flt[l], lbs)
        ng = hgrn_norm_g[l][None, :]
        g_ln, b_ln = ln_g[l][None, :], ln_b[l][None, :]

        mkb, mkt, mvt = _memkv(mem2, w_memb[l], batch=bp)
        act, qa, ka, kt, vt, lft = _proj(yp, *wl, layer=l, batch=bp, seq_len=seq,
                                         tm=_pick_tile(seq, 512), prompt=True)
        mix_hg, hg = _hgrn(act, ng, None, layer=l, batch=bp, seq_len=seq, tt=_pick_tile(seq, 2048),
                           nheads=2)
        mix_at = _attn_prompt(act, qa, ka, vt, mkb, mvt, batch=bp, seq_len=seq, tq=tb)
        yp = _out(yp, mix_hg, mix_at, w_outb[l], g_ln, b_ln, alpha=alpha, tm=_pick_tile(mp, 1024))
        p_hg.append(hg); p_kt.append(kt); p_vt.append(vt); p_lft.append(lft)
        p_mkt.append(mkt); p_mvt.append(mvt)

        act, k, v, lft, d_nat, d_t = _proj(ys, *wl, layer=l, batch=bs, seq_len=dseq, tm=ms, prompt=False)
        mix_hg, hg = _hgrn(act, ng, state_hgrn, layer=l, batch=bs, seq_len=dseq, tt=dseq, nheads=2)
        mix_at = _attn_sample(act, k, v, d_nat, d_t, past_kt, past_vt, past_lft, mem_kt, mem_vt,
                              layer=l, batch=bs, ts=dseq)
        ys = _out(ys, mix_hg, mix_at, w_outb[l], g_ln, b_ln, alpha=alpha, tm=ms)
        s_hg.append(hg); s_k.append(k); s_v.append(v); s_lft.append(lft)

    st = jnp.stack
    s_lf = st(s_lft)[:, :FX_HEADS, :].reshape(depth, FX_HEADS, bs, dseq)
    return (yp.reshape(bp, seq, d_model), ys.reshape(bs, dseq, d_model),
            st(p_hg),
            _token_major(st(p_kt), FX_HEADS),
            _token_major(st(p_vt), FX_HEADS),
            jnp.swapaxes(st(p_lft), 2, 3),
            _token_major(st(p_mkt), MEM_HEADS),
            _token_major(st(p_mvt), MEM_HEADS),
            st(s_hg),
            st(s_k).reshape(depth, bs, dseq, FX_HEADS, FX_DIM),
            st(s_v).reshape(depth, bs, dseq, FX_HEADS, FX_DIM),
            jnp.transpose(s_lf, (0, 2, 3, 1)))
```

```python
import functools

import jax
import jax.numpy as jnp
from jax import lax
from jax.experimental import pallas as pl
from jax.experimental.pallas import tpu as pltpu

F32 = jnp.float32
BF16 = jnp.bfloat16

HG_HEADS = 4
HG_DIM = 128
HG_WIDTH = HG_HEADS * HG_DIM
FX_HEADS = 4
FX_DIM = 64
FX_WIDTH = FX_HEADS * FX_DIM
MEM_HEADS = 4
MEM_DIM = 64
MEM_WIDTH = MEM_HEADS * MEM_DIM
HG_GROUP = 128
LN_EPS = 1e-5
RMS_EPS = 1e-6
LANES = 128
SUBLANES = 8
VMEM_LIMIT = 56 * 1024 * 1024
AUG = 3
LOG2E = 1.4426950408889634

C_HQ, C_HK, C_HV, C_HG, C_HGATE = 0, 512, 1024, 1536, 2048
ACT_PROMPT = dict(FGATE=2560, MQ=2816, MGATE=3072, WIDTH=3328)
ACT_SAMPLE = dict(FQ=2560, FGATE=2816, MQ=3072, MGATE=3328, WIDTH=3584)
W_HQ, W_HF, W_HI, W_HGATE, W_FQ, W_FK, W_FV, W_FGATE, W_MQ, W_MGATE = (
    0, 512, 1024, 1536, 2048, 2304, 2560, 2816, 3072, 3328)


def _nt(a, b, precision=None):
    return lax.dot_general(a, b, (((1,), (1,)), ((), ())), precision=precision,
                           preferred_element_type=F32)


def _mm(a, b, precision=None):
    return jnp.dot(a, b, precision=precision, preferred_element_type=F32)


def _silu(x):
    return x * jax.nn.sigmoid(x)


def _log_sigmoid(x):
    return jnp.minimum(x, 0.0) - jnp.log(1.0 + jnp.exp(-jnp.abs(x)))


def _head_mask(hh):
    lane = lax.broadcasted_iota(jnp.int32, (1, LANES), 1)
    return (lane // FX_DIM) == hh


def _pair_lanes(h):
    return slice((h // 2) * LANES, (h // 2 + 1) * LANES)


def _split3(x):
    hi = x.astype(BF16).astype(F32)
    r = x - hi
    mid = r.astype(BF16).astype(F32)
    return hi, mid, r - mid


def _aug_operand(x_pair, d, hh, is_query):
    lane = lax.broadcasted_iota(jnp.int32, (1, LANES), 1)
    a = lane - (1 - hh) * FX_DIM
    hi, mid, lo = _split3(d if is_query else -d)
    p0, o0 = (0, AUG) if is_query else (AUG, 0)
    extra = jnp.where(a == p0, hi, jnp.where(a == p0 + 1, mid, jnp.where(a == p0 + 2, lo,
                      jnp.where((a >= o0) & (a < o0 + AUG), 1.0, 0.0))))
    return jnp.where(_head_mask(hh), x_pair, extra).astype(BF16)


def _proj_kernel(x_ref, w_ref, wfl_ref, wflt_ref, bfl_ref, bflt_ref, lbs_ref, *refs,
                 layer, seg, tiles_per_seq, prompt):
    act_ref = refs[0]
    cols = ACT_PROMPT if prompt else ACT_SAMPLE
    tm = x_ref.shape[0]
    xb = x_ref[...].astype(BF16)

    def proj(c0, width):
        return _mm(xb, w_ref[:, c0:c0 + width])

    lbs = lbs_ref[...]
    e = jnp.exp(lbs - jnp.max(lbs, axis=0, keepdims=True))
    sm = e / jnp.sum(e, axis=0, keepdims=True)
    lb = jnp.sum(sm[:layer + 1], axis=0, keepdims=True) - sm[0:1]

    lf_nat = _log_sigmoid(_mm(xb, wfl_ref[...]) + bfl_ref[...])
    lf_t = _log_sigmoid(_nt(wflt_ref[...], xb) + bflt_ref[...])

    r = lax.broadcasted_iota(jnp.int32, (tm, tm), 0)
    c = lax.broadcasted_iota(jnp.int32, (tm, tm), 1)
    same = (r // seg) == (c // seg)
    lower = jnp.where((c <= r) & same, 1.0, 0.0).astype(BF16)
    d_nat = _cumsum_rows([lf_nat], lower)[0]

    fq = proj(W_FQ, FX_WIDTH)
    k = proj(W_FK, FX_WIDTH)
    v = proj(W_FV, FX_WIDTH)
    if prompt:
        _, qa_ref, ka_ref, kt_ref, vt_ref, lft_ref, cn_ref, kv_ref = refs
        if tiles_per_seq > 1:
            @pl.when(pl.program_id(0) % tiles_per_seq == 0)
            def _():
                cn_ref[...] = jnp.zeros_like(cn_ref)
            d_nat = d_nat + cn_ref[0:1, :]
            cn_ref[...] = jnp.broadcast_to(d_nat[tm - 1:tm, :], cn_ref.shape)
        lft_ref[0] = lf_t[:FX_HEADS]
        kv_ref[:, 0:FX_WIDTH] = k
        kv_ref[:, FX_WIDTH:] = v
        kt_ref[0] = kv_ref[:, 0:FX_WIDTH].T
        vt_ref[0] = kv_ref[:, FX_WIDTH:].T
        d2 = d_nat * LOG2E
        for h in range(FX_HEADS):
            hl = slice(h * LANES, (h + 1) * LANES)
            qa_ref[:, hl] = _aug_operand(fq[:, _pair_lanes(h)] * (FX_DIM ** -0.5 * LOG2E),
                                         d2[:, h:h + 1], h % 2, True)
            ka_ref[:, hl] = _aug_operand(k[:, _pair_lanes(h)], d2[:, h:h + 1], h % 2, False)
    else:
        _, k_ref, v_ref, lft_ref, dn_ref, dt_ref = refs
        upper = jnp.where((r <= c) & same, 1.0, 0.0).astype(F32)
        dt_ref[...] = _mm(lf_t, upper, precision=lax.Precision.HIGHEST)
        lft_ref[...] = lf_t
        k_ref[...] = k
        v_ref[...] = v
        dn_ref[...] = d_nat
        act_ref[:, cols["FQ"]:cols["FQ"] + FX_WIDTH] = fq

    act_ref[:, C_HQ:C_HQ + HG_WIDTH] = _silu(proj(W_HQ, HG_WIDTH))
    forget = lb + (1.0 - lb) * jax.nn.sigmoid(proj(W_HF, HG_WIDTH))
    act_ref[:, C_HK:C_HK + HG_WIDTH] = 1.0 - forget
    act_ref[:, C_HG:C_HG + HG_WIDTH] = jnp.log2(forget)
    act_ref[:, C_HV:C_HV + HG_WIDTH] = proj(W_HI, HG_WIDTH)
    act_ref[:, C_HGATE:C_HGATE + HG_WIDTH] = _silu(proj(W_HGATE, HG_WIDTH))
    act_ref[:, cols["FGATE"]:cols["FGATE"] + FX_WIDTH] = _silu(proj(W_FGATE, FX_WIDTH))
    act_ref[:, cols["MQ"]:cols["MQ"] + MEM_WIDTH] = proj(W_MQ, MEM_WIDTH)
    act_ref[:, cols["MGATE"]:cols["MGATE"] + MEM_WIDTH] = _silu(proj(W_MGATE, MEM_WIDTH))


def _proj(x, w_main, w_fl, w_flt, b_fl, b_flt, lbs, *, layer, batch, seq_len, tm, prompt):
    m, d = x.shape
    ntiles = m // tm
    seg = min(seq_len, tm)
    tps = max(seq_len // tm, 1)
    assert prompt or ntiles == 1
    kern = functools.partial(_proj_kernel, layer=layer, seg=seg, tiles_per_seq=tps, prompt=prompt)
    full = lambda shape: pl.BlockSpec(shape, lambda i: (0,) * len(shape))
    rows = lambda width: pl.BlockSpec((tm, width), lambda i: (i, 0))
    feat = lambda nfeat: pl.BlockSpec((1, nfeat, tm), lambda i: (i // tps, 0, i % tps))
    if prompt:
        out_specs = [rows(ACT_PROMPT["WIDTH"]), rows(FX_HEADS * LANES), rows(FX_HEADS * LANES),
                     feat(FX_WIDTH), feat(FX_WIDTH), feat(FX_HEADS)]
        out_shape = [jax.ShapeDtypeStruct((m, ACT_PROMPT["WIDTH"]), F32),
                     jax.ShapeDtypeStruct((m, FX_HEADS * LANES), BF16),
                     jax.ShapeDtypeStruct((m, FX_HEADS * LANES), BF16),
                     jax.ShapeDtypeStruct((batch, FX_WIDTH, seq_len), F32),
                     jax.ShapeDtypeStruct((batch, FX_WIDTH, seq_len), F32),
                     jax.ShapeDtypeStruct((batch, FX_HEADS, seq_len), F32)]
        scratch = [pltpu.VMEM((SUBLANES, LANES), F32), pltpu.VMEM((tm, 2 * FX_WIDTH), F32)]
    else:
        out_specs = [rows(ACT_SAMPLE["WIDTH"]), rows(FX_WIDTH), rows(FX_WIDTH), full((SUBLANES, m)),
                     rows(LANES), full((SUBLANES, m))]
        out_shape = [jax.ShapeDtypeStruct((m, ACT_SAMPLE["WIDTH"]), F32),
                     jax.ShapeDtypeStruct((m, FX_WIDTH), F32),
                     jax.ShapeDtypeStruct((m, FX_WIDTH), F32),
                     jax.ShapeDtypeStruct((SUBLANES, m), F32),
                     jax.ShapeDtypeStruct((m, LANES), F32),
                     jax.ShapeDtypeStruct((SUBLANES, m), F32)]
        scratch = []
    return pl.pallas_call(
        kern,
        grid=(ntiles,),
        in_specs=[
            pl.BlockSpec((tm, d), lambda i: (i, 0)),
            full(w_main.shape), full(w_fl.shape), full(w_flt.shape),
            full(b_fl.shape), full(b_flt.shape), full(lbs.shape),
        ],
        out_specs=out_specs,
        out_shape=out_shape,
        scratch_shapes=scratch,
        compiler_params=pltpu.CompilerParams(
            dimension_semantics=("arbitrary",), vmem_limit_bytes=VMEM_LIMIT),
        name="proj_prompt" if prompt else "proj_sample",
    )(x, w_main, w_fl, w_flt, b_fl, b_flt, lbs)


def _half_ref(gc, m, rows):
    n, w = gc.shape
    if 2 * m >= SUBLANES:
        g3 = gc.reshape(n // (2 * m), 2 * m, w)
        return jnp.broadcast_to(g3[:, m - 1:m, :], g3.shape).reshape(n, w)
    pos = rows % (2 * m)
    if m == 2:
        return jnp.where(pos == 0, pltpu.roll(gc, n - 1, axis=0),
                         jnp.where(pos == 1, gc,
                                   jnp.where(pos == 2, pltpu.roll(gc, 1, axis=0),
                                             pltpu.roll(gc, 2, axis=0))))
    return jnp.where(pos == 0, gc, pltpu.roll(gc, 1, axis=0))


def _level_exponent(g2, m, rows):
    n = g2.shape[0]
    if m >= SUBLANES:
        pieces = []
        for r0 in range(0, n, 2 * m):
            first, second = g2[r0:r0 + m], g2[r0 + m:r0 + 2 * m]
            ref = first[m - 1:m]
            pieces += [ref - first, second - ref]
        return jnp.concatenate(pieces, axis=0)
    ref = _half_ref(g2, m, rows)
    return jnp.where((rows % (2 * m)) < m, ref - g2, g2 - ref)


def _pair_levels(n):
    rr = lax.broadcasted_iota(jnp.int32, (n, n), 0)
    cc = lax.broadcasted_iota(jnp.int32, (n, n), 1)
    xf = (rr ^ cc).astype(F32)
    level = (lax.bitcast_convert_type(xf, jnp.int32) >> 23) - 127
    return jnp.where(rr > cc, level, -1)


def _cumsum_rows(gs, lower):
    w = gs[0].shape[1]
    pieces = [p.astype(BF16) for g in gs for p in _split3(g)]
    s = _mm(lower, jnp.concatenate(pieces, axis=1))
    return [s[:, (3 * i) * w:(3 * i + 1) * w] + s[:, (3 * i + 1) * w:(3 * i + 2) * w]
            + s[:, (3 * i + 2) * w:(3 * i + 3) * w] for i in range(len(gs))]


def _hgrn_intra(q, k, g2, levels):
    n = q.shape[0]
    rows = lax.broadcasted_iota(jnp.int32, (n, HG_DIM), 0)
    a = jnp.zeros((n, n), F32)
    m, bit = n // 2, n.bit_length() - 2
    while m >= 1:
        f = jnp.exp2(_level_exponent(g2, m, rows))
        a_m = _nt((q * f).astype(BF16), (k * f).astype(BF16))
        a = jnp.where(levels == bit, a_m, a)
        m, bit = m // 2, bit - 1
    return a


def _hgrn_kernel(*refs, group, has_init):
    if has_init:
        q_ref, k_ref, v_ref, g_ref, gate_ref, ng_ref, s0_ref, o_ref, so_ref = refs[:9]
    else:
        q_ref, k_ref, v_ref, g_ref, gate_ref, ng_ref, o_ref, so_ref = refs[:8]
    st_ref, a_ref, qe_ref, ke_ref, vb_ref, dg_ref, dec_ref = refs[-7:]
    t = pl.program_id(2)
    nheads = st_ref.shape[0]
    ngroups = q_ref.shape[0] // group
    lanes = [slice(hh * HG_DIM, (hh + 1) * HG_DIM) for hh in range(nheads)]

    @pl.when(t == 0)
    def _():
        for hh in range(nheads):
            st_ref[hh] = s0_ref[0, 0, hh].T if has_init else jnp.zeros((HG_DIM, HG_DIM), F32)

    ng = ng_ref[...]
    levels = _pair_levels(group)
    rr = lax.broadcasted_iota(jnp.int32, (group, group), 0)
    cc = lax.broadcasted_iota(jnp.int32, (group, group), 1)
    lower = jnp.where(rr >= cc, 1.0, 0.0).astype(BF16)

    def prepare(i, s):
        sl = pl.ds(pl.multiple_of(i * group, group), group)
        g2s = _cumsum_rows([g_ref[sl, ls] for ls in lanes], lower)
        for hh, ls in enumerate(lanes):
            q, k, v, g2 = q_ref[sl, ls], k_ref[sl, ls], v_ref[sl, ls], g2s[hh]
            g_last = g2[group - 1:group, :]
            a_ref[s, hh] = _hgrn_intra(q, k, g2, levels).astype(BF16)
            qe_ref[s, hh] = (q * jnp.exp2(g2)).astype(BF16)
            ke_ref[s, hh] = (k * jnp.exp2(g_last - g2)).astype(BF16)
            vb_ref[s, hh] = v.astype(BF16)
            dg_ref[s, hh] = jnp.sum(q * k, axis=-1, keepdims=True) * v
            dec_ref[s, hh] = jnp.broadcast_to(jnp.exp2(g_last), (SUBLANES, HG_DIM))

    def finish(i, s):
        sl = pl.ds(pl.multiple_of(i * group, group), group)
        for hh, ls in enumerate(lanes):
            st = st_ref[hh]
            vb = vb_ref[s, hh]
            o = _nt(qe_ref[s, hh], st.astype(BF16)) + _mm(a_ref[s, hh], vb) + dg_ref[s, hh]
            upd = lax.dot_general(vb, ke_ref[s, hh], (((0,), (0,)), ((), ())), preferred_element_type=F32)
            st_ref[hh] = st * dec_ref[s, hh, 0:1, :] + upd
            o = o * lax.rsqrt(jnp.mean(o * o, axis=-1, keepdims=True) + RMS_EPS) * ng
            o_ref[sl, ls] = (o * gate_ref[sl, ls]).astype(o_ref.dtype)

    prepare(0, 0)
    if ngroups > 1:
        def step(ip, c):
            finish(2 * ip, 0)
            prepare(2 * ip + 1, 1)
            finish(2 * ip + 1, 1)
            prepare(2 * ip + 2, 0)
            return c

        lax.fori_loop(0, ngroups // 2 - 1, step, 0)
        finish(ngroups - 2, 0)
        prepare(ngroups - 1, 1)
        finish(ngroups - 1, 1)
    else:
        finish(0, 0)

    @pl.when(t == pl.num_programs(2) - 1)
    def _():
        for hh in range(nheads):
            so_ref[0, hh] = st_ref[hh].T


def _hgrn(act, norm_g, s0, *, layer, batch, seq_len, tt, nheads):
    m = act.shape[0]
    nt = seq_len // tt
    has_init = s0 is not None
    group = min(tt, HG_GROUP)
    assert tt == group or (tt // group) % 2 == 0, "groups are processed in pairs"
    width = nheads * HG_DIM
    blk = lambda cb: pl.BlockSpec((tt, width), lambda b, h, t, cb=cb: (b * nt + t, cb // width + h))
    in_specs = [blk(C_HQ), blk(C_HK), blk(C_HV), blk(C_HG), blk(C_HGATE),
                pl.BlockSpec((1, HG_DIM), lambda b, h, t: (0, 0))]
    args = [act, act, act, act, act, norm_g]
    if has_init:
        in_specs.append(pl.BlockSpec((1, 1, nheads, HG_DIM, HG_DIM), lambda b, h, t: (layer, b, h, 0, 0)))
        args.append(s0)
    return pl.pallas_call(
        functools.partial(_hgrn_kernel, group=group, has_init=has_init),
        grid=(batch, HG_HEADS // nheads, nt),
        in_specs=in_specs,
        out_specs=[
            pl.BlockSpec((tt, width), lambda b, h, t: (b * nt + t, h)),
            pl.BlockSpec((1, nheads, HG_DIM, HG_DIM), lambda b, h, t: (b, h, 0, 0)),
        ],
        out_shape=[
            jax.ShapeDtypeStruct((m, HG_WIDTH), BF16),
            jax.ShapeDtypeStruct((batch, HG_HEADS, HG_DIM, HG_DIM), F32),
        ],
        scratch_shapes=[
            pltpu.VMEM((nheads, HG_DIM, HG_DIM), F32),
            pltpu.VMEM((2, nheads, group, group), BF16),
            pltpu.VMEM((2, nheads, group, HG_DIM), BF16),
            pltpu.VMEM((2, nheads, group, HG_DIM), BF16),
            pltpu.VMEM((2, nheads, group, HG_DIM), BF16),
            pltpu.VMEM((2, nheads, group, HG_DIM), F32),
            pltpu.VMEM((2, nheads, SUBLANES, HG_DIM), F32),
        ],
        compiler_params=pltpu.CompilerParams(
            dimension_semantics=("arbitrary", "arbitrary", "arbitrary"), vmem_limit_bytes=VMEM_LIMIT),
        name="hgrn",
    )(*args)


VROWS = FX_DIM + 16


def _softmax_t(st, m):
    m_new = jnp.maximum(m, jnp.max(st, axis=0, keepdims=True))
    return m_new, jnp.exp2(m - m_new), jnp.exp2(st - m_new).astype(BF16)


def _with_ones(vt):
    return jnp.concatenate([vt.astype(BF16), jnp.ones((VROWS - FX_DIM, vt.shape[1]), BF16)], axis=0)


def _normalised(acc):
    return acc[0:FX_DIM] / acc[FX_DIM:FX_DIM + 1]


def _attn_prompt_kernel(qa_ref, fgate_ref, mq_ref, mgate_ref, ka_ref, vt_ref, mk_ref, mvt_ref,
                        o_ref, vtb_ref, mvtb_ref, st_ref, p_ref, acc_ref, *, tq):
    seq = qa_ref.shape[0]
    tk = tq
    ntiles = seq // tq
    heads = range(FX_HEADS)

    for j in range(seq // tk):
        for h in heads:
            vtb_ref[j, h] = _with_ones(vt_ref[0, h * FX_DIM:(h + 1) * FX_DIM, j * tk:(j + 1) * tk])
    for h in range(MEM_HEADS):
        mvtb_ref[h] = _with_ones(mvt_ref[0, h * MEM_DIM:(h + 1) * MEM_DIM, :])

    key = lax.broadcasted_iota(jnp.int32, (tk, tq), 0)
    qry = lax.broadcasted_iota(jnp.int32, (tk, tq), 1)

    def tile_rows(i):
        return pl.ds(pl.multiple_of(i * tq, tq), tq)

    def scores(i, j, h):
        hl = slice(h * LANES, (h + 1) * LANES)
        return _nt(ka_ref[tile_rows(j), hl], qa_ref[tile_rows(i), hl])

    def start(i, s):
        for h in heads:
            st_ref[s, h] = scores(i, 0, h)
            p_ref[s, h] = jnp.zeros((tk, tq), BF16)
            acc_ref[s, h] = jnp.zeros((VROWS, tq), F32)

    def finish(i, s, carry):
        rows = tile_rows(i)
        prev = jnp.maximum(i - 1, 0)
        o_heads = []
        for h in heads:
            m, alpha_prev = carry[h]
            acc = alpha_prev * acc_ref[s, h] + _mm(vtb_ref[prev, h], p_ref[s, h])
            m, alpha, p = _softmax_t(jnp.where(key <= qry, st_ref[s, h], -jnp.inf), m)
            acc = alpha * acc + _mm(vtb_ref[i, h], p)
            o_heads.append(_normalised(acc))
        for pr in range(FX_HEADS // 2):
            ls = _pair_lanes(2 * pr)
            o_t = jnp.concatenate(o_heads[2 * pr:2 * pr + 2], axis=0)
            o_ref[rows, ls] = (o_t.T * fgate_ref[rows, ls]).astype(o_ref.dtype)

        for pr in range(MEM_HEADS // 2):
            ls = _pair_lanes(2 * pr)
            mq_pair = mq_ref[rows, ls] * (MEM_DIM ** -0.5 * LOG2E)
            halves = []
            for hh in range(2):
                qm = jnp.where(_head_mask(hh), mq_pair, 0.0).astype(BF16)
                st = _nt(mk_ref[:, ls], qm)
                pm = jnp.exp2(st - jnp.max(st, axis=0, keepdims=True))
                halves.append(_normalised(_mm(mvtb_ref[2 * pr + hh], pm.astype(BF16))))
            o_t = jnp.concatenate(halves, axis=0)
            lm = slice(FX_WIDTH + pr * LANES, FX_WIDTH + (pr + 1) * LANES)
            o_ref[rows, lm] = (o_t.T * mgate_ref[rows, ls]).astype(o_ref.dtype)

    def qtile(i, s):
        def body(j, carry):
            prev = jnp.maximum(j - 1, 0)
            out = []
            for h in heads:
                m, alpha_prev = carry[h]
                pv = _mm(vtb_ref[prev, h], p_ref[s, h])
                st = st_ref[s, h]
                st_ref[s, h] = scores(i, j + 1, h)
                m, alpha, p = _softmax_t(st, m)
                acc_ref[s, h] = alpha_prev * acc_ref[s, h] + pv
                p_ref[s, h] = p
                out.append((m, alpha))
            return tuple(out)

        init = tuple((jnp.full((1, tq), -jnp.inf, F32), jnp.ones((1, tq), F32)) for _ in heads)
        carry = lax.fori_loop(0, i, body, init)
        finish(i, s, carry)
        start(jnp.minimum(i + 1, ntiles - 1), 1 - s)

    start(0, 0)

    def tile_pair(ip, c):
        qtile(2 * ip, 0)
        qtile(2 * ip + 1, 1)
        return c

    lax.fori_loop(0, ntiles // 2, tile_pair, 0)


def _attn_prompt(act, qa, ka, vt, mk, mvt, *, batch, seq_len, tq):
    m = act.shape[0]
    n_mem = mk.shape[0] // batch
    tk = tq
    assert (seq_len // tq) % 2 == 0, "query tiles are processed in pairs"
    ablk = lambda c0: pl.BlockSpec((seq_len, FX_WIDTH), lambda b, c0=c0: (b, c0 // FX_WIDTH))
    return pl.pallas_call(
        functools.partial(_attn_prompt_kernel, tq=tq),
        grid=(batch,),
        in_specs=[
            pl.BlockSpec((seq_len, FX_HEADS * LANES), lambda b: (b, 0)),
            ablk(ACT_PROMPT["FGATE"]), ablk(ACT_PROMPT["MQ"]), ablk(ACT_PROMPT["MGATE"]),
            pl.BlockSpec((seq_len, FX_HEADS * LANES), lambda b: (b, 0)),
            pl.BlockSpec((1, FX_WIDTH, seq_len), lambda b: (b, 0, 0)),
            pl.BlockSpec((n_mem, MEM_WIDTH), lambda b: (b, 0)),
            pl.BlockSpec((1, MEM_WIDTH, n_mem), lambda b: (b, 0, 0)),
        ],
        out_specs=pl.BlockSpec((seq_len, FX_WIDTH + MEM_WIDTH), lambda b: (b, 0)),
        out_shape=jax.ShapeDtypeStruct((m, FX_WIDTH + MEM_WIDTH), BF16),
        scratch_shapes=[
            pltpu.VMEM((seq_len // tk, FX_HEADS, VROWS, tk), BF16),
            pltpu.VMEM((MEM_HEADS, VROWS, n_mem), BF16),
            pltpu.VMEM((2, FX_HEADS, tk, tq), F32),
            pltpu.VMEM((2, FX_HEADS, tk, tq), BF16),
            pltpu.VMEM((2, FX_HEADS, VROWS, tq), F32),
        ],
        compiler_params=pltpu.CompilerParams(
            dimension_semantics=("arbitrary",), vmem_limit_bytes=VMEM_LIMIT),
        name="attn_prompt",
    )(qa, act, act, act, ka, vt, mk, mvt)


def _stack_heads(x, scale):
    lane = lax.broadcasted_iota(jnp.int32, (1, x.shape[1]), 1)
    return jnp.concatenate([jnp.where(lane // FX_DIM == h, x * scale, 0.0) for h in range(FX_HEADS)],
                           axis=0).astype(BF16)


def _unstack_heads(o, ts):
    lane = lax.broadcasted_iota(jnp.int32, (1, o.shape[1]), 1)
    out = jnp.zeros((ts, o.shape[1]), F32)
    for h in range(FX_HEADS):
        out = jnp.where(lane // FX_DIM == h, o[h * ts:(h + 1) * ts, :], out)
    return out


def _per_head_rows(vals, ts):
    return jnp.concatenate([jnp.broadcast_to(v, (ts, v.shape[1])) for v in vals], axis=0)


def _softmax_update(s, pv_fn, m, l, acc):
    m_new = jnp.maximum(m, jnp.max(s, axis=-1, keepdims=True))
    alpha = jnp.exp(m - m_new)
    p = jnp.exp(s - m_new)
    l = alpha * l + jnp.sum(p, axis=-1, keepdims=True)
    acc = alpha * acc + pv_fn(p.astype(BF16))
    return m_new, l, acc


def _attn_sample_kernel(fq_ref, fgate_ref, mq_ref, mgate_ref, kn_ref, vn_ref, dn_ref, dt_ref,
                        pkt_ref, pvt_ref, plf_ref, mkt_ref, mvt_ref, o_ref, *, chunk):
    ts = fq_ref.shape[0]
    n_new = kn_ref.shape[0]
    past = pkt_ref.shape[3]
    nchunks = past // chunk
    b = pl.program_id(0)
    hrows = FX_HEADS * ts

    r = lax.broadcasted_iota(jnp.int32, (chunk, chunk), 0)
    c = lax.broadcasted_iota(jnp.int32, (chunk, chunk), 1)
    upper = jnp.where(r <= c, 1.0, 0.0).astype(F32)
    total = jnp.zeros((FX_HEADS, 1), F32)
    d_past = []
    for ci in range(nchunks):
        d = _mm(plf_ref[0, 0, :, ci * chunk:(ci + 1) * chunk], upper,
                precision=lax.Precision.HIGHEST) + total
        d_past.append(d)
        total = d[:, chunk - 1:chunk]

    dn = dn_ref[...]
    dt_new = dt_ref[...]
    tot = [total[h:h + 1, :] for h in range(FX_HEADS)]
    dq = jnp.concatenate([tot[h] + dn[:, h:h + 1] for h in range(FX_HEADS)], axis=0)
    q4 = _stack_heads(fq_ref[...], FX_DIM ** -0.5)

    carry = (jnp.full((hrows, 1), -jnp.inf, F32), jnp.zeros((hrows, 1), F32),
             jnp.zeros((hrows, FX_WIDTH), F32))
    for ci in range(nchunks):
        cs = slice(ci * chunk, (ci + 1) * chunk)
        dk = _per_head_rows([d_past[ci][h:h + 1, :] for h in range(FX_HEADS)], ts)
        s = _mm(q4, pkt_ref[0, 0, :, cs].astype(BF16)) + dq - dk
        vt = pvt_ref[0, 0, :, cs].astype(BF16)
        carry = _softmax_update(s, lambda p, vt=vt: _nt(p, vt), *carry)

    row = lax.broadcasted_iota(jnp.int32, (hrows, n_new), 0)
    col = lax.broadcasted_iota(jnp.int32, (hrows, n_new), 1)
    visible = (col // ts == b) & (col % ts <= row % ts)
    dk = _per_head_rows([tot[h] + dt_new[h:h + 1, :] for h in range(FX_HEADS)], ts)
    s = _nt(q4, kn_ref[...].astype(BF16)) + dq - dk
    vn = vn_ref[...].astype(BF16)
    _, l, acc = _softmax_update(jnp.where(visible, s, -jnp.inf), lambda p: _mm(p, vn), *carry)
    o_fx = _unstack_heads(acc / l, ts) * fgate_ref[...]
    o_ref[:, 0:FX_WIDTH] = o_fx.astype(o_ref.dtype)

    mq4 = _stack_heads(mq_ref[...], MEM_DIM ** -0.5)
    s = _mm(mq4, mkt_ref[0, 0].astype(BF16))
    p = jnp.exp(s - jnp.max(s, axis=-1, keepdims=True))
    o = _nt(p.astype(BF16), mvt_ref[0, 0].astype(BF16)) / jnp.sum(p, axis=-1, keepdims=True)
    o_mem = _unstack_heads(o, ts) * mgate_ref[...]
    o_ref[:, FX_WIDTH:] = o_mem.astype(o_ref.dtype)


def _attn_sample(act, k_new, v_new, d_nat, d_t, past_kt, past_vt, past_lft, mem_kt, mem_vt,
                 *, layer, batch, ts):
    m = act.shape[0]
    past = past_kt.shape[3]
    n_mem = mem_kt.shape[3]
    chunk = min(past, 512)
    ablk = lambda c0: pl.BlockSpec((ts, FX_WIDTH), lambda b, c0=c0: (b, c0 // FX_WIDTH))
    cache = lambda nfeat, n: pl.BlockSpec((1, 1, nfeat, n), lambda b: (layer, b, 0, 0))
    return pl.pallas_call(
        functools.partial(_attn_sample_kernel, chunk=chunk),
        grid=(batch,),
        in_specs=[
            ablk(ACT_SAMPLE["FQ"]), ablk(ACT_SAMPLE["FGATE"]), ablk(ACT_SAMPLE["MQ"]),
            ablk(ACT_SAMPLE["MGATE"]),
            pl.BlockSpec((m, FX_WIDTH), lambda b: (0, 0)),
            pl.BlockSpec((m, FX_WIDTH), lambda b: (0, 0)),
            pl.BlockSpec((ts, LANES), lambda b: (b, 0)),
            pl.BlockSpec((SUBLANES, m), lambda b: (0, 0)),
            cache(FX_WIDTH, past), cache(FX_WIDTH, past), cache(FX_HEADS, past),
            cache(MEM_WIDTH, n_mem), cache(MEM_WIDTH, n_mem),
        ],
        out_specs=pl.BlockSpec((ts, FX_WIDTH + MEM_WIDTH), lambda b: (b, 0)),
        out_shape=jax.ShapeDtypeStruct((m, FX_WIDTH + MEM_WIDTH), BF16),
        compiler_params=pltpu.CompilerParams(
            dimension_semantics=("arbitrary",), vmem_limit_bytes=VMEM_LIMIT),
        name="attn_sample",
    )(act, act, act, act, k_new, v_new, d_nat, d_t, past_kt, past_vt, past_lft, mem_kt, mem_vt)


def _memkv_kernel(x_ref, w_ref, kb_ref, kt_ref, vt_ref):
    kv = _mm(x_ref[...].astype(BF16), w_ref[...])
    kb_ref[...] = kv[:, :MEM_WIDTH].astype(BF16)
    kt_ref[0] = kv[:, :MEM_WIDTH].T
    vt_ref[0] = kv[:, MEM_WIDTH:].T


def _memkv(mem, w, *, batch):
    m, d = mem.shape
    n_mem = m // batch
    return pl.pallas_call(
        _memkv_kernel,
        grid=(batch,),
        in_specs=[pl.BlockSpec((n_mem, d), lambda i: (i, 0)),
                  pl.BlockSpec(w.shape, lambda i: (0, 0))],
        out_specs=[pl.BlockSpec((n_mem, MEM_WIDTH), lambda i: (i, 0)),
                   pl.BlockSpec((1, MEM_WIDTH, n_mem), lambda i: (i, 0, 0)),
                   pl.BlockSpec((1, MEM_WIDTH, n_mem), lambda i: (i, 0, 0))],
        out_shape=[jax.ShapeDtypeStruct((m, MEM_WIDTH), BF16),
                   jax.ShapeDtypeStruct((batch, MEM_WIDTH, n_mem), F32),
                   jax.ShapeDtypeStruct((batch, MEM_WIDTH, n_mem), F32)],
        compiler_params=pltpu.CompilerParams(
            dimension_semantics=("arbitrary",), vmem_limit_bytes=VMEM_LIMIT),
        name="memkv",
    )(mem, w)


def _out_kernel(x_ref, hg_ref, at_ref, w_ref, g_ref, b_ref, y_ref, *, alpha):
    acc = _mm(hg_ref[...], w_ref[0:HG_WIDTH, :]) + _mm(at_ref[...], w_ref[HG_WIDTH:, :])
    z = alpha * x_ref[...] + acc
    mu = jnp.mean(z, axis=-1, keepdims=True)
    zc = z - mu
    var = jnp.mean(zc * zc, axis=-1, keepdims=True)
    y_ref[...] = zc * lax.rsqrt(var + LN_EPS) * g_ref[...] + b_ref[...]


def _out(x, mix_hg, mix_at, w, g, b, *, alpha, tm):
    m, d = x.shape
    return pl.pallas_call(
        functools.partial(_out_kernel, alpha=alpha),
        grid=(m // tm,),
        in_specs=[
            pl.BlockSpec((tm, d), lambda i: (i, 0)),
            pl.BlockSpec((tm, HG_WIDTH), lambda i: (i, 0)),
            pl.BlockSpec((tm, FX_WIDTH + MEM_WIDTH), lambda i: (i, 0)),
            pl.BlockSpec(w.shape, lambda i: (0, 0)),
            pl.BlockSpec((1, d), lambda i: (0, 0)),
            pl.BlockSpec((1, d), lambda i: (0, 0)),
        ],
        out_specs=pl.BlockSpec((tm, d), lambda i: (i, 0)),
        out_shape=jax.ShapeDtypeStruct((m, d), F32),
        compiler_params=pltpu.CompilerParams(
            dimension_semantics=("arbitrary",), vmem_limit_bytes=VMEM_LIMIT),
        name="out",
    )(x, mix_hg, mix_at, w, g, b)


def _pick_tile(n, target):
    t = min(n, target)
    while n % t:
        t //= 2
    return t


def _feature_major(x):
    lead = x.shape[:-3]
    n, heads, dim = x.shape[-3:]
    nl = len(lead)
    perm = tuple(range(nl)) + (nl + 1, nl + 2, nl)
    return jnp.transpose(x, perm).reshape(lead + (heads * dim, n))


def _token_major(x, heads):
    lead = x.shape[:-2]
    width, n = x.shape[-2:]
    nl = len(lead)
    x = x.reshape(lead + (heads, width // heads, n))
    return jnp.transpose(x, tuple(range(nl)) + (nl + 2, nl, nl + 1))


def kernel(x_prompt, x_sample, mem_prompt, state_hgrn, cache_fox_k, cache_fox_v, cache_fox_logf,
           cache_mem_k, cache_mem_v, w_in, b_fox_forget, hgrn_lower_bounds, hgrn_norm_g,
           w_mem_kv, w_out, ln_g, ln_b):
    depth, d_model, _ = w_in.shape
    bp, seq, _ = x_prompt.shape
    bs, dseq, _ = x_sample.shape
    n_mem = mem_prompt.shape[1]
    alpha = (2 * depth) ** 0.25

    fl0 = 4 * HG_WIDTH + 4 * FX_WIDTH
    w_main = jnp.concatenate([w_in[:, :, :fl0], w_in[:, :, fl0 + FX_HEADS:]], axis=-1).astype(BF16)
    w_flc = w_in[:, :, fl0:fl0 + FX_HEADS]
    w_fl = jnp.pad(w_flc, ((0, 0), (0, 0), (0, LANES - FX_HEADS))).astype(BF16)
    w_flt = jnp.pad(jnp.swapaxes(w_flc, 1, 2), ((0, 0), (0, SUBLANES - FX_HEADS), (0, 0))).astype(BF16)
    b_fl = jnp.pad(b_fox_forget, ((0, 0), (0, LANES - FX_HEADS)))[:, None, :]
    b_flt = jnp.pad(b_fox_forget, ((0, 0), (0, SUBLANES - FX_HEADS)))[:, :, None]
    lbs = hgrn_lower_bounds.astype(F32)
    w_memb = w_mem_kv.astype(BF16)
    w_outb = w_out.astype(BF16)

    past_kt = _feature_major(cache_fox_k)
    past_vt = _feature_major(cache_fox_v)
    past_lft = jnp.swapaxes(cache_fox_logf, 2, 3)
    mem_kt = _feature_major(cache_mem_k)
    mem_vt = _feature_major(cache_mem_v)

    mp, ms = bp * seq, bs * dseq
    tb = _pick_tile(seq, 256)
    yp = x_prompt.reshape(mp, d_model)
    ys = x_sample.reshape(ms, d_model)
    mem2 = mem_prompt.reshape(bp * n_mem, d_model)

    p_hg, p_kt, p_vt, p_lft, p_mkt, p_mvt = [], [], [], [], [], []
    s_hg, s_k, s_v, s_lft = [], [], [], []
    for l in range(depth):
        wl = (w_main[l], w_fl[l], w_flt[l], b_fl[l], b_flt[l], lbs)
        ng = hgrn_norm_g[l][None, :]
        g_ln, b_ln = ln_g[l][None, :], ln_b[l][None, :]

        mkb, mkt, mvt = _memkv(mem2, w_memb[l], batch=bp)
        act, qa, ka, kt, vt, lft = _proj(yp, *wl, layer=l, batch=bp, seq_len=seq,
                                         tm=_pick_tile(seq, 512), prompt=True)
        mix_hg, hg = _hgrn(act, ng, None, layer=l, batch=bp, seq_len=seq, tt=_pick_tile(seq, 2048),
                           nheads=2)
        mix_at = _attn_prompt(act, qa, ka, vt, mkb, mvt, batch=bp, seq_len=seq, tq=tb)
        yp = _out(yp, mix_hg, mix_at, w_outb[l], g_ln, b_ln, alpha=alpha, tm=_pick_tile(mp, 1024))
        p_hg.append(hg); p_kt.append(kt); p_vt.append(vt); p_lft.append(lft)
        p_mkt.append(mkt); p_mvt.append(mvt)

        act, k, v, lft, d_nat, d_t = _proj(ys, *wl, layer=l, batch=bs, seq_len=dseq, tm=ms, prompt=False)
        mix_hg, hg = _hgrn(act, ng, state_hgrn, layer=l, batch=bs, seq_len=dseq, tt=dseq, nheads=HG_HEADS)
        mix_at = _attn_sample(act, k, v, d_nat, d_t, past_kt, past_vt, past_lft, mem_kt, mem_vt,
                              layer=l, batch=bs, ts=dseq)
        ys = _out(ys, mix_hg, mix_at, w_outb[l], g_ln, b_ln, alpha=alpha, tm=ms)
        s_hg.append(hg); s_k.append(k); s_v.append(v); s_lft.append(lft)

    st = jnp.stack
    s_lf = st(s_lft)[:, :FX_HEADS, :].reshape(depth, FX_HEADS, bs, dseq)
    return (yp.reshape(bp, seq, d_model), ys.reshape(bs, dseq, d_model),
            st(p_hg),
            _token_major(st(p_kt), FX_HEADS),
            _token_major(st(p_vt), FX_HEADS),
            jnp.swapaxes(st(p_lft), 2, 3),
            _token_major(st(p_mkt), MEM_HEADS),
            _token_major(st(p_mvt), MEM_HEADS),
            st(s_hg),
            st(s_k).reshape(depth, bs, dseq, FX_HEADS, FX_DIM),
            st(s_v).reshape(depth, bs, dseq, FX_HEADS, FX_DIM),
            jnp.transpose(s_lf, (0, 2, 3, 1)))
```

```python
import functools

import jax
import jax.numpy as jnp
from jax import lax
from jax.experimental import pallas as pl
from jax.experimental.pallas import tpu as pltpu

F32 = jnp.float32
BF16 = jnp.bfloat16

HG_HEADS = 4
HG_DIM = 128
HG_WIDTH = HG_HEADS * HG_DIM
FX_HEADS = 4
FX_DIM = 64
FX_WIDTH = FX_HEADS * FX_DIM
MEM_HEADS = 4
MEM_DIM = 64
MEM_WIDTH = MEM_HEADS * MEM_DIM
HG_GROUP = 128
LN_EPS = 1e-5
RMS_EPS = 1e-6
LANES = 128
SUBLANES = 8
VMEM_LIMIT = 56 * 1024 * 1024
AUG = 3
LOG2E = 1.4426950408889634

C_HQ, C_HK, C_HV, C_HG, C_HGATE = 0, 512, 1024, 1536, 2048
ACT_PROMPT = dict(FGATE=2560, MQ=2816, MGATE=3072, WIDTH=3328)
ACT_SAMPLE = dict(FQ=2560, FGATE=2816, MQ=3072, MGATE=3328, WIDTH=3584)
W_HQ, W_HF, W_HI, W_HGATE, W_FQ, W_FK, W_FV, W_FGATE, W_MQ, W_MGATE = (
    0, 512, 1024, 1536, 2048, 2304, 2560, 2816, 3072, 3328)


def _nt(a, b, precision=None):
    return lax.dot_general(a, b, (((1,), (1,)), ((), ())), precision=precision,
                           preferred_element_type=F32)


def _mm(a, b, precision=None):
    return jnp.dot(a, b, precision=precision, preferred_element_type=F32)


def _silu(x):
    return x * jax.nn.sigmoid(x)


def _log_sigmoid(x):
    return jnp.minimum(x, 0.0) - jnp.log(1.0 + jnp.exp(-jnp.abs(x)))


def _head_mask(hh):
    lane = lax.broadcasted_iota(jnp.int32, (1, LANES), 1)
    return (lane // FX_DIM) == hh


def _pair_lanes(h):
    return slice((h // 2) * LANES, (h // 2 + 1) * LANES)


def _split3(x):
    hi = x.astype(BF16).astype(F32)
    r = x - hi
    mid = r.astype(BF16).astype(F32)
    return hi, mid, r - mid


def _aug_operand(x_pair, d, hh, is_query):
    lane = lax.broadcasted_iota(jnp.int32, (1, LANES), 1)
    a = lane - (1 - hh) * FX_DIM
    hi, mid, lo = _split3(d if is_query else -d)
    p0, o0 = (0, AUG) if is_query else (AUG, 0)
    extra = jnp.where(a == p0, hi, jnp.where(a == p0 + 1, mid, jnp.where(a == p0 + 2, lo,
                      jnp.where((a >= o0) & (a < o0 + AUG), 1.0, 0.0))))
    return jnp.where(_head_mask(hh), x_pair, extra).astype(BF16)


def _proj_kernel(x_ref, w_ref, wfl_ref, wflt_ref, bfl_ref, bflt_ref, lbs_ref, *refs,
                 layer, seg, tiles_per_seq, prompt):
    act_ref = refs[0]
    cols = ACT_PROMPT if prompt else ACT_SAMPLE
    tm = x_ref.shape[0]
    xb = x_ref[...].astype(BF16)

    def proj(c0, width):
        return _mm(xb, w_ref[:, c0:c0 + width])

    lbs = lbs_ref[...]
    e = jnp.exp(lbs - jnp.max(lbs, axis=0, keepdims=True))
    sm = e / jnp.sum(e, axis=0, keepdims=True)
    lb = jnp.sum(sm[:layer + 1], axis=0, keepdims=True) - sm[0:1]

    lf_nat = _log_sigmoid(_mm(xb, wfl_ref[...]) + bfl_ref[...])
    lf_t = _log_sigmoid(_nt(wflt_ref[...], xb) + bflt_ref[...])

    r = lax.broadcasted_iota(jnp.int32, (tm, tm), 0)
    c = lax.broadcasted_iota(jnp.int32, (tm, tm), 1)
    same = (r // seg) == (c // seg)
    lower = jnp.where((c <= r) & same, 1.0, 0.0).astype(BF16)
    d_nat = _cumsum_rows([lf_nat], lower)[0]

    fq = proj(W_FQ, FX_WIDTH)
    k = proj(W_FK, FX_WIDTH)
    v = proj(W_FV, FX_WIDTH)
    if prompt:
        _, qa_ref, ka_ref, kt_ref, vt_ref, lft_ref, cn_ref, kv_ref = refs
        if tiles_per_seq > 1:
            @pl.when(pl.program_id(0) % tiles_per_seq == 0)
            def _():
                cn_ref[...] = jnp.zeros_like(cn_ref)
            d_nat = d_nat + cn_ref[0:1, :]
            cn_ref[...] = jnp.broadcast_to(d_nat[tm - 1:tm, :], cn_ref.shape)
        lft_ref[0] = lf_t[:FX_HEADS]
        kv_ref[:, 0:FX_WIDTH] = k
        kv_ref[:, FX_WIDTH:] = v
        kt_ref[0] = kv_ref[:, 0:FX_WIDTH].T
        vt_ref[0] = kv_ref[:, FX_WIDTH:].T
        d2 = d_nat * LOG2E
        for h in range(FX_HEADS):
            hl = slice(h * LANES, (h + 1) * LANES)
            qa_ref[:, hl] = _aug_operand(fq[:, _pair_lanes(h)] * (FX_DIM ** -0.5 * LOG2E),
                                         d2[:, h:h + 1], h % 2, True)
            ka_ref[:, hl] = _aug_operand(k[:, _pair_lanes(h)], d2[:, h:h + 1], h % 2, False)
    else:
        _, k_ref, v_ref, lft_ref, dn_ref, dt_ref = refs
        upper = jnp.where((r <= c) & same, 1.0, 0.0).astype(F32)
        dt_ref[...] = _mm(lf_t, upper, precision=lax.Precision.HIGHEST)
        lft_ref[...] = lf_t
        k_ref[...] = k
        v_ref[...] = v
        dn_ref[...] = d_nat
        act_ref[:, cols["FQ"]:cols["FQ"] + FX_WIDTH] = fq

    act_ref[:, C_HQ:C_HQ + HG_WIDTH] = _silu(proj(W_HQ, HG_WIDTH))
    forget = lb + (1.0 - lb) * jax.nn.sigmoid(proj(W_HF, HG_WIDTH))
    act_ref[:, C_HK:C_HK + HG_WIDTH] = 1.0 - forget
    act_ref[:, C_HG:C_HG + HG_WIDTH] = jnp.log2(forget)
    act_ref[:, C_HV:C_HV + HG_WIDTH] = proj(W_HI, HG_WIDTH)
    act_ref[:, C_HGATE:C_HGATE + HG_WIDTH] = _silu(proj(W_HGATE, HG_WIDTH))
    act_ref[:, cols["FGATE"]:cols["FGATE"] + FX_WIDTH] = _silu(proj(W_FGATE, FX_WIDTH))
    act_ref[:, cols["MQ"]:cols["MQ"] + MEM_WIDTH] = proj(W_MQ, MEM_WIDTH)
    act_ref[:, cols["MGATE"]:cols["MGATE"] + MEM_WIDTH] = _silu(proj(W_MGATE, MEM_WIDTH))


def _proj(x, w_main, w_fl, w_flt, b_fl, b_flt, lbs, *, layer, batch, seq_len, tm, prompt):
    m, d = x.shape
    ntiles = m // tm
    seg = min(seq_len, tm)
    tps = max(seq_len // tm, 1)
    assert prompt or ntiles == 1
    kern = functools.partial(_proj_kernel, layer=layer, seg=seg, tiles_per_seq=tps, prompt=prompt)
    full = lambda shape: pl.BlockSpec(shape, lambda i: (0,) * len(shape))
    rows = lambda width: pl.BlockSpec((tm, width), lambda i: (i, 0))
    feat = lambda nfeat: pl.BlockSpec((1, nfeat, tm), lambda i: (i // tps, 0, i % tps))
    if prompt:
        out_specs = [rows(ACT_PROMPT["WIDTH"]), rows(FX_HEADS * LANES), rows(FX_HEADS * LANES),
                     feat(FX_WIDTH), feat(FX_WIDTH), feat(FX_HEADS)]
        out_shape = [jax.ShapeDtypeStruct((m, ACT_PROMPT["WIDTH"]), F32),
                     jax.ShapeDtypeStruct((m, FX_HEADS * LANES), BF16),
                     jax.ShapeDtypeStruct((m, FX_HEADS * LANES), BF16),
                     jax.ShapeDtypeStruct((batch, FX_WIDTH, seq_len), F32),
                     jax.ShapeDtypeStruct((batch, FX_WIDTH, seq_len), F32),
                     jax.ShapeDtypeStruct((batch, FX_HEADS, seq_len), F32)]
        scratch = [pltpu.VMEM((SUBLANES, LANES), F32), pltpu.VMEM((tm, 2 * FX_WIDTH), F32)]
    else:
        out_specs = [rows(ACT_SAMPLE["WIDTH"]), rows(FX_WIDTH), rows(FX_WIDTH), full((SUBLANES, m)),
                     rows(LANES), full((SUBLANES, m))]
        out_shape = [jax.ShapeDtypeStruct((m, ACT_SAMPLE["WIDTH"]), F32),
                     jax.ShapeDtypeStruct((m, FX_WIDTH), F32),
                     jax.ShapeDtypeStruct((m, FX_WIDTH), F32),
                     jax.ShapeDtypeStruct((SUBLANES, m), F32),
                     jax.ShapeDtypeStruct((m, LANES), F32),
                     jax.ShapeDtypeStruct((SUBLANES, m), F32)]
        scratch = []
    return pl.pallas_call(
        kern,
        grid=(ntiles,),
        in_specs=[
            pl.BlockSpec((tm, d), lambda i: (i, 0)),
            full(w_main.shape), full(w_fl.shape), full(w_flt.shape),
            full(b_fl.shape), full(b_flt.shape), full(lbs.shape),
        ],
        out_specs=out_specs,
        out_shape=out_shape,
        scratch_shapes=scratch,
        compiler_params=pltpu.CompilerParams(
            dimension_semantics=("arbitrary",), vmem_limit_bytes=VMEM_LIMIT),
        name="proj_prompt" if prompt else "proj_sample",
    )(x, w_main, w_fl, w_flt, b_fl, b_flt, lbs)


def _half_ref(gc, m, rows):
    n, w = gc.shape
    if 2 * m >= SUBLANES:
        g3 = gc.reshape(n // (2 * m), 2 * m, w)
        return jnp.broadcast_to(g3[:, m - 1:m, :], g3.shape).reshape(n, w)
    pos = rows % (2 * m)
    if m == 2:
        return jnp.where(pos == 0, pltpu.roll(gc, n - 1, axis=0),
                         jnp.where(pos == 1, gc,
                                   jnp.where(pos == 2, pltpu.roll(gc, 1, axis=0),
                                             pltpu.roll(gc, 2, axis=0))))
    return jnp.where(pos == 0, gc, pltpu.roll(gc, 1, axis=0))


def _level_exponent(g2, m, rows):
    n = g2.shape[0]
    if m >= SUBLANES:
        pieces = []
        for r0 in range(0, n, 2 * m):
            first, second = g2[r0:r0 + m], g2[r0 + m:r0 + 2 * m]
            ref = first[m - 1:m]
            pieces += [ref - first, second - ref]
        return jnp.concatenate(pieces, axis=0)
    ref = _half_ref(g2, m, rows)
    return jnp.where((rows % (2 * m)) < m, ref - g2, g2 - ref)


def _pair_levels(n):
    rr = lax.broadcasted_iota(jnp.int32, (n, n), 0)
    cc = lax.broadcasted_iota(jnp.int32, (n, n), 1)
    xf = (rr ^ cc).astype(F32)
    level = (lax.bitcast_convert_type(xf, jnp.int32) >> 23) - 127
    return jnp.where(rr > cc, level, -1)


def _cumsum_rows(gs, lower):
    w = gs[0].shape[1]
    pieces = [p.astype(BF16) for g in gs for p in _split3(g)]
    s = _mm(lower, jnp.concatenate(pieces, axis=1))
    return [s[:, (3 * i) * w:(3 * i + 1) * w] + s[:, (3 * i + 1) * w:(3 * i + 2) * w]
            + s[:, (3 * i + 2) * w:(3 * i + 3) * w] for i in range(len(gs))]


def _hgrn_intra(q, k, g2, levels):
    n = q.shape[0]
    rows = lax.broadcasted_iota(jnp.int32, (n, HG_DIM), 0)
    a = jnp.zeros((n, n), F32)
    m, bit = n // 2, n.bit_length() - 2
    while m >= 1:
        f = jnp.exp2(_level_exponent(g2, m, rows))
        a_m = _nt((q * f).astype(BF16), (k * f).astype(BF16))
        a = jnp.where(levels == bit, a_m, a)
        m, bit = m // 2, bit - 1
    return a


def _hgrn_kernel(*refs, group, has_init):
    if has_init:
        q_ref, k_ref, v_ref, g_ref, gate_ref, ng_ref, s0_ref, o_ref, so_ref = refs[:9]
    else:
        q_ref, k_ref, v_ref, g_ref, gate_ref, ng_ref, o_ref, so_ref = refs[:8]
    st_ref, a_ref, qe_ref, ke_ref, vb_ref, dg_ref, dec_ref = refs[-7:]
    t = pl.program_id(2)
    nheads = st_ref.shape[0]
    ngroups = q_ref.shape[0] // group
    lanes = [slice(hh * HG_DIM, (hh + 1) * HG_DIM) for hh in range(nheads)]

    @pl.when(t == 0)
    def _():
        for hh in range(nheads):
            st_ref[hh] = s0_ref[0, 0, hh].T if has_init else jnp.zeros((HG_DIM, HG_DIM), F32)

    ng = ng_ref[...]
    levels = _pair_levels(group)
    rr = lax.broadcasted_iota(jnp.int32, (group, group), 0)
    cc = lax.broadcasted_iota(jnp.int32, (group, group), 1)
    lower = jnp.where(rr >= cc, 1.0, 0.0).astype(BF16)

    def prepare(i, s):
        sl = pl.ds(pl.multiple_of(i * group, group), group)
        g2s = _cumsum_rows([g_ref[sl, ls] for ls in lanes], lower)
        for hh, ls in enumerate(lanes):
            q, k, v, g2 = q_ref[sl, ls], k_ref[sl, ls], v_ref[sl, ls], g2s[hh]
            g_last = g2[group - 1:group, :]
            a_ref[s, hh] = _hgrn_intra(q, k, g2, levels).astype(BF16)
            qe_ref[s, hh] = (q * jnp.exp2(g2)).astype(BF16)
            ke_ref[s, hh] = (k * jnp.exp2(g_last - g2)).astype(BF16)
            vb_ref[s, hh] = v.astype(BF16)
            dg_ref[s, hh] = jnp.sum(q * k, axis=-1, keepdims=True) * v
            dec_ref[s, hh] = jnp.broadcast_to(jnp.exp2(g_last), (SUBLANES, HG_DIM))

    def finish(i, s):
        sl = pl.ds(pl.multiple_of(i * group, group), group)
        for hh, ls in enumerate(lanes):
            st = st_ref[hh]
            vb = vb_ref[s, hh]
            o = _nt(qe_ref[s, hh], st.astype(BF16)) + _mm(a_ref[s, hh], vb) + dg_ref[s, hh]
            upd = lax.dot_general(vb, ke_ref[s, hh], (((0,), (0,)), ((), ())), preferred_element_type=F32)
            st_ref[hh] = st * dec_ref[s, hh, 0:1, :] + upd
            o = o * lax.rsqrt(jnp.mean(o * o, axis=-1, keepdims=True) + RMS_EPS) * ng
            o_ref[sl, ls] = (o * gate_ref[sl, ls]).astype(o_ref.dtype)

    prepare(0, 0)
    if ngroups > 1:
        def step(ip, c):
            finish(2 * ip, 0)
            prepare(2 * ip + 1, 1)
            finish(2 * ip + 1, 1)
            prepare(2 * ip + 2, 0)
            return c

        lax.fori_loop(0, ngroups // 2 - 1, step, 0)
        finish(ngroups - 2, 0)
        prepare(ngroups - 1, 1)
        finish(ngroups - 1, 1)
    else:
        finish(0, 0)

    @pl.when(t == pl.num_programs(2) - 1)
    def _():
        for hh in range(nheads):
            so_ref[0, hh] = st_ref[hh].T


def _hgrn(act, norm_g, s0, *, layer, batch, seq_len, tt, nheads):
    m = act.shape[0]
    nt = seq_len // tt
    has_init = s0 is not None
    group = min(tt, HG_GROUP)
    assert tt == group or (tt // group) % 2 == 0, "groups are processed in pairs"
    width = nheads * HG_DIM
    blk = lambda cb: pl.BlockSpec((tt, width), lambda b, h, t, cb=cb: (b * nt + t, cb // width + h))
    in_specs = [blk(C_HQ), blk(C_HK), blk(C_HV), blk(C_HG), blk(C_HGATE),
                pl.BlockSpec((1, HG_DIM), lambda b, h, t: (0, 0))]
    args = [act, act, act, act, act, norm_g]
    if has_init:
        in_specs.append(pl.BlockSpec((1, 1, nheads, HG_DIM, HG_DIM), lambda b, h, t: (layer, b, h, 0, 0)))
        args.append(s0)
    return pl.pallas_call(
        functools.partial(_hgrn_kernel, group=group, has_init=has_init),
        grid=(batch, HG_HEADS // nheads, nt),
        in_specs=in_specs,
        out_specs=[
            pl.BlockSpec((tt, width), lambda b, h, t: (b * nt + t, h)),
            pl.BlockSpec((1, nheads, HG_DIM, HG_DIM), lambda b, h, t: (b, h, 0, 0)),
        ],
        out_shape=[
            jax.ShapeDtypeStruct((m, HG_WIDTH), BF16),
            jax.ShapeDtypeStruct((batch, HG_HEADS, HG_DIM, HG_DIM), F32),
        ],
        scratch_shapes=[
            pltpu.VMEM((nheads, HG_DIM, HG_DIM), F32),
            pltpu.VMEM((2, nheads, group, group), BF16),
            pltpu.VMEM((2, nheads, group, HG_DIM), BF16),
            pltpu.VMEM((2, nheads, group, HG_DIM), BF16),
            pltpu.VMEM((2, nheads, group, HG_DIM), BF16),
            pltpu.VMEM((2, nheads, group, HG_DIM), F32),
            pltpu.VMEM((2, nheads, SUBLANES, HG_DIM), F32),
        ],
        compiler_params=pltpu.CompilerParams(
            dimension_semantics=("arbitrary", "arbitrary", "arbitrary"), vmem_limit_bytes=VMEM_LIMIT),
        name="hgrn",
    )(*args)


VROWS = FX_DIM + 16


def _softmax_t(st, m):
    m_new = jnp.maximum(m, jnp.max(st, axis=0, keepdims=True))
    return m_new, jnp.exp2(m - m_new), jnp.exp2(st - m_new).astype(BF16)


def _with_ones(vt):
    return jnp.concatenate([vt.astype(BF16), jnp.ones((VROWS - FX_DIM, vt.shape[1]), BF16)], axis=0)


def _normalised(acc):
    return acc[0:FX_DIM] / acc[FX_DIM:FX_DIM + 1]


def _attn_prompt_kernel(qa_ref, fgate_ref, mq_ref, mgate_ref, ka_ref, vt_ref, mk_ref, mvt_ref,
                        o_ref, vtb_ref, mvtb_ref, st_ref, p_ref, acc_ref, *, tq):
    seq = qa_ref.shape[0]
    tk = tq
    ntiles = seq // tq
    heads = range(FX_HEADS)

    for j in range(seq // tk):
        for h in heads:
            vtb_ref[j, h] = _with_ones(vt_ref[0, h * FX_DIM:(h + 1) * FX_DIM, j * tk:(j + 1) * tk])
    for h in range(MEM_HEADS):
        mvtb_ref[h] = _with_ones(mvt_ref[0, h * MEM_DIM:(h + 1) * MEM_DIM, :])

    key = lax.broadcasted_iota(jnp.int32, (tk, tq), 0)
    qry = lax.broadcasted_iota(jnp.int32, (tk, tq), 1)

    def tile_rows(i):
        return pl.ds(pl.multiple_of(i * tq, tq), tq)

    def scores(i, j, h):
        hl = slice(h * LANES, (h + 1) * LANES)
        return _nt(ka_ref[tile_rows(j), hl], qa_ref[tile_rows(i), hl])

    def start(i, s):
        for h in heads:
            st_ref[s, h] = scores(i, 0, h)
            p_ref[s, h] = jnp.zeros((tk, tq), BF16)
            acc_ref[s, h] = jnp.zeros((VROWS, tq), F32)

    def finish(i, s, carry):
        rows = tile_rows(i)
        prev = jnp.maximum(i - 1, 0)
        o_heads = []
        for h in heads:
            m, alpha_prev = carry[h]
            acc = alpha_prev * acc_ref[s, h] + _mm(vtb_ref[prev, h], p_ref[s, h])
            m, alpha, p = _softmax_t(jnp.where(key <= qry, st_ref[s, h], -jnp.inf), m)
            acc = alpha * acc + _mm(vtb_ref[i, h], p)
            o_heads.append(_normalised(acc))
        for pr in range(FX_HEADS // 2):
            ls = _pair_lanes(2 * pr)
            o_t = jnp.concatenate(o_heads[2 * pr:2 * pr + 2], axis=0)
            o_ref[rows, ls] = (o_t.T * fgate_ref[rows, ls]).astype(o_ref.dtype)

    def mem_scores(i, s):
        rows = tile_rows(i)
        for h in range(MEM_HEADS):
            mq_pair = mq_ref[rows, _pair_lanes(h)] * (MEM_DIM ** -0.5 * LOG2E)
            qm = jnp.where(_head_mask(h % 2), mq_pair, 0.0).astype(BF16)
            st_ref[s, h] = _nt(mk_ref[:, _pair_lanes(h)], qm)

    def mem_finish(i, s):
        rows = tile_rows(i)
        for pr in range(MEM_HEADS // 2):
            ls = _pair_lanes(2 * pr)
            halves = []
            for hh in range(2):
                st = st_ref[s, 2 * pr + hh]
                pm = jnp.exp2(st - jnp.max(st, axis=0, keepdims=True))
                halves.append(_normalised(_mm(mvtb_ref[2 * pr + hh], pm.astype(BF16))))
            o_t = jnp.concatenate(halves, axis=0)
            lm = slice(FX_WIDTH + pr * LANES, FX_WIDTH + (pr + 1) * LANES)
            o_ref[rows, lm] = (o_t.T * mgate_ref[rows, ls]).astype(o_ref.dtype)

    mem_scores(0, 0)

    def mem_pair(ip, c):
        mem_finish(2 * ip, 0)
        mem_scores(2 * ip + 1, 1)
        mem_finish(2 * ip + 1, 1)
        mem_scores(jnp.minimum(2 * ip + 2, ntiles - 1), 0)
        return c

    lax.fori_loop(0, ntiles // 2, mem_pair, 0)

    def qtile(i, s):
        def body(j, carry):
            prev = jnp.maximum(j - 1, 0)
            out = []
            for h in heads:
                m, alpha_prev = carry[h]
                pv = _mm(vtb_ref[prev, h], p_ref[s, h])
                st = st_ref[s, h]
                st_ref[s, h] = scores(i, j + 1, h)
                m, alpha, p = _softmax_t(st, m)
                acc_ref[s, h] = alpha_prev * acc_ref[s, h] + pv
                p_ref[s, h] = p
                out.append((m, alpha))
            return tuple(out)

        init = tuple((jnp.full((1, tq), -jnp.inf, F32), jnp.ones((1, tq), F32)) for _ in heads)
        carry = lax.fori_loop(0, i, body, init)
        finish(i, s, carry)
        start(jnp.minimum(i + 1, ntiles - 1), 1 - s)

    start(0, 0)

    def tile_pair(ip, c):
        qtile(2 * ip, 0)
        qtile(2 * ip + 1, 1)
        return c

    lax.fori_loop(0, ntiles // 2, tile_pair, 0)


def _attn_prompt(act, qa, ka, vt, mk, mvt, *, batch, seq_len, tq):
    m = act.shape[0]
    n_mem = mk.shape[0] // batch
    tk = tq
    assert (seq_len // tq) % 2 == 0, "query tiles are processed in pairs"
    ablk = lambda c0: pl.BlockSpec((seq_len, FX_WIDTH), lambda b, c0=c0: (b, c0 // FX_WIDTH))
    return pl.pallas_call(
        functools.partial(_attn_prompt_kernel, tq=tq),
        grid=(batch,),
        in_specs=[
            pl.BlockSpec((seq_len, FX_HEADS * LANES), lambda b: (b, 0)),
            ablk(ACT_PROMPT["FGATE"]), ablk(ACT_PROMPT["MQ"]), ablk(ACT_PROMPT["MGATE"]),
            pl.BlockSpec((seq_len, FX_HEADS * LANES), lambda b: (b, 0)),
            pl.BlockSpec((1, FX_WIDTH, seq_len), lambda b: (b, 0, 0)),
            pl.BlockSpec((n_mem, MEM_WIDTH), lambda b: (b, 0)),
            pl.BlockSpec((1, MEM_WIDTH, n_mem), lambda b: (b, 0, 0)),
        ],
        out_specs=pl.BlockSpec((seq_len, FX_WIDTH + MEM_WIDTH), lambda b: (b, 0)),
        out_shape=jax.ShapeDtypeStruct((m, FX_WIDTH + MEM_WIDTH), BF16),
        scratch_shapes=[
            pltpu.VMEM((seq_len // tk, FX_HEADS, VROWS, tk), BF16),
            pltpu.VMEM((MEM_HEADS, VROWS, n_mem), BF16),
            pltpu.VMEM((2, FX_HEADS, tk, tq), F32),
            pltpu.VMEM((2, FX_HEADS, tk, tq), BF16),
            pltpu.VMEM((2, FX_HEADS, VROWS, tq), F32),
        ],
        compiler_params=pltpu.CompilerParams(
            dimension_semantics=("arbitrary",), vmem_limit_bytes=VMEM_LIMIT),
        name="attn_prompt",
    )(qa, act, act, act, ka, vt, mk, mvt)


def _stack_heads(x, scale):
    lane = lax.broadcasted_iota(jnp.int32, (1, x.shape[1]), 1)
    return jnp.concatenate([jnp.where(lane // FX_DIM == h, x * scale, 0.0) for h in range(FX_HEADS)],
                           axis=0).astype(BF16)


def _unstack_heads(o, ts):
    lane = lax.broadcasted_iota(jnp.int32, (1, o.shape[1]), 1)
    out = jnp.zeros((ts, o.shape[1]), F32)
    for h in range(FX_HEADS):
        out = jnp.where(lane // FX_DIM == h, o[h * ts:(h + 1) * ts, :], out)
    return out


def _per_head_rows(vals, ts):
    return jnp.concatenate([jnp.broadcast_to(v, (ts, v.shape[1])) for v in vals], axis=0)


def _softmax_update(s, pv_fn, m, l, acc):
    m_new = jnp.maximum(m, jnp.max(s, axis=-1, keepdims=True))
    alpha = jnp.exp(m - m_new)
    p = jnp.exp(s - m_new)
    l = alpha * l + jnp.sum(p, axis=-1, keepdims=True)
    acc = alpha * acc + pv_fn(p.astype(BF16))
    return m_new, l, acc


def _attn_sample_kernel(fq_ref, fgate_ref, mq_ref, mgate_ref, kn_ref, vn_ref, dn_ref, dt_ref,
                        pkt_ref, pvt_ref, plf_ref, mkt_ref, mvt_ref, o_ref, *, chunk):
    ts = fq_ref.shape[0]
    n_new = kn_ref.shape[0]
    past = pkt_ref.shape[3]
    nchunks = past // chunk
    b = pl.program_id(0)
    hrows = FX_HEADS * ts

    r = lax.broadcasted_iota(jnp.int32, (chunk, chunk), 0)
    c = lax.broadcasted_iota(jnp.int32, (chunk, chunk), 1)
    upper = jnp.where(r <= c, 1.0, 0.0).astype(F32)
    total = jnp.zeros((FX_HEADS, 1), F32)
    d_past = []
    for ci in range(nchunks):
        d = _mm(plf_ref[0, 0, :, ci * chunk:(ci + 1) * chunk], upper,
                precision=lax.Precision.HIGHEST) + total
        d_past.append(d)
        total = d[:, chunk - 1:chunk]

    dn = dn_ref[...]
    dt_new = dt_ref[...]
    tot = [total[h:h + 1, :] for h in range(FX_HEADS)]
    dq = jnp.concatenate([tot[h] + dn[:, h:h + 1] for h in range(FX_HEADS)], axis=0)
    q4 = _stack_heads(fq_ref[...], FX_DIM ** -0.5)

    carry = (jnp.full((hrows, 1), -jnp.inf, F32), jnp.zeros((hrows, 1), F32),
             jnp.zeros((hrows, FX_WIDTH), F32))
    for ci in range(nchunks):
        cs = slice(ci * chunk, (ci + 1) * chunk)
        dk = _per_head_rows([d_past[ci][h:h + 1, :] for h in range(FX_HEADS)], ts)
        s = _mm(q4, pkt_ref[0, 0, :, cs].astype(BF16)) + dq - dk
        vt = pvt_ref[0, 0, :, cs].astype(BF16)
        carry = _softmax_update(s, lambda p, vt=vt: _nt(p, vt), *carry)

    row = lax.broadcasted_iota(jnp.int32, (hrows, n_new), 0)
    col = lax.broadcasted_iota(jnp.int32, (hrows, n_new), 1)
    visible = (col // ts == b) & (col % ts <= row % ts)
    dk = _per_head_rows([tot[h] + dt_new[h:h + 1, :] for h in range(FX_HEADS)], ts)
    s = _nt(q4, kn_ref[...].astype(BF16)) + dq - dk
    vn = vn_ref[...].astype(BF16)
    _, l, acc = _softmax_update(jnp.where(visible, s, -jnp.inf), lambda p: _mm(p, vn), *carry)
    o_fx = _unstack_heads(acc / l, ts) * fgate_ref[...]
    o_ref[:, 0:FX_WIDTH] = o_fx.astype(o_ref.dtype)

    mq4 = _stack_heads(mq_ref[...], MEM_DIM ** -0.5)
    s = _mm(mq4, mkt_ref[0, 0].astype(BF16))
    p = jnp.exp(s - jnp.max(s, axis=-1, keepdims=True))
    o = _nt(p.astype(BF16), mvt_ref[0, 0].astype(BF16)) / jnp.sum(p, axis=-1, keepdims=True)
    o_mem = _unstack_heads(o, ts) * mgate_ref[...]
    o_ref[:, FX_WIDTH:] = o_mem.astype(o_ref.dtype)


def _attn_sample(act, k_new, v_new, d_nat, d_t, past_kt, past_vt, past_lft, mem_kt, mem_vt,
                 *, layer, batch, ts):
    m = act.shape[0]
    past = past_kt.shape[3]
    n_mem = mem_kt.shape[3]
    chunk = min(past, 512)
    ablk = lambda c0: pl.BlockSpec((ts, FX_WIDTH), lambda b, c0=c0: (b, c0 // FX_WIDTH))
    cache = lambda nfeat, n: pl.BlockSpec((1, 1, nfeat, n), lambda b: (layer, b, 0, 0))
    return pl.pallas_call(
        functools.partial(_attn_sample_kernel, chunk=chunk),
        grid=(batch,),
        in_specs=[
            ablk(ACT_SAMPLE["FQ"]), ablk(ACT_SAMPLE["FGATE"]), ablk(ACT_SAMPLE["MQ"]),
            ablk(ACT_SAMPLE["MGATE"]),
            pl.BlockSpec((m, FX_WIDTH), lambda b: (0, 0)),
            pl.BlockSpec((m, FX_WIDTH), lambda b: (0, 0)),
            pl.BlockSpec((ts, LANES), lambda b: (b, 0)),
            pl.BlockSpec((SUBLANES, m), lambda b: (0, 0)),
            cache(FX_WIDTH, past), cache(FX_WIDTH, past), cache(FX_HEADS, past),
            cache(MEM_WIDTH, n_mem), cache(MEM_WIDTH, n_mem),
        ],
        out_specs=pl.BlockSpec((ts, FX_WIDTH + MEM_WIDTH), lambda b: (b, 0)),
        out_shape=jax.ShapeDtypeStruct((m, FX_WIDTH + MEM_WIDTH), BF16),
        compiler_params=pltpu.CompilerParams(
            dimension_semantics=("arbitrary",), vmem_limit_bytes=VMEM_LIMIT),
        name="attn_sample",
    )(act, act, act, act, k_new, v_new, d_nat, d_t, past_kt, past_vt, past_lft, mem_kt, mem_vt)


def _memkv_kernel(x_ref, w_ref, kb_ref, kt_ref, vt_ref):
    kv = _mm(x_ref[...].astype(BF16), w_ref[...])
    kb_ref[...] = kv[:, :MEM_WIDTH].astype(BF16)
    kt_ref[0] = kv[:, :MEM_WIDTH].T
    vt_ref[0] = kv[:, MEM_WIDTH:].T


def _memkv(mem, w, *, batch):
    m, d = mem.shape
    n_mem = m // batch
    return pl.pallas_call(
        _memkv_kernel,
        grid=(batch,),
        in_specs=[pl.BlockSpec((n_mem, d), lambda i: (i, 0)),
                  pl.BlockSpec(w.shape, lambda i: (0, 0))],
        out_specs=[pl.BlockSpec((n_mem, MEM_WIDTH), lambda i: (i, 0)),
                   pl.BlockSpec((1, MEM_WIDTH, n_mem), lambda i: (i, 0, 0)),
                   pl.BlockSpec((1, MEM_WIDTH, n_mem), lambda i: (i, 0, 0))],
        out_shape=[jax.ShapeDtypeStruct((m, MEM_WIDTH), BF16),
                   jax.ShapeDtypeStruct((batch, MEM_WIDTH, n_mem), F32),
                   jax.ShapeDtypeStruct((batch, MEM_WIDTH, n_mem), F32)],
        compiler_params=pltpu.CompilerParams(
            dimension_semantics=("arbitrary",), vmem_limit_bytes=VMEM_LIMIT),
        name="memkv",
    )(mem, w)


def _out_kernel(x_ref, hg_ref, at_ref, w_ref, g_ref, b_ref, y_ref, *, alpha):
    acc = _mm(hg_ref[...], w_ref[0:HG_WIDTH, :]) + _mm(at_ref[...], w_ref[HG_WIDTH:, :])
    z = alpha * x_ref[...] + acc
    mu = jnp.mean(z, axis=-1, keepdims=True)
    zc = z - mu
    var = jnp.mean(zc * zc, axis=-1, keepdims=True)
    y_ref[...] = zc * lax.rsqrt(var + LN_EPS) * g_ref[...] + b_ref[...]


def _out(x, mix_hg, mix_at, w, g, b, *, alpha, tm):
    m, d = x.shape
    return pl.pallas_call(
        functools.partial(_out_kernel, alpha=alpha),
        grid=(m // tm,),
        in_specs=[
            pl.BlockSpec((tm, d), lambda i: (i, 0)),
            pl.BlockSpec((tm, HG_WIDTH), lambda i: (i, 0)),
            pl.BlockSpec((tm, FX_WIDTH + MEM_WIDTH), lambda i: (i, 0)),
            pl.BlockSpec(w.shape, lambda i: (0, 0)),
            pl.BlockSpec((1, d), lambda i: (0, 0)),
            pl.BlockSpec((1, d), lambda i: (0, 0)),
        ],
        out_specs=pl.BlockSpec((tm, d), lambda i: (i, 0)),
        out_shape=jax.ShapeDtypeStruct((m, d), F32),
        compiler_params=pltpu.CompilerParams(
            dimension_semantics=("arbitrary",), vmem_limit_bytes=VMEM_LIMIT),
        name="out",
    )(x, mix_hg, mix_at, w, g, b)


def _pick_tile(n, target):
    t = min(n, target)
    while n % t:
        t //= 2
    return t


def _feature_major(x):
    lead = x.shape[:-3]
    n, heads, dim = x.shape[-3:]
    nl = len(lead)
    perm = tuple(range(nl)) + (nl + 1, nl + 2, nl)
    return jnp.transpose(x, perm).reshape(lead + (heads * dim, n))


def _token_major(x, heads):
    lead = x.shape[:-2]
    width, n = x.shape[-2:]
    nl = len(lead)
    x = x.reshape(lead + (heads, width // heads, n))
    return jnp.transpose(x, tuple(range(nl)) + (nl + 2, nl, nl + 1))


def kernel(x_prompt, x_sample, mem_prompt, state_hgrn, cache_fox_k, cache_fox_v, cache_fox_logf,
           cache_mem_k, cache_mem_v, w_in, b_fox_forget, hgrn_lower_bounds, hgrn_norm_g,
           w_mem_kv, w_out, ln_g, ln_b):
    depth, d_model, _ = w_in.shape
    bp, seq, _ = x_prompt.shape
    bs, dseq, _ = x_sample.shape
    n_mem = mem_prompt.shape[1]
    alpha = (2 * depth) ** 0.25

    fl0 = 4 * HG_WIDTH + 4 * FX_WIDTH
    w_main = jnp.concatenate([w_in[:, :, :fl0], w_in[:, :, fl0 + FX_HEADS:]], axis=-1).astype(BF16)
    w_flc = w_in[:, :, fl0:fl0 + FX_HEADS]
    w_fl = jnp.pad(w_flc, ((0, 0), (0, 0), (0, LANES - FX_HEADS))).astype(BF16)
    w_flt = jnp.pad(jnp.swapaxes(w_flc, 1, 2), ((0, 0), (0, SUBLANES - FX_HEADS), (0, 0))).astype(BF16)
    b_fl = jnp.pad(b_fox_forget, ((0, 0), (0, LANES - FX_HEADS)))[:, None, :]
    b_flt = jnp.pad(b_fox_forget, ((0, 0), (0, SUBLANES - FX_HEADS)))[:, :, None]
    lbs = hgrn_lower_bounds.astype(F32)
    w_memb = w_mem_kv.astype(BF16)
    w_outb = w_out.astype(BF16)

    past_kt = _feature_major(cache_fox_k)
    past_vt = _feature_major(cache_fox_v)
    past_lft = jnp.swapaxes(cache_fox_logf, 2, 3)
    mem_kt = _feature_major(cache_mem_k)
    mem_vt = _feature_major(cache_mem_v)

    mp, ms = bp * seq, bs * dseq
    tb = _pick_tile(seq, 256)
    yp = x_prompt.reshape(mp, d_model)
    ys = x_sample.reshape(ms, d_model)
    mem2 = mem_prompt.reshape(bp * n_mem, d_model)

    p_hg, p_kt, p_vt, p_lft, p_mkt, p_mvt = [], [], [], [], [], []
    s_hg, s_k, s_v, s_lft = [], [], [], []
    for l in range(depth):
        wl = (w_main[l], w_fl[l], w_flt[l], b_fl[l], b_flt[l], lbs)
        ng = hgrn_norm_g[l][None, :]
        g_ln, b_ln = ln_g[l][None, :], ln_b[l][None, :]

        mkb, mkt, mvt = _memkv(mem2, w_memb[l], batch=bp)
        act, qa, ka, kt, vt, lft = _proj(yp, *wl, layer=l, batch=bp, seq_len=seq,
                                         tm=_pick_tile(seq, 512), prompt=True)
        mix_hg, hg = _hgrn(act, ng, None, layer=l, batch=bp, seq_len=seq, tt=_pick_tile(seq, 2048),
                           nheads=2)
        mix_at = _attn_prompt(act, qa, ka, vt, mkb, mvt, batch=bp, seq_len=seq, tq=tb)
        yp = _out(yp, mix_hg, mix_at, w_outb[l], g_ln, b_ln, alpha=alpha, tm=_pick_tile(mp, 1024))
        p_hg.append(hg); p_kt.append(kt); p_vt.append(vt); p_lft.append(lft)
        p_mkt.append(mkt); p_mvt.append(mvt)

        act, k, v, lft, d_nat, d_t = _proj(ys, *wl, layer=l, batch=bs, seq_len=dseq, tm=ms, prompt=False)
        mix_hg, hg = _hgrn(act, ng, state_hgrn, layer=l, batch=bs, seq_len=dseq, tt=dseq, nheads=HG_HEADS)
        mix_at = _attn_sample(act, k, v, d_nat, d_t, past_kt, past_vt, past_lft, mem_kt, mem_vt,
                              layer=l, batch=bs, ts=dseq)
        ys = _out(ys, mix_hg, mix_at, w_outb[l], g_ln, b_ln, alpha=alpha, tm=ms)
        s_hg.append(hg); s_k.append(k); s_v.append(v); s_lft.append(lft)

    st = jnp.stack
    s_lf = st(s_lft)[:, :FX_HEADS, :].reshape(depth, FX_HEADS, bs, dseq)
    return (yp.reshape(bp, seq, d_model), ys.reshape(bs, dseq, d_model),
            st(p_hg),
            _token_major(st(p_kt), FX_HEADS),
            _token_major(st(p_vt), FX_HEADS),
            jnp.swapaxes(st(p_lft), 2, 3),
            _token_major(st(p_mkt), MEM_HEADS),
            _token_major(st(p_mvt), MEM_HEADS),
            st(s_hg),
            st(s_k).reshape(depth, bs, dseq, FX_HEADS, FX_DIM),
            st(s_v).reshape(depth, bs, dseq, FX_HEADS, FX_DIM),
            jnp.transpose(s_lf, (0, 2, 3, 1)))
```

```python
import functools

import jax
import jax.numpy as jnp
from jax import lax
from jax.experimental import pallas as pl
from jax.experimental.pallas import tpu as pltpu

F32 = jnp.float32
BF16 = jnp.bfloat16

HG_HEADS = 4
HG_DIM = 128
HG_WIDTH = HG_HEADS * HG_DIM
FX_HEADS = 4
FX_DIM = 64
FX_WIDTH = FX_HEADS * FX_DIM
MEM_HEADS = 4
MEM_DIM = 64
MEM_WIDTH = MEM_HEADS * MEM_DIM
HG_GROUP = 128
LN_EPS = 1e-5
RMS_EPS = 1e-6
LANES = 128
SUBLANES = 8
VMEM_LIMIT = 56 * 1024 * 1024
AUG = 3
LOG2E = 1.4426950408889634

C_HQ, C_HK, C_HV, C_HG, C_HGATE = 0, 512, 1024, 1536, 2048
ACT_PROMPT = dict(FGATE=2560, MQ=2816, MGATE=3072, WIDTH=3328)
ACT_SAMPLE = dict(FQ=2560, FGATE=2816, MQ=3072, MGATE=3328, WIDTH=3584)
W_HQ, W_HF, W_HI, W_HGATE, W_FQ, W_FK, W_FV, W_FGATE, W_MQ, W_MGATE = (
    0, 512, 1024, 1536, 2048, 2304, 2560, 2816, 3072, 3328)


def _nt(a, b, precision=None):
    return lax.dot_general(a, b, (((1,), (1,)), ((), ())), precision=precision,
                           preferred_element_type=F32)


def _mm(a, b, precision=None):
    return jnp.dot(a, b, precision=precision, preferred_element_type=F32)


def _silu(x):
    return x * jax.nn.sigmoid(x)


def _log_sigmoid(x):
    return jnp.minimum(x, 0.0) - jnp.log(1.0 + jnp.exp(-jnp.abs(x)))


def _head_mask(hh):
    lane = lax.broadcasted_iota(jnp.int32, (1, LANES), 1)
    return (lane // FX_DIM) == hh


def _pair_lanes(h):
    return slice((h // 2) * LANES, (h // 2 + 1) * LANES)


def _split3(x):
    hi = x.astype(BF16).astype(F32)
    r = x - hi
    mid = r.astype(BF16).astype(F32)
    return hi, mid, r - mid


def _aug_operand(x_pair, d, hh, is_query):
    lane = lax.broadcasted_iota(jnp.int32, (1, LANES), 1)
    a = lane - (1 - hh) * FX_DIM
    hi, mid, lo = _split3(d if is_query else -d)
    p0, o0 = (0, AUG) if is_query else (AUG, 0)
    extra = jnp.where(a == p0, hi, jnp.where(a == p0 + 1, mid, jnp.where(a == p0 + 2, lo,
                      jnp.where((a >= o0) & (a < o0 + AUG), 1.0, 0.0))))
    return jnp.where(_head_mask(hh), x_pair, extra).astype(BF16)


def _proj_kernel(x_ref, w_ref, wfl_ref, wflt_ref, bfl_ref, bflt_ref, lbs_ref, *refs,
                 layer, seg, tiles_per_seq, prompt):
    act_ref = refs[0]
    cols = ACT_PROMPT if prompt else ACT_SAMPLE
    tm = x_ref.shape[0]
    xb = x_ref[...].astype(BF16)

    def proj(c0, width):
        return _mm(xb, w_ref[:, c0:c0 + width])

    lbs = lbs_ref[...]
    e = jnp.exp(lbs - jnp.max(lbs, axis=0, keepdims=True))
    sm = e / jnp.sum(e, axis=0, keepdims=True)
    lb = jnp.sum(sm[:layer + 1], axis=0, keepdims=True) - sm[0:1]

    lf_nat = _log_sigmoid(_mm(xb, wfl_ref[...]) + bfl_ref[...])
    lf_t = _log_sigmoid(_nt(wflt_ref[...], xb) + bflt_ref[...])

    r = lax.broadcasted_iota(jnp.int32, (tm, tm), 0)
    c = lax.broadcasted_iota(jnp.int32, (tm, tm), 1)
    same = (r // seg) == (c // seg)
    lower = jnp.where((c <= r) & same, 1.0, 0.0).astype(BF16)
    d_nat = _cumsum_rows([lf_nat], lower)[0]

    fq = proj(W_FQ, FX_WIDTH)
    k = proj(W_FK, FX_WIDTH)
    v = proj(W_FV, FX_WIDTH)
    if prompt:
        _, qa_ref, ka_ref, kt_ref, vt_ref, lft_ref, cn_ref, kv_ref = refs
        if tiles_per_seq > 1:
            @pl.when(pl.program_id(0) % tiles_per_seq == 0)
            def _():
                cn_ref[...] = jnp.zeros_like(cn_ref)
            d_nat = d_nat + cn_ref[0:1, :]
            cn_ref[...] = jnp.broadcast_to(d_nat[tm - 1:tm, :], cn_ref.shape)
        lft_ref[0] = lf_t[:FX_HEADS]
        kv_ref[:, 0:FX_WIDTH] = k
        kv_ref[:, FX_WIDTH:] = v
        kt_ref[0] = kv_ref[:, 0:FX_WIDTH].T
        vt_ref[0] = kv_ref[:, FX_WIDTH:].T
        d2 = d_nat * LOG2E
        for h in range(FX_HEADS):
            hl = slice(h * LANES, (h + 1) * LANES)
            qa_ref[:, hl] = _aug_operand(fq[:, _pair_lanes(h)] * (FX_DIM ** -0.5 * LOG2E),
                                         d2[:, h:h + 1], h % 2, True)
            ka_ref[:, hl] = _aug_operand(k[:, _pair_lanes(h)], d2[:, h:h + 1], h % 2, False)
    else:
        _, k_ref, v_ref, lft_ref, dn_ref, dt_ref = refs
        upper = jnp.where((r <= c) & same, 1.0, 0.0).astype(F32)
        dt_ref[...] = _mm(lf_t, upper, precision=lax.Precision.HIGHEST)
        lft_ref[...] = lf_t
        k_ref[...] = k
        v_ref[...] = v
        dn_ref[...] = d_nat
        act_ref[:, cols["FQ"]:cols["FQ"] + FX_WIDTH] = fq

    act_ref[:, C_HQ:C_HQ + HG_WIDTH] = _silu(proj(W_HQ, HG_WIDTH))
    forget = lb + (1.0 - lb) * jax.nn.sigmoid(proj(W_HF, HG_WIDTH))
    act_ref[:, C_HK:C_HK + HG_WIDTH] = 1.0 - forget
    act_ref[:, C_HG:C_HG + HG_WIDTH] = jnp.log2(forget)
    act_ref[:, C_HV:C_HV + HG_WIDTH] = proj(W_HI, HG_WIDTH)
    act_ref[:, C_HGATE:C_HGATE + HG_WIDTH] = _silu(proj(W_HGATE, HG_WIDTH))
    act_ref[:, cols["FGATE"]:cols["FGATE"] + FX_WIDTH] = _silu(proj(W_FGATE, FX_WIDTH))
    act_ref[:, cols["MQ"]:cols["MQ"] + MEM_WIDTH] = proj(W_MQ, MEM_WIDTH)
    act_ref[:, cols["MGATE"]:cols["MGATE"] + MEM_WIDTH] = _silu(proj(W_MGATE, MEM_WIDTH))


def _proj(x, w_main, w_fl, w_flt, b_fl, b_flt, lbs, *, layer, batch, seq_len, tm, prompt):
    m, d = x.shape
    ntiles = m // tm
    seg = min(seq_len, tm)
    tps = max(seq_len // tm, 1)
    assert prompt or ntiles == 1
    kern = functools.partial(_proj_kernel, layer=layer, seg=seg, tiles_per_seq=tps, prompt=prompt)
    full = lambda shape: pl.BlockSpec(shape, lambda i: (0,) * len(shape))
    rows = lambda width: pl.BlockSpec((tm, width), lambda i: (i, 0))
    feat = lambda nfeat: pl.BlockSpec((1, nfeat, tm), lambda i: (i // tps, 0, i % tps))
    if prompt:
        out_specs = [rows(ACT_PROMPT["WIDTH"]), rows(FX_HEADS * LANES), rows(FX_HEADS * LANES),
                     feat(FX_WIDTH), feat(FX_WIDTH), feat(FX_HEADS)]
        out_shape = [jax.ShapeDtypeStruct((m, ACT_PROMPT["WIDTH"]), F32),
                     jax.ShapeDtypeStruct((m, FX_HEADS * LANES), BF16),
                     jax.ShapeDtypeStruct((m, FX_HEADS * LANES), BF16),
                     jax.ShapeDtypeStruct((batch, FX_WIDTH, seq_len), F32),
                     jax.ShapeDtypeStruct((batch, FX_WIDTH, seq_len), F32),
                     jax.ShapeDtypeStruct((batch, FX_HEADS, seq_len), F32)]
        scratch = [pltpu.VMEM((SUBLANES, LANES), F32), pltpu.VMEM((tm, 2 * FX_WIDTH), F32)]
    else:
        out_specs = [rows(ACT_SAMPLE["WIDTH"]), rows(FX_WIDTH), rows(FX_WIDTH), full((SUBLANES, m)),
                     rows(LANES), full((SUBLANES, m))]
        out_shape = [jax.ShapeDtypeStruct((m, ACT_SAMPLE["WIDTH"]), F32),
                     jax.ShapeDtypeStruct((m, FX_WIDTH), F32),
                     jax.ShapeDtypeStruct((m, FX_WIDTH), F32),
                     jax.ShapeDtypeStruct((SUBLANES, m), F32),
                     jax.ShapeDtypeStruct((m, LANES), F32),
                     jax.ShapeDtypeStruct((SUBLANES, m), F32)]
        scratch = []
    return pl.pallas_call(
        kern,
        grid=(ntiles,),
        in_specs=[
            pl.BlockSpec((tm, d), lambda i: (i, 0)),
            full(w_main.shape), full(w_fl.shape), full(w_flt.shape),
            full(b_fl.shape), full(b_flt.shape), full(lbs.shape),
        ],
        out_specs=out_specs,
        out_shape=out_shape,
        scratch_shapes=scratch,
        compiler_params=pltpu.CompilerParams(
            dimension_semantics=("arbitrary",), vmem_limit_bytes=VMEM_LIMIT),
        name="proj_prompt" if prompt else "proj_sample",
    )(x, w_main, w_fl, w_flt, b_fl, b_flt, lbs)


def _half_ref(gc, m, rows):
    n, w = gc.shape
    if 2 * m >= SUBLANES:
        g3 = gc.reshape(n // (2 * m), 2 * m, w)
        return jnp.broadcast_to(g3[:, m - 1:m, :], g3.shape).reshape(n, w)
    pos = rows % (2 * m)
    if m == 2:
        return jnp.where(pos == 0, pltpu.roll(gc, n - 1, axis=0),
                         jnp.where(pos == 1, gc,
                                   jnp.where(pos == 2, pltpu.roll(gc, 1, axis=0),
                                             pltpu.roll(gc, 2, axis=0))))
    return jnp.where(pos == 0, gc, pltpu.roll(gc, 1, axis=0))


def _level_exponent(g2, m, rows):
    n = g2.shape[0]
    if m >= SUBLANES:
        pieces = []
        for r0 in range(0, n, 2 * m):
            first, second = g2[r0:r0 + m], g2[r0 + m:r0 + 2 * m]
            ref = first[m - 1:m]
            pieces += [ref - first, second - ref]
        return jnp.concatenate(pieces, axis=0)
    ref = _half_ref(g2, m, rows)
    return jnp.where((rows % (2 * m)) < m, ref - g2, g2 - ref)


def _pair_levels(n):
    rr = lax.broadcasted_iota(jnp.int32, (n, n), 0)
    cc = lax.broadcasted_iota(jnp.int32, (n, n), 1)
    xf = (rr ^ cc).astype(F32)
    level = (lax.bitcast_convert_type(xf, jnp.int32) >> 23) - 127
    return jnp.where(rr > cc, level, -1)


def _cumsum_rows(gs, lower):
    w = gs[0].shape[1]
    pieces = [p.astype(BF16) for g in gs for p in _split3(g)]
    s = _mm(lower, jnp.concatenate(pieces, axis=1))
    return [s[:, (3 * i) * w:(3 * i + 1) * w] + s[:, (3 * i + 1) * w:(3 * i + 2) * w]
            + s[:, (3 * i + 2) * w:(3 * i + 3) * w] for i in range(len(gs))]


def _hgrn_intra(q, k, g2, levels):
    n = q.shape[0]
    rows = lax.broadcasted_iota(jnp.int32, (n, HG_DIM), 0)
    a = jnp.zeros((n, n), F32)
    m, bit = n // 2, n.bit_length() - 2
    while m >= 1:
        f = jnp.exp2(_level_exponent(g2, m, rows))
        a_m = _nt((q * f).astype(BF16), (k * f).astype(BF16))
        a = jnp.where(levels == bit, a_m, a)
        m, bit = m // 2, bit - 1
    return a


def _hgrn_kernel(*refs, group, has_init):
    if has_init:
        q_ref, k_ref, v_ref, g_ref, gate_ref, ng_ref, s0_ref, o_ref, so_ref = refs[:9]
    else:
        q_ref, k_ref, v_ref, g_ref, gate_ref, ng_ref, o_ref, so_ref = refs[:8]
    st_ref, a_ref, qe_ref, ke_ref, vb_ref, dg_ref, dec_ref = refs[-7:]
    t = pl.program_id(2)
    nheads = st_ref.shape[0]
    ngroups = q_ref.shape[0] // group
    lanes = [slice(hh * HG_DIM, (hh + 1) * HG_DIM) for hh in range(nheads)]

    @pl.when(t == 0)
    def _():
        for hh in range(nheads):
            st_ref[hh] = s0_ref[0, 0, hh].T if has_init else jnp.zeros((HG_DIM, HG_DIM), F32)

    ng = ng_ref[...]
    levels = _pair_levels(group)
    rr = lax.broadcasted_iota(jnp.int32, (group, group), 0)
    cc = lax.broadcasted_iota(jnp.int32, (group, group), 1)
    lower = jnp.where(rr >= cc, 1.0, 0.0).astype(BF16)

    def prepare(i, s):
        sl = pl.ds(pl.multiple_of(i * group, group), group)
        g2s = _cumsum_rows([g_ref[sl, ls] for ls in lanes], lower)
        for hh, ls in enumerate(lanes):
            q, k, v, g2 = q_ref[sl, ls], k_ref[sl, ls], v_ref[sl, ls], g2s[hh]
            g_last = g2[group - 1:group, :]
            a_ref[s, hh] = _hgrn_intra(q, k, g2, levels).astype(BF16)
            qe_ref[s, hh] = (q * jnp.exp2(g2)).astype(BF16)
            ke_ref[s, hh] = (k * jnp.exp2(g_last - g2)).astype(BF16)
            vb_ref[s, hh] = v.astype(BF16)
            dg_ref[s, hh] = jnp.sum(q * k, axis=-1, keepdims=True) * v
            dec_ref[s, hh] = jnp.broadcast_to(jnp.exp2(g_last), (SUBLANES, HG_DIM))

    def finish(i, s):
        sl = pl.ds(pl.multiple_of(i * group, group), group)
        for hh, ls in enumerate(lanes):
            st = st_ref[hh]
            vb = vb_ref[s, hh]
            o = _nt(qe_ref[s, hh], st.astype(BF16)) + _mm(a_ref[s, hh], vb) + dg_ref[s, hh]
            upd = lax.dot_general(vb, ke_ref[s, hh], (((0,), (0,)), ((), ())), preferred_element_type=F32)
            st_ref[hh] = st * dec_ref[s, hh, 0:1, :] + upd
            o = o * lax.rsqrt(jnp.mean(o * o, axis=-1, keepdims=True) + RMS_EPS) * ng
            o_ref[sl, ls] = (o * gate_ref[sl, ls]).astype(o_ref.dtype)

    prepare(0, 0)
    if ngroups > 1:
        def step(ip, c):
            finish(2 * ip, 0)
            prepare(2 * ip + 1, 1)
            finish(2 * ip + 1, 1)
            prepare(2 * ip + 2, 0)
            return c

        lax.fori_loop(0, ngroups // 2 - 1, step, 0)
        finish(ngroups - 2, 0)
        prepare(ngroups - 1, 1)
        finish(ngroups - 1, 1)
    else:
        finish(0, 0)

    @pl.when(t == pl.num_programs(2) - 1)
    def _():
        for hh in range(nheads):
            so_ref[0, hh] = st_ref[hh].T


def _hgrn(act, norm_g, s0, *, layer, batch, seq_len, tt, nheads):
    m = act.shape[0]
    nt = seq_len // tt
    has_init = s0 is not None
    group = min(tt, HG_GROUP)
    assert tt == group or (tt // group) % 2 == 0, "groups are processed in pairs"
    width = nheads * HG_DIM
    blk = lambda cb: pl.BlockSpec((tt, width), lambda b, h, t, cb=cb: (b * nt + t, cb // width + h))
    in_specs = [blk(C_HQ), blk(C_HK), blk(C_HV), blk(C_HG), blk(C_HGATE),
                pl.BlockSpec((1, HG_DIM), lambda b, h, t: (0, 0))]
    args = [act, act, act, act, act, norm_g]
    if has_init:
        in_specs.append(pl.BlockSpec((1, 1, nheads, HG_DIM, HG_DIM), lambda b, h, t: (layer, b, h, 0, 0)))
        args.append(s0)
    return pl.pallas_call(
        functools.partial(_hgrn_kernel, group=group, has_init=has_init),
        grid=(batch, HG_HEADS // nheads, nt),
        in_specs=in_specs,
        out_specs=[
            pl.BlockSpec((tt, width), lambda b, h, t: (b * nt + t, h)),
            pl.BlockSpec((1, nheads, HG_DIM, HG_DIM), lambda b, h, t: (b, h, 0, 0)),
        ],
        out_shape=[
            jax.ShapeDtypeStruct((m, HG_WIDTH), BF16),
            jax.ShapeDtypeStruct((batch, HG_HEADS, HG_DIM, HG_DIM), F32),
        ],
        scratch_shapes=[
            pltpu.VMEM((nheads, HG_DIM, HG_DIM), F32),
            pltpu.VMEM((2, nheads, group, group), BF16),
            pltpu.VMEM((2, nheads, group, HG_DIM), BF16),
            pltpu.VMEM((2, nheads, group, HG_DIM), BF16),
            pltpu.VMEM((2, nheads, group, HG_DIM), BF16),
            pltpu.VMEM((2, nheads, group, HG_DIM), F32),
            pltpu.VMEM((2, nheads, SUBLANES, HG_DIM), F32),
        ],
        compiler_params=pltpu.CompilerParams(
            dimension_semantics=("arbitrary", "arbitrary", "arbitrary"), vmem_limit_bytes=VMEM_LIMIT),
        name="hgrn",
    )(*args)


VROWS = FX_DIM + 16


def _softmax_t(st, m):
    m_new = jnp.maximum(m, jnp.max(st, axis=0, keepdims=True))
    return m_new, jnp.exp2(m - m_new), jnp.exp2(st - m_new).astype(BF16)


def _with_ones(vt):
    return jnp.concatenate([vt.astype(BF16), jnp.ones((VROWS - FX_DIM, vt.shape[1]), BF16)], axis=0)


def _normalised(acc):
    return acc[0:FX_DIM] / acc[FX_DIM:FX_DIM + 1]


def _attn_prompt_kernel(qa_ref, fgate_ref, mq_ref, mgate_ref, ka_ref, vt_ref, mk_ref, mvt_ref,
                        o_ref, vtb_ref, mvtb_ref, st_ref, p_ref, acc_ref, *, tq):
    seq = qa_ref.shape[0]
    tk = tq
    ntiles = seq // tq
    heads = range(FX_HEADS)

    for j in range(seq // tk):
        for h in heads:
            vtb_ref[j, h] = _with_ones(vt_ref[0, h * FX_DIM:(h + 1) * FX_DIM, j * tk:(j + 1) * tk])
    for h in range(MEM_HEADS):
        mvtb_ref[h] = _with_ones(mvt_ref[0, h * MEM_DIM:(h + 1) * MEM_DIM, :])

    key = lax.broadcasted_iota(jnp.int32, (tk, tq), 0)
    qry = lax.broadcasted_iota(jnp.int32, (tk, tq), 1)

    def tile_rows(i):
        return pl.ds(pl.multiple_of(i * tq, tq), tq)

    def scores(i, j, h):
        hl = slice(h * LANES, (h + 1) * LANES)
        return _nt(ka_ref[tile_rows(j), hl], qa_ref[tile_rows(i), hl])

    def start(i, s):
        for h in heads:
            st_ref[s, 0, h] = scores(i, 0, h)
            p_ref[s, 1, h] = jnp.zeros((tk, tq), BF16)
            acc_ref[s, h] = jnp.zeros((VROWS, tq), F32)

    def block(i, s, j, b, carry):
        prev = jnp.maximum(j - 1, 0)
        out = []
        for h in heads:
            m, alpha_prev = carry[h]
            pv = _mm(vtb_ref[prev, h], p_ref[s, 1 - b, h])
            st = st_ref[s, b, h]
            st_ref[s, 1 - b, h] = scores(i, j + 1, h)
            m, alpha, p = _softmax_t(st, m)
            acc_ref[s, h] = alpha_prev * acc_ref[s, h] + pv
            p_ref[s, b, h] = p
            out.append((m, alpha))
        return tuple(out)

    def finish(i, s, carry):
        rows = tile_rows(i)
        prev = jnp.maximum(i - 1, 0)
        o_heads = []
        for h in heads:
            m, alpha_prev = carry[h]
            acc = alpha_prev * acc_ref[s, h] + _mm(vtb_ref[prev, h], p_ref[s, 1 - s, h])
            m, alpha, p = _softmax_t(jnp.where(key <= qry, st_ref[s, s, h], -jnp.inf), m)
            acc = alpha * acc + _mm(vtb_ref[i, h], p)
            o_heads.append(_normalised(acc))
        for pr in range(FX_HEADS // 2):
            ls = _pair_lanes(2 * pr)
            o_t = jnp.concatenate(o_heads[2 * pr:2 * pr + 2], axis=0)
            o_ref[rows, ls] = (o_t.T * fgate_ref[rows, ls]).astype(o_ref.dtype)

    def mem_scores(i, s):
        rows = tile_rows(i)
        for h in range(MEM_HEADS):
            mq_pair = mq_ref[rows, _pair_lanes(h)] * (MEM_DIM ** -0.5 * LOG2E)
            qm = jnp.where(_head_mask(h % 2), mq_pair, 0.0).astype(BF16)
            st_ref[s, 0, h] = _nt(mk_ref[:, _pair_lanes(h)], qm)

    def mem_finish(i, s):
        rows = tile_rows(i)
        for pr in range(MEM_HEADS // 2):
            ls = _pair_lanes(2 * pr)
            halves = []
            for hh in range(2):
                st = st_ref[s, 0, 2 * pr + hh]
                pm = jnp.exp2(st - jnp.max(st, axis=0, keepdims=True))
                halves.append(_normalised(_mm(mvtb_ref[2 * pr + hh], pm.astype(BF16))))
            o_t = jnp.concatenate(halves, axis=0)
            lm = slice(FX_WIDTH + pr * LANES, FX_WIDTH + (pr + 1) * LANES)
            o_ref[rows, lm] = (o_t.T * mgate_ref[rows, ls]).astype(o_ref.dtype)

    mem_scores(0, 0)

    def mem_pair(ip, c):
        mem_finish(2 * ip, 0)
        mem_scores(2 * ip + 1, 1)
        mem_finish(2 * ip + 1, 1)
        mem_scores(jnp.minimum(2 * ip + 2, ntiles - 1), 0)
        return c

    lax.fori_loop(0, ntiles // 2, mem_pair, 0)

    def qtile(ip, s):
        i = 2 * ip + s

        def two_blocks(jp, carry):
            return block(i, s, 2 * jp + 1, 1, block(i, s, 2 * jp, 0, carry))

        init = tuple((jnp.full((1, tq), -jnp.inf, F32), jnp.ones((1, tq), F32)) for _ in heads)
        carry = lax.fori_loop(0, ip, two_blocks, init)
        if s == 1:
            carry = block(i, s, i - 1, 0, carry)
        finish(i, s, carry)
        start(jnp.minimum(i + 1, ntiles - 1), 1 - s)

    start(0, 0)

    def tile_pair(ip, c):
        qtile(ip, 0)
        qtile(ip, 1)
        return c

    lax.fori_loop(0, ntiles // 2, tile_pair, 0)


def _attn_prompt(act, qa, ka, vt, mk, mvt, *, batch, seq_len, tq):
    m = act.shape[0]
    n_mem = mk.shape[0] // batch
    tk = tq
    assert (seq_len // tq) % 2 == 0, "query tiles are processed in pairs"
    ablk = lambda c0: pl.BlockSpec((seq_len, FX_WIDTH), lambda b, c0=c0: (b, c0 // FX_WIDTH))
    return pl.pallas_call(
        functools.partial(_attn_prompt_kernel, tq=tq),
        grid=(batch,),
        in_specs=[
            pl.BlockSpec((seq_len, FX_HEADS * LANES), lambda b: (b, 0)),
            ablk(ACT_PROMPT["FGATE"]), ablk(ACT_PROMPT["MQ"]), ablk(ACT_PROMPT["MGATE"]),
            pl.BlockSpec((seq_len, FX_HEADS * LANES), lambda b: (b, 0)),
            pl.BlockSpec((1, FX_WIDTH, seq_len), lambda b: (b, 0, 0)),
            pl.BlockSpec((n_mem, MEM_WIDTH), lambda b: (b, 0)),
            pl.BlockSpec((1, MEM_WIDTH, n_mem), lambda b: (b, 0, 0)),
        ],
        out_specs=pl.BlockSpec((seq_len, FX_WIDTH + MEM_WIDTH), lambda b: (b, 0)),
        out_shape=jax.ShapeDtypeStruct((m, FX_WIDTH + MEM_WIDTH), BF16),
        scratch_shapes=[
            pltpu.VMEM((seq_len // tk, FX_HEADS, VROWS, tk), BF16),
            pltpu.VMEM((MEM_HEADS, VROWS, n_mem), BF16),
            pltpu.VMEM((2, 2, FX_HEADS, tk, tq), F32),
            pltpu.VMEM((2, 2, FX_HEADS, tk, tq), BF16),
            pltpu.VMEM((2, FX_HEADS, VROWS, tq), F32),
        ],
        compiler_params=pltpu.CompilerParams(
            dimension_semantics=("arbitrary",), vmem_limit_bytes=VMEM_LIMIT),
        name="attn_prompt",
    )(qa, act, act, act, ka, vt, mk, mvt)


def _stack_heads(x, scale):
    lane = lax.broadcasted_iota(jnp.int32, (1, x.shape[1]), 1)
    return jnp.concatenate([jnp.where(lane // FX_DIM == h, x * scale, 0.0) for h in range(FX_HEADS)],
                           axis=0).astype(BF16)


def _unstack_heads(o, ts):
    lane = lax.broadcasted_iota(jnp.int32, (1, o.shape[1]), 1)
    out = jnp.zeros((ts, o.shape[1]), F32)
    for h in range(FX_HEADS):
        out = jnp.where(lane // FX_DIM == h, o[h * ts:(h + 1) * ts, :], out)
    return out


def _per_head_rows(vals, ts):
    return jnp.concatenate([jnp.broadcast_to(v, (ts, v.shape[1])) for v in vals], axis=0)


def _softmax_update(s, pv_fn, m, l, acc):
    m_new = jnp.maximum(m, jnp.max(s, axis=-1, keepdims=True))
    alpha = jnp.exp(m - m_new)
    p = jnp.exp(s - m_new)
    l = alpha * l + jnp.sum(p, axis=-1, keepdims=True)
    acc = alpha * acc + pv_fn(p.astype(BF16))
    return m_new, l, acc


def _attn_sample_kernel(fq_ref, fgate_ref, mq_ref, mgate_ref, kn_ref, vn_ref, dn_ref, dt_ref,
                        pkt_ref, pvt_ref, plf_ref, mkt_ref, mvt_ref, o_ref, *, chunk):
    ts = fq_ref.shape[0]
    n_new = kn_ref.shape[0]
    past = pkt_ref.shape[3]
    nchunks = past // chunk
    b = pl.program_id(0)
    hrows = FX_HEADS * ts

    r = lax.broadcasted_iota(jnp.int32, (chunk, chunk), 0)
    c = lax.broadcasted_iota(jnp.int32, (chunk, chunk), 1)
    upper = jnp.where(r <= c, 1.0, 0.0).astype(F32)
    total = jnp.zeros((FX_HEADS, 1), F32)
    d_past = []
    for ci in range(nchunks):
        d = _mm(plf_ref[0, 0, :, ci * chunk:(ci + 1) * chunk], upper,
                precision=lax.Precision.HIGHEST) + total
        d_past.append(d)
        total = d[:, chunk - 1:chunk]

    dn = dn_ref[...]
    dt_new = dt_ref[...]
    tot = [total[h:h + 1, :] for h in range(FX_HEADS)]
    dq = jnp.concatenate([tot[h] + dn[:, h:h + 1] for h in range(FX_HEADS)], axis=0)
    q4 = _stack_heads(fq_ref[...], FX_DIM ** -0.5)

    carry = (jnp.full((hrows, 1), -jnp.inf, F32), jnp.zeros((hrows, 1), F32),
             jnp.zeros((hrows, FX_WIDTH), F32))
    for ci in range(nchunks):
        cs = slice(ci * chunk, (ci + 1) * chunk)
        dk = _per_head_rows([d_past[ci][h:h + 1, :] for h in range(FX_HEADS)], ts)
        s = _mm(q4, pkt_ref[0, 0, :, cs].astype(BF16)) + dq - dk
        vt = pvt_ref[0, 0, :, cs].astype(BF16)
        carry = _softmax_update(s, lambda p, vt=vt: _nt(p, vt), *carry)

    row = lax.broadcasted_iota(jnp.int32, (hrows, n_new), 0)
    col = lax.broadcasted_iota(jnp.int32, (hrows, n_new), 1)
    visible = (col // ts == b) & (col % ts <= row % ts)
    dk = _per_head_rows([tot[h] + dt_new[h:h + 1, :] for h in range(FX_HEADS)], ts)
    s = _nt(q4, kn_ref[...].astype(BF16)) + dq - dk
    vn = vn_ref[...].astype(BF16)
    _, l, acc = _softmax_update(jnp.where(visible, s, -jnp.inf), lambda p: _mm(p, vn), *carry)
    o_fx = _unstack_heads(acc / l, ts) * fgate_ref[...]
    o_ref[:, 0:FX_WIDTH] = o_fx.astype(o_ref.dtype)

    mq4 = _stack_heads(mq_ref[...], MEM_DIM ** -0.5)
    s = _mm(mq4, mkt_ref[0, 0].astype(BF16))
    p = jnp.exp(s - jnp.max(s, axis=-1, keepdims=True))
    o = _nt(p.astype(BF16), mvt_ref[0, 0].astype(BF16)) / jnp.sum(p, axis=-1, keepdims=True)
    o_mem = _unstack_heads(o, ts) * mgate_ref[...]
    o_ref[:, FX_WIDTH:] = o_mem.astype(o_ref.dtype)


def _attn_sample(act, k_new, v_new, d_nat, d_t, past_kt, past_vt, past_lft, mem_kt, mem_vt,
                 *, layer, batch, ts):
    m = act.shape[0]
    past = past_kt.shape[3]
    n_mem = mem_kt.shape[3]
    chunk = min(past, 512)
    ablk = lambda c0: pl.BlockSpec((ts, FX_WIDTH), lambda b, c0=c0: (b, c0 // FX_WIDTH))
    cache = lambda nfeat, n: pl.BlockSpec((1, 1, nfeat, n), lambda b: (layer, b, 0, 0))
    return pl.pallas_call(
        functools.partial(_attn_sample_kernel, chunk=chunk),
        grid=(batch,),
        in_specs=[
            ablk(ACT_SAMPLE["FQ"]), ablk(ACT_SAMPLE["FGATE"]), ablk(ACT_SAMPLE["MQ"]),
            ablk(ACT_SAMPLE["MGATE"]),
            pl.BlockSpec((m, FX_WIDTH), lambda b: (0, 0)),
            pl.BlockSpec((m, FX_WIDTH), lambda b: (0, 0)),
            pl.BlockSpec((ts, LANES), lambda b: (b, 0)),
            pl.BlockSpec((SUBLANES, m), lambda b: (0, 0)),
            cache(FX_WIDTH, past), cache(FX_WIDTH, past), cache(FX_HEADS, past),
            cache(MEM_WIDTH, n_mem), cache(MEM_WIDTH, n_mem),
        ],
        out_specs=pl.BlockSpec((ts, FX_WIDTH + MEM_WIDTH), lambda b: (b, 0)),
        out_shape=jax.ShapeDtypeStruct((m, FX_WIDTH + MEM_WIDTH), BF16),
        compiler_params=pltpu.CompilerParams(
            dimension_semantics=("arbitrary",), vmem_limit_bytes=VMEM_LIMIT),
        name="attn_sample",
    )(act, act, act, act, k_new, v_new, d_nat, d_t, past_kt, past_vt, past_lft, mem_kt, mem_vt)


def _memkv_kernel(x_ref, w_ref, kb_ref, kt_ref, vt_ref):
    kv = _mm(x_ref[...].astype(BF16), w_ref[...])
    kb_ref[...] = kv[:, :MEM_WIDTH].astype(BF16)
    kt_ref[0] = kv[:, :MEM_WIDTH].T
    vt_ref[0] = kv[:, MEM_WIDTH:].T


def _memkv(mem, w, *, batch):
    m, d = mem.shape
    n_mem = m // batch
    return pl.pallas_call(
        _memkv_kernel,
        grid=(batch,),
        in_specs=[pl.BlockSpec((n_mem, d), lambda i: (i, 0)),
                  pl.BlockSpec(w.shape, lambda i: (0, 0))],
        out_specs=[pl.BlockSpec((n_mem, MEM_WIDTH), lambda i: (i, 0)),
                   pl.BlockSpec((1, MEM_WIDTH, n_mem), lambda i: (i, 0, 0)),
                   pl.BlockSpec((1, MEM_WIDTH, n_mem), lambda i: (i, 0, 0))],
        out_shape=[jax.ShapeDtypeStruct((m, MEM_WIDTH), BF16),
                   jax.ShapeDtypeStruct((batch, MEM_WIDTH, n_mem), F32),
                   jax.ShapeDtypeStruct((batch, MEM_WIDTH, n_mem), F32)],
        compiler_params=pltpu.CompilerParams(
            dimension_semantics=("arbitrary",), vmem_limit_bytes=VMEM_LIMIT),
        name="memkv",
    )(mem, w)


def _out_kernel(x_ref, hg_ref, at_ref, w_ref, g_ref, b_ref, y_ref, *, alpha):
    acc = _mm(hg_ref[...], w_ref[0:HG_WIDTH, :]) + _mm(at_ref[...], w_ref[HG_WIDTH:, :])
    z = alpha * x_ref[...] + acc
    mu = jnp.mean(z, axis=-1, keepdims=True)
    zc = z - mu
    var = jnp.mean(zc * zc, axis=-1, keepdims=True)
    y_ref[...] = zc * lax.rsqrt(var + LN_EPS) * g_ref[...] + b_ref[...]


def _out(x, mix_hg, mix_at, w, g, b, *, alpha, tm):
    m, d = x.shape
    return pl.pallas_call(
        functools.partial(_out_kernel, alpha=alpha),
        grid=(m // tm,),
        in_specs=[
            pl.BlockSpec((tm, d), lambda i: (i, 0)),
            pl.BlockSpec((tm, HG_WIDTH), lambda i: (i, 0)),
            pl.BlockSpec((tm, FX_WIDTH + MEM_WIDTH), lambda i: (i, 0)),
            pl.BlockSpec(w.shape, lambda i: (0, 0)),
            pl.BlockSpec((1, d), lambda i: (0, 0)),
            pl.BlockSpec((1, d), lambda i: (0, 0)),
        ],
        out_specs=pl.BlockSpec((tm, d), lambda i: (i, 0)),
        out_shape=jax.ShapeDtypeStruct((m, d), F32),
        compiler_params=pltpu.CompilerParams(
            dimension_semantics=("arbitrary",), vmem_limit_bytes=VMEM_LIMIT),
        name="out",
    )(x, mix_hg, mix_at, w, g, b)


def _pick_tile(n, target):
    t = min(n, target)
    while n % t:
        t //= 2
    return t


def _feature_major(x):
    lead = x.shape[:-3]
    n, heads, dim = x.shape[-3:]
    nl = len(lead)
    perm = tuple(range(nl)) + (nl + 1, nl + 2, nl)
    return jnp.transpose(x, perm).reshape(lead + (heads * dim, n))


def _token_major(x, heads):
    lead = x.shape[:-2]
    width, n = x.shape[-2:]
    nl = len(lead)
    x = x.reshape(lead + (heads, width // heads, n))
    return jnp.transpose(x, tuple(range(nl)) + (nl + 2, nl, nl + 1))


def kernel(x_prompt, x_sample, mem_prompt, state_hgrn, cache_fox_k, cache_fox_v, cache_fox_logf,
           cache_mem_k, cache_mem_v, w_in, b_fox_forget, hgrn_lower_bounds, hgrn_norm_g,
           w_mem_kv, w_out, ln_g, ln_b):
    depth, d_model, _ = w_in.shape
    bp, seq, _ = x_prompt.shape
    bs, dseq, _ = x_sample.shape
    n_mem = mem_prompt.shape[1]
    alpha = (2 * depth) ** 0.25

    fl0 = 4 * HG_WIDTH + 4 * FX_WIDTH
    w_main = jnp.concatenate([w_in[:, :, :fl0], w_in[:, :, fl0 + FX_HEADS:]], axis=-1).astype(BF16)
    w_flc = w_in[:, :, fl0:fl0 + FX_HEADS]
    w_fl = jnp.pad(w_flc, ((0, 0), (0, 0), (0, LANES - FX_HEADS))).astype(BF16)
    w_flt = jnp.pad(jnp.swapaxes(w_flc, 1, 2), ((0, 0), (0, SUBLANES - FX_HEADS), (0, 0))).astype(BF16)
    b_fl = jnp.pad(b_fox_forget, ((0, 0), (0, LANES - FX_HEADS)))[:, None, :]
    b_flt = jnp.pad(b_fox_forget, ((0, 0), (0, SUBLANES - FX_HEADS)))[:, :, None]
    lbs = hgrn_lower_bounds.astype(F32)
    w_memb = w_mem_kv.astype(BF16)
    w_outb = w_out.astype(BF16)

    past_kt = _feature_major(cache_fox_k)
    past_vt = _feature_major(cache_fox_v)
    past_lft = jnp.swapaxes(cache_fox_logf, 2, 3)
    mem_kt = _feature_major(cache_mem_k)
    mem_vt = _feature_major(cache_mem_v)

    mp, ms = bp * seq, bs * dseq
    tb = _pick_tile(seq, 256)
    yp = x_prompt.reshape(mp, d_model)
    ys = x_sample.reshape(ms, d_model)
    mem2 = mem_prompt.reshape(bp * n_mem, d_model)

    p_hg, p_kt, p_vt, p_lft, p_mkt, p_mvt = [], [], [], [], [], []
    s_hg, s_k, s_v, s_lft = [], [], [], []
    for l in range(depth):
        wl = (w_main[l], w_fl[l], w_flt[l], b_fl[l], b_flt[l], lbs)
        ng = hgrn_norm_g[l][None, :]
        g_ln, b_ln = ln_g[l][None, :], ln_b[l][None, :]

        mkb, mkt, mvt = _memkv(mem2, w_memb[l], batch=bp)
        act, qa, ka, kt, vt, lft = _proj(yp, *wl, layer=l, batch=bp, seq_len=seq,
                                         tm=_pick_tile(seq, 512), prompt=True)
        mix_hg, hg = _hgrn(act, ng, None, layer=l, batch=bp, seq_len=seq, tt=_pick_tile(seq, 2048),
                           nheads=2)
        mix_at = _attn_prompt(act, qa, ka, vt, mkb, mvt, batch=bp, seq_len=seq, tq=tb)
        yp = _out(yp, mix_hg, mix_at, w_outb[l], g_ln, b_ln, alpha=alpha, tm=_pick_tile(mp, 1024))
        p_hg.append(hg); p_kt.append(kt); p_vt.append(vt); p_lft.append(lft)
        p_mkt.append(mkt); p_mvt.append(mvt)

        act, k, v, lft, d_nat, d_t = _proj(ys, *wl, layer=l, batch=bs, seq_len=dseq, tm=ms, prompt=False)
        mix_hg, hg = _hgrn(act, ng, state_hgrn, layer=l, batch=bs, seq_len=dseq, tt=dseq, nheads=HG_HEADS)
        mix_at = _attn_sample(act, k, v, d_nat, d_t, past_kt, past_vt, past_lft, mem_kt, mem_vt,
                              layer=l, batch=bs, ts=dseq)
        ys = _out(ys, mix_hg, mix_at, w_outb[l], g_ln, b_ln, alpha=alpha, tm=ms)
        s_hg.append(hg); s_k.append(k); s_v.append(v); s_lft.append(lft)

    st = jnp.stack
    s_lf = st(s_lft)[:, :FX_HEADS, :].reshape(depth, FX_HEADS, bs, dseq)
    return (yp.reshape(bp, seq, d_model), ys.reshape(bs, dseq, d_model),
            st(p_hg),
            _token_major(st(p_kt), FX_HEADS),
            _token_major(st(p_vt), FX_HEADS),
            jnp.swapaxes(st(p_lft), 2, 3),
            _token_major(st(p_mkt), MEM_HEADS),
            _token_major(st(p_mvt), MEM_HEADS),
            st(s_hg),
            st(s_k).reshape(depth, bs, dseq, FX_HEADS, FX_DIM),
            st(s_v).reshape(depth, bs, dseq, FX_HEADS, FX_DIM),
            jnp.transpose(s_lf, (0, 2, 3, 1)))
```

```python
import functools

import jax
import jax.numpy as jnp
from jax import lax
from jax.experimental import pallas as pl
from jax.experimental.pallas import tpu as pltpu

F32 = jnp.float32
BF16 = jnp.bfloat16

HG_HEADS = 4
HG_DIM = 128
HG_WIDTH = HG_HEADS * HG_DIM
FX_HEADS = 4
FX_DIM = 64
FX_WIDTH = FX_HEADS * FX_DIM
MEM_HEADS = 4
MEM_DIM = 64
MEM_WIDTH = MEM_HEADS * MEM_DIM
HG_GROUP = 128
LN_EPS = 1e-5
RMS_EPS = 1e-6
LANES = 128
SUBLANES = 8
VMEM_LIMIT = 56 * 1024 * 1024
AUG = 3
LOG2E = 1.4426950408889634

C_HQ, C_HK, C_HV, C_HG, C_HGATE = 0, 512, 1024, 1536, 2048
ACT_PROMPT = dict(FGATE=2560, MQ=2816, MGATE=3072, WIDTH=3328)
ACT_SAMPLE = dict(FQ=2560, FGATE=2816, MQ=3072, MGATE=3328, WIDTH=3584)
W_HQ, W_HF, W_HI, W_HGATE, W_FQ, W_FK, W_FV, W_FGATE, W_MQ, W_MGATE = (
    0, 512, 1024, 1536, 2048, 2304, 2560, 2816, 3072, 3328)


def _nt(a, b, precision=None):
    return lax.dot_general(a, b, (((1,), (1,)), ((), ())), precision=precision,
                           preferred_element_type=F32)


def _mm(a, b, precision=None):
    return jnp.dot(a, b, precision=precision, preferred_element_type=F32)


def _silu(x):
    return x * jax.nn.sigmoid(x)


def _log_sigmoid(x):
    return jnp.minimum(x, 0.0) - jnp.log(1.0 + jnp.exp(-jnp.abs(x)))


def _head_mask(hh):
    lane = lax.broadcasted_iota(jnp.int32, (1, LANES), 1)
    return (lane // FX_DIM) == hh


def _pair_lanes(h):
    return slice((h // 2) * LANES, (h // 2 + 1) * LANES)


def _split3(x):
    hi = x.astype(BF16).astype(F32)
    r = x - hi
    mid = r.astype(BF16).astype(F32)
    return hi, mid, r - mid


def _aug_operand(x_pair, d, hh, is_query):
    lane = lax.broadcasted_iota(jnp.int32, (1, LANES), 1)
    a = lane - (1 - hh) * FX_DIM
    hi, mid, lo = _split3(d if is_query else -d)
    p0, o0 = (0, AUG) if is_query else (AUG, 0)
    extra = jnp.where(a == p0, hi, jnp.where(a == p0 + 1, mid, jnp.where(a == p0 + 2, lo,
                      jnp.where((a >= o0) & (a < o0 + AUG), 1.0, 0.0))))
    return jnp.where(_head_mask(hh), x_pair, extra).astype(BF16)


def _proj_kernel(x_ref, w_ref, wfl_ref, wflt_ref, bfl_ref, bflt_ref, lbs_ref, *refs,
                 layer, seg, tiles_per_seq, prompt):
    act_ref = refs[0]
    cols = ACT_PROMPT if prompt else ACT_SAMPLE
    tm = x_ref.shape[0]
    xb = x_ref[...].astype(BF16)

    def proj(c0, width):
        return _mm(xb, w_ref[:, c0:c0 + width])

    lbs = lbs_ref[...]
    e = jnp.exp(lbs - jnp.max(lbs, axis=0, keepdims=True))
    sm = e / jnp.sum(e, axis=0, keepdims=True)
    lb = jnp.sum(sm[:layer + 1], axis=0, keepdims=True) - sm[0:1]

    lf_nat = _log_sigmoid(_mm(xb, wfl_ref[...]) + bfl_ref[...])
    lf_t = _log_sigmoid(_nt(wflt_ref[...], xb) + bflt_ref[...])

    r = lax.broadcasted_iota(jnp.int32, (tm, tm), 0)
    c = lax.broadcasted_iota(jnp.int32, (tm, tm), 1)
    same = (r // seg) == (c // seg)
    lower = jnp.where((c <= r) & same, 1.0, 0.0).astype(BF16)
    d_nat = _cumsum_rows([lf_nat], lower)[0]

    fq = proj(W_FQ, FX_WIDTH)
    k = proj(W_FK, FX_WIDTH)
    v = proj(W_FV, FX_WIDTH)
    if prompt:
        _, qa_ref, ka_ref, kt_ref, vt_ref, lft_ref, cn_ref, kv_ref = refs
        if tiles_per_seq > 1:
            @pl.when(pl.program_id(0) % tiles_per_seq == 0)
            def _():
                cn_ref[...] = jnp.zeros_like(cn_ref)
            d_nat = d_nat + cn_ref[0:1, :]
            cn_ref[...] = jnp.broadcast_to(d_nat[tm - 1:tm, :], cn_ref.shape)
        lft_ref[0] = lf_t[:FX_HEADS]
        kv_ref[:, 0:FX_WIDTH] = k
        kv_ref[:, FX_WIDTH:] = v
        kt_ref[0] = kv_ref[:, 0:FX_WIDTH].T
        vt_ref[0] = kv_ref[:, FX_WIDTH:].T
        d2 = d_nat * LOG2E
        for h in range(FX_HEADS):
            hl = slice(h * LANES, (h + 1) * LANES)
            qa_ref[:, hl] = _aug_operand(fq[:, _pair_lanes(h)] * (FX_DIM ** -0.5 * LOG2E),
                                         d2[:, h:h + 1], h % 2, True)
            ka_ref[:, hl] = _aug_operand(k[:, _pair_lanes(h)], d2[:, h:h + 1], h % 2, False)
    else:
        _, k_ref, v_ref, lft_ref, dn_ref, dt_ref = refs
        upper = jnp.where((r <= c) & same, 1.0, 0.0).astype(F32)
        dt_ref[...] = _mm(lf_t, upper, precision=lax.Precision.HIGHEST)
        lft_ref[...] = lf_t
        k_ref[...] = k
        v_ref[...] = v
        dn_ref[...] = d_nat
        act_ref[:, cols["FQ"]:cols["FQ"] + FX_WIDTH] = fq

    act_ref[:, C_HQ:C_HQ + HG_WIDTH] = _silu(proj(W_HQ, HG_WIDTH))
    forget = lb + (1.0 - lb) * jax.nn.sigmoid(proj(W_HF, HG_WIDTH))
    act_ref[:, C_HK:C_HK + HG_WIDTH] = 1.0 - forget
    act_ref[:, C_HG:C_HG + HG_WIDTH] = jnp.log2(forget)
    act_ref[:, C_HV:C_HV + HG_WIDTH] = proj(W_HI, HG_WIDTH)
    act_ref[:, C_HGATE:C_HGATE + HG_WIDTH] = _silu(proj(W_HGATE, HG_WIDTH))
    act_ref[:, cols["FGATE"]:cols["FGATE"] + FX_WIDTH] = _silu(proj(W_FGATE, FX_WIDTH))
    act_ref[:, cols["MQ"]:cols["MQ"] + MEM_WIDTH] = proj(W_MQ, MEM_WIDTH)
    act_ref[:, cols["MGATE"]:cols["MGATE"] + MEM_WIDTH] = _silu(proj(W_MGATE, MEM_WIDTH))


def _proj(x, w_main, w_fl, w_flt, b_fl, b_flt, lbs, *, layer, batch, seq_len, tm, prompt):
    m, d = x.shape
    ntiles = m // tm
    seg = min(seq_len, tm)
    tps = max(seq_len // tm, 1)
    assert prompt or ntiles == 1
    kern = functools.partial(_proj_kernel, layer=layer, seg=seg, tiles_per_seq=tps, prompt=prompt)
    full = lambda shape: pl.BlockSpec(shape, lambda i: (0,) * len(shape))
    rows = lambda width: pl.BlockSpec((tm, width), lambda i: (i, 0))
    feat = lambda nfeat: pl.BlockSpec((1, nfeat, tm), lambda i: (i // tps, 0, i % tps))
    if prompt:
        out_specs = [rows(ACT_PROMPT["WIDTH"]), rows(FX_HEADS * LANES), rows(FX_HEADS * LANES),
                     feat(FX_WIDTH), feat(FX_WIDTH), feat(FX_HEADS)]
        out_shape = [jax.ShapeDtypeStruct((m, ACT_PROMPT["WIDTH"]), F32),
                     jax.ShapeDtypeStruct((m, FX_HEADS * LANES), BF16),
                     jax.ShapeDtypeStruct((m, FX_HEADS * LANES), BF16),
                     jax.ShapeDtypeStruct((batch, FX_WIDTH, seq_len), F32),
                     jax.ShapeDtypeStruct((batch, FX_WIDTH, seq_len), F32),
                     jax.ShapeDtypeStruct((batch, FX_HEADS, seq_len), F32)]
        scratch = [pltpu.VMEM((SUBLANES, LANES), F32), pltpu.VMEM((tm, 2 * FX_WIDTH), F32)]
    else:
        out_specs = [rows(ACT_SAMPLE["WIDTH"]), rows(FX_WIDTH), rows(FX_WIDTH), full((SUBLANES, m)),
                     rows(LANES), full((SUBLANES, m))]
        out_shape = [jax.ShapeDtypeStruct((m, ACT_SAMPLE["WIDTH"]), F32),
                     jax.ShapeDtypeStruct((m, FX_WIDTH), F32),
                     jax.ShapeDtypeStruct((m, FX_WIDTH), F32),
                     jax.ShapeDtypeStruct((SUBLANES, m), F32),
                     jax.ShapeDtypeStruct((m, LANES), F32),
                     jax.ShapeDtypeStruct((SUBLANES, m), F32)]
        scratch = []
    return pl.pallas_call(
        kern,
        grid=(ntiles,),
        in_specs=[
            pl.BlockSpec((tm, d), lambda i: (i, 0)),
            full(w_main.shape), full(w_fl.shape), full(w_flt.shape),
            full(b_fl.shape), full(b_flt.shape), full(lbs.shape),
        ],
        out_specs=out_specs,
        out_shape=out_shape,
        scratch_shapes=scratch,
        compiler_params=pltpu.CompilerParams(
            dimension_semantics=("arbitrary",), vmem_limit_bytes=VMEM_LIMIT),
        name="proj_prompt" if prompt else "proj_sample",
    )(x, w_main, w_fl, w_flt, b_fl, b_flt, lbs)


def _half_ref(gc, m, rows):
    n, w = gc.shape
    if 2 * m >= SUBLANES:
        g3 = gc.reshape(n // (2 * m), 2 * m, w)
        return jnp.broadcast_to(g3[:, m - 1:m, :], g3.shape).reshape(n, w)
    pos = rows % (2 * m)
    if m == 2:
        return jnp.where(pos == 0, pltpu.roll(gc, n - 1, axis=0),
                         jnp.where(pos == 1, gc,
                                   jnp.where(pos == 2, pltpu.roll(gc, 1, axis=0),
                                             pltpu.roll(gc, 2, axis=0))))
    return jnp.where(pos == 0, gc, pltpu.roll(gc, 1, axis=0))


def _level_exponent(g2, m, rows):
    ref = _half_ref(g2, m, rows)
    return jnp.where((rows % (2 * m)) < m, ref - g2, g2 - ref)


def _pair_levels(n):
    rr = lax.broadcasted_iota(jnp.int32, (n, n), 0)
    cc = lax.broadcasted_iota(jnp.int32, (n, n), 1)
    xf = (rr ^ cc).astype(F32)
    level = (lax.bitcast_convert_type(xf, jnp.int32) >> 23) - 127
    return jnp.where(rr > cc, level, -1)


def _cumsum_rows(gs, lower):
    w = gs[0].shape[1]
    pieces = [p.astype(BF16) for g in gs for p in _split3(g)]
    s = _mm(lower, jnp.concatenate(pieces, axis=1))
    return [s[:, (3 * i) * w:(3 * i + 1) * w] + s[:, (3 * i + 1) * w:(3 * i + 2) * w]
            + s[:, (3 * i + 2) * w:(3 * i + 3) * w] for i in range(len(gs))]


def _hgrn_intra(q, k, g2, levels):
    n, w = q.shape
    rows = lax.broadcasted_iota(jnp.int32, (n, w), 0)
    nslab = n // SUBLANES
    slab = lambda x, r: x[r * SUBLANES:(r + 1) * SUBLANES]
    a = [jnp.zeros((SUBLANES, n), F32)] * nslab
    m, bit = n // 2, n.bit_length() - 2
    while m >= 1:
        if m >= SUBLANES:
            qs, ks, targets = [], [], []
            for r0 in range(0, n, 2 * m):
                first, second = slice(r0, r0 + m), slice(r0 + m, r0 + 2 * m)
                ref = g2[r0 + m - 1:r0 + m]
                ks += [k[first] * jnp.exp2(ref - g2[first]), jnp.zeros((m, w), F32)]
                qs.append(q[second] * jnp.exp2(g2[second] - ref))
                targets += range((r0 + m) // SUBLANES, (r0 + 2 * m) // SUBLANES)
            a_m = _nt(jnp.concatenate(qs, axis=0).astype(BF16), jnp.concatenate(ks, axis=0).astype(BF16))
            for i, r in enumerate(targets):
                a[r] = jnp.where(slab(levels, r) == bit, slab(a_m, i), a[r])
        else:
            f = jnp.exp2(_level_exponent(g2, m, rows))
            a_m = _nt((q * f).astype(BF16), (k * f).astype(BF16))
            a = [jnp.where(slab(levels, r) == bit, slab(a_m, r), a[r]) for r in range(nslab)]
        m, bit = m // 2, bit - 1
    return jnp.concatenate(a, axis=0)


def _hgrn_kernel(*refs, group, has_init):
    if has_init:
        q_ref, k_ref, v_ref, g_ref, gate_ref, ng_ref, s0_ref, o_ref, so_ref = refs[:9]
    else:
        q_ref, k_ref, v_ref, g_ref, gate_ref, ng_ref, o_ref, so_ref = refs[:8]
    st_ref, a_ref, qe_ref, ke_ref, vb_ref, dg_ref, dec_ref = refs[-7:]
    t = pl.program_id(2)
    nheads = st_ref.shape[0]
    ngroups = q_ref.shape[0] // group
    lanes = [slice(hh * HG_DIM, (hh + 1) * HG_DIM) for hh in range(nheads)]

    @pl.when(t == 0)
    def _():
        for hh in range(nheads):
            st_ref[hh] = s0_ref[0, 0, hh].T if has_init else jnp.zeros((HG_DIM, HG_DIM), F32)

    ng = ng_ref[...]
    levels = _pair_levels(group)
    rr = lax.broadcasted_iota(jnp.int32, (group, group), 0)
    cc = lax.broadcasted_iota(jnp.int32, (group, group), 1)
    lower = jnp.where(rr >= cc, 1.0, 0.0).astype(BF16)

    def prepare(i, s):
        sl = pl.ds(pl.multiple_of(i * group, group), group)
        g2s = _cumsum_rows([g_ref[sl, ls] for ls in lanes], lower)
        for hh, ls in enumerate(lanes):
            q, k, v, g2 = q_ref[sl, ls], k_ref[sl, ls], v_ref[sl, ls], g2s[hh]
            g_last = g2[group - 1:group, :]
            a_ref[s, hh] = _hgrn_intra(q, k, g2, levels).astype(BF16)
            qe_ref[s, hh] = (q * jnp.exp2(g2)).astype(BF16)
            ke_ref[s, hh] = (k * jnp.exp2(g_last - g2)).astype(BF16)
            vb_ref[s, hh] = v.astype(BF16)
            dg_ref[s, hh] = jnp.sum(q * k, axis=-1, keepdims=True) * v
            dec_ref[s, hh] = jnp.broadcast_to(jnp.exp2(g_last), (SUBLANES, HG_DIM))

    def finish(i, s):
        sl = pl.ds(pl.multiple_of(i * group, group), group)
        for hh, ls in enumerate(lanes):
            st = st_ref[hh]
            vb = vb_ref[s, hh]
            o = _nt(qe_ref[s, hh], st.astype(BF16)) + _mm(a_ref[s, hh], vb) + dg_ref[s, hh]
            upd = lax.dot_general(vb, ke_ref[s, hh], (((0,), (0,)), ((), ())), preferred_element_type=F32)
            st_ref[hh] = st * dec_ref[s, hh, 0:1, :] + upd
            o = o * lax.rsqrt(jnp.mean(o * o, axis=-1, keepdims=True) + RMS_EPS) * ng
            o_ref[sl, ls] = (o * gate_ref[sl, ls]).astype(o_ref.dtype)

    prepare(0, 0)
    if ngroups > 1:
        def step(ip, c):
            finish(2 * ip, 0)
            prepare(2 * ip + 1, 1)
            finish(2 * ip + 1, 1)
            prepare(2 * ip + 2, 0)
            return c

        lax.fori_loop(0, ngroups // 2 - 1, step, 0)
        finish(ngroups - 2, 0)
        prepare(ngroups - 1, 1)
        finish(ngroups - 1, 1)
    else:
        finish(0, 0)

    @pl.when(t == pl.num_programs(2) - 1)
    def _():
        for hh in range(nheads):
            so_ref[0, hh] = st_ref[hh].T


def _hgrn(act, norm_g, s0, *, layer, batch, seq_len, tt, nheads):
    m = act.shape[0]
    nt = seq_len // tt
    has_init = s0 is not None
    group = min(tt, HG_GROUP)
    assert tt == group or (tt // group) % 2 == 0, "groups are processed in pairs"
    width = nheads * HG_DIM
    blk = lambda cb: pl.BlockSpec((tt, width), lambda b, h, t, cb=cb: (b * nt + t, cb // width + h))
    in_specs = [blk(C_HQ), blk(C_HK), blk(C_HV), blk(C_HG), blk(C_HGATE),
                pl.BlockSpec((1, HG_DIM), lambda b, h, t: (0, 0))]
    args = [act, act, act, act, act, norm_g]
    if has_init:
        in_specs.append(pl.BlockSpec((1, 1, nheads, HG_DIM, HG_DIM), lambda b, h, t: (layer, b, h, 0, 0)))
        args.append(s0)
    return pl.pallas_call(
        functools.partial(_hgrn_kernel, group=group, has_init=has_init),
        grid=(batch, HG_HEADS // nheads, nt),
        in_specs=in_specs,
        out_specs=[
            pl.BlockSpec((tt, width), lambda b, h, t: (b * nt + t, h)),
            pl.BlockSpec((1, nheads, HG_DIM, HG_DIM), lambda b, h, t: (b, h, 0, 0)),
        ],
        out_shape=[
            jax.ShapeDtypeStruct((m, HG_WIDTH), BF16),
            jax.ShapeDtypeStruct((batch, HG_HEADS, HG_DIM, HG_DIM), F32),
        ],
        scratch_shapes=[
            pltpu.VMEM((nheads, HG_DIM, HG_DIM), F32),
            pltpu.VMEM((2, nheads, group, group), BF16),
            pltpu.VMEM((2, nheads, group, HG_DIM), BF16),
            pltpu.VMEM((2, nheads, group, HG_DIM), BF16),
            pltpu.VMEM((2, nheads, group, HG_DIM), BF16),
            pltpu.VMEM((2, nheads, group, HG_DIM), F32),
            pltpu.VMEM((2, nheads, SUBLANES, HG_DIM), F32),
        ],
        compiler_params=pltpu.CompilerParams(
            dimension_semantics=("arbitrary", "arbitrary", "arbitrary"), vmem_limit_bytes=VMEM_LIMIT),
        name="hgrn",
    )(*args)


VROWS = FX_DIM + 16


def _softmax_t(st, m):
    m_new = jnp.maximum(m, jnp.max(st, axis=0, keepdims=True))
    return m_new, jnp.exp2(m - m_new), jnp.exp2(st - m_new).astype(BF16)


def _with_ones(vt):
    return jnp.concatenate([vt.astype(BF16), jnp.ones((VROWS - FX_DIM, vt.shape[1]), BF16)], axis=0)


def _normalised(acc):
    return acc[0:FX_DIM] / acc[FX_DIM:FX_DIM + 1]


def _attn_prompt_kernel(qa_ref, fgate_ref, mq_ref, mgate_ref, ka_ref, vt_ref, mk_ref, mvt_ref,
                        o_ref, vtb_ref, mvtb_ref, st_ref, p_ref, acc_ref, *, tq):
    seq = qa_ref.shape[0]
    tk = tq
    ntiles = seq // tq
    heads = range(FX_HEADS)

    for j in range(seq // tk):
        for h in heads:
            vtb_ref[j, h] = _with_ones(vt_ref[0, h * FX_DIM:(h + 1) * FX_DIM, j * tk:(j + 1) * tk])
    for h in range(MEM_HEADS):
        mvtb_ref[h] = _with_ones(mvt_ref[0, h * MEM_DIM:(h + 1) * MEM_DIM, :])

    key = lax.broadcasted_iota(jnp.int32, (tk, tq), 0)
    qry = lax.broadcasted_iota(jnp.int32, (tk, tq), 1)

    def tile_rows(i):
        return pl.ds(pl.multiple_of(i * tq, tq), tq)

    def scores(i, j, h):
        hl = slice(h * LANES, (h + 1) * LANES)
        return _nt(ka_ref[tile_rows(j), hl], qa_ref[tile_rows(i), hl])

    def start(i, s):
        for h in heads:
            st_ref[s, 0, h] = scores(i, 0, h)
            p_ref[s, 1, h] = jnp.zeros((tk, tq), BF16)
            acc_ref[s, h] = jnp.zeros((VROWS, tq), F32)

    def block(i, s, j, b, carry):
        prev = jnp.maximum(j - 1, 0)
        out = []
        for h in heads:
            m, alpha_prev = carry[h]
            pv = _mm(vtb_ref[prev, h], p_ref[s, 1 - b, h])
            st = st_ref[s, b, h]
            st_ref[s, 1 - b, h] = scores(i, j + 1, h)
            m, alpha, p = _softmax_t(st, m)
            acc_ref[s, h] = alpha_prev * acc_ref[s, h] + pv
            p_ref[s, b, h] = p
            out.append((m, alpha))
        return tuple(out)

    def finish(i, s, carry):
        rows = tile_rows(i)
        prev = jnp.maximum(i - 1, 0)
        o_heads = []
        for h in heads:
            m, alpha_prev = carry[h]
            acc = alpha_prev * acc_ref[s, h] + _mm(vtb_ref[prev, h], p_ref[s, 1 - s, h])
            m, alpha, p = _softmax_t(jnp.where(key <= qry, st_ref[s, s, h], -jnp.inf), m)
            acc = alpha * acc + _mm(vtb_ref[i, h], p)
            o_heads.append(_normalised(acc))
        for pr in range(FX_HEADS // 2):
            ls = _pair_lanes(2 * pr)
            o_t = jnp.concatenate(o_heads[2 * pr:2 * pr + 2], axis=0)
            o_ref[rows, ls] = (o_t.T * fgate_ref[rows, ls]).astype(o_ref.dtype)

    def mem_scores(i, s):
        rows = tile_rows(i)
        for h in range(MEM_HEADS):
            mq_pair = mq_ref[rows, _pair_lanes(h)] * (MEM_DIM ** -0.5 * LOG2E)
            qm = jnp.where(_head_mask(h % 2), mq_pair, 0.0).astype(BF16)
            st_ref[s, 0, h] = _nt(mk_ref[:, _pair_lanes(h)], qm)

    def mem_finish(i, s):
        rows = tile_rows(i)
        for pr in range(MEM_HEADS // 2):
            ls = _pair_lanes(2 * pr)
            halves = []
            for hh in range(2):
                st = st_ref[s, 0, 2 * pr + hh]
                pm = jnp.exp2(st - jnp.max(st, axis=0, keepdims=True))
                halves.append(_normalised(_mm(mvtb_ref[2 * pr + hh], pm.astype(BF16))))
            o_t = jnp.concatenate(halves, axis=0)
            lm = slice(FX_WIDTH + pr * LANES, FX_WIDTH + (pr + 1) * LANES)
            o_ref[rows, lm] = (o_t.T * mgate_ref[rows, ls]).astype(o_ref.dtype)

    mem_scores(0, 0)

    def mem_pair(ip, c):
        mem_finish(2 * ip, 0)
        mem_scores(2 * ip + 1, 1)
        mem_finish(2 * ip + 1, 1)
        mem_scores(jnp.minimum(2 * ip + 2, ntiles - 1), 0)
        return c

    lax.fori_loop(0, ntiles // 2, mem_pair, 0)

    def qtile(ip, s):
        i = 2 * ip + s

        def two_blocks(jp, carry):
            return block(i, s, 2 * jp + 1, 1, block(i, s, 2 * jp, 0, carry))

        init = tuple((jnp.full((1, tq), -jnp.inf, F32), jnp.ones((1, tq), F32)) for _ in heads)
        carry = lax.fori_loop(0, ip, two_blocks, init)
        if s == 1:
            carry = block(i, s, i - 1, 0, carry)
        finish(i, s, carry)
        start(jnp.minimum(i + 1, ntiles - 1), 1 - s)

    start(0, 0)

    def tile_pair(ip, c):
        qtile(ip, 0)
        qtile(ip, 1)
        return c

    lax.fori_loop(0, ntiles // 2, tile_pair, 0)


def _attn_prompt(act, qa, ka, vt, mk, mvt, *, batch, seq_len, tq):
    m = act.shape[0]
    n_mem = mk.shape[0] // batch
    tk = tq
    assert (seq_len // tq) % 2 == 0, "query tiles are processed in pairs"
    ablk = lambda c0: pl.BlockSpec((seq_len, FX_WIDTH), lambda b, c0=c0: (b, c0 // FX_WIDTH))
    return pl.pallas_call(
        functools.partial(_attn_prompt_kernel, tq=tq),
        grid=(batch,),
        in_specs=[
            pl.BlockSpec((seq_len, FX_HEADS * LANES), lambda b: (b, 0)),
            ablk(ACT_PROMPT["FGATE"]), ablk(ACT_PROMPT["MQ"]), ablk(ACT_PROMPT["MGATE"]),
            pl.BlockSpec((seq_len, FX_HEADS * LANES), lambda b: (b, 0)),
            pl.BlockSpec((1, FX_WIDTH, seq_len), lambda b: (b, 0, 0)),
            pl.BlockSpec((n_mem, MEM_WIDTH), lambda b: (b, 0)),
            pl.BlockSpec((1, MEM_WIDTH, n_mem), lambda b: (b, 0, 0)),
        ],
        out_specs=pl.BlockSpec((seq_len, FX_WIDTH + MEM_WIDTH), lambda b: (b, 0)),
        out_shape=jax.ShapeDtypeStruct((m, FX_WIDTH + MEM_WIDTH), BF16),
        scratch_shapes=[
            pltpu.VMEM((seq_len // tk, FX_HEADS, VROWS, tk), BF16),
            pltpu.VMEM((MEM_HEADS, VROWS, n_mem), BF16),
            pltpu.VMEM((2, 2, FX_HEADS, tk, tq), F32),
            pltpu.VMEM((2, 2, FX_HEADS, tk, tq), BF16),
            pltpu.VMEM((2, FX_HEADS, VROWS, tq), F32),
        ],
        compiler_params=pltpu.CompilerParams(
            dimension_semantics=("arbitrary",), vmem_limit_bytes=VMEM_LIMIT),
        name="attn_prompt",
    )(qa, act, act, act, ka, vt, mk, mvt)


def _stack_heads(x, scale):
    lane = lax.broadcasted_iota(jnp.int32, (1, x.shape[1]), 1)
    return jnp.concatenate([jnp.where(lane // FX_DIM == h, x * scale, 0.0) for h in range(FX_HEADS)],
                           axis=0).astype(BF16)


def _unstack_heads(o, ts):
    lane = lax.broadcasted_iota(jnp.int32, (1, o.shape[1]), 1)
    out = jnp.zeros((ts, o.shape[1]), F32)
    for h in range(FX_HEADS):
        out = jnp.where(lane // FX_DIM == h, o[h * ts:(h + 1) * ts, :], out)
    return out


def _per_head_rows(vals, ts):
    return jnp.concatenate([jnp.broadcast_to(v, (ts, v.shape[1])) for v in vals], axis=0)


def _softmax_update(s, pv_fn, m, l, acc):
    m_new = jnp.maximum(m, jnp.max(s, axis=-1, keepdims=True))
    alpha = jnp.exp(m - m_new)
    p = jnp.exp(s - m_new)
    l = alpha * l + jnp.sum(p, axis=-1, keepdims=True)
    acc = alpha * acc + pv_fn(p.astype(BF16))
    return m_new, l, acc


def _attn_sample_kernel(fq_ref, fgate_ref, mq_ref, mgate_ref, kn_ref, vn_ref, dn_ref, dt_ref,
                        pkt_ref, pvt_ref, plf_ref, mkt_ref, mvt_ref, o_ref, *, chunk):
    ts = fq_ref.shape[0]
    n_new = kn_ref.shape[0]
    past = pkt_ref.shape[3]
    nchunks = past // chunk
    b = pl.program_id(0)
    hrows = FX_HEADS * ts

    r = lax.broadcasted_iota(jnp.int32, (chunk, chunk), 0)
    c = lax.broadcasted_iota(jnp.int32, (chunk, chunk), 1)
    upper = jnp.where(r <= c, 1.0, 0.0).astype(F32)
    total = jnp.zeros((FX_HEADS, 1), F32)
    d_past = []
    for ci in range(nchunks):
        d = _mm(plf_ref[0, 0, :, ci * chunk:(ci + 1) * chunk], upper,
                precision=lax.Precision.HIGHEST) + total
        d_past.append(d)
        total = d[:, chunk - 1:chunk]

    dn = dn_ref[...]
    dt_new = dt_ref[...]
    tot = [total[h:h + 1, :] for h in range(FX_HEADS)]
    dq = jnp.concatenate([tot[h] + dn[:, h:h + 1] for h in range(FX_HEADS)], axis=0)
    q4 = _stack_heads(fq_ref[...], FX_DIM ** -0.5)

    carry = (jnp.full((hrows, 1), -jnp.inf, F32), jnp.zeros((hrows, 1), F32),
             jnp.zeros((hrows, FX_WIDTH), F32))
    for ci in range(nchunks):
        cs = slice(ci * chunk, (ci + 1) * chunk)
        dk = _per_head_rows([d_past[ci][h:h + 1, :] for h in range(FX_HEADS)], ts)
        s = _mm(q4, pkt_ref[0, 0, :, cs].astype(BF16)) + dq - dk
        vt = pvt_ref[0, 0, :, cs].astype(BF16)
        carry = _softmax_update(s, lambda p, vt=vt: _nt(p, vt), *carry)

    row = lax.broadcasted_iota(jnp.int32, (hrows, n_new), 0)
    col = lax.broadcasted_iota(jnp.int32, (hrows, n_new), 1)
    visible = (col // ts == b) & (col % ts <= row % ts)
    dk = _per_head_rows([tot[h] + dt_new[h:h + 1, :] for h in range(FX_HEADS)], ts)
    s = _nt(q4, kn_ref[...].astype(BF16)) + dq - dk
    vn = vn_ref[...].astype(BF16)
    _, l, acc = _softmax_update(jnp.where(visible, s, -jnp.inf), lambda p: _mm(p, vn), *carry)
    o_fx = _unstack_heads(acc / l, ts) * fgate_ref[...]
    o_ref[:, 0:FX_WIDTH] = o_fx.astype(o_ref.dtype)

    mq4 = _stack_heads(mq_ref[...], MEM_DIM ** -0.5)
    s = _mm(mq4, mkt_ref[0, 0].astype(BF16))
    p = jnp.exp(s - jnp.max(s, axis=-1, keepdims=True))
    o = _nt(p.astype(BF16), mvt_ref[0, 0].astype(BF16)) / jnp.sum(p, axis=-1, keepdims=True)
    o_mem = _unstack_heads(o, ts) * mgate_ref[...]
    o_ref[:, FX_WIDTH:] = o_mem.astype(o_ref.dtype)


def _attn_sample(act, k_new, v_new, d_nat, d_t, past_kt, past_vt, past_lft, mem_kt, mem_vt,
                 *, layer, batch, ts):
    m = act.shape[0]
    past = past_kt.shape[3]
    n_mem = mem_kt.shape[3]
    chunk = min(past, 512)
    ablk = lambda c0: pl.BlockSpec((ts, FX_WIDTH), lambda b, c0=c0: (b, c0 // FX_WIDTH))
    cache = lambda nfeat, n: pl.BlockSpec((1, 1, nfeat, n), lambda b: (layer, b, 0, 0))
    return pl.pallas_call(
        functools.partial(_attn_sample_kernel, chunk=chunk),
        grid=(batch,),
        in_specs=[
            ablk(ACT_SAMPLE["FQ"]), ablk(ACT_SAMPLE["FGATE"]), ablk(ACT_SAMPLE["MQ"]),
            ablk(ACT_SAMPLE["MGATE"]),
            pl.BlockSpec((m, FX_WIDTH), lambda b: (0, 0)),
            pl.BlockSpec((m, FX_WIDTH), lambda b: (0, 0)),
            pl.BlockSpec((ts, LANES), lambda b: (b, 0)),
            pl.BlockSpec((SUBLANES, m), lambda b: (0, 0)),
            cache(FX_WIDTH, past), cache(FX_WIDTH, past), cache(FX_HEADS, past),
            cache(MEM_WIDTH, n_mem), cache(MEM_WIDTH, n_mem),
        ],
        out_specs=pl.BlockSpec((ts, FX_WIDTH + MEM_WIDTH), lambda b: (b, 0)),
        out_shape=jax.ShapeDtypeStruct((m, FX_WIDTH + MEM_WIDTH), BF16),
        compiler_params=pltpu.CompilerParams(
            dimension_semantics=("arbitrary",), vmem_limit_bytes=VMEM_LIMIT),
        name="attn_sample",
    )(act, act, act, act, k_new, v_new, d_nat, d_t, past_kt, past_vt, past_lft, mem_kt, mem_vt)


def _memkv_kernel(x_ref, w_ref, kb_ref, kt_ref, vt_ref):
    kv = _mm(x_ref[...].astype(BF16), w_ref[...])
    kb_ref[...] = kv[:, :MEM_WIDTH].astype(BF16)
    kt_ref[0] = kv[:, :MEM_WIDTH].T
    vt_ref[0] = kv[:, MEM_WIDTH:].T


def _memkv(mem, w, *, batch):
    m, d = mem.shape
    n_mem = m // batch
    return pl.pallas_call(
        _memkv_kernel,
        grid=(batch,),
        in_specs=[pl.BlockSpec((n_mem, d), lambda i: (i, 0)),
                  pl.BlockSpec(w.shape, lambda i: (0, 0))],
        out_specs=[pl.BlockSpec((n_mem, MEM_WIDTH), lambda i: (i, 0)),
                   pl.BlockSpec((1, MEM_WIDTH, n_mem), lambda i: (i, 0, 0)),
                   pl.BlockSpec((1, MEM_WIDTH, n_mem), lambda i: (i, 0, 0))],
        out_shape=[jax.ShapeDtypeStruct((m, MEM_WIDTH), BF16),
                   jax.ShapeDtypeStruct((batch, MEM_WIDTH, n_mem), F32),
                   jax.ShapeDtypeStruct((batch, MEM_WIDTH, n_mem), F32)],
        compiler_params=pltpu.CompilerParams(
            dimension_semantics=("arbitrary",), vmem_limit_bytes=VMEM_LIMIT),
        name="memkv",
    )(mem, w)


def _out_kernel(x_ref, hg_ref, at_ref, w_ref, g_ref, b_ref, y_ref, *, alpha):
    acc = _mm(hg_ref[...], w_ref[0:HG_WIDTH, :]) + _mm(at_ref[...], w_ref[HG_WIDTH:, :])
    z = alpha * x_ref[...] + acc
    mu = jnp.mean(z, axis=-1, keepdims=True)
    zc = z - mu
    var = jnp.mean(zc * zc, axis=-1, keepdims=True)
    y_ref[...] = zc * lax.rsqrt(var + LN_EPS) * g_ref[...] + b_ref[...]


def _out(x, mix_hg, mix_at, w, g, b, *, alpha, tm):
    m, d = x.shape
    return pl.pallas_call(
        functools.partial(_out_kernel, alpha=alpha),
        grid=(m // tm,),
        in_specs=[
            pl.BlockSpec((tm, d), lambda i: (i, 0)),
            pl.BlockSpec((tm, HG_WIDTH), lambda i: (i, 0)),
            pl.BlockSpec((tm, FX_WIDTH + MEM_WIDTH), lambda i: (i, 0)),
            pl.BlockSpec(w.shape, lambda i: (0, 0)),
            pl.BlockSpec((1, d), lambda i: (0, 0)),
            pl.BlockSpec((1, d), lambda i: (0, 0)),
        ],
        out_specs=pl.BlockSpec((tm, d), lambda i: (i, 0)),
        out_shape=jax.ShapeDtypeStruct((m, d), F32),
        compiler_params=pltpu.CompilerParams(
            dimension_semantics=("arbitrary",), vmem_limit_bytes=VMEM_LIMIT),
        name="out",
    )(x, mix_hg, mix_at, w, g, b)


def _pick_tile(n, target):
    t = min(n, target)
    while n % t:
        t //= 2
    return t


def _feature_major(x):
    lead = x.shape[:-3]
    n, heads, dim = x.shape[-3:]
    nl = len(lead)
    perm = tuple(range(nl)) + (nl + 1, nl + 2, nl)
    return jnp.transpose(x, perm).reshape(lead + (heads * dim, n))


def _token_major(x, heads):
    lead = x.shape[:-2]
    width, n = x.shape[-2:]
    nl = len(lead)
    x = x.reshape(lead + (heads, width // heads, n))
    return jnp.transpose(x, tuple(range(nl)) + (nl + 2, nl, nl + 1))


def kernel(x_prompt, x_sample, mem_prompt, state_hgrn, cache_fox_k, cache_fox_v, cache_fox_logf,
           cache_mem_k, cache_mem_v, w_in, b_fox_forget, hgrn_lower_bounds, hgrn_norm_g,
           w_mem_kv, w_out, ln_g, ln_b):
    depth, d_model, _ = w_in.shape
    bp, seq, _ = x_prompt.shape
    bs, dseq, _ = x_sample.shape
    n_mem = mem_prompt.shape[1]
    alpha = (2 * depth) ** 0.25

    fl0 = 4 * HG_WIDTH + 4 * FX_WIDTH
    w_main = jnp.concatenate([w_in[:, :, :fl0], w_in[:, :, fl0 + FX_HEADS:]], axis=-1).astype(BF16)
    w_flc = w_in[:, :, fl0:fl0 + FX_HEADS]
    w_fl = jnp.pad(w_flc, ((0, 0), (0, 0), (0, LANES - FX_HEADS))).astype(BF16)
    w_flt = jnp.pad(jnp.swapaxes(w_flc, 1, 2), ((0, 0), (0, SUBLANES - FX_HEADS), (0, 0))).astype(BF16)
    b_fl = jnp.pad(b_fox_forget, ((0, 0), (0, LANES - FX_HEADS)))[:, None, :]
    b_flt = jnp.pad(b_fox_forget, ((0, 0), (0, SUBLANES - FX_HEADS)))[:, :, None]
    lbs = hgrn_lower_bounds.astype(F32)
    w_memb = w_mem_kv.astype(BF16)
    w_outb = w_out.astype(BF16)

    past_kt = _feature_major(cache_fox_k)
    past_vt = _feature_major(cache_fox_v)
    past_lft = jnp.swapaxes(cache_fox_logf, 2, 3)
    mem_kt = _feature_major(cache_mem_k)
    mem_vt = _feature_major(cache_mem_v)

    mp, ms = bp * seq, bs * dseq
    tb = _pick_tile(seq, 256)
    yp = x_prompt.reshape(mp, d_model)
    ys = x_sample.reshape(ms, d_model)
    mem2 = mem_prompt.reshape(bp * n_mem, d_model)

    p_hg, p_kt, p_vt, p_lft, p_mkt, p_mvt = [], [], [], [], [], []
    s_hg, s_k, s_v, s_lft = [], [], [], []
    for l in range(depth):
        wl = (w_main[l], w_fl[l], w_flt[l], b_fl[l], b_flt[l], lbs)
        ng = hgrn_norm_g[l][None, :]
        g_ln, b_ln = ln_g[l][None, :], ln_b[l][None, :]

        mkb, mkt, mvt = _memkv(mem2, w_memb[l], batch=bp)
        act, qa, ka, kt, vt, lft = _proj(yp, *wl, layer=l, batch=bp, seq_len=seq,
                                         tm=_pick_tile(seq, 512), prompt=True)
        mix_hg, hg = _hgrn(act, ng, None, layer=l, batch=bp, seq_len=seq, tt=_pick_tile(seq, 2048),
                           nheads=HG_HEADS)
        mix_at = _attn_prompt(act, qa, ka, vt, mkb, mvt, batch=bp, seq_len=seq, tq=tb)
        yp = _out(yp, mix_hg, mix_at, w_outb[l], g_ln, b_ln, alpha=alpha, tm=_pick_tile(mp, 1024))
        p_hg.append(hg); p_kt.append(kt); p_vt.append(vt); p_lft.append(lft)
        p_mkt.append(mkt); p_mvt.append(mvt)

        act, k, v, lft, d_nat, d_t = _proj(ys, *wl, layer=l, batch=bs, seq_len=dseq, tm=ms, prompt=False)
        mix_hg, hg = _hgrn(act, ng, state_hgrn, layer=l, batch=bs, seq_len=dseq, tt=dseq, nheads=HG_HEADS)
        mix_at = _attn_sample(act, k, v, d_nat, d_t, past_kt, past_vt, past_lft, mem_kt, mem_vt,
                              layer=l, batch=bs, ts=dseq)
        ys = _out(ys, mix_hg, mix_at, w_outb[l], g_ln, b_ln, alpha=alpha, tm=ms)
        s_hg.append(hg); s_k.append(k); s_v.append(v); s_lft.append(lft)

    st = jnp.stack
    s_lf = st(s_lft)[:, :FX_HEADS, :].reshape(depth, FX_HEADS, bs, dseq)
    return (yp.reshape(bp, seq, d_model), ys.reshape(bs, dseq, d_model),
            st(p_hg),
            _token_major(st(p_kt), FX_HEADS),
            _token_major(st(p_vt), FX_HEADS),
            jnp.swapaxes(st(p_lft), 2, 3),
            _token_major(st(p_mkt), MEM_HEADS),
            _token_major(st(p_mvt), MEM_HEADS),
            st(s_hg),
            st(s_k).reshape(depth, bs, dseq, FX_HEADS, FX_DIM),
            st(s_v).reshape(depth, bs, dseq, FX_HEADS, FX_DIM),
            jnp.transpose(s_lf, (0, 2, 3, 1)))
```

```python
import functools

import jax
import jax.numpy as jnp
from jax import lax
from jax.experimental import pallas as pl
from jax.experimental.pallas import tpu as pltpu

F32 = jnp.float32
BF16 = jnp.bfloat16

HG_HEADS = 4
HG_DIM = 128
HG_WIDTH = HG_HEADS * HG_DIM
FX_HEADS = 4
FX_DIM = 64
FX_WIDTH = FX_HEADS * FX_DIM
MEM_HEADS = 4
MEM_DIM = 64
MEM_WIDTH = MEM_HEADS * MEM_DIM
HG_GROUP = 128
LN_EPS = 1e-5
RMS_EPS = 1e-6
LANES = 128
SUBLANES = 8
VMEM_LIMIT = 56 * 1024 * 1024
AUG = 3
LOG2E = 1.4426950408889634

C_HQ, C_HK, C_HV, C_HG, C_HGATE = 0, 512, 1024, 1536, 2048
ACT_PROMPT = dict(FGATE=2560, MQ=2816, MGATE=3072, WIDTH=3328)
ACT_SAMPLE = dict(FQ=2560, FGATE=2816, MQ=3072, MGATE=3328, WIDTH=3584)
W_HQ, W_HF, W_HI, W_HGATE, W_FQ, W_FK, W_FV, W_FGATE, W_MQ, W_MGATE = (
    0, 512, 1024, 1536, 2048, 2304, 2560, 2816, 3072, 3328)


def _nt(a, b, precision=None):
    return lax.dot_general(a, b, (((1,), (1,)), ((), ())), precision=precision,
                           preferred_element_type=F32)


def _mm(a, b, precision=None):
    return jnp.dot(a, b, precision=precision, preferred_element_type=F32)


def _silu(x):
    return x * jax.nn.sigmoid(x)


def _log_sigmoid(x):
    return jnp.minimum(x, 0.0) - jnp.log(1.0 + jnp.exp(-jnp.abs(x)))


def _head_mask(hh):
    lane = lax.broadcasted_iota(jnp.int32, (1, LANES), 1)
    return (lane // FX_DIM) == hh


def _pair_lanes(h):
    return slice((h // 2) * LANES, (h // 2 + 1) * LANES)


def _split3(x):
    hi = x.astype(BF16).astype(F32)
    r = x - hi
    mid = r.astype(BF16).astype(F32)
    return hi, mid, r - mid


def _aug_operand(x_pair, d, hh, is_query):
    lane = lax.broadcasted_iota(jnp.int32, (1, LANES), 1)
    a = lane - (1 - hh) * FX_DIM
    hi, mid, lo = _split3(d if is_query else -d)
    p0, o0 = (0, AUG) if is_query else (AUG, 0)
    extra = jnp.where(a == p0, hi, jnp.where(a == p0 + 1, mid, jnp.where(a == p0 + 2, lo,
                      jnp.where((a >= o0) & (a < o0 + AUG), 1.0, 0.0))))
    return jnp.where(_head_mask(hh), x_pair, extra).astype(BF16)


def _proj_kernel(x_ref, w_ref, wfl_ref, wflt_ref, bfl_ref, bflt_ref, lbs_ref, *refs,
                 layer, seg, tiles_per_seq, prompt):
    act_ref = refs[0]
    cols = ACT_PROMPT if prompt else ACT_SAMPLE
    tm = x_ref.shape[0]
    xb = x_ref[...].astype(BF16)

    def proj(c0, width):
        return _nt(xb, w_ref[c0:c0 + width, :])

    lbs = lbs_ref[...]
    e = jnp.exp(lbs - jnp.max(lbs, axis=0, keepdims=True))
    sm = e / jnp.sum(e, axis=0, keepdims=True)
    lb = jnp.sum(sm[:layer + 1], axis=0, keepdims=True) - sm[0:1]

    lf_nat = _log_sigmoid(_mm(xb, wfl_ref[...]) + bfl_ref[...])
    lf_t = _log_sigmoid(_nt(wflt_ref[...], xb) + bflt_ref[...])

    r = lax.broadcasted_iota(jnp.int32, (tm, tm), 0)
    c = lax.broadcasted_iota(jnp.int32, (tm, tm), 1)
    same = (r // seg) == (c // seg)
    lower = jnp.where((c <= r) & same, 1.0, 0.0).astype(BF16)
    d_nat = _cumsum_rows([lf_nat], lower)[0]

    fq = proj(W_FQ, FX_WIDTH)
    k = proj(W_FK, FX_WIDTH)
    v = proj(W_FV, FX_WIDTH)
    if prompt:
        _, qa_ref, ka_ref, kt_ref, vt_ref, lft_ref, cn_ref, kv_ref = refs
        if tiles_per_seq > 1:
            @pl.when(pl.program_id(0) % tiles_per_seq == 0)
            def _():
                cn_ref[...] = jnp.zeros_like(cn_ref)
            d_nat = d_nat + cn_ref[0:1, :]
            cn_ref[...] = jnp.broadcast_to(d_nat[tm - 1:tm, :], cn_ref.shape)
        lft_ref[0] = lf_t[:FX_HEADS]
        kv_ref[:, 0:FX_WIDTH] = k
        kv_ref[:, FX_WIDTH:] = v
        kt_ref[0] = kv_ref[:, 0:FX_WIDTH].T
        vt_ref[0] = kv_ref[:, FX_WIDTH:].T
        d2 = d_nat * LOG2E
        for h in range(FX_HEADS):
            hl = slice(h * LANES, (h + 1) * LANES)
            qa_ref[:, hl] = _aug_operand(fq[:, _pair_lanes(h)] * (FX_DIM ** -0.5 * LOG2E),
                                         d2[:, h:h + 1], h % 2, True)
            ka_ref[:, hl] = _aug_operand(k[:, _pair_lanes(h)], d2[:, h:h + 1], h % 2, False)
    else:
        _, k_ref, v_ref, lft_ref, dn_ref, dt_ref = refs
        upper = jnp.where((r <= c) & same, 1.0, 0.0).astype(F32)
        dt_ref[...] = _mm(lf_t, upper, precision=lax.Precision.HIGHEST)
        lft_ref[...] = lf_t
        k_ref[...] = k
        v_ref[...] = v
        dn_ref[...] = d_nat
        act_ref[:, cols["FQ"]:cols["FQ"] + FX_WIDTH] = fq

    act_ref[:, C_HQ:C_HQ + HG_WIDTH] = _silu(proj(W_HQ, HG_WIDTH))
    forget = lb + (1.0 - lb) * jax.nn.sigmoid(proj(W_HF, HG_WIDTH))
    act_ref[:, C_HK:C_HK + HG_WIDTH] = 1.0 - forget
    act_ref[:, C_HG:C_HG + HG_WIDTH] = jnp.log2(forget)
    act_ref[:, C_HV:C_HV + HG_WIDTH] = proj(W_HI, HG_WIDTH)
    act_ref[:, C_HGATE:C_HGATE + HG_WIDTH] = _silu(proj(W_HGATE, HG_WIDTH))
    act_ref[:, cols["FGATE"]:cols["FGATE"] + FX_WIDTH] = _silu(proj(W_FGATE, FX_WIDTH))
    act_ref[:, cols["MQ"]:cols["MQ"] + MEM_WIDTH] = proj(W_MQ, MEM_WIDTH)
    act_ref[:, cols["MGATE"]:cols["MGATE"] + MEM_WIDTH] = _silu(proj(W_MGATE, MEM_WIDTH))


def _proj(x, w_main, w_fl, w_flt, b_fl, b_flt, lbs, *, layer, batch, seq_len, tm, prompt):
    m, d = x.shape
    ntiles = m // tm
    seg = min(seq_len, tm)
    tps = max(seq_len // tm, 1)
    assert prompt or ntiles == 1
    kern = functools.partial(_proj_kernel, layer=layer, seg=seg, tiles_per_seq=tps, prompt=prompt)
    full = lambda shape: pl.BlockSpec(shape, lambda i: (0,) * len(shape))
    rows = lambda width: pl.BlockSpec((tm, width), lambda i: (i, 0))
    feat = lambda nfeat: pl.BlockSpec((1, nfeat, tm), lambda i: (i // tps, 0, i % tps))
    if prompt:
        out_specs = [rows(ACT_PROMPT["WIDTH"]), rows(FX_HEADS * LANES), rows(FX_HEADS * LANES),
                     feat(FX_WIDTH), feat(FX_WIDTH), feat(FX_HEADS)]
        out_shape = [jax.ShapeDtypeStruct((m, ACT_PROMPT["WIDTH"]), F32),
                     jax.ShapeDtypeStruct((m, FX_HEADS * LANES), BF16),
                     jax.ShapeDtypeStruct((m, FX_HEADS * LANES), BF16),
                     jax.ShapeDtypeStruct((batch, FX_WIDTH, seq_len), F32),
                     jax.ShapeDtypeStruct((batch, FX_WIDTH, seq_len), F32),
                     jax.ShapeDtypeStruct((batch, FX_HEADS, seq_len), F32)]
        scratch = [pltpu.VMEM((SUBLANES, LANES), F32), pltpu.VMEM((tm, 2 * FX_WIDTH), F32)]
    else:
        out_specs = [rows(ACT_SAMPLE["WIDTH"]), rows(FX_WIDTH), rows(FX_WIDTH), full((SUBLANES, m)),
                     rows(LANES), full((SUBLANES, m))]
        out_shape = [jax.ShapeDtypeStruct((m, ACT_SAMPLE["WIDTH"]), F32),
                     jax.ShapeDtypeStruct((m, FX_WIDTH), F32),
                     jax.ShapeDtypeStruct((m, FX_WIDTH), F32),
                     jax.ShapeDtypeStruct((SUBLANES, m), F32),
                     jax.ShapeDtypeStruct((m, LANES), F32),
                     jax.ShapeDtypeStruct((SUBLANES, m), F32)]
        scratch = []
    return pl.pallas_call(
        kern,
        grid=(ntiles,),
        in_specs=[
            pl.BlockSpec((tm, d), lambda i: (i, 0)),
            pl.BlockSpec((w_main.shape[0], d), lambda i: (0, layer)), full(w_fl.shape), full(w_flt.shape),
            full(b_fl.shape), full(b_flt.shape), full(lbs.shape),
        ],
        out_specs=out_specs,
        out_shape=out_shape,
        scratch_shapes=scratch,
        compiler_params=pltpu.CompilerParams(
            dimension_semantics=("arbitrary",), vmem_limit_bytes=VMEM_LIMIT),
        name="proj_prompt" if prompt else "proj_sample",
    )(x, w_main, w_fl, w_flt, b_fl, b_flt, lbs)


def _half_ref(gc, m, rows):
    n, w = gc.shape
    if 2 * m >= SUBLANES:
        g3 = gc.reshape(n // (2 * m), 2 * m, w)
        return jnp.broadcast_to(g3[:, m - 1:m, :], g3.shape).reshape(n, w)
    pos = rows % (2 * m)
    if m == 2:
        return jnp.where(pos == 0, pltpu.roll(gc, n - 1, axis=0),
                         jnp.where(pos == 1, gc,
                                   jnp.where(pos == 2, pltpu.roll(gc, 1, axis=0),
                                             pltpu.roll(gc, 2, axis=0))))
    return jnp.where(pos == 0, gc, pltpu.roll(gc, 1, axis=0))


def _level_exponent(g2, m, rows):
    ref = _half_ref(g2, m, rows)
    return jnp.where((rows % (2 * m)) < m, ref - g2, g2 - ref)


def _pair_levels(n):
    rr = lax.broadcasted_iota(jnp.int32, (n, n), 0)
    cc = lax.broadcasted_iota(jnp.int32, (n, n), 1)
    xf = (rr ^ cc).astype(F32)
    level = (lax.bitcast_convert_type(xf, jnp.int32) >> 23) - 127
    return jnp.where(rr > cc, level, -1)


def _cumsum_rows(gs, lower):
    w = gs[0].shape[1]
    pieces = [p.astype(BF16) for g in gs for p in _split3(g)]
    s = _mm(lower, jnp.concatenate(pieces, axis=1))
    return [s[:, (3 * i) * w:(3 * i + 1) * w] + s[:, (3 * i + 1) * w:(3 * i + 2) * w]
            + s[:, (3 * i + 2) * w:(3 * i + 3) * w] for i in range(len(gs))]


def _hgrn_intra(q, k, g2, levels):
    n, w = q.shape
    rows = lax.broadcasted_iota(jnp.int32, (n, w), 0)
    nslab = n // SUBLANES
    slab = lambda x, r: x[r * SUBLANES:(r + 1) * SUBLANES]
    a = [jnp.zeros((SUBLANES, n), F32)] * nslab
    m, bit = n // 2, n.bit_length() - 2
    while m >= 1:
        if m >= SUBLANES:
            qs, ks, targets = [], [], []
            for r0 in range(0, n, 2 * m):
                first, second = slice(r0, r0 + m), slice(r0 + m, r0 + 2 * m)
                ref = g2[r0 + m - 1:r0 + m]
                ks += [k[first] * jnp.exp2(ref - g2[first]), jnp.zeros((m, w), F32)]
                qs.append(q[second] * jnp.exp2(g2[second] - ref))
                targets += range((r0 + m) // SUBLANES, (r0 + 2 * m) // SUBLANES)
            a_m = _nt(jnp.concatenate(qs, axis=0).astype(BF16), jnp.concatenate(ks, axis=0).astype(BF16))
            for i, r in enumerate(targets):
                a[r] = jnp.where(slab(levels, r) == bit, slab(a_m, i), a[r])
        else:
            f = jnp.exp2(_level_exponent(g2, m, rows))
            a_m = _nt((q * f).astype(BF16), (k * f).astype(BF16))
            a = [jnp.where(slab(levels, r) == bit, slab(a_m, r), a[r]) for r in range(nslab)]
        m, bit = m // 2, bit - 1
    return jnp.concatenate(a, axis=0)


def _hgrn_kernel(*refs, group, has_init):
    if has_init:
        q_ref, k_ref, v_ref, g_ref, gate_ref, ng_ref, s0_ref, o_ref, so_ref = refs[:9]
    else:
        q_ref, k_ref, v_ref, g_ref, gate_ref, ng_ref, o_ref, so_ref = refs[:8]
    st_ref, a_ref, qe_ref, ke_ref, vb_ref, dg_ref, dec_ref = refs[-7:]
    t = pl.program_id(2)
    nheads = st_ref.shape[0]
    ngroups = q_ref.shape[0] // group
    lanes = [slice(hh * HG_DIM, (hh + 1) * HG_DIM) for hh in range(nheads)]

    @pl.when(t == 0)
    def _():
        for hh in range(nheads):
            st_ref[hh] = s0_ref[0, 0, hh].T if has_init else jnp.zeros((HG_DIM, HG_DIM), F32)

    ng = ng_ref[...]
    levels = _pair_levels(group)
    rr = lax.broadcasted_iota(jnp.int32, (group, group), 0)
    cc = lax.broadcasted_iota(jnp.int32, (group, group), 1)
    lower = jnp.where(rr >= cc, 1.0, 0.0).astype(BF16)

    def prepare(i, s):
        sl = pl.ds(pl.multiple_of(i * group, group), group)
        g2s = _cumsum_rows([g_ref[sl, ls] for ls in lanes], lower)
        for hh, ls in enumerate(lanes):
            q, k, v, g2 = q_ref[sl, ls], k_ref[sl, ls], v_ref[sl, ls], g2s[hh]
            g_last = g2[group - 1:group, :]
            a_ref[s, hh] = _hgrn_intra(q, k, g2, levels).astype(BF16)
            qe_ref[s, hh] = (q * jnp.exp2(g2)).astype(BF16)
            ke_ref[s, hh] = (k * jnp.exp2(g_last - g2)).astype(BF16)
            vb_ref[s, hh] = v.astype(BF16)
            dg_ref[s, hh] = jnp.sum(q * k, axis=-1, keepdims=True) * v
            dec_ref[s, hh] = jnp.broadcast_to(jnp.exp2(g_last), (SUBLANES, HG_DIM))

    def finish(i, s):
        sl = pl.ds(pl.multiple_of(i * group, group), group)
        for hh, ls in enumerate(lanes):
            st = st_ref[hh]
            vb = vb_ref[s, hh]
            o = _nt(qe_ref[s, hh], st.astype(BF16)) + _mm(a_ref[s, hh], vb) + dg_ref[s, hh]
            upd = lax.dot_general(vb, ke_ref[s, hh], (((0,), (0,)), ((), ())), preferred_element_type=F32)
            st_ref[hh] = st * dec_ref[s, hh, 0:1, :] + upd
            o = o * lax.rsqrt(jnp.mean(o * o, axis=-1, keepdims=True) + RMS_EPS) * ng
            o_ref[sl, ls] = (o * gate_ref[sl, ls]).astype(o_ref.dtype)

    prepare(0, 0)
    if ngroups > 1:
        def step(ip, c):
            finish(2 * ip, 0)
            prepare(2 * ip + 1, 1)
            finish(2 * ip + 1, 1)
            prepare(2 * ip + 2, 0)
            return c

        lax.fori_loop(0, ngroups // 2 - 1, step, 0)
        finish(ngroups - 2, 0)
        prepare(ngroups - 1, 1)
        finish(ngroups - 1, 1)
    else:
        finish(0, 0)

    @pl.when(t == pl.num_programs(2) - 1)
    def _():
        for hh in range(nheads):
            so_ref[0, hh] = st_ref[hh].T


def _hgrn(act, norm_g, s0, *, layer, batch, seq_len, tt, nheads):
    m = act.shape[0]
    nt = seq_len // tt
    has_init = s0 is not None
    group = min(tt, HG_GROUP)
    assert tt == group or (tt // group) % 2 == 0, "groups are processed in pairs"
    width = nheads * HG_DIM
    blk = lambda cb: pl.BlockSpec((tt, width), lambda b, h, t, cb=cb: (b * nt + t, cb // width + h))
    in_specs = [blk(C_HQ), blk(C_HK), blk(C_HV), blk(C_HG), blk(C_HGATE),
                pl.BlockSpec((1, HG_DIM), lambda b, h, t: (0, 0))]
    args = [act, act, act, act, act, norm_g]
    if has_init:
        in_specs.append(pl.BlockSpec((1, 1, nheads, HG_DIM, HG_DIM), lambda b, h, t: (layer, b, h, 0, 0)))
        args.append(s0)
    return pl.pallas_call(
        functools.partial(_hgrn_kernel, group=group, has_init=has_init),
        grid=(batch, HG_HEADS // nheads, nt),
        in_specs=in_specs,
        out_specs=[
            pl.BlockSpec((tt, width), lambda b, h, t: (b * nt + t, h)),
            pl.BlockSpec((1, nheads, HG_DIM, HG_DIM), lambda b, h, t: (b, h, 0, 0)),
        ],
        out_shape=[
            jax.ShapeDtypeStruct((m, HG_WIDTH), BF16),
            jax.ShapeDtypeStruct((batch, HG_HEADS, HG_DIM, HG_DIM), F32),
        ],
        scratch_shapes=[
            pltpu.VMEM((nheads, HG_DIM, HG_DIM), F32),
            pltpu.VMEM((2, nheads, group, group), BF16),
            pltpu.VMEM((2, nheads, group, HG_DIM), BF16),
            pltpu.VMEM((2, nheads, group, HG_DIM), BF16),
            pltpu.VMEM((2, nheads, group, HG_DIM), BF16),
            pltpu.VMEM((2, nheads, group, HG_DIM), F32),
            pltpu.VMEM((2, nheads, SUBLANES, HG_DIM), F32),
        ],
        compiler_params=pltpu.CompilerParams(
            dimension_semantics=("arbitrary", "arbitrary", "arbitrary"), vmem_limit_bytes=VMEM_LIMIT),
        name="hgrn",
    )(*args)


VROWS = FX_DIM + 16


def _softmax_t(st, m):
    m_new = jnp.maximum(m, jnp.max(st, axis=0, keepdims=True))
    return m_new, jnp.exp2(m - m_new), jnp.exp2(st - m_new).astype(BF16)


def _with_ones(vt):
    return jnp.concatenate([vt.astype(BF16), jnp.ones((VROWS - FX_DIM, vt.shape[1]), BF16)], axis=0)


def _normalised(acc):
    return acc[0:FX_DIM] / acc[FX_DIM:FX_DIM + 1]


def _attn_prompt_kernel(qa_ref, fgate_ref, mq_ref, mgate_ref, ka_ref, vt_ref, mk_ref, mvt_ref,
                        o_ref, vtb_ref, mvtb_ref, st_ref, p_ref, acc_ref, *, tq):
    seq = qa_ref.shape[0]
    tk = tq
    ntiles = seq // tq
    heads = range(FX_HEADS)

    for j in range(seq // tk):
        for h in heads:
            vtb_ref[j, h] = _with_ones(vt_ref[0, h * FX_DIM:(h + 1) * FX_DIM, j * tk:(j + 1) * tk])
    for h in range(MEM_HEADS):
        mvtb_ref[h] = _with_ones(mvt_ref[0, h * MEM_DIM:(h + 1) * MEM_DIM, :])

    key = lax.broadcasted_iota(jnp.int32, (tk, tq), 0)
    qry = lax.broadcasted_iota(jnp.int32, (tk, tq), 1)

    def tile_rows(i):
        return pl.ds(pl.multiple_of(i * tq, tq), tq)

    def scores(i, j, h):
        hl = slice(h * LANES, (h + 1) * LANES)
        return _nt(ka_ref[tile_rows(j), hl], qa_ref[tile_rows(i), hl])

    def start(i, s):
        for h in heads:
            st_ref[s, 0, h] = scores(i, 0, h)
            p_ref[s, 1, h] = jnp.zeros((tk, tq), BF16)
            acc_ref[s, h] = jnp.zeros((VROWS, tq), F32)

    def block(i, s, j, b, carry):
        prev = jnp.maximum(j - 1, 0)
        out = []
        for h in heads:
            m, alpha_prev = carry[h]
            pv = _mm(vtb_ref[prev, h], p_ref[s, 1 - b, h])
            st = st_ref[s, b, h]
            st_ref[s, 1 - b, h] = scores(i, j + 1, h)
            m, alpha, p = _softmax_t(st, m)
            acc_ref[s, h] = alpha_prev * acc_ref[s, h] + pv
            p_ref[s, b, h] = p
            out.append((m, alpha))
        return tuple(out)

    def finish(i, s, carry):
        rows = tile_rows(i)
        prev = jnp.maximum(i - 1, 0)
        o_heads = []
        for h in heads:
            m, alpha_prev = carry[h]
            acc = alpha_prev * acc_ref[s, h] + _mm(vtb_ref[prev, h], p_ref[s, 1 - s, h])
            m, alpha, p = _softmax_t(jnp.where(key <= qry, st_ref[s, s, h], -jnp.inf), m)
            acc = alpha * acc + _mm(vtb_ref[i, h], p)
            o_heads.append(_normalised(acc))
        for pr in range(FX_HEADS // 2):
            ls = _pair_lanes(2 * pr)
            o_t = jnp.concatenate(o_heads[2 * pr:2 * pr + 2], axis=0)
            o_ref[rows, ls] = (o_t.T * fgate_ref[rows, ls]).astype(o_ref.dtype)

    def mem_scores(i, s):
        rows = tile_rows(i)
        for h in range(MEM_HEADS):
            mq_pair = mq_ref[rows, _pair_lanes(h)] * (MEM_DIM ** -0.5 * LOG2E)
            qm = jnp.where(_head_mask(h % 2), mq_pair, 0.0).astype(BF16)
            st_ref[s, 0, h] = _nt(mk_ref[:, _pair_lanes(h)], qm)

    def mem_finish(i, s):
        rows = tile_rows(i)
        for pr in range(MEM_HEADS // 2):
            ls = _pair_lanes(2 * pr)
            halves = []
            for hh in range(2):
                st = st_ref[s, 0, 2 * pr + hh]
                pm = jnp.exp2(st - jnp.max(st, axis=0, keepdims=True))
                halves.append(_normalised(_mm(mvtb_ref[2 * pr + hh], pm.astype(BF16))))
            o_t = jnp.concatenate(halves, axis=0)
            lm = slice(FX_WIDTH + pr * LANES, FX_WIDTH + (pr + 1) * LANES)
            o_ref[rows, lm] = (o_t.T * mgate_ref[rows, ls]).astype(o_ref.dtype)

    mem_scores(0, 0)

    def mem_pair(ip, c):
        mem_finish(2 * ip, 0)
        mem_scores(2 * ip + 1, 1)
        mem_finish(2 * ip + 1, 1)
        mem_scores(jnp.minimum(2 * ip + 2, ntiles - 1), 0)
        return c

    lax.fori_loop(0, ntiles // 2, mem_pair, 0)

    def qtile(ip, s):
        i = 2 * ip + s

        def two_blocks(jp, carry):
            return block(i, s, 2 * jp + 1, 1, block(i, s, 2 * jp, 0, carry))

        init = tuple((jnp.full((1, tq), -jnp.inf, F32), jnp.ones((1, tq), F32)) for _ in heads)
        carry = lax.fori_loop(0, ip, two_blocks, init)
        if s == 1:
            carry = block(i, s, i - 1, 0, carry)
        finish(i, s, carry)
        start(jnp.minimum(i + 1, ntiles - 1), 1 - s)

    start(0, 0)

    def tile_pair(ip, c):
        qtile(ip, 0)
        qtile(ip, 1)
        return c

    lax.fori_loop(0, ntiles // 2, tile_pair, 0)


def _attn_prompt(act, qa, ka, vt, mk, mvt, *, batch, seq_len, tq):
    m = act.shape[0]
    n_mem = mk.shape[0] // batch
    tk = tq
    assert (seq_len // tq) % 2 == 0, "query tiles are processed in pairs"
    ablk = lambda c0: pl.BlockSpec((seq_len, FX_WIDTH), lambda b, c0=c0: (b, c0 // FX_WIDTH))
    return pl.pallas_call(
        functools.partial(_attn_prompt_kernel, tq=tq),
        grid=(batch,),
        in_specs=[
            pl.BlockSpec((seq_len, FX_HEADS * LANES), lambda b: (b, 0)),
            ablk(ACT_PROMPT["FGATE"]), ablk(ACT_PROMPT["MQ"]), ablk(ACT_PROMPT["MGATE"]),
            pl.BlockSpec((seq_len, FX_HEADS * LANES), lambda b: (b, 0)),
            pl.BlockSpec((1, FX_WIDTH, seq_len), lambda b: (b, 0, 0)),
            pl.BlockSpec((n_mem, MEM_WIDTH), lambda b: (b, 0)),
            pl.BlockSpec((1, MEM_WIDTH, n_mem), lambda b: (b, 0, 0)),
        ],
        out_specs=pl.BlockSpec((seq_len, FX_WIDTH + MEM_WIDTH), lambda b: (b, 0)),
        out_shape=jax.ShapeDtypeStruct((m, FX_WIDTH + MEM_WIDTH), BF16),
        scratch_shapes=[
            pltpu.VMEM((seq_len // tk, FX_HEADS, VROWS, tk), BF16),
            pltpu.VMEM((MEM_HEADS, VROWS, n_mem), BF16),
            pltpu.VMEM((2, 2, FX_HEADS, tk, tq), F32),
            pltpu.VMEM((2, 2, FX_HEADS, tk, tq), BF16),
            pltpu.VMEM((2, FX_HEADS, VROWS, tq), F32),
        ],
        compiler_params=pltpu.CompilerParams(
            dimension_semantics=("arbitrary",), vmem_limit_bytes=VMEM_LIMIT),
        name="attn_prompt",
    )(qa, act, act, act, ka, vt, mk, mvt)


def _stack_heads(x, scale):
    lane = lax.broadcasted_iota(jnp.int32, (1, x.shape[1]), 1)
    return jnp.concatenate([jnp.where(lane // FX_DIM == h, x * scale, 0.0) for h in range(FX_HEADS)],
                           axis=0).astype(BF16)


def _unstack_heads(o, ts):
    lane = lax.broadcasted_iota(jnp.int32, (1, o.shape[1]), 1)
    out = jnp.zeros((ts, o.shape[1]), F32)
    for h in range(FX_HEADS):
        out = jnp.where(lane // FX_DIM == h, o[h * ts:(h + 1) * ts, :], out)
    return out


def _per_head_rows(vals, ts):
    return jnp.concatenate([jnp.broadcast_to(v, (ts, v.shape[1])) for v in vals], axis=0)


def _softmax_update(s, pv_fn, m, l, acc):
    m_new = jnp.maximum(m, jnp.max(s, axis=-1, keepdims=True))
    alpha = jnp.exp(m - m_new)
    p = jnp.exp(s - m_new)
    l = alpha * l + jnp.sum(p, axis=-1, keepdims=True)
    acc = alpha * acc + pv_fn(p.astype(BF16))
    return m_new, l, acc


def _attn_sample_kernel(fq_ref, fgate_ref, mq_ref, mgate_ref, kn_ref, vn_ref, dn_ref, dt_ref,
                        pkt_ref, pvt_ref, plf_ref, mkt_ref, mvt_ref, o_ref, *, chunk):
    ts = fq_ref.shape[0]
    n_new = kn_ref.shape[0]
    past = pkt_ref.shape[3]
    nchunks = past // chunk
    b = pl.program_id(0)
    hrows = FX_HEADS * ts

    r = lax.broadcasted_iota(jnp.int32, (chunk, chunk), 0)
    c = lax.broadcasted_iota(jnp.int32, (chunk, chunk), 1)
    upper = jnp.where(r <= c, 1.0, 0.0).astype(F32)
    total = jnp.zeros((FX_HEADS, 1), F32)
    d_past = []
    for ci in range(nchunks):
        d = _mm(plf_ref[0, 0, :, ci * chunk:(ci + 1) * chunk], upper,
                precision=lax.Precision.HIGHEST) + total
        d_past.append(d)
        total = d[:, chunk - 1:chunk]

    dn = dn_ref[...]
    dt_new = dt_ref[...]
    tot = [total[h:h + 1, :] for h in range(FX_HEADS)]
    dq = jnp.concatenate([tot[h] + dn[:, h:h + 1] for h in range(FX_HEADS)], axis=0)
    q4 = _stack_heads(fq_ref[...], FX_DIM ** -0.5)

    carry = (jnp.full((hrows, 1), -jnp.inf, F32), jnp.zeros((hrows, 1), F32),
             jnp.zeros((hrows, FX_WIDTH), F32))
    for ci in range(nchunks):
        cs = slice(ci * chunk, (ci + 1) * chunk)
        dk = _per_head_rows([d_past[ci][h:h + 1, :] for h in range(FX_HEADS)], ts)
        s = _mm(q4, pkt_ref[0, 0, :, cs].astype(BF16)) + dq - dk
        vt = pvt_ref[0, 0, :, cs].astype(BF16)
        carry = _softmax_update(s, lambda p, vt=vt: _nt(p, vt), *carry)

    row = lax.broadcasted_iota(jnp.int32, (hrows, n_new), 0)
    col = lax.broadcasted_iota(jnp.int32, (hrows, n_new), 1)
    visible = (col // ts == b) & (col % ts <= row % ts)
    dk = _per_head_rows([tot[h] + dt_new[h:h + 1, :] for h in range(FX_HEADS)], ts)
    s = _nt(q4, kn_ref[...].astype(BF16)) + dq - dk
    vn = vn_ref[...].astype(BF16)
    _, l, acc = _softmax_update(jnp.where(visible, s, -jnp.inf), lambda p: _mm(p, vn), *carry)
    o_fx = _unstack_heads(acc / l, ts) * fgate_ref[...]
    o_ref[:, 0:FX_WIDTH] = o_fx.astype(o_ref.dtype)

    mq4 = _stack_heads(mq_ref[...], MEM_DIM ** -0.5)
    s = _mm(mq4, mkt_ref[0, 0].astype(BF16))
    p = jnp.exp(s - jnp.max(s, axis=-1, keepdims=True))
    o = _nt(p.astype(BF16), mvt_ref[0, 0].astype(BF16)) / jnp.sum(p, axis=-1, keepdims=True)
    o_mem = _unstack_heads(o, ts) * mgate_ref[...]
    o_ref[:, FX_WIDTH:] = o_mem.astype(o_ref.dtype)


def _attn_sample(act, k_new, v_new, d_nat, d_t, past_kt, past_vt, past_lft, mem_kt, mem_vt,
                 *, layer, batch, ts):
    m = act.shape[0]
    past = past_kt.shape[3]
    n_mem = mem_kt.shape[3]
    chunk = min(past, 512)
    ablk = lambda c0: pl.BlockSpec((ts, FX_WIDTH), lambda b, c0=c0: (b, c0 // FX_WIDTH))
    cache = lambda nfeat, n: pl.BlockSpec((1, 1, nfeat, n), lambda b: (layer, b, 0, 0))
    return pl.pallas_call(
        functools.partial(_attn_sample_kernel, chunk=chunk),
        grid=(batch,),
        in_specs=[
            ablk(ACT_SAMPLE["FQ"]), ablk(ACT_SAMPLE["FGATE"]), ablk(ACT_SAMPLE["MQ"]),
            ablk(ACT_SAMPLE["MGATE"]),
            pl.BlockSpec((m, FX_WIDTH), lambda b: (0, 0)),
            pl.BlockSpec((m, FX_WIDTH), lambda b: (0, 0)),
            pl.BlockSpec((ts, LANES), lambda b: (b, 0)),
            pl.BlockSpec((SUBLANES, m), lambda b: (0, 0)),
            cache(FX_WIDTH, past), cache(FX_WIDTH, past), cache(FX_HEADS, past),
            cache(MEM_WIDTH, n_mem), cache(MEM_WIDTH, n_mem),
        ],
        out_specs=pl.BlockSpec((ts, FX_WIDTH + MEM_WIDTH), lambda b: (b, 0)),
        out_shape=jax.ShapeDtypeStruct((m, FX_WIDTH + MEM_WIDTH), BF16),
        compiler_params=pltpu.CompilerParams(
            dimension_semantics=("arbitrary",), vmem_limit_bytes=VMEM_LIMIT),
        name="attn_sample",
    )(act, act, act, act, k_new, v_new, d_nat, d_t, past_kt, past_vt, past_lft, mem_kt, mem_vt)


def _memkv_kernel(x_ref, w_ref, kb_ref, kt_ref, vt_ref):
    kv = _mm(x_ref[...].astype(BF16), w_ref[...])
    kb_ref[...] = kv[:, :MEM_WIDTH].astype(BF16)
    kt_ref[0] = kv[:, :MEM_WIDTH].T
    vt_ref[0] = kv[:, MEM_WIDTH:].T


def _memkv(mem, w, *, batch):
    m, d = mem.shape
    n_mem = m // batch
    return pl.pallas_call(
        _memkv_kernel,
        grid=(batch,),
        in_specs=[pl.BlockSpec((n_mem, d), lambda i: (i, 0)),
                  pl.BlockSpec(w.shape, lambda i: (0, 0))],
        out_specs=[pl.BlockSpec((n_mem, MEM_WIDTH), lambda i: (i, 0)),
                   pl.BlockSpec((1, MEM_WIDTH, n_mem), lambda i: (i, 0, 0)),
                   pl.BlockSpec((1, MEM_WIDTH, n_mem), lambda i: (i, 0, 0))],
        out_shape=[jax.ShapeDtypeStruct((m, MEM_WIDTH), BF16),
                   jax.ShapeDtypeStruct((batch, MEM_WIDTH, n_mem), F32),
                   jax.ShapeDtypeStruct((batch, MEM_WIDTH, n_mem), F32)],
        compiler_params=pltpu.CompilerParams(
            dimension_semantics=("arbitrary",), vmem_limit_bytes=VMEM_LIMIT),
        name="memkv",
    )(mem, w)


def _out_kernel(x_ref, hg_ref, at_ref, w_ref, g_ref, b_ref, y_ref, *, alpha):
    acc = _mm(hg_ref[...], w_ref[0:HG_WIDTH, :]) + _mm(at_ref[...], w_ref[HG_WIDTH:, :])
    z = alpha * x_ref[...] + acc
    mu = jnp.mean(z, axis=-1, keepdims=True)
    zc = z - mu
    var = jnp.mean(zc * zc, axis=-1, keepdims=True)
    y_ref[...] = zc * lax.rsqrt(var + LN_EPS) * g_ref[...] + b_ref[...]


def _out(x, mix_hg, mix_at, w, g, b, *, alpha, tm):
    m, d = x.shape
    return pl.pallas_call(
        functools.partial(_out_kernel, alpha=alpha),
        grid=(m // tm,),
        in_specs=[
            pl.BlockSpec((tm, d), lambda i: (i, 0)),
            pl.BlockSpec((tm, HG_WIDTH), lambda i: (i, 0)),
            pl.BlockSpec((tm, FX_WIDTH + MEM_WIDTH), lambda i: (i, 0)),
            pl.BlockSpec(w.shape, lambda i: (0, 0)),
            pl.BlockSpec((1, d), lambda i: (0, 0)),
            pl.BlockSpec((1, d), lambda i: (0, 0)),
        ],
        out_specs=pl.BlockSpec((tm, d), lambda i: (i, 0)),
        out_shape=jax.ShapeDtypeStruct((m, d), F32),
        compiler_params=pltpu.CompilerParams(
            dimension_semantics=("arbitrary",), vmem_limit_bytes=VMEM_LIMIT),
        name="out",
    )(x, mix_hg, mix_at, w, g, b)


def _pick_tile(n, target):
    t = min(n, target)
    while n % t:
        t //= 2
    return t


def _feature_major(x):
    lead = x.shape[:-3]
    n, heads, dim = x.shape[-3:]
    nl = len(lead)
    perm = tuple(range(nl)) + (nl + 1, nl + 2, nl)
    return jnp.transpose(x, perm).reshape(lead + (heads * dim, n))


def _token_major(x, heads):
    lead = x.shape[:-2]
    width, n = x.shape[-2:]
    nl = len(lead)
    x = x.reshape(lead + (heads, width // heads, n))
    return jnp.transpose(x, tuple(range(nl)) + (nl + 2, nl, nl + 1))


def kernel(x_prompt, x_sample, mem_prompt, state_hgrn, cache_fox_k, cache_fox_v, cache_fox_logf,
           cache_mem_k, cache_mem_v, w_in, b_fox_forget, hgrn_lower_bounds, hgrn_norm_g,
           w_mem_kv, w_out, ln_g, ln_b):
    depth, d_model, _ = w_in.shape
    bp, seq, _ = x_prompt.shape
    bs, dseq, _ = x_sample.shape
    n_mem = mem_prompt.shape[1]
    alpha = (2 * depth) ** 0.25

    fl0 = 4 * HG_WIDTH + 4 * FX_WIDTH
    w_t = jnp.transpose(w_in, (2, 0, 1))
    w_main = jnp.concatenate([w_t[:fl0], w_t[fl0 + FX_HEADS:]], axis=0).astype(BF16)
    w_main = w_main.reshape(w_main.shape[0], depth * d_model)
    w_flc = w_in[:, :, fl0:fl0 + FX_HEADS]
    w_fl = jnp.pad(w_flc, ((0, 0), (0, 0), (0, LANES - FX_HEADS))).astype(BF16)
    w_flt = jnp.pad(jnp.swapaxes(w_flc, 1, 2), ((0, 0), (0, SUBLANES - FX_HEADS), (0, 0))).astype(BF16)
    b_fl = jnp.pad(b_fox_forget, ((0, 0), (0, LANES - FX_HEADS)))[:, None, :]
    b_flt = jnp.pad(b_fox_forget, ((0, 0), (0, SUBLANES - FX_HEADS)))[:, :, None]
    lbs = hgrn_lower_bounds.astype(F32)
    w_memb = w_mem_kv.astype(BF16)
    w_outb = w_out.astype(BF16)

    past_kt = _feature_major(cache_fox_k)
    past_vt = _feature_major(cache_fox_v)
    past_lft = jnp.swapaxes(cache_fox_logf, 2, 3)
    mem_kt = _feature_major(cache_mem_k)
    mem_vt = _feature_major(cache_mem_v)

    mp, ms = bp * seq, bs * dseq
    tb = _pick_tile(seq, 256)
    yp = x_prompt.reshape(mp, d_model)
    ys = x_sample.reshape(ms, d_model)
    mem2 = mem_prompt.reshape(bp * n_mem, d_model)

    p_hg, p_kt, p_vt, p_lft, p_mkt, p_mvt = [], [], [], [], [], []
    s_hg, s_k, s_v, s_lft = [], [], [], []
    for l in range(depth):
        wl = (w_main, w_fl[l], w_flt[l], b_fl[l], b_flt[l], lbs)
        ng = hgrn_norm_g[l][None, :]
        g_ln, b_ln = ln_g[l][None, :], ln_b[l][None, :]

        mkb, mkt, mvt = _memkv(mem2, w_memb[l], batch=bp)
        act, qa, ka, kt, vt, lft = _proj(yp, *wl, layer=l, batch=bp, seq_len=seq,
                                         tm=_pick_tile(seq, 512), prompt=True)
        mix_hg, hg = _hgrn(act, ng, None, layer=l, batch=bp, seq_len=seq, tt=_pick_tile(seq, 2048),
                           nheads=HG_HEADS)
        mix_at = _attn_prompt(act, qa, ka, vt, mkb, mvt, batch=bp, seq_len=seq, tq=tb)
        yp = _out(yp, mix_hg, mix_at, w_outb[l], g_ln, b_ln, alpha=alpha, tm=_pick_tile(mp, 1024))
        p_hg.append(hg); p_kt.append(kt); p_vt.append(vt); p_lft.append(lft)
        p_mkt.append(mkt); p_mvt.append(mvt)

        act, k, v, lft, d_nat, d_t = _proj(ys, *wl, layer=l, batch=bs, seq_len=dseq, tm=ms, prompt=False)
        mix_hg, hg = _hgrn(act, ng, state_hgrn, layer=l, batch=bs, seq_len=dseq, tt=dseq, nheads=HG_HEADS)
        mix_at = _attn_sample(act, k, v, d_nat, d_t, past_kt, past_vt, past_lft, mem_kt, mem_vt,
                              layer=l, batch=bs, ts=dseq)
        ys = _out(ys, mix_hg, mix_at, w_outb[l], g_ln, b_ln, alpha=alpha, tm=ms)
        s_hg.append(hg); s_k.append(k); s_v.append(v); s_lft.append(lft)

    st = jnp.stack
    s_lf = st(s_lft)[:, :FX_HEADS, :].reshape(depth, FX_HEADS, bs, dseq)
    return (yp.reshape(bp, seq, d_model), ys.reshape(bs, dseq, d_model),
            st(p_hg),
            _token_major(st(p_kt), FX_HEADS),
            _token_major(st(p_vt), FX_HEADS),
            jnp.swapaxes(st(p_lft), 2, 3),
            _token_major(st(p_mkt), MEM_HEADS),
            _token_major(st(p_mvt), MEM_HEADS),
            st(s_hg),
            st(s_k).reshape(depth, bs, dseq, FX_HEADS, FX_DIM),
            st(s_v).reshape(depth, bs, dseq, FX_HEADS, FX_DIM),
            jnp.transpose(s_lf, (0, 2, 3, 1)))
```

```python
import functools

import jax
import jax.numpy as jnp
from jax import lax
from jax.experimental import pallas as pl
from jax.experimental.pallas import tpu as pltpu

F32 = jnp.float32
BF16 = jnp.bfloat16

HG_HEADS = 4
HG_DIM = 128
HG_WIDTH = HG_HEADS * HG_DIM
FX_HEADS = 4
FX_DIM = 64
FX_WIDTH = FX_HEADS * FX_DIM
MEM_HEADS = 4
MEM_DIM = 64
MEM_WIDTH = MEM_HEADS * MEM_DIM
HG_GROUP = 128
LN_EPS = 1e-5
RMS_EPS = 1e-6
LANES = 128
SUBLANES = 8
VMEM_LIMIT = 56 * 1024 * 1024
AUG = 3
LOG2E = 1.4426950408889634
SAMPLE_CHUNK = 512

C_HQ, C_HK, C_HV, C_HG, C_HGATE = 0, 512, 1024, 1536, 2048
ACT_PROMPT = dict(FGATE=2560, MQ=2816, MGATE=3072, WIDTH=3328)
ACT_SAMPLE = dict(FQ=2560, FGATE=2816, MQ=3072, MGATE=3328, WIDTH=3584)
W_HQ, W_HF, W_HI, W_HGATE, W_FQ, W_FK, W_FV, W_FGATE, W_MQ, W_MGATE = (
    0, 512, 1024, 1536, 2048, 2304, 2560, 2816, 3072, 3328)


def _nt(a, b, precision=None):
    return lax.dot_general(a, b, (((1,), (1,)), ((), ())), precision=precision,
                           preferred_element_type=F32)


def _mm(a, b, precision=None):
    return jnp.dot(a, b, precision=precision, preferred_element_type=F32)


def _silu(x):
    return x * jax.nn.sigmoid(x)


def _log_sigmoid(x):
    return jnp.minimum(x, 0.0) - jnp.log(1.0 + jnp.exp(-jnp.abs(x)))


def _head_mask(hh):
    lane = lax.broadcasted_iota(jnp.int32, (1, LANES), 1)
    return (lane // FX_DIM) == hh


def _pair_lanes(h):
    return slice((h // 2) * LANES, (h // 2 + 1) * LANES)


def _split3(x):
    hi = x.astype(BF16).astype(F32)
    r = x - hi
    mid = r.astype(BF16).astype(F32)
    return hi, mid, r - mid


def _aug_operand(x_pair, d, hh, is_query):
    lane = lax.broadcasted_iota(jnp.int32, (1, LANES), 1)
    a = lane - (1 - hh) * FX_DIM
    hi, mid, lo = _split3(d if is_query else -d)
    p0, o0 = (0, AUG) if is_query else (AUG, 0)
    extra = jnp.where(a == p0, hi, jnp.where(a == p0 + 1, mid, jnp.where(a == p0 + 2, lo,
                      jnp.where((a >= o0) & (a < o0 + AUG), 1.0, 0.0))))
    return jnp.where(_head_mask(hh), x_pair, extra).astype(BF16)


def _proj_kernel(x_ref, w_ref, wfl_ref, wflt_ref, bfl_ref, bflt_ref, lbs_ref, *refs,
                 layer, seg, tiles_per_seq, prompt):
    act_ref = refs[0]
    cols = ACT_PROMPT if prompt else ACT_SAMPLE
    tm = x_ref.shape[0]
    xb = x_ref[...].astype(BF16)

    def proj(c0, width):
        return _mm(xb, w_ref[:, c0:c0 + width])

    lbs = lbs_ref[...]
    e = jnp.exp(lbs - jnp.max(lbs, axis=0, keepdims=True))
    sm = e / jnp.sum(e, axis=0, keepdims=True)
    lb = jnp.sum(sm[:layer + 1], axis=0, keepdims=True) - sm[0:1]

    lf_nat = _log_sigmoid(_mm(xb, wfl_ref[...]) + bfl_ref[...])
    lf_t = _log_sigmoid(_nt(wflt_ref[...], xb) + bflt_ref[...])

    r = lax.broadcasted_iota(jnp.int32, (tm, tm), 0)
    c = lax.broadcasted_iota(jnp.int32, (tm, tm), 1)
    same = (r // seg) == (c // seg)
    lower = jnp.where((c <= r) & same, 1.0, 0.0).astype(BF16)
    d_nat = _cumsum_rows([lf_nat], lower)[0]

    fq = proj(W_FQ, FX_WIDTH)
    k = proj(W_FK, FX_WIDTH)
    v = proj(W_FV, FX_WIDTH)
    if prompt:
        _, qa_ref, ka_ref, kt_ref, vt_ref, lft_ref, cn_ref, kv_ref = refs
        if tiles_per_seq > 1:
            @pl.when(pl.program_id(0) % tiles_per_seq == 0)
            def _():
                cn_ref[...] = jnp.zeros_like(cn_ref)
            d_nat = d_nat + cn_ref[0:1, :]
            cn_ref[...] = jnp.broadcast_to(d_nat[tm - 1:tm, :], cn_ref.shape)
        lft_ref[0] = lf_t[:FX_HEADS]
        kv_ref[:, 0:FX_WIDTH] = k
        kv_ref[:, FX_WIDTH:] = v
        kt_ref[0] = kv_ref[:, 0:FX_WIDTH].T
        vt_ref[0] = kv_ref[:, FX_WIDTH:].T
        d2 = d_nat * LOG2E
        for h in range(FX_HEADS):
            hl = slice(h * LANES, (h + 1) * LANES)
            qa_ref[:, hl] = _aug_operand(fq[:, _pair_lanes(h)] * (FX_DIM ** -0.5 * LOG2E),
                                         d2[:, h:h + 1], h % 2, True)
            ka_ref[:, hl] = _aug_operand(k[:, _pair_lanes(h)], d2[:, h:h + 1], h % 2, False)
    else:
        _, k_ref, v_ref, lft_ref, dn_ref, dt_ref = refs
        upper = jnp.where((r <= c) & same, 1.0, 0.0).astype(F32)
        dt_ref[...] = _mm(lf_t, upper, precision=lax.Precision.HIGHEST)
        lft_ref[...] = lf_t
        k_ref[...] = k
        v_ref[...] = v
        dn_ref[...] = d_nat
        act_ref[:, cols["FQ"]:cols["FQ"] + FX_WIDTH] = fq

    act_ref[:, C_HQ:C_HQ + HG_WIDTH] = _silu(proj(W_HQ, HG_WIDTH))
    forget = lb + (1.0 - lb) * jax.nn.sigmoid(proj(W_HF, HG_WIDTH))
    act_ref[:, C_HK:C_HK + HG_WIDTH] = 1.0 - forget
    act_ref[:, C_HG:C_HG + HG_WIDTH] = jnp.log2(forget)
    act_ref[:, C_HV:C_HV + HG_WIDTH] = proj(W_HI, HG_WIDTH)
    act_ref[:, C_HGATE:C_HGATE + HG_WIDTH] = _silu(proj(W_HGATE, HG_WIDTH))
    act_ref[:, cols["FGATE"]:cols["FGATE"] + FX_WIDTH] = _silu(proj(W_FGATE, FX_WIDTH))
    act_ref[:, cols["MQ"]:cols["MQ"] + MEM_WIDTH] = proj(W_MQ, MEM_WIDTH)
    act_ref[:, cols["MGATE"]:cols["MGATE"] + MEM_WIDTH] = _silu(proj(W_MGATE, MEM_WIDTH))


def _proj(x, w_main, w_fl, w_flt, b_fl, b_flt, lbs, *, layer, batch, seq_len, tm, prompt):
    m, d = x.shape
    ntiles = m // tm
    seg = min(seq_len, tm)
    tps = max(seq_len // tm, 1)
    assert prompt or ntiles == 1
    kern = functools.partial(_proj_kernel, layer=layer, seg=seg, tiles_per_seq=tps, prompt=prompt)
    full = lambda shape: pl.BlockSpec(shape, lambda i: (0,) * len(shape))
    rows = lambda width: pl.BlockSpec((tm, width), lambda i: (i, 0))
    feat = lambda nfeat: pl.BlockSpec((1, nfeat, tm), lambda i: (i // tps, 0, i % tps))
    if prompt:
        out_specs = [rows(ACT_PROMPT["WIDTH"]), rows(FX_HEADS * LANES), rows(FX_HEADS * LANES),
                     feat(FX_WIDTH), feat(FX_WIDTH), feat(FX_HEADS)]
        out_shape = [jax.ShapeDtypeStruct((m, ACT_PROMPT["WIDTH"]), F32),
                     jax.ShapeDtypeStruct((m, FX_HEADS * LANES), BF16),
                     jax.ShapeDtypeStruct((m, FX_HEADS * LANES), BF16),
                     jax.ShapeDtypeStruct((batch, FX_WIDTH, seq_len), F32),
                     jax.ShapeDtypeStruct((batch, FX_WIDTH, seq_len), F32),
                     jax.ShapeDtypeStruct((batch, FX_HEADS, seq_len), F32)]
        scratch = [pltpu.VMEM((SUBLANES, LANES), F32), pltpu.VMEM((tm, 2 * FX_WIDTH), F32)]
    else:
        out_specs = [rows(ACT_SAMPLE["WIDTH"]), rows(FX_WIDTH), rows(FX_WIDTH), full((SUBLANES, m)),
                     rows(LANES), full((SUBLANES, m))]
        out_shape = [jax.ShapeDtypeStruct((m, ACT_SAMPLE["WIDTH"]), F32),
                     jax.ShapeDtypeStruct((m, FX_WIDTH), F32),
                     jax.ShapeDtypeStruct((m, FX_WIDTH), F32),
                     jax.ShapeDtypeStruct((SUBLANES, m), F32),
                     jax.ShapeDtypeStruct((m, LANES), F32),
                     jax.ShapeDtypeStruct((SUBLANES, m), F32)]
        scratch = []
    return pl.pallas_call(
        kern,
        grid=(ntiles,),
        in_specs=[
            pl.BlockSpec((tm, d), lambda i: (i, 0)),
            full(w_main.shape), full(w_fl.shape), full(w_flt.shape),
            full(b_fl.shape), full(b_flt.shape), full(lbs.shape),
        ],
        out_specs=out_specs,
        out_shape=out_shape,
        scratch_shapes=scratch,
        compiler_params=pltpu.CompilerParams(
            dimension_semantics=("arbitrary",), vmem_limit_bytes=VMEM_LIMIT),
        name="proj_prompt" if prompt else "proj_sample",
    )(x, w_main, w_fl, w_flt, b_fl, b_flt, lbs)


def _half_ref(gc, m, rows):
    n, w = gc.shape
    if 2 * m >= SUBLANES:
        g3 = gc.reshape(n // (2 * m), 2 * m, w)
        return jnp.broadcast_to(g3[:, m - 1:m, :], g3.shape).reshape(n, w)
    pos = rows % (2 * m)
    if m == 2:
        return jnp.where(pos == 0, pltpu.roll(gc, n - 1, axis=0),
                         jnp.where(pos == 1, gc,
                                   jnp.where(pos == 2, pltpu.roll(gc, 1, axis=0),
                                             pltpu.roll(gc, 2, axis=0))))
    return jnp.where(pos == 0, gc, pltpu.roll(gc, 1, axis=0))


def _level_exponent(g2, m, rows):
    ref = _half_ref(g2, m, rows)
    return jnp.where((rows % (2 * m)) < m, ref - g2, g2 - ref)


def _pair_levels(n):
    rr = lax.broadcasted_iota(jnp.int32, (n, n), 0)
    cc = lax.broadcasted_iota(jnp.int32, (n, n), 1)
    xf = (rr ^ cc).astype(F32)
    level = (lax.bitcast_convert_type(xf, jnp.int32) >> 23) - 127
    return jnp.where(rr > cc, level, -1)


def _cumsum_rows(gs, lower):
    w = gs[0].shape[1]
    pieces = [p.astype(BF16) for g in gs for p in _split3(g)]
    s = _mm(lower, jnp.concatenate(pieces, axis=1))
    return [s[:, (3 * i) * w:(3 * i + 1) * w] + s[:, (3 * i + 1) * w:(3 * i + 2) * w]
            + s[:, (3 * i + 2) * w:(3 * i + 3) * w] for i in range(len(gs))]


def _hgrn_intra(q, k, g2, levels):
    n, w = q.shape
    rows = lax.broadcasted_iota(jnp.int32, (n, w), 0)
    nslab = n // SUBLANES
    slab = lambda x, r: x[r * SUBLANES:(r + 1) * SUBLANES]
    a = [jnp.zeros((SUBLANES, n), F32)] * nslab
    m, bit = n // 2, n.bit_length() - 2
    while m >= 1:
        if m >= SUBLANES:
            qs, ks, targets = [], [], []
            for r0 in range(0, n, 2 * m):
                first, second = slice(r0, r0 + m), slice(r0 + m, r0 + 2 * m)
                ref = g2[r0 + m - 1:r0 + m]
                ks += [k[first] * jnp.exp2(ref - g2[first]), jnp.zeros((m, w), F32)]
                qs.append(q[second] * jnp.exp2(g2[second] - ref))
                targets += range((r0 + m) // SUBLANES, (r0 + 2 * m) // SUBLANES)
            a_m = _nt(jnp.concatenate(qs, axis=0).astype(BF16), jnp.concatenate(ks, axis=0).astype(BF16))
            for i, r in enumerate(targets):
                a[r] = jnp.where(slab(levels, r) == bit, slab(a_m, i), a[r])
        else:
            f = jnp.exp2(_level_exponent(g2, m, rows))
            a_m = _nt((q * f).astype(BF16), (k * f).astype(BF16))
            a = [jnp.where(slab(levels, r) == bit, slab(a_m, r), a[r]) for r in range(nslab)]
        m, bit = m // 2, bit - 1
    return jnp.concatenate(a, axis=0)


def _hgrn_kernel(*refs, group, has_init):
    if has_init:
        q_ref, k_ref, v_ref, g_ref, gate_ref, ng_ref, s0_ref, o_ref, so_ref = refs[:9]
    else:
        q_ref, k_ref, v_ref, g_ref, gate_ref, ng_ref, o_ref, so_ref = refs[:8]
    st_ref, a_ref, qe_ref, ke_ref, vb_ref, dg_ref, dec_ref = refs[-7:]
    t = pl.program_id(2)
    nheads = st_ref.shape[0]
    ngroups = q_ref.shape[0] // group
    lanes = [slice(hh * HG_DIM, (hh + 1) * HG_DIM) for hh in range(nheads)]

    @pl.when(t == 0)
    def _():
        for hh in range(nheads):
            st_ref[hh] = s0_ref[0, 0, hh].T if has_init else jnp.zeros((HG_DIM, HG_DIM), F32)

    ng = ng_ref[...]
    levels = _pair_levels(group)
    rr = lax.broadcasted_iota(jnp.int32, (group, group), 0)
    cc = lax.broadcasted_iota(jnp.int32, (group, group), 1)
    lower = jnp.where(rr >= cc, 1.0, 0.0).astype(BF16)

    def prepare(i, s):
        sl = pl.ds(pl.multiple_of(i * group, group), group)
        g2s = _cumsum_rows([g_ref[sl, ls] for ls in lanes], lower)
        for hh, ls in enumerate(lanes):
            q, k, v, g2 = q_ref[sl, ls], k_ref[sl, ls], v_ref[sl, ls], g2s[hh]
            g_last = g2[group - 1:group, :]
            a_ref[s, hh] = _hgrn_intra(q, k, g2, levels).astype(BF16)
            qe_ref[s, hh] = (q * jnp.exp2(g2)).astype(BF16)
            ke_ref[s, hh] = (k * jnp.exp2(g_last - g2)).astype(BF16)
            vb_ref[s, hh] = v.astype(BF16)
            dg_ref[s, hh] = jnp.sum(q * k, axis=-1, keepdims=True) * v
            dec_ref[s, hh] = jnp.broadcast_to(jnp.exp2(g_last), (SUBLANES, HG_DIM))

    def finish(i, s):
        sl = pl.ds(pl.multiple_of(i * group, group), group)
        for hh, ls in enumerate(lanes):
            st = st_ref[hh]
            vb = vb_ref[s, hh]
            o = _nt(qe_ref[s, hh], st.astype(BF16)) + _mm(a_ref[s, hh], vb) + dg_ref[s, hh]
            upd = lax.dot_general(vb, ke_ref[s, hh], (((0,), (0,)), ((), ())), preferred_element_type=F32)
            st_ref[hh] = st * dec_ref[s, hh, 0:1, :] + upd
            o = o * lax.rsqrt(jnp.mean(o * o, axis=-1, keepdims=True) + RMS_EPS) * ng
            o_ref[sl, ls] = (o * gate_ref[sl, ls]).astype(o_ref.dtype)

    prepare(0, 0)
    if ngroups > 1:
        def step(ip, c):
            finish(2 * ip, 0)
            prepare(2 * ip + 1, 1)
            finish(2 * ip + 1, 1)
            prepare(2 * ip + 2, 0)
            return c

        lax.fori_loop(0, ngroups // 2 - 1, step, 0)
        finish(ngroups - 2, 0)
        prepare(ngroups - 1, 1)
        finish(ngroups - 1, 1)
    else:
        finish(0, 0)

    @pl.when(t == pl.num_programs(2) - 1)
    def _():
        for hh in range(nheads):
            so_ref[0, hh] = st_ref[hh].T


def _hgrn(act, norm_g, s0, *, layer, batch, seq_len, tt, nheads):
    m = act.shape[0]
    nt = seq_len // tt
    has_init = s0 is not None
    group = min(tt, HG_GROUP)
    assert tt == group or (tt // group) % 2 == 0, "groups are processed in pairs"
    width = nheads * HG_DIM
    blk = lambda cb: pl.BlockSpec((tt, width), lambda b, h, t, cb=cb: (b * nt + t, cb // width + h))
    in_specs = [blk(C_HQ), blk(C_HK), blk(C_HV), blk(C_HG), blk(C_HGATE),
                pl.BlockSpec((1, HG_DIM), lambda b, h, t: (0, 0))]
    args = [act, act, act, act, act, norm_g]
    if has_init:
        in_specs.append(pl.BlockSpec((1, 1, nheads, HG_DIM, HG_DIM), lambda b, h, t: (layer, b, h, 0, 0)))
        args.append(s0)
    return pl.pallas_call(
        functools.partial(_hgrn_kernel, group=group, has_init=has_init),
        grid=(batch, HG_HEADS // nheads, nt),
        in_specs=in_specs,
        out_specs=[
            pl.BlockSpec((tt, width), lambda b, h, t: (b * nt + t, h)),
            pl.BlockSpec((1, nheads, HG_DIM, HG_DIM), lambda b, h, t: (b, h, 0, 0)),
        ],
        out_shape=[
            jax.ShapeDtypeStruct((m, HG_WIDTH), BF16),
            jax.ShapeDtypeStruct((batch, HG_HEADS, HG_DIM, HG_DIM), F32),
        ],
        scratch_shapes=[
            pltpu.VMEM((nheads, HG_DIM, HG_DIM), F32),
            pltpu.VMEM((2, nheads, group, group), BF16),
            pltpu.VMEM((2, nheads, group, HG_DIM), BF16),
            pltpu.VMEM((2, nheads, group, HG_DIM), BF16),
            pltpu.VMEM((2, nheads, group, HG_DIM), BF16),
            pltpu.VMEM((2, nheads, group, HG_DIM), F32),
            pltpu.VMEM((2, nheads, SUBLANES, HG_DIM), F32),
        ],
        compiler_params=pltpu.CompilerParams(
            dimension_semantics=("arbitrary", "arbitrary", "arbitrary"), vmem_limit_bytes=VMEM_LIMIT),
        name="hgrn",
    )(*args)


VROWS = FX_DIM + 16


def _softmax_t(st, m):
    m_new = jnp.maximum(m, jnp.max(st, axis=0, keepdims=True))
    return m_new, jnp.exp2(m - m_new), jnp.exp2(st - m_new).astype(BF16)


def _with_ones(vt):
    return jnp.concatenate([vt.astype(BF16), jnp.ones((VROWS - FX_DIM, vt.shape[1]), BF16)], axis=0)


def _normalised(acc):
    return acc[0:FX_DIM] / acc[FX_DIM:FX_DIM + 1]


def _attn_prompt_kernel(qa_ref, fgate_ref, mq_ref, mgate_ref, ka_ref, vt_ref, mk_ref, mvt_ref,
                        o_ref, vtb_ref, mvtb_ref, st_ref, p_ref, acc_ref, *, tq):
    seq = qa_ref.shape[0]
    tk = tq
    ntiles = seq // tq
    heads = range(FX_HEADS)

    for j in range(seq // tk):
        for h in heads:
            vtb_ref[j, h] = _with_ones(vt_ref[0, h * FX_DIM:(h + 1) * FX_DIM, j * tk:(j + 1) * tk])
    for h in range(MEM_HEADS):
        mvtb_ref[h] = _with_ones(mvt_ref[0, h * MEM_DIM:(h + 1) * MEM_DIM, :])

    key = lax.broadcasted_iota(jnp.int32, (tk, tq), 0)
    qry = lax.broadcasted_iota(jnp.int32, (tk, tq), 1)

    def tile_rows(i):
        return pl.ds(pl.multiple_of(i * tq, tq), tq)

    def scores(i, j, h):
        hl = slice(h * LANES, (h + 1) * LANES)
        return _nt(ka_ref[tile_rows(j), hl], qa_ref[tile_rows(i), hl])

    def start(i, s):
        for h in heads:
            st_ref[s, 0, h] = scores(i, 0, h)
            p_ref[s, 1, h] = jnp.zeros((tk, tq), BF16)
            acc_ref[s, h] = jnp.zeros((VROWS, tq), F32)

    def block(i, s, j, b, carry):
        prev = jnp.maximum(j - 1, 0)
        out = []
        for h in heads:
            m, alpha_prev = carry[h]
            pv = _mm(vtb_ref[prev, h], p_ref[s, 1 - b, h])
            st = st_ref[s, b, h]
            st_ref[s, 1 - b, h] = scores(i, j + 1, h)
            m, alpha, p = _softmax_t(st, m)
            acc_ref[s, h] = alpha_prev * acc_ref[s, h] + pv
            p_ref[s, b, h] = p
            out.append((m, alpha))
        return tuple(out)

    def finish(i, s, carry):
        rows = tile_rows(i)
        prev = jnp.maximum(i - 1, 0)
        o_heads = []
        for h in heads:
            m, alpha_prev = carry[h]
            acc = alpha_prev * acc_ref[s, h] + _mm(vtb_ref[prev, h], p_ref[s, 1 - s, h])
            m, alpha, p = _softmax_t(jnp.where(key <= qry, st_ref[s, s, h], -jnp.inf), m)
            acc = alpha * acc + _mm(vtb_ref[i, h], p)
            o_heads.append(_normalised(acc))
        for pr in range(FX_HEADS // 2):
            ls = _pair_lanes(2 * pr)
            o_t = jnp.concatenate(o_heads[2 * pr:2 * pr + 2], axis=0)
            o_ref[rows, ls] = (o_t.T * fgate_ref[rows, ls]).astype(o_ref.dtype)

    def mem_scores(i, s):
        rows = tile_rows(i)
        for h in range(MEM_HEADS):
            mq_pair = mq_ref[rows, _pair_lanes(h)] * (MEM_DIM ** -0.5 * LOG2E)
            qm = jnp.where(_head_mask(h % 2), mq_pair, 0.0).astype(BF16)
            st_ref[s, 0, h] = _nt(mk_ref[:, _pair_lanes(h)], qm)

    def mem_finish(i, s):
        rows = tile_rows(i)
        for pr in range(MEM_HEADS // 2):
            ls = _pair_lanes(2 * pr)
            halves = []
            for hh in range(2):
                st = st_ref[s, 0, 2 * pr + hh]
                pm = jnp.exp2(st - jnp.max(st, axis=0, keepdims=True))
                halves.append(_normalised(_mm(mvtb_ref[2 * pr + hh], pm.astype(BF16))))
            o_t = jnp.concatenate(halves, axis=0)
            lm = slice(FX_WIDTH + pr * LANES, FX_WIDTH + (pr + 1) * LANES)
            o_ref[rows, lm] = (o_t.T * mgate_ref[rows, ls]).astype(o_ref.dtype)

    mem_scores(0, 0)

    def mem_pair(ip, c):
        mem_finish(2 * ip, 0)
        mem_scores(2 * ip + 1, 1)
        mem_finish(2 * ip + 1, 1)
        mem_scores(jnp.minimum(2 * ip + 2, ntiles - 1), 0)
        return c

    lax.fori_loop(0, ntiles // 2, mem_pair, 0)

    def qtile(ip, s):
        i = 2 * ip + s

        def two_blocks(jp, carry):
            return block(i, s, 2 * jp + 1, 1, block(i, s, 2 * jp, 0, carry))

        init = tuple((jnp.full((1, tq), -jnp.inf, F32), jnp.ones((1, tq), F32)) for _ in heads)
        carry = lax.fori_loop(0, ip, two_blocks, init)
        if s == 1:
            carry = block(i, s, i - 1, 0, carry)
        finish(i, s, carry)
        start(jnp.minimum(i + 1, ntiles - 1), 1 - s)

    start(0, 0)

    def tile_pair(ip, c):
        qtile(ip, 0)
        qtile(ip, 1)
        return c

    lax.fori_loop(0, ntiles // 2, tile_pair, 0)


def _attn_prompt(act, qa, ka, vt, mk, mvt, *, batch, seq_len, tq):
    m = act.shape[0]
    n_mem = mk.shape[0] // batch
    tk = tq
    assert (seq_len // tq) % 2 == 0, "query tiles are processed in pairs"
    ablk = lambda c0: pl.BlockSpec((seq_len, FX_WIDTH), lambda b, c0=c0: (b, c0 // FX_WIDTH))
    return pl.pallas_call(
        functools.partial(_attn_prompt_kernel, tq=tq),
        grid=(batch,),
        in_specs=[
            pl.BlockSpec((seq_len, FX_HEADS * LANES), lambda b: (b, 0)),
            ablk(ACT_PROMPT["FGATE"]), ablk(ACT_PROMPT["MQ"]), ablk(ACT_PROMPT["MGATE"]),
            pl.BlockSpec((seq_len, FX_HEADS * LANES), lambda b: (b, 0)),
            pl.BlockSpec((1, FX_WIDTH, seq_len), lambda b: (b, 0, 0)),
            pl.BlockSpec((n_mem, MEM_WIDTH), lambda b: (b, 0)),
            pl.BlockSpec((1, MEM_WIDTH, n_mem), lambda b: (b, 0, 0)),
        ],
        out_specs=pl.BlockSpec((seq_len, FX_WIDTH + MEM_WIDTH), lambda b: (b, 0)),
        out_shape=jax.ShapeDtypeStruct((m, FX_WIDTH + MEM_WIDTH), BF16),
        scratch_shapes=[
            pltpu.VMEM((seq_len // tk, FX_HEADS, VROWS, tk), BF16),
            pltpu.VMEM((MEM_HEADS, VROWS, n_mem), BF16),
            pltpu.VMEM((2, 2, FX_HEADS, tk, tq), F32),
            pltpu.VMEM((2, 2, FX_HEADS, tk, tq), BF16),
            pltpu.VMEM((2, FX_HEADS, VROWS, tq), F32),
        ],
        compiler_params=pltpu.CompilerParams(
            dimension_semantics=("arbitrary",), vmem_limit_bytes=VMEM_LIMIT),
        name="attn_prompt",
    )(qa, act, act, act, ka, vt, mk, mvt)


def _stack_heads(x, scale):
    lane = lax.broadcasted_iota(jnp.int32, (1, x.shape[1]), 1)
    return jnp.concatenate([jnp.where(lane // FX_DIM == h, x * scale, 0.0) for h in range(FX_HEADS)],
                           axis=0).astype(BF16)


def _unstack_heads(o, ts):
    lane = lax.broadcasted_iota(jnp.int32, (1, o.shape[1]), 1)
    out = jnp.zeros((ts, o.shape[1]), F32)
    for h in range(FX_HEADS):
        out = jnp.where(lane // FX_DIM == h, o[h * ts:(h + 1) * ts, :], out)
    return out


def _per_head_rows(vals, ts):
    return jnp.concatenate([jnp.broadcast_to(v, (ts, v.shape[1])) for v in vals], axis=0)


def _attn_sample_kernel(fq_ref, fgate_ref, mq_ref, mgate_ref, kn_ref, vn_ref, dn_ref, dt_ref,
                        pkt_ref, pvt_ref, plf_ref, mkt_ref, mvt_ref, o_ref, s_ref, *, chunk):
    ts = fq_ref.shape[0]
    n_new = kn_ref.shape[0]
    past = pkt_ref.shape[3]
    nchunks = past // chunk
    b = pl.program_id(0)
    hrows = FX_HEADS * ts

    r = lax.broadcasted_iota(jnp.int32, (chunk, chunk), 0)
    c = lax.broadcasted_iota(jnp.int32, (chunk, chunk), 1)
    upper = jnp.where(r <= c, 1.0, 0.0).astype(BF16)
    chunks = [slice(ci * chunk, (ci + 1) * chunk) for ci in range(nchunks)]
    pieces = jnp.concatenate([p.astype(BF16) for p in _split3(plf_ref[0, 0])], axis=0)
    local = _mm(pieces, upper)
    nrows = nchunks * SUBLANES
    local = local[0:nrows] + local[nrows:2 * nrows] + local[2 * nrows:]
    d_past, total = [], jnp.zeros((SUBLANES, 1), F32)
    for ci in range(nchunks):
        d = local[ci * SUBLANES:(ci + 1) * SUBLANES]
        d_past.append(d + total)
        total = total + d[:, chunk - 1:chunk]

    dn = dn_ref[...]
    dt_new = dt_ref[...]
    tot = [total[h:h + 1, :] for h in range(FX_HEADS)]
    dq = jnp.concatenate([tot[h] + dn[:, h:h + 1] for h in range(FX_HEADS)], axis=0)
    q4 = _stack_heads(fq_ref[...], FX_DIM ** -0.5)

    for ci, cs in enumerate(chunks):
        dk = _per_head_rows([d_past[ci][h:h + 1, :] for h in range(FX_HEADS)], ts)
        s_ref[:, cs] = _mm(q4, pkt_ref[0, 0, :, cs].astype(BF16)) + dq - dk
    row = lax.broadcasted_iota(jnp.int32, (hrows, n_new), 0)
    col = lax.broadcasted_iota(jnp.int32, (hrows, n_new), 1)
    visible = (col // ts == b) & (col % ts <= row % ts)
    dk = _per_head_rows([tot[h] + dt_new[h:h + 1, :] for h in range(FX_HEADS)], ts)
    s_new = jnp.where(visible, _nt(q4, kn_ref[...].astype(BF16)) + dq - dk, -jnp.inf)

    m = jnp.max(s_new, axis=-1, keepdims=True)
    for cs in chunks:
        m = jnp.maximum(m, jnp.max(s_ref[:, cs], axis=-1, keepdims=True))
    p_new = jnp.exp(s_new - m)
    l = jnp.sum(p_new, axis=-1, keepdims=True)
    acc = _mm(p_new.astype(BF16), vn_ref[...].astype(BF16))
    for cs in chunks:
        p = jnp.exp(s_ref[:, cs] - m)
        l = l + jnp.sum(p, axis=-1, keepdims=True)
        acc = acc + _nt(p.astype(BF16), pvt_ref[0, 0, :, cs].astype(BF16))
    o_fx = _unstack_heads(acc / l, ts) * fgate_ref[...]
    o_ref[:, 0:FX_WIDTH] = o_fx.astype(o_ref.dtype)

    mq4 = _stack_heads(mq_ref[...], MEM_DIM ** -0.5)
    s = _mm(mq4, mkt_ref[0, 0].astype(BF16))
    p = jnp.exp(s - jnp.max(s, axis=-1, keepdims=True))
    o = _nt(p.astype(BF16), mvt_ref[0, 0].astype(BF16)) / jnp.sum(p, axis=-1, keepdims=True)
    o_mem = _unstack_heads(o, ts) * mgate_ref[...]
    o_ref[:, FX_WIDTH:] = o_mem.astype(o_ref.dtype)


def _attn_sample(act, k_new, v_new, d_nat, d_t, past_kt, past_vt, past_lft, mem_kt, mem_vt,
                 *, layer, batch, ts):
    m = act.shape[0]
    past = past_kt.shape[3]
    n_mem = mem_kt.shape[3]
    chunk = past_lft.shape[3]
    ablk = lambda c0: pl.BlockSpec((ts, FX_WIDTH), lambda b, c0=c0: (b, c0 // FX_WIDTH))
    cache = lambda nfeat, n: pl.BlockSpec((1, 1, nfeat, n), lambda b: (layer, b, 0, 0))
    return pl.pallas_call(
        functools.partial(_attn_sample_kernel, chunk=chunk),
        grid=(batch,),
        in_specs=[
            ablk(ACT_SAMPLE["FQ"]), ablk(ACT_SAMPLE["FGATE"]), ablk(ACT_SAMPLE["MQ"]),
            ablk(ACT_SAMPLE["MGATE"]),
            pl.BlockSpec((m, FX_WIDTH), lambda b: (0, 0)),
            pl.BlockSpec((m, FX_WIDTH), lambda b: (0, 0)),
            pl.BlockSpec((ts, LANES), lambda b: (b, 0)),
            pl.BlockSpec((SUBLANES, m), lambda b: (0, 0)),
            cache(FX_WIDTH, past), cache(FX_WIDTH, past), cache(past // chunk * SUBLANES, chunk),
            cache(MEM_WIDTH, n_mem), cache(MEM_WIDTH, n_mem),
        ],
        out_specs=pl.BlockSpec((ts, FX_WIDTH + MEM_WIDTH), lambda b: (b, 0)),
        out_shape=jax.ShapeDtypeStruct((m, FX_WIDTH + MEM_WIDTH), BF16),
        scratch_shapes=[pltpu.VMEM((FX_HEADS * ts, past), F32)],
        compiler_params=pltpu.CompilerParams(
            dimension_semantics=("arbitrary",), vmem_limit_bytes=VMEM_LIMIT),
        name="attn_sample",
    )(act, act, act, act, k_new, v_new, d_nat, d_t, past_kt, past_vt, past_lft, mem_kt, mem_vt)


def _memkv_kernel(x_ref, w_ref, kb_ref, kt_ref, vt_ref):
    kv = _mm(x_ref[...].astype(BF16), w_ref[...])
    kb_ref[...] = kv[:, :MEM_WIDTH].astype(BF16)
    kt_ref[0] = kv[:, :MEM_WIDTH].T
    vt_ref[0] = kv[:, MEM_WIDTH:].T


def _memkv(mem, w, *, batch):
    m, d = mem.shape
    n_mem = m // batch
    return pl.pallas_call(
        _memkv_kernel,
        grid=(batch,),
        in_specs=[pl.BlockSpec((n_mem, d), lambda i: (i, 0)),
                  pl.BlockSpec(w.shape, lambda i: (0, 0))],
        out_specs=[pl.BlockSpec((n_mem, MEM_WIDTH), lambda i: (i, 0)),
                   pl.BlockSpec((1, MEM_WIDTH, n_mem), lambda i: (i, 0, 0)),
                   pl.BlockSpec((1, MEM_WIDTH, n_mem), lambda i: (i, 0, 0))],
        out_shape=[jax.ShapeDtypeStruct((m, MEM_WIDTH), BF16),
                   jax.ShapeDtypeStruct((batch, MEM_WIDTH, n_mem), F32),
                   jax.ShapeDtypeStruct((batch, MEM_WIDTH, n_mem), F32)],
        compiler_params=pltpu.CompilerParams(
            dimension_semantics=("arbitrary",), vmem_limit_bytes=VMEM_LIMIT),
        name="memkv",
    )(mem, w)


def _out_kernel(x_ref, hg_ref, at_ref, w_ref, g_ref, b_ref, y_ref, *, alpha):
    acc = _mm(hg_ref[...], w_ref[0:HG_WIDTH, :]) + _mm(at_ref[...], w_ref[HG_WIDTH:, :])
    z = alpha * x_ref[...] + acc
    mu = jnp.mean(z, axis=-1, keepdims=True)
    zc = z - mu
    var = jnp.mean(zc * zc, axis=-1, keepdims=True)
    y_ref[...] = zc * lax.rsqrt(var + LN_EPS) * g_ref[...] + b_ref[...]


def _out(x, mix_hg, mix_at, w, g, b, *, alpha, tm):
    m, d = x.shape
    return pl.pallas_call(
        functools.partial(_out_kernel, alpha=alpha),
        grid=(m // tm,),
        in_specs=[
            pl.BlockSpec((tm, d), lambda i: (i, 0)),
            pl.BlockSpec((tm, HG_WIDTH), lambda i: (i, 0)),
            pl.BlockSpec((tm, FX_WIDTH + MEM_WIDTH), lambda i: (i, 0)),
            pl.BlockSpec(w.shape, lambda i: (0, 0)),
            pl.BlockSpec((1, d), lambda i: (0, 0)),
            pl.BlockSpec((1, d), lambda i: (0, 0)),
        ],
        out_specs=pl.BlockSpec((tm, d), lambda i: (i, 0)),
        out_shape=jax.ShapeDtypeStruct((m, d), F32),
        compiler_params=pltpu.CompilerParams(
            dimension_semantics=("arbitrary",), vmem_limit_bytes=VMEM_LIMIT),
        name="out",
    )(x, mix_hg, mix_at, w, g, b)


def _pick_tile(n, target):
    t = min(n, target)
    while n % t:
        t //= 2
    return t


def _feature_major(x):
    lead = x.shape[:-3]
    n, heads, dim = x.shape[-3:]
    nl = len(lead)
    perm = tuple(range(nl)) + (nl + 1, nl + 2, nl)
    return jnp.transpose(x, perm).reshape(lead + (heads * dim, n))


def _token_major(x, heads):
    lead = x.shape[:-2]
    width, n = x.shape[-2:]
    nl = len(lead)
    x = x.reshape(lead + (heads, width // heads, n))
    return jnp.transpose(x, tuple(range(nl)) + (nl + 2, nl, nl + 1))


def kernel(x_prompt, x_sample, mem_prompt, state_hgrn, cache_fox_k, cache_fox_v, cache_fox_logf,
           cache_mem_k, cache_mem_v, w_in, b_fox_forget, hgrn_lower_bounds, hgrn_norm_g,
           w_mem_kv, w_out, ln_g, ln_b):
    depth, d_model, _ = w_in.shape
    bp, seq, _ = x_prompt.shape
    bs, dseq, _ = x_sample.shape
    n_mem = mem_prompt.shape[1]
    alpha = (2 * depth) ** 0.25

    fl0 = 4 * HG_WIDTH + 4 * FX_WIDTH
    w_main = jnp.concatenate([w_in[:, :, :fl0], w_in[:, :, fl0 + FX_HEADS:]], axis=-1).astype(BF16)
    w_flc = w_in[:, :, fl0:fl0 + FX_HEADS]
    w_fl = jnp.pad(w_flc, ((0, 0), (0, 0), (0, LANES - FX_HEADS))).astype(BF16)
    w_flt = jnp.pad(jnp.swapaxes(w_flc, 1, 2), ((0, 0), (0, SUBLANES - FX_HEADS), (0, 0))).astype(BF16)
    b_fl = jnp.pad(b_fox_forget, ((0, 0), (0, LANES - FX_HEADS)))[:, None, :]
    b_flt = jnp.pad(b_fox_forget, ((0, 0), (0, SUBLANES - FX_HEADS)))[:, :, None]
    lbs = hgrn_lower_bounds.astype(F32)
    w_memb = w_mem_kv.astype(BF16)
    w_outb = w_out.astype(BF16)

    past_kt = _feature_major(cache_fox_k)
    past_vt = _feature_major(cache_fox_v)
    past = cache_fox_logf.shape[2]
    chunk = min(past, SAMPLE_CHUNK)
    past_lft = jnp.swapaxes(cache_fox_logf, 2, 3).reshape(depth, bs, FX_HEADS, past // chunk, chunk)
    past_lft = jnp.pad(jnp.swapaxes(past_lft, 2, 3),
                       ((0, 0), (0, 0), (0, 0), (0, SUBLANES - FX_HEADS), (0, 0)))
    past_lft = past_lft.reshape(depth, bs, past // chunk * SUBLANES, chunk)
    mem_kt = _feature_major(cache_mem_k)
    mem_vt = _feature_major(cache_mem_v)

    mp, ms = bp * seq, bs * dseq
    tb = _pick_tile(seq, 256)
    yp = x_prompt.reshape(mp, d_model)
    ys = x_sample.reshape(ms, d_model)
    mem2 = mem_prompt.reshape(bp * n_mem, d_model)

    p_hg, p_kt, p_vt, p_lft, p_mkt, p_mvt = [], [], [], [], [], []
    s_hg, s_k, s_v, s_lft = [], [], [], []
    for l in range(depth):
        wl = (w_main[l], w_fl[l], w_flt[l], b_fl[l], b_flt[l], lbs)
        ng = hgrn_norm_g[l][None, :]
        g_ln, b_ln = ln_g[l][None, :], ln_b[l][None, :]

        mkb, mkt, mvt = _memkv(mem2, w_memb[l], batch=bp)
        act, qa, ka, kt, vt, lft = _proj(yp, *wl, layer=l, batch=bp, seq_len=seq,
                                         tm=_pick_tile(seq, 512), prompt=True)
        mix_hg, hg = _hgrn(act, ng, None, layer=l, batch=bp, seq_len=seq, tt=_pick_tile(seq, 2048),
                           nheads=HG_HEADS)
        mix_at = _attn_prompt(act, qa, ka, vt, mkb, mvt, batch=bp, seq_len=seq, tq=tb)
        yp = _out(yp, mix_hg, mix_at, w_outb[l], g_ln, b_ln, alpha=alpha, tm=_pick_tile(mp, 1024))
        p_hg.append(hg); p_kt.append(kt); p_vt.append(vt); p_lft.append(lft)
        p_mkt.append(mkt); p_mvt.append(mvt)

        act, k, v, lft, d_nat, d_t = _proj(ys, *wl, layer=l, batch=bs, seq_len=dseq, tm=ms, prompt=False)
        mix_hg, hg = _hgrn(act, ng, state_hgrn, layer=l, batch=bs, seq_len=dseq, tt=dseq, nheads=HG_HEADS)
        mix_at = _attn_sample(act, k, v, d_nat, d_t, past_kt, past_vt, past_lft, mem_kt, mem_vt,
                              layer=l, batch=bs, ts=dseq)
        ys = _out(ys, mix_hg, mix_at, w_outb[l], g_ln, b_ln, alpha=alpha, tm=ms)
        s_hg.append(hg); s_k.append(k); s_v.append(v); s_lft.append(lft)

    st = jnp.stack
    s_lf = st(s_lft)[:, :FX_HEADS, :].reshape(depth, FX_HEADS, bs, dseq)
    return (yp.reshape(bp, seq, d_model), ys.reshape(bs, dseq, d_model),
            st(p_hg),
            _token_major(st(p_kt), FX_HEADS),
            _token_major(st(p_vt), FX_HEADS),
            jnp.swapaxes(st(p_lft), 2, 3),
            _token_major(st(p_mkt), MEM_HEADS),
            _token_major(st(p_mvt), MEM_HEADS),
            st(s_hg),
            st(s_k).reshape(depth, bs, dseq, FX_HEADS, FX_DIM),
            st(s_v).reshape(depth, bs, dseq, FX_HEADS, FX_DIM),
            jnp.transpose(s_lf, (0, 2, 3, 1)))
```

```python
import functools

import jax
import jax.numpy as jnp
from jax import lax
from jax.experimental import pallas as pl
from jax.experimental.pallas import tpu as pltpu

F32 = jnp.float32
BF16 = jnp.bfloat16

HG_HEADS = 4
HG_DIM = 128
HG_WIDTH = HG_HEADS * HG_DIM
FX_HEADS = 4
FX_DIM = 64
FX_WIDTH = FX_HEADS * FX_DIM
MEM_HEADS = 4
MEM_DIM = 64
MEM_WIDTH = MEM_HEADS * MEM_DIM
HG_GROUP = 128
LN_EPS = 1e-5
RMS_EPS = 1e-6
LANES = 128
SUBLANES = 8
VMEM_LIMIT = 56 * 1024 * 1024
AUG = 3
LOG2E = 1.4426950408889634
SAMPLE_CHUNK = 512

C_HQ, C_HK, C_HV, C_HG, C_HGATE = 0, 512, 1024, 1536, 2048
ACT_PROMPT = dict(FGATE=2560, MQ=2816, MGATE=3072, WIDTH=3328)
ACT_SAMPLE = dict(FQ=2560, FGATE=2816, MQ=3072, MGATE=3328, WIDTH=3584)
W_HQ, W_HF, W_HI, W_HGATE, W_FQ, W_FK, W_FV, W_FGATE, W_MQ, W_MGATE = (
    0, 512, 1024, 1536, 2048, 2304, 2560, 2816, 3072, 3328)


def _nt(a, b, precision=None):
    return lax.dot_general(a, b, (((1,), (1,)), ((), ())), precision=precision,
                           preferred_element_type=F32)


def _mm(a, b, precision=None):
    return jnp.dot(a, b, precision=precision, preferred_element_type=F32)


def _silu(x):
    return x * jax.nn.sigmoid(x)


def _log_sigmoid(x):
    return jnp.minimum(x, 0.0) - jnp.log(1.0 + jnp.exp(-jnp.abs(x)))


def _head_mask(hh):
    lane = lax.broadcasted_iota(jnp.int32, (1, LANES), 1)
    return (lane // FX_DIM) == hh


def _pair_lanes(h):
    return slice((h // 2) * LANES, (h // 2 + 1) * LANES)


def _split3(x):
    hi = x.astype(BF16).astype(F32)
    r = x - hi
    mid = r.astype(BF16).astype(F32)
    return hi, mid, r - mid


def _aug_operand(x_pair, d, hh, is_query):
    lane = lax.broadcasted_iota(jnp.int32, (1, LANES), 1)
    a = lane - (1 - hh) * FX_DIM
    hi, mid, lo = _split3(d if is_query else -d)
    p0, o0 = (0, AUG) if is_query else (AUG, 0)
    extra = jnp.where(a == p0, hi, jnp.where(a == p0 + 1, mid, jnp.where(a == p0 + 2, lo,
                      jnp.where((a >= o0) & (a < o0 + AUG), 1.0, 0.0))))
    return jnp.where(_head_mask(hh), x_pair, extra).astype(BF16)


def _proj_kernel(x_ref, w_ref, wfl_ref, wflt_ref, bfl_ref, bflt_ref, lbs_ref, *refs,
                 layer, seg, tiles_per_seq, prompt):
    nprev = 3 if prompt and layer > 0 else 0
    prev, refs = refs[:nprev], refs[nprev:]
    act_ref = refs[0]
    cols = ACT_PROMPT if prompt else ACT_SAMPLE
    tm = x_ref.shape[0]
    xb = x_ref[...].astype(BF16)

    def proj(c0, width):
        return _mm(xb, w_ref[:, c0:c0 + width])

    lbs = lbs_ref[...]
    e = jnp.exp(lbs - jnp.max(lbs, axis=0, keepdims=True))
    sm = e / jnp.sum(e, axis=0, keepdims=True)
    lb = jnp.sum(sm[:layer + 1], axis=0, keepdims=True) - sm[0:1]

    lf_nat = _log_sigmoid(_mm(xb, wfl_ref[...]) + bfl_ref[...])
    lf_t = _log_sigmoid(_nt(wflt_ref[...], xb) + bflt_ref[...])

    r = lax.broadcasted_iota(jnp.int32, (tm, tm), 0)
    c = lax.broadcasted_iota(jnp.int32, (tm, tm), 1)
    same = (r // seg) == (c // seg)
    lower = jnp.where((c <= r) & same, 1.0, 0.0).astype(BF16)
    d_nat = _cumsum_rows([lf_nat], lower)[0]

    fq = proj(W_FQ, FX_WIDTH)
    k = proj(W_FK, FX_WIDTH)
    v = proj(W_FV, FX_WIDTH)
    if prompt:
        _, qa_ref, ka_ref, kt_ref, vt_ref, lft_ref, cn_ref, kv_ref = refs
        if tiles_per_seq > 1:
            @pl.when(pl.program_id(0) % tiles_per_seq == 0)
            def _():
                cn_ref[...] = jnp.zeros_like(cn_ref)
            d_nat = d_nat + cn_ref[0:1, :]
            cn_ref[...] = jnp.broadcast_to(d_nat[tm - 1:tm, :], cn_ref.shape)
        for src, dst in zip(prev, (kt_ref, vt_ref, lft_ref)):
            dst[0:layer] = src[...]
        lft_ref[layer, 0] = lf_t[:FX_HEADS]
        kv_ref[:, 0:FX_WIDTH] = k
        kv_ref[:, FX_WIDTH:] = v
        kt_ref[layer, 0] = kv_ref[:, 0:FX_WIDTH].T
        vt_ref[layer, 0] = kv_ref[:, FX_WIDTH:].T
        d2 = d_nat * LOG2E
        for h in range(FX_HEADS):
            hl = slice(h * LANES, (h + 1) * LANES)
            qa_ref[:, hl] = _aug_operand(fq[:, _pair_lanes(h)] * (FX_DIM ** -0.5 * LOG2E),
                                         d2[:, h:h + 1], h % 2, True)
            ka_ref[:, hl] = _aug_operand(k[:, _pair_lanes(h)], d2[:, h:h + 1], h % 2, False)
    else:
        _, k_ref, v_ref, lft_ref, dn_ref, dt_ref = refs
        upper = jnp.where((r <= c) & same, 1.0, 0.0).astype(F32)
        dt_ref[...] = _mm(lf_t, upper, precision=lax.Precision.HIGHEST)
        lft_ref[...] = lf_t
        k_ref[...] = k
        v_ref[...] = v
        dn_ref[...] = d_nat
        act_ref[:, cols["FQ"]:cols["FQ"] + FX_WIDTH] = fq

    act_ref[:, C_HQ:C_HQ + HG_WIDTH] = _silu(proj(W_HQ, HG_WIDTH))
    forget = lb + (1.0 - lb) * jax.nn.sigmoid(proj(W_HF, HG_WIDTH))
    act_ref[:, C_HK:C_HK + HG_WIDTH] = 1.0 - forget
    act_ref[:, C_HG:C_HG + HG_WIDTH] = jnp.log2(forget)
    act_ref[:, C_HV:C_HV + HG_WIDTH] = proj(W_HI, HG_WIDTH)
    act_ref[:, C_HGATE:C_HGATE + HG_WIDTH] = _silu(proj(W_HGATE, HG_WIDTH))
    act_ref[:, cols["FGATE"]:cols["FGATE"] + FX_WIDTH] = _silu(proj(W_FGATE, FX_WIDTH))
    act_ref[:, cols["MQ"]:cols["MQ"] + MEM_WIDTH] = proj(W_MQ, MEM_WIDTH)
    act_ref[:, cols["MGATE"]:cols["MGATE"] + MEM_WIDTH] = _silu(proj(W_MGATE, MEM_WIDTH))


def _proj(x, w_main, w_fl, w_flt, b_fl, b_flt, lbs, prev=(), *, layer, batch, seq_len, tm, prompt):
    m, d = x.shape
    ntiles = m // tm
    seg = min(seq_len, tm)
    tps = max(seq_len // tm, 1)
    assert prompt or ntiles == 1
    assert len(prev) == (3 if prompt and layer > 0 else 0)
    kern = functools.partial(_proj_kernel, layer=layer, seg=seg, tiles_per_seq=tps, prompt=prompt)
    full = lambda shape: pl.BlockSpec(shape, lambda i: (0,) * len(shape))
    rows = lambda width: pl.BlockSpec((tm, width), lambda i: (i, 0))
    feat = lambda nl, nfeat: pl.BlockSpec((nl, 1, nfeat, tm), lambda i: (0, i // tps, 0, i % tps))
    prev_specs = []
    if prompt:
        nfeats = (FX_WIDTH, FX_WIDTH, FX_HEADS)
        prev_specs = [feat(layer, nf) for nf in nfeats] if layer > 0 else []
        out_specs = [rows(ACT_PROMPT["WIDTH"]), rows(FX_HEADS * LANES), rows(FX_HEADS * LANES)]
        out_specs += [feat(layer + 1, nf) for nf in nfeats]
        out_shape = [jax.ShapeDtypeStruct((m, ACT_PROMPT["WIDTH"]), F32),
                     jax.ShapeDtypeStruct((m, FX_HEADS * LANES), BF16),
                     jax.ShapeDtypeStruct((m, FX_HEADS * LANES), BF16)]
        out_shape += [jax.ShapeDtypeStruct((layer + 1, batch, nf, seq_len), F32) for nf in nfeats]
        scratch = [pltpu.VMEM((SUBLANES, LANES), F32), pltpu.VMEM((tm, 2 * FX_WIDTH), F32)]
    else:
        out_specs = [rows(ACT_SAMPLE["WIDTH"]), rows(FX_WIDTH), rows(FX_WIDTH), full((SUBLANES, m)),
                     rows(LANES), full((SUBLANES, m))]
        out_shape = [jax.ShapeDtypeStruct((m, ACT_SAMPLE["WIDTH"]), F32),
                     jax.ShapeDtypeStruct((m, FX_WIDTH), F32),
                     jax.ShapeDtypeStruct((m, FX_WIDTH), F32),
                     jax.ShapeDtypeStruct((SUBLANES, m), F32),
                     jax.ShapeDtypeStruct((m, LANES), F32),
                     jax.ShapeDtypeStruct((SUBLANES, m), F32)]
        scratch = []
    return pl.pallas_call(
        kern,
        grid=(ntiles,),
        in_specs=[
            pl.BlockSpec((tm, d), lambda i: (i, 0)),
            full(w_main.shape), full(w_fl.shape), full(w_flt.shape),
            full(b_fl.shape), full(b_flt.shape), full(lbs.shape),
        ] + prev_specs,
        out_specs=out_specs,
        out_shape=out_shape,
        scratch_shapes=scratch,
        compiler_params=pltpu.CompilerParams(
            dimension_semantics=("arbitrary",), vmem_limit_bytes=VMEM_LIMIT),
        name="proj_prompt" if prompt else "proj_sample",
    )(x, w_main, w_fl, w_flt, b_fl, b_flt, lbs, *prev)


def _half_ref(gc, m, rows):
    n, w = gc.shape
    if 2 * m >= SUBLANES:
        g3 = gc.reshape(n // (2 * m), 2 * m, w)
        return jnp.broadcast_to(g3[:, m - 1:m, :], g3.shape).reshape(n, w)
    pos = rows % (2 * m)
    if m == 2:
        return jnp.where(pos == 0, pltpu.roll(gc, n - 1, axis=0),
                         jnp.where(pos == 1, gc,
                                   jnp.where(pos == 2, pltpu.roll(gc, 1, axis=0),
                                             pltpu.roll(gc, 2, axis=0))))
    return jnp.where(pos == 0, gc, pltpu.roll(gc, 1, axis=0))


def _level_exponent(g2, m, rows):
    ref = _half_ref(g2, m, rows)
    return jnp.where((rows % (2 * m)) < m, ref - g2, g2 - ref)


def _pair_levels(n):
    rr = lax.broadcasted_iota(jnp.int32, (n, n), 0)
    cc = lax.broadcasted_iota(jnp.int32, (n, n), 1)
    xf = (rr ^ cc).astype(F32)
    level = (lax.bitcast_convert_type(xf, jnp.int32) >> 23) - 127
    return jnp.where(rr > cc, level, -1)


def _cumsum_rows(gs, lower):
    w = gs[0].shape[1]
    pieces = [p.astype(BF16) for g in gs for p in _split3(g)]
    s = _mm(lower, jnp.concatenate(pieces, axis=1))
    return [s[:, (3 * i) * w:(3 * i + 1) * w] + s[:, (3 * i + 1) * w:(3 * i + 2) * w]
            + s[:, (3 * i + 2) * w:(3 * i + 3) * w] for i in range(len(gs))]


def _hgrn_intra(q, k, g2, levels):
    n, w = q.shape
    rows = lax.broadcasted_iota(jnp.int32, (n, w), 0)
    nslab = n // SUBLANES
    slab = lambda x, r: x[r * SUBLANES:(r + 1) * SUBLANES]
    a = [jnp.zeros((SUBLANES, n), F32)] * nslab
    m, bit = n // 2, n.bit_length() - 2
    while m >= 1:
        if m >= SUBLANES:
            qs, ks, targets = [], [], []
            for r0 in range(0, n, 2 * m):
                first, second = slice(r0, r0 + m), slice(r0 + m, r0 + 2 * m)
                ref = g2[r0 + m - 1:r0 + m]
                ks += [k[first] * jnp.exp2(ref - g2[first]), jnp.zeros((m, w), F32)]
                qs.append(q[second] * jnp.exp2(g2[second] - ref))
                targets += range((r0 + m) // SUBLANES, (r0 + 2 * m) // SUBLANES)
            a_m = _nt(jnp.concatenate(qs, axis=0).astype(BF16), jnp.concatenate(ks, axis=0).astype(BF16))
            for i, r in enumerate(targets):
                a[r] = jnp.where(slab(levels, r) == bit, slab(a_m, i), a[r])
        else:
            f = jnp.exp2(_level_exponent(g2, m, rows))
            a_m = _nt((q * f).astype(BF16), (k * f).astype(BF16))
            a = [jnp.where(slab(levels, r) == bit, slab(a_m, r), a[r]) for r in range(nslab)]
        m, bit = m // 2, bit - 1
    return jnp.concatenate(a, axis=0)


def _hgrn_kernel(*refs, group, has_init):
    if has_init:
        q_ref, k_ref, v_ref, g_ref, gate_ref, ng_ref, s0_ref, o_ref, so_ref = refs[:9]
    else:
        q_ref, k_ref, v_ref, g_ref, gate_ref, ng_ref, o_ref, so_ref = refs[:8]
    st_ref, a_ref, qe_ref, ke_ref, vb_ref, dg_ref, dec_ref = refs[-7:]
    t = pl.program_id(2)
    nheads = st_ref.shape[0]
    ngroups = q_ref.shape[0] // group
    lanes = [slice(hh * HG_DIM, (hh + 1) * HG_DIM) for hh in range(nheads)]

    @pl.when(t == 0)
    def _():
        for hh in range(nheads):
            st_ref[hh] = s0_ref[0, 0, hh].T if has_init else jnp.zeros((HG_DIM, HG_DIM), F32)

    ng = ng_ref[...]
    levels = _pair_levels(group)
    rr = lax.broadcasted_iota(jnp.int32, (group, group), 0)
    cc = lax.broadcasted_iota(jnp.int32, (group, group), 1)
    lower = jnp.where(rr >= cc, 1.0, 0.0).astype(BF16)

    def prepare(i, s):
        sl = pl.ds(pl.multiple_of(i * group, group), group)
        g2s = _cumsum_rows([g_ref[sl, ls] for ls in lanes], lower)
        for hh, ls in enumerate(lanes):
            q, k, v, g2 = q_ref[sl, ls], k_ref[sl, ls], v_ref[sl, ls], g2s[hh]
            g_last = g2[group - 1:group, :]
            a_ref[s, hh] = _hgrn_intra(q, k, g2, levels).astype(BF16)
            qe_ref[s, hh] = (q * jnp.exp2(g2)).astype(BF16)
            ke_ref[s, hh] = (k * jnp.exp2(g_last - g2)).astype(BF16)
            vb_ref[s, hh] = v.astype(BF16)
            dg_ref[s, hh] = jnp.sum(q * k, axis=-1, keepdims=True) * v
            dec_ref[s, hh] = jnp.broadcast_to(jnp.exp2(g_last), (SUBLANES, HG_DIM))

    def finish(i, s):
        sl = pl.ds(pl.multiple_of(i * group, group), group)
        for hh, ls in enumerate(lanes):
            st = st_ref[hh]
            vb = vb_ref[s, hh]
            o = _nt(qe_ref[s, hh], st.astype(BF16)) + _mm(a_ref[s, hh], vb) + dg_ref[s, hh]
            upd = lax.dot_general(vb, ke_ref[s, hh], (((0,), (0,)), ((), ())), preferred_element_type=F32)
            st_ref[hh] = st * dec_ref[s, hh, 0:1, :] + upd
            o = o * lax.rsqrt(jnp.mean(o * o, axis=-1, keepdims=True) + RMS_EPS) * ng
            o_ref[sl, ls] = (o * gate_ref[sl, ls]).astype(o_ref.dtype)

    prepare(0, 0)
    if ngroups > 1:
        def step(ip, c):
            finish(2 * ip, 0)
            prepare(2 * ip + 1, 1)
            finish(2 * ip + 1, 1)
            prepare(2 * ip + 2, 0)
            return c

        lax.fori_loop(0, ngroups // 2 - 1, step, 0)
        finish(ngroups - 2, 0)
        prepare(ngroups - 1, 1)
        finish(ngroups - 1, 1)
    else:
        finish(0, 0)

    @pl.when(t == pl.num_programs(2) - 1)
    def _():
        for hh in range(nheads):
            so_ref[0, hh] = st_ref[hh].T


def _hgrn(act, norm_g, s0, *, layer, batch, seq_len, tt, nheads):
    m = act.shape[0]
    nt = seq_len // tt
    has_init = s0 is not None
    group = min(tt, HG_GROUP)
    assert tt == group or (tt // group) % 2 == 0, "groups are processed in pairs"
    width = nheads * HG_DIM
    blk = lambda cb: pl.BlockSpec((tt, width), lambda b, h, t, cb=cb: (b * nt + t, cb // width + h))
    in_specs = [blk(C_HQ), blk(C_HK), blk(C_HV), blk(C_HG), blk(C_HGATE),
                pl.BlockSpec((1, HG_DIM), lambda b, h, t: (0, 0))]
    args = [act, act, act, act, act, norm_g]
    if has_init:
        in_specs.append(pl.BlockSpec((1, 1, nheads, HG_DIM, HG_DIM), lambda b, h, t: (layer, b, h, 0, 0)))
        args.append(s0)
    return pl.pallas_call(
        functools.partial(_hgrn_kernel, group=group, has_init=has_init),
        grid=(batch, HG_HEADS // nheads, nt),
        in_specs=in_specs,
        out_specs=[
            pl.BlockSpec((tt, width), lambda b, h, t: (b * nt + t, h)),
            pl.BlockSpec((1, nheads, HG_DIM, HG_DIM), lambda b, h, t: (b, h, 0, 0)),
        ],
        out_shape=[
            jax.ShapeDtypeStruct((m, HG_WIDTH), BF16),
            jax.ShapeDtypeStruct((batch, HG_HEADS, HG_DIM, HG_DIM), F32),
        ],
        scratch_shapes=[
            pltpu.VMEM((nheads, HG_DIM, HG_DIM), F32),
            pltpu.VMEM((2, nheads, group, group), BF16),
            pltpu.VMEM((2, nheads, group, HG_DIM), BF16),
            pltpu.VMEM((2, nheads, group, HG_DIM), BF16),
            pltpu.VMEM((2, nheads, group, HG_DIM), BF16),
            pltpu.VMEM((2, nheads, group, HG_DIM), F32),
            pltpu.VMEM((2, nheads, SUBLANES, HG_DIM), F32),
        ],
        compiler_params=pltpu.CompilerParams(
            dimension_semantics=("arbitrary", "arbitrary", "arbitrary"), vmem_limit_bytes=VMEM_LIMIT),
        name="hgrn",
    )(*args)


VROWS = FX_DIM + 16


def _softmax_t(st, m):
    m_new = jnp.maximum(m, jnp.max(st, axis=0, keepdims=True))
    return m_new, jnp.exp2(m - m_new), jnp.exp2(st - m_new).astype(BF16)


def _with_ones(vt):
    return jnp.concatenate([vt.astype(BF16), jnp.ones((VROWS - FX_DIM, vt.shape[1]), BF16)], axis=0)


def _normalised(acc):
    return acc[0:FX_DIM] / acc[FX_DIM:FX_DIM + 1]


def _attn_prompt_kernel(qa_ref, fgate_ref, mq_ref, mgate_ref, ka_ref, vt_ref, mk_ref, mvt_ref,
                        o_ref, vtb_ref, mvtb_ref, st_ref, p_ref, acc_ref, *, tq):
    seq = qa_ref.shape[0]
    tk = tq
    ntiles = seq // tq
    heads = range(FX_HEADS)

    for j in range(seq // tk):
        for h in heads:
            vtb_ref[j, h] = _with_ones(vt_ref[0, 0, h * FX_DIM:(h + 1) * FX_DIM, j * tk:(j + 1) * tk])
    for h in range(MEM_HEADS):
        mvtb_ref[h] = _with_ones(mvt_ref[0, h * MEM_DIM:(h + 1) * MEM_DIM, :])

    key = lax.broadcasted_iota(jnp.int32, (tk, tq), 0)
    qry = lax.broadcasted_iota(jnp.int32, (tk, tq), 1)

    def tile_rows(i):
        return pl.ds(pl.multiple_of(i * tq, tq), tq)

    def scores(i, j, h):
        hl = slice(h * LANES, (h + 1) * LANES)
        return _nt(ka_ref[tile_rows(j), hl], qa_ref[tile_rows(i), hl])

    def start(i, s):
        for h in heads:
            st_ref[s, 0, h] = scores(i, 0, h)
            p_ref[s, 1, h] = jnp.zeros((tk, tq), BF16)
            acc_ref[s, h] = jnp.zeros((VROWS, tq), F32)

    def block(i, s, j, b, carry):
        prev = jnp.maximum(j - 1, 0)
        out = []
        for h in heads:
            m, alpha_prev = carry[h]
            pv = _mm(vtb_ref[prev, h], p_ref[s, 1 - b, h])
            st = st_ref[s, b, h]
            st_ref[s, 1 - b, h] = scores(i, j + 1, h)
            m, alpha, p = _softmax_t(st, m)
            acc_ref[s, h] = alpha_prev * acc_ref[s, h] + pv
            p_ref[s, b, h] = p
            out.append((m, alpha))
        return tuple(out)

    def finish(i, s, carry):
        rows = tile_rows(i)
        prev = jnp.maximum(i - 1, 0)
        o_heads = []
        for h in heads:
            m, alpha_prev = carry[h]
            acc = alpha_prev * acc_ref[s, h] + _mm(vtb_ref[prev, h], p_ref[s, 1 - s, h])
            m, alpha, p = _softmax_t(jnp.where(key <= qry, st_ref[s, s, h], -jnp.inf), m)
            acc = alpha * acc + _mm(vtb_ref[i, h], p)
            o_heads.append(_normalised(acc))
        for pr in range(FX_HEADS // 2):
            ls = _pair_lanes(2 * pr)
            o_t = jnp.concatenate(o_heads[2 * pr:2 * pr + 2], axis=0)
            o_ref[rows, ls] = (o_t.T * fgate_ref[rows, ls]).astype(o_ref.dtype)

    def mem_scores(i, s):
        rows = tile_rows(i)
        for h in range(MEM_HEADS):
            mq_pair = mq_ref[rows, _pair_lanes(h)] * (MEM_DIM ** -0.5 * LOG2E)
            qm = jnp.where(_head_mask(h % 2), mq_pair, 0.0).astype(BF16)
            st_ref[s, 0, h] = _nt(mk_ref[:, _pair_lanes(h)], qm)

    def mem_finish(i, s):
        rows = tile_rows(i)
        for pr in range(MEM_HEADS // 2):
            ls = _pair_lanes(2 * pr)
            halves = []
            for hh in range(2):
                st = st_ref[s, 0, 2 * pr + hh]
                pm = jnp.exp2(st - jnp.max(st, axis=0, keepdims=True))
                halves.append(_normalised(_mm(mvtb_ref[2 * pr + hh], pm.astype(BF16))))
            o_t = jnp.concatenate(halves, axis=0)
            lm = slice(FX_WIDTH + pr * LANES, FX_WIDTH + (pr + 1) * LANES)
            o_ref[rows, lm] = (o_t.T * mgate_ref[rows, ls]).astype(o_ref.dtype)

    mem_scores(0, 0)

    def mem_pair(ip, c):
        mem_finish(2 * ip, 0)
        mem_scores(2 * ip + 1, 1)
        mem_finish(2 * ip + 1, 1)
        mem_scores(jnp.minimum(2 * ip + 2, ntiles - 1), 0)
        return c

    lax.fori_loop(0, ntiles // 2, mem_pair, 0)

    def qtile(ip, s):
        i = 2 * ip + s

        def two_blocks(jp, carry):
            return block(i, s, 2 * jp + 1, 1, block(i, s, 2 * jp, 0, carry))

        init = tuple((jnp.full((1, tq), -jnp.inf, F32), jnp.ones((1, tq), F32)) for _ in heads)
        carry = lax.fori_loop(0, ip, two_blocks, init)
        if s == 1:
            carry = block(i, s, i - 1, 0, carry)
        finish(i, s, carry)
        start(jnp.minimum(i + 1, ntiles - 1), 1 - s)

    start(0, 0)

    def tile_pair(ip, c):
        qtile(ip, 0)
        qtile(ip, 1)
        return c

    lax.fori_loop(0, ntiles // 2, tile_pair, 0)


def _attn_prompt(act, qa, ka, vt, mk, mvt, *, layer, batch, seq_len, tq):
    m = act.shape[0]
    n_mem = mk.shape[0] // batch
    tk = tq
    assert (seq_len // tq) % 2 == 0, "query tiles are processed in pairs"
    ablk = lambda c0: pl.BlockSpec((seq_len, FX_WIDTH), lambda b, c0=c0: (b, c0 // FX_WIDTH))
    return pl.pallas_call(
        functools.partial(_attn_prompt_kernel, tq=tq),
        grid=(batch,),
        in_specs=[
            pl.BlockSpec((seq_len, FX_HEADS * LANES), lambda b: (b, 0)),
            ablk(ACT_PROMPT["FGATE"]), ablk(ACT_PROMPT["MQ"]), ablk(ACT_PROMPT["MGATE"]),
            pl.BlockSpec((seq_len, FX_HEADS * LANES), lambda b: (b, 0)),
            pl.BlockSpec((1, 1, FX_WIDTH, seq_len), lambda b: (layer, b, 0, 0)),
            pl.BlockSpec((n_mem, MEM_WIDTH), lambda b: (b, 0)),
            pl.BlockSpec((1, MEM_WIDTH, n_mem), lambda b: (b, 0, 0)),
        ],
        out_specs=pl.BlockSpec((seq_len, FX_WIDTH + MEM_WIDTH), lambda b: (b, 0)),
        out_shape=jax.ShapeDtypeStruct((m, FX_WIDTH + MEM_WIDTH), BF16),
        scratch_shapes=[
            pltpu.VMEM((seq_len // tk, FX_HEADS, VROWS, tk), BF16),
            pltpu.VMEM((MEM_HEADS, VROWS, n_mem), BF16),
            pltpu.VMEM((2, 2, FX_HEADS, tk, tq), F32),
            pltpu.VMEM((2, 2, FX_HEADS, tk, tq), BF16),
            pltpu.VMEM((2, FX_HEADS, VROWS, tq), F32),
        ],
        compiler_params=pltpu.CompilerParams(
            dimension_semantics=("arbitrary",), vmem_limit_bytes=VMEM_LIMIT),
        name="attn_prompt",
    )(qa, act, act, act, ka, vt, mk, mvt)


def _stack_heads(x, scale):
    lane = lax.broadcasted_iota(jnp.int32, (1, x.shape[1]), 1)
    return jnp.concatenate([jnp.where(lane // FX_DIM == h, x * scale, 0.0) for h in range(FX_HEADS)],
                           axis=0).astype(BF16)


def _unstack_heads(o, ts):
    lane = lax.broadcasted_iota(jnp.int32, (1, o.shape[1]), 1)
    out = jnp.zeros((ts, o.shape[1]), F32)
    for h in range(FX_HEADS):
        out = jnp.where(lane // FX_DIM == h, o[h * ts:(h + 1) * ts, :], out)
    return out


def _per_head_rows(vals, ts):
    return jnp.concatenate([jnp.broadcast_to(v, (ts, v.shape[1])) for v in vals], axis=0)


def _attn_sample_kernel(fq_ref, fgate_ref, mq_ref, mgate_ref, kn_ref, vn_ref, dn_ref, dt_ref,
                        pkt_ref, pvt_ref, plf_ref, mkt_ref, mvt_ref, o_ref, s_ref, *, chunk):
    ts = fq_ref.shape[0]
    n_new = kn_ref.shape[0]
    past = pkt_ref.shape[3]
    nchunks = past // chunk
    b = pl.program_id(0)
    hrows = FX_HEADS * ts

    r = lax.broadcasted_iota(jnp.int32, (chunk, chunk), 0)
    c = lax.broadcasted_iota(jnp.int32, (chunk, chunk), 1)
    upper = jnp.where(r <= c, 1.0, 0.0).astype(BF16)
    chunks = [slice(ci * chunk, (ci + 1) * chunk) for ci in range(nchunks)]
    pieces = jnp.concatenate([p.astype(BF16) for p in _split3(plf_ref[0, 0])], axis=0)
    local = _mm(pieces, upper)
    nrows = nchunks * SUBLANES
    local = local[0:nrows] + local[nrows:2 * nrows] + local[2 * nrows:]
    d_past, total = [], jnp.zeros((SUBLANES, 1), F32)
    for ci in range(nchunks):
        d = local[ci * SUBLANES:(ci + 1) * SUBLANES]
        d_past.append(d + total)
        total = total + d[:, chunk - 1:chunk]

    dn = dn_ref[...]
    dt_new = dt_ref[...]
    tot = [total[h:h + 1, :] for h in range(FX_HEADS)]
    dq = jnp.concatenate([tot[h] + dn[:, h:h + 1] for h in range(FX_HEADS)], axis=0)
    q4 = _stack_heads(fq_ref[...], FX_DIM ** -0.5)

    for ci, cs in enumerate(chunks):
        dk = _per_head_rows([d_past[ci][h:h + 1, :] for h in range(FX_HEADS)], ts)
        s_ref[:, cs] = _mm(q4, pkt_ref[0, 0, :, cs].astype(BF16)) + dq - dk
    row = lax.broadcasted_iota(jnp.int32, (hrows, n_new), 0)
    col = lax.broadcasted_iota(jnp.int32, (hrows, n_new), 1)
    visible = (col // ts == b) & (col % ts <= row % ts)
    dk = _per_head_rows([tot[h] + dt_new[h:h + 1, :] for h in range(FX_HEADS)], ts)
    s_new = jnp.where(visible, _nt(q4, kn_ref[...].astype(BF16)) + dq - dk, -jnp.inf)

    m = jnp.max(s_new, axis=-1, keepdims=True)
    for cs in chunks:
        m = jnp.maximum(m, jnp.max(s_ref[:, cs], axis=-1, keepdims=True))
    p_new = jnp.exp(s_new - m)
    l = jnp.sum(p_new, axis=-1, keepdims=True)
    acc = _mm(p_new.astype(BF16), vn_ref[...].astype(BF16))
    for cs in chunks:
        p = jnp.exp(s_ref[:, cs] - m)
        l = l + jnp.sum(p, axis=-1, keepdims=True)
        acc = acc + _nt(p.astype(BF16), pvt_ref[0, 0, :, cs].astype(BF16))
    o_fx = _unstack_heads(acc / l, ts) * fgate_ref[...]
    o_ref[:, 0:FX_WIDTH] = o_fx.astype(o_ref.dtype)

    mq4 = _stack_heads(mq_ref[...], MEM_DIM ** -0.5)
    s = _mm(mq4, mkt_ref[0, 0].astype(BF16))
    p = jnp.exp(s - jnp.max(s, axis=-1, keepdims=True))
    o = _nt(p.astype(BF16), mvt_ref[0, 0].astype(BF16)) / jnp.sum(p, axis=-1, keepdims=True)
    o_mem = _unstack_heads(o, ts) * mgate_ref[...]
    o_ref[:, FX_WIDTH:] = o_mem.astype(o_ref.dtype)


def _attn_sample(act, k_new, v_new, d_nat, d_t, past_kt, past_vt, past_lft, mem_kt, mem_vt,
                 *, layer, batch, ts):
    m = act.shape[0]
    past = past_kt.shape[3]
    n_mem = mem_kt.shape[3]
    chunk = past_lft.shape[3]
    ablk = lambda c0: pl.BlockSpec((ts, FX_WIDTH), lambda b, c0=c0: (b, c0 // FX_WIDTH))
    cache = lambda nfeat, n: pl.BlockSpec((1, 1, nfeat, n), lambda b: (layer, b, 0, 0))
    return pl.pallas_call(
        functools.partial(_attn_sample_kernel, chunk=chunk),
        grid=(batch,),
        in_specs=[
            ablk(ACT_SAMPLE["FQ"]), ablk(ACT_SAMPLE["FGATE"]), ablk(ACT_SAMPLE["MQ"]),
            ablk(ACT_SAMPLE["MGATE"]),
            pl.BlockSpec((m, FX_WIDTH), lambda b: (0, 0)),
            pl.BlockSpec((m, FX_WIDTH), lambda b: (0, 0)),
            pl.BlockSpec((ts, LANES), lambda b: (b, 0)),
            pl.BlockSpec((SUBLANES, m), lambda b: (0, 0)),
            cache(FX_WIDTH, past), cache(FX_WIDTH, past), cache(past // chunk * SUBLANES, chunk),
            cache(MEM_WIDTH, n_mem), cache(MEM_WIDTH, n_mem),
        ],
        out_specs=pl.BlockSpec((ts, FX_WIDTH + MEM_WIDTH), lambda b: (b, 0)),
        out_shape=jax.ShapeDtypeStruct((m, FX_WIDTH + MEM_WIDTH), BF16),
        scratch_shapes=[pltpu.VMEM((FX_HEADS * ts, past), F32)],
        compiler_params=pltpu.CompilerParams(
            dimension_semantics=("arbitrary",), vmem_limit_bytes=VMEM_LIMIT),
        name="attn_sample",
    )(act, act, act, act, k_new, v_new, d_nat, d_t, past_kt, past_vt, past_lft, mem_kt, mem_vt)


def _memkv_kernel(x_ref, w_ref, kb_ref, kt_ref, vt_ref):
    kv = _mm(x_ref[...].astype(BF16), w_ref[...])
    kb_ref[...] = kv[:, :MEM_WIDTH].astype(BF16)
    kt_ref[0] = kv[:, :MEM_WIDTH].T
    vt_ref[0] = kv[:, MEM_WIDTH:].T


def _memkv(mem, w, *, batch):
    m, d = mem.shape
    n_mem = m // batch
    return pl.pallas_call(
        _memkv_kernel,
        grid=(batch,),
        in_specs=[pl.BlockSpec((n_mem, d), lambda i: (i, 0)),
                  pl.BlockSpec(w.shape, lambda i: (0, 0))],
        out_specs=[pl.BlockSpec((n_mem, MEM_WIDTH), lambda i: (i, 0)),
                   pl.BlockSpec((1, MEM_WIDTH, n_mem), lambda i: (i, 0, 0)),
                   pl.BlockSpec((1, MEM_WIDTH, n_mem), lambda i: (i, 0, 0))],
        out_shape=[jax.ShapeDtypeStruct((m, MEM_WIDTH), BF16),
                   jax.ShapeDtypeStruct((batch, MEM_WIDTH, n_mem), F32),
                   jax.ShapeDtypeStruct((batch, MEM_WIDTH, n_mem), F32)],
        compiler_params=pltpu.CompilerParams(
            dimension_semantics=("arbitrary",), vmem_limit_bytes=VMEM_LIMIT),
        name="memkv",
    )(mem, w)


def _out_kernel(x_ref, hg_ref, at_ref, w_ref, g_ref, b_ref, y_ref, *, alpha):
    acc = _mm(hg_ref[...], w_ref[0:HG_WIDTH, :]) + _mm(at_ref[...], w_ref[HG_WIDTH:, :])
    z = alpha * x_ref[...] + acc
    mu = jnp.mean(z, axis=-1, keepdims=True)
    zc = z - mu
    var = jnp.mean(zc * zc, axis=-1, keepdims=True)
    y_ref[...] = zc * lax.rsqrt(var + LN_EPS) * g_ref[...] + b_ref[...]


def _out(x, mix_hg, mix_at, w, g, b, *, alpha, tm):
    m, d = x.shape
    return pl.pallas_call(
        functools.partial(_out_kernel, alpha=alpha),
        grid=(m // tm,),
        in_specs=[
            pl.BlockSpec((tm, d), lambda i: (i, 0)),
            pl.BlockSpec((tm, HG_WIDTH), lambda i: (i, 0)),
            pl.BlockSpec((tm, FX_WIDTH + MEM_WIDTH), lambda i: (i, 0)),
            pl.BlockSpec(w.shape, lambda i: (0, 0)),
            pl.BlockSpec((1, d), lambda i: (0, 0)),
            pl.BlockSpec((1, d), lambda i: (0, 0)),
        ],
        out_specs=pl.BlockSpec((tm, d), lambda i: (i, 0)),
        out_shape=jax.ShapeDtypeStruct((m, d), F32),
        compiler_params=pltpu.CompilerParams(
            dimension_semantics=("arbitrary",), vmem_limit_bytes=VMEM_LIMIT),
        name="out",
    )(x, mix_hg, mix_at, w, g, b)


def _pick_tile(n, target):
    t = min(n, target)
    while n % t:
        t //= 2
    return t


def _feature_major(x):
    lead = x.shape[:-3]
    n, heads, dim = x.shape[-3:]
    nl = len(lead)
    perm = tuple(range(nl)) + (nl + 1, nl + 2, nl)
    return jnp.transpose(x, perm).reshape(lead + (heads * dim, n))


def _token_major(x, heads):
    lead = x.shape[:-2]
    width, n = x.shape[-2:]
    nl = len(lead)
    x = x.reshape(lead + (heads, width // heads, n))
    return jnp.transpose(x, tuple(range(nl)) + (nl + 2, nl, nl + 1))


def kernel(x_prompt, x_sample, mem_prompt, state_hgrn, cache_fox_k, cache_fox_v, cache_fox_logf,
           cache_mem_k, cache_mem_v, w_in, b_fox_forget, hgrn_lower_bounds, hgrn_norm_g,
           w_mem_kv, w_out, ln_g, ln_b):
    depth, d_model, _ = w_in.shape
    bp, seq, _ = x_prompt.shape
    bs, dseq, _ = x_sample.shape
    n_mem = mem_prompt.shape[1]
    alpha = (2 * depth) ** 0.25

    fl0 = 4 * HG_WIDTH + 4 * FX_WIDTH
    w_main = jnp.concatenate([w_in[:, :, :fl0], w_in[:, :, fl0 + FX_HEADS:]], axis=-1).astype(BF16)
    w_flc = w_in[:, :, fl0:fl0 + FX_HEADS]
    w_fl = jnp.pad(w_flc, ((0, 0), (0, 0), (0, LANES - FX_HEADS))).astype(BF16)
    w_flt = jnp.pad(jnp.swapaxes(w_flc, 1, 2), ((0, 0), (0, SUBLANES - FX_HEADS), (0, 0))).astype(BF16)
    b_fl = jnp.pad(b_fox_forget, ((0, 0), (0, LANES - FX_HEADS)))[:, None, :]
    b_flt = jnp.pad(b_fox_forget, ((0, 0), (0, SUBLANES - FX_HEADS)))[:, :, None]
    lbs = hgrn_lower_bounds.astype(F32)
    w_memb = w_mem_kv.astype(BF16)
    w_outb = w_out.astype(BF16)

    past_kt = _feature_major(cache_fox_k)
    past_vt = _feature_major(cache_fox_v)
    past = cache_fox_logf.shape[2]
    chunk = min(past, SAMPLE_CHUNK)
    past_lft = jnp.swapaxes(cache_fox_logf, 2, 3).reshape(depth, bs, FX_HEADS, past // chunk, chunk)
    past_lft = jnp.pad(jnp.swapaxes(past_lft, 2, 3),
                       ((0, 0), (0, 0), (0, 0), (0, SUBLANES - FX_HEADS), (0, 0)))
    past_lft = past_lft.reshape(depth, bs, past // chunk * SUBLANES, chunk)
    mem_kt = _feature_major(cache_mem_k)
    mem_vt = _feature_major(cache_mem_v)

    mp, ms = bp * seq, bs * dseq
    tb = _pick_tile(seq, 256)
    yp = x_prompt.reshape(mp, d_model)
    ys = x_sample.reshape(ms, d_model)
    mem2 = mem_prompt.reshape(bp * n_mem, d_model)

    p_hg, p_mkt, p_mvt = [], [], []
    s_hg, s_k, s_v, s_lft = [], [], [], []
    fox_leaves = ()
    for l in range(depth):
        wl = (w_main[l], w_fl[l], w_flt[l], b_fl[l], b_flt[l], lbs)
        ng = hgrn_norm_g[l][None, :]
        g_ln, b_ln = ln_g[l][None, :], ln_b[l][None, :]

        mkb, mkt, mvt = _memkv(mem2, w_memb[l], batch=bp)
        act, qa, ka, *fox_leaves = _proj(yp, *wl, fox_leaves, layer=l, batch=bp, seq_len=seq,
                                         tm=_pick_tile(seq, 512), prompt=True)
        mix_hg, hg = _hgrn(act, ng, None, layer=l, batch=bp, seq_len=seq, tt=_pick_tile(seq, 2048),
                           nheads=HG_HEADS)
        mix_at = _attn_prompt(act, qa, ka, fox_leaves[1], mkb, mvt, layer=l, batch=bp, seq_len=seq, tq=tb)
        yp = _out(yp, mix_hg, mix_at, w_outb[l], g_ln, b_ln, alpha=alpha, tm=_pick_tile(mp, 1024))
        p_hg.append(hg); p_mkt.append(mkt); p_mvt.append(mvt)

        act, k, v, lft, d_nat, d_t = _proj(ys, *wl, layer=l, batch=bs, seq_len=dseq, tm=ms, prompt=False)
        mix_hg, hg = _hgrn(act, ng, state_hgrn, layer=l, batch=bs, seq_len=dseq, tt=dseq, nheads=HG_HEADS)
        mix_at = _attn_sample(act, k, v, d_nat, d_t, past_kt, past_vt, past_lft, mem_kt, mem_vt,
                              layer=l, batch=bs, ts=dseq)
        ys = _out(ys, mix_hg, mix_at, w_outb[l], g_ln, b_ln, alpha=alpha, tm=ms)
        s_hg.append(hg); s_k.append(k); s_v.append(v); s_lft.append(lft)

    st = jnp.stack
    s_lf = st(s_lft)[:, :FX_HEADS, :].reshape(depth, FX_HEADS, bs, dseq)
    return (yp.reshape(bp, seq, d_model), ys.reshape(bs, dseq, d_model),
            st(p_hg),
            _token_major(fox_leaves[0], FX_HEADS),
            _token_major(fox_leaves[1], FX_HEADS),
            jnp.swapaxes(fox_leaves[2], 2, 3),
            _token_major(st(p_mkt), MEM_HEADS),
            _token_major(st(p_mvt), MEM_HEADS),
            st(s_hg),
            st(s_k).reshape(depth, bs, dseq, FX_HEADS, FX_DIM),
            st(s_v).reshape(depth, bs, dseq, FX_HEADS, FX_DIM),
            jnp.transpose(s_lf, (0, 2, 3, 1)))
```

```python
import functools

import jax
import jax.numpy as jnp
from jax import lax
from jax.experimental import pallas as pl
from jax.experimental.pallas import tpu as pltpu

F32 = jnp.float32
BF16 = jnp.bfloat16

HG_HEADS = 4
HG_DIM = 128
HG_WIDTH = HG_HEADS * HG_DIM
FX_HEADS = 4
FX_DIM = 64
FX_WIDTH = FX_HEADS * FX_DIM
MEM_HEADS = 4
MEM_DIM = 64
MEM_WIDTH = MEM_HEADS * MEM_DIM
HG_GROUP = 128
LN_EPS = 1e-5
RMS_EPS = 1e-6
LANES = 128
SUBLANES = 8
VMEM_LIMIT = 56 * 1024 * 1024
AUG = 3
LOG2E = 1.4426950408889634
SAMPLE_CHUNK = 512
OUT_ROWS = 256

C_HQ, C_HK, C_HV, C_HG, C_HGATE = 0, 512, 1024, 1536, 2048
ACT_PROMPT = dict(FGATE=2560, MQ=2816, MGATE=3072, WIDTH=3328)
ACT_SAMPLE = dict(FQ=2560, FGATE=2816, MQ=3072, MGATE=3328, WIDTH=3584)
W_HQ, W_HF, W_HI, W_HGATE, W_FQ, W_FK, W_FV, W_FGATE, W_MQ, W_MGATE = (
    0, 512, 1024, 1536, 2048, 2304, 2560, 2816, 3072, 3328)


def _nt(a, b, precision=None):
    return lax.dot_general(a, b, (((1,), (1,)), ((), ())), precision=precision,
                           preferred_element_type=F32)


def _mm(a, b, precision=None):
    return jnp.dot(a, b, precision=precision, preferred_element_type=F32)


def _silu(x):
    return x * jax.nn.sigmoid(x)


def _log_sigmoid(x):
    return jnp.minimum(x, 0.0) - jnp.log(1.0 + jnp.exp(-jnp.abs(x)))


def _head_mask(hh):
    lane = lax.broadcasted_iota(jnp.int32, (1, LANES), 1)
    return (lane // FX_DIM) == hh


def _pair_lanes(h):
    return slice((h // 2) * LANES, (h // 2 + 1) * LANES)


def _split3(x):
    hi = x.astype(BF16).astype(F32)
    r = x - hi
    mid = r.astype(BF16).astype(F32)
    return hi, mid, r - mid


def _aug_operand(x_pair, d, hh, is_query):
    lane = lax.broadcasted_iota(jnp.int32, (1, LANES), 1)
    a = lane - (1 - hh) * FX_DIM
    hi, mid, lo = _split3(d if is_query else -d)
    p0, o0 = (0, AUG) if is_query else (AUG, 0)
    extra = jnp.where(a == p0, hi, jnp.where(a == p0 + 1, mid, jnp.where(a == p0 + 2, lo,
                      jnp.where((a >= o0) & (a < o0 + AUG), 1.0, 0.0))))
    return jnp.where(_head_mask(hh), x_pair, extra).astype(BF16)


def _proj_kernel(x_ref, w_ref, wfl_ref, wflt_ref, bfl_ref, bflt_ref, lbs_ref, *refs,
                 layer, seg, tiles_per_seq, prompt):
    nprev = 3 if prompt and layer > 0 else 0
    prev, refs = refs[:nprev], refs[nprev:]
    act_ref = refs[0]
    cols = ACT_PROMPT if prompt else ACT_SAMPLE
    tm = x_ref.shape[0]
    xb = x_ref[...].astype(BF16)

    def proj(c0, width):
        return _mm(xb, w_ref[:, c0:c0 + width])

    lbs = lbs_ref[...]
    e = jnp.exp(lbs - jnp.max(lbs, axis=0, keepdims=True))
    sm = e / jnp.sum(e, axis=0, keepdims=True)
    lb = jnp.sum(sm[:layer + 1], axis=0, keepdims=True) - sm[0:1]

    lf_nat = _log_sigmoid(_mm(xb, wfl_ref[...]) + bfl_ref[...])
    lf_t = _log_sigmoid(_nt(wflt_ref[...], xb) + bflt_ref[...])

    r = lax.broadcasted_iota(jnp.int32, (tm, tm), 0)
    c = lax.broadcasted_iota(jnp.int32, (tm, tm), 1)
    same = (r // seg) == (c // seg)
    lower = jnp.where((c <= r) & same, 1.0, 0.0).astype(BF16)
    d_nat = _cumsum_rows([lf_nat], lower)[0]

    fq = proj(W_FQ, FX_WIDTH)
    k = proj(W_FK, FX_WIDTH)
    v = proj(W_FV, FX_WIDTH)
    if prompt:
        _, qa_ref, ka_ref, kt_ref, vt_ref, lft_ref, cn_ref, kv_ref = refs
        if tiles_per_seq > 1:
            @pl.when(pl.program_id(0) % tiles_per_seq == 0)
            def _():
                cn_ref[...] = jnp.zeros_like(cn_ref)
            d_nat = d_nat + cn_ref[0:1, :]
            cn_ref[...] = jnp.broadcast_to(d_nat[tm - 1:tm, :], cn_ref.shape)
        for src, dst in zip(prev, (kt_ref, vt_ref, lft_ref)):
            dst[0:layer] = src[...]
        lft_ref[layer, 0] = lf_t[:FX_HEADS]
        kv_ref[:, 0:FX_WIDTH] = k
        kv_ref[:, FX_WIDTH:] = v
        kt_ref[layer, 0] = kv_ref[:, 0:FX_WIDTH].T
        vt_ref[layer, 0] = kv_ref[:, FX_WIDTH:].T
        d2 = d_nat * LOG2E
        for h in range(FX_HEADS):
            hl = slice(h * LANES, (h + 1) * LANES)
            qa_ref[:, hl] = _aug_operand(fq[:, _pair_lanes(h)] * (FX_DIM ** -0.5 * LOG2E),
                                         d2[:, h:h + 1], h % 2, True)
            ka_ref[:, hl] = _aug_operand(k[:, _pair_lanes(h)], d2[:, h:h + 1], h % 2, False)
    else:
        _, k_ref, v_ref, lft_ref, dn_ref, dt_ref = refs
        upper = jnp.where((r <= c) & same, 1.0, 0.0).astype(F32)
        dt_ref[...] = _mm(lf_t, upper, precision=lax.Precision.HIGHEST)
        lft_ref[...] = lf_t
        k_ref[...] = k
        v_ref[...] = v
        dn_ref[...] = d_nat
        act_ref[:, cols["FQ"]:cols["FQ"] + FX_WIDTH] = fq

    act_ref[:, C_HQ:C_HQ + HG_WIDTH] = _silu(proj(W_HQ, HG_WIDTH))
    forget = lb + (1.0 - lb) * jax.nn.sigmoid(proj(W_HF, HG_WIDTH))
    act_ref[:, C_HK:C_HK + HG_WIDTH] = 1.0 - forget
    act_ref[:, C_HG:C_HG + HG_WIDTH] = jnp.log2(forget)
    act_ref[:, C_HV:C_HV + HG_WIDTH] = proj(W_HI, HG_WIDTH)
    act_ref[:, C_HGATE:C_HGATE + HG_WIDTH] = _silu(proj(W_HGATE, HG_WIDTH))
    act_ref[:, cols["FGATE"]:cols["FGATE"] + FX_WIDTH] = _silu(proj(W_FGATE, FX_WIDTH))
    act_ref[:, cols["MQ"]:cols["MQ"] + MEM_WIDTH] = proj(W_MQ, MEM_WIDTH)
    act_ref[:, cols["MGATE"]:cols["MGATE"] + MEM_WIDTH] = _silu(proj(W_MGATE, MEM_WIDTH))


def _proj(x, w_main, w_fl, w_flt, b_fl, b_flt, lbs, prev=(), *, layer, batch, seq_len, tm, prompt):
    m, d = x.shape
    ntiles = m // tm
    seg = min(seq_len, tm)
    tps = max(seq_len // tm, 1)
    assert prompt or ntiles == 1
    assert len(prev) == (3 if prompt and layer > 0 else 0)
    kern = functools.partial(_proj_kernel, layer=layer, seg=seg, tiles_per_seq=tps, prompt=prompt)
    full = lambda shape: pl.BlockSpec(shape, lambda i: (0,) * len(shape))
    rows = lambda width: pl.BlockSpec((tm, width), lambda i: (i, 0))
    feat = lambda nl, nfeat: pl.BlockSpec((nl, 1, nfeat, tm), lambda i: (0, i // tps, 0, i % tps))
    prev_specs = []
    if prompt:
        nfeats = (FX_WIDTH, FX_WIDTH, FX_HEADS)
        prev_specs = [feat(layer, nf) for nf in nfeats] if layer > 0 else []
        out_specs = [rows(ACT_PROMPT["WIDTH"]), rows(FX_HEADS * LANES), rows(FX_HEADS * LANES)]
        out_specs += [feat(layer + 1, nf) for nf in nfeats]
        out_shape = [jax.ShapeDtypeStruct((m, ACT_PROMPT["WIDTH"]), F32),
                     jax.ShapeDtypeStruct((m, FX_HEADS * LANES), BF16),
                     jax.ShapeDtypeStruct((m, FX_HEADS * LANES), BF16)]
        out_shape += [jax.ShapeDtypeStruct((layer + 1, batch, nf, seq_len), F32) for nf in nfeats]
        scratch = [pltpu.VMEM((SUBLANES, LANES), F32), pltpu.VMEM((tm, 2 * FX_WIDTH), F32)]
    else:
        out_specs = [rows(ACT_SAMPLE["WIDTH"]), rows(FX_WIDTH), rows(FX_WIDTH), full((SUBLANES, m)),
                     rows(LANES), full((SUBLANES, m))]
        out_shape = [jax.ShapeDtypeStruct((m, ACT_SAMPLE["WIDTH"]), F32),
                     jax.ShapeDtypeStruct((m, FX_WIDTH), F32),
                     jax.ShapeDtypeStruct((m, FX_WIDTH), F32),
                     jax.ShapeDtypeStruct((SUBLANES, m), F32),
                     jax.ShapeDtypeStruct((m, LANES), F32),
                     jax.ShapeDtypeStruct((SUBLANES, m), F32)]
        scratch = []
    return pl.pallas_call(
        kern,
        grid=(ntiles,),
        in_specs=[
            pl.BlockSpec((tm, d), lambda i: (i, 0)),
            full(w_main.shape), full(w_fl.shape), full(w_flt.shape),
            full(b_fl.shape), full(b_flt.shape), full(lbs.shape),
        ] + prev_specs,
        out_specs=out_specs,
        out_shape=out_shape,
        scratch_shapes=scratch,
        compiler_params=pltpu.CompilerParams(
            dimension_semantics=("arbitrary",), vmem_limit_bytes=VMEM_LIMIT),
        name="proj_prompt" if prompt else "proj_sample",
    )(x, w_main, w_fl, w_flt, b_fl, b_flt, lbs, *prev)


def _half_ref(gc, m, rows):
    n, w = gc.shape
    if 2 * m >= SUBLANES:
        g3 = gc.reshape(n // (2 * m), 2 * m, w)
        return jnp.broadcast_to(g3[:, m - 1:m, :], g3.shape).reshape(n, w)
    pos = rows % (2 * m)
    if m == 2:
        return jnp.where(pos == 0, pltpu.roll(gc, n - 1, axis=0),
                         jnp.where(pos == 1, gc,
                                   jnp.where(pos == 2, pltpu.roll(gc, 1, axis=0),
                                             pltpu.roll(gc, 2, axis=0))))
    return jnp.where(pos == 0, gc, pltpu.roll(gc, 1, axis=0))


def _level_exponent(g2, m, rows):
    ref = _half_ref(g2, m, rows)
    return jnp.where((rows % (2 * m)) < m, ref - g2, g2 - ref)


def _pair_levels(n):
    rr = lax.broadcasted_iota(jnp.int32, (n, n), 0)
    cc = lax.broadcasted_iota(jnp.int32, (n, n), 1)
    xf = (rr ^ cc).astype(F32)
    level = (lax.bitcast_convert_type(xf, jnp.int32) >> 23) - 127
    return jnp.where(rr > cc, level, -1)


def _cumsum_rows(gs, lower):
    w = gs[0].shape[1]
    pieces = [p.astype(BF16) for g in gs for p in _split3(g)]
    s = _mm(lower, jnp.concatenate(pieces, axis=1))
    return [s[:, (3 * i) * w:(3 * i + 1) * w] + s[:, (3 * i + 1) * w:(3 * i + 2) * w]
            + s[:, (3 * i + 2) * w:(3 * i + 3) * w] for i in range(len(gs))]


def _hgrn_intra(q, k, g2, levels):
    n, w = q.shape
    rows = lax.broadcasted_iota(jnp.int32, (n, w), 0)
    nslab = n // SUBLANES
    slab = lambda x, r: x[r * SUBLANES:(r + 1) * SUBLANES]
    a = [jnp.zeros((SUBLANES, n), F32)] * nslab
    m, bit = n // 2, n.bit_length() - 2
    while m >= 1:
        if m >= SUBLANES:
            qs, ks, targets = [], [], []
            for r0 in range(0, n, 2 * m):
                first, second = slice(r0, r0 + m), slice(r0 + m, r0 + 2 * m)
                ref = g2[r0 + m - 1:r0 + m]
                ks += [k[first] * jnp.exp2(ref - g2[first]), jnp.zeros((m, w), F32)]
                qs.append(q[second] * jnp.exp2(g2[second] - ref))
                targets += range((r0 + m) // SUBLANES, (r0 + 2 * m) // SUBLANES)
            a_m = _nt(jnp.concatenate(qs, axis=0).astype(BF16), jnp.concatenate(ks, axis=0).astype(BF16))
            for i, r in enumerate(targets):
                a[r] = jnp.where(slab(levels, r) == bit, slab(a_m, i), a[r])
        else:
            f = jnp.exp2(_level_exponent(g2, m, rows))
            a_m = _nt((q * f).astype(BF16), (k * f).astype(BF16))
            a = [jnp.where(slab(levels, r) == bit, slab(a_m, r), a[r]) for r in range(nslab)]
        m, bit = m // 2, bit - 1
    return jnp.concatenate(a, axis=0)


def _hgrn_kernel(*refs, group, has_init):
    if has_init:
        q_ref, k_ref, v_ref, g_ref, gate_ref, ng_ref, s0_ref, o_ref, so_ref = refs[:9]
    else:
        q_ref, k_ref, v_ref, g_ref, gate_ref, ng_ref, o_ref, so_ref = refs[:8]
    st_ref, a_ref, qe_ref, ke_ref, vb_ref, dg_ref, dec_ref = refs[-7:]
    t = pl.program_id(2)
    nheads = st_ref.shape[0]
    ngroups = q_ref.shape[0] // group
    lanes = [slice(hh * HG_DIM, (hh + 1) * HG_DIM) for hh in range(nheads)]

    @pl.when(t == 0)
    def _():
        for hh in range(nheads):
            st_ref[hh] = s0_ref[0, 0, hh].T if has_init else jnp.zeros((HG_DIM, HG_DIM), F32)

    ng = ng_ref[...]
    levels = _pair_levels(group)
    rr = lax.broadcasted_iota(jnp.int32, (group, group), 0)
    cc = lax.broadcasted_iota(jnp.int32, (group, group), 1)
    lower = jnp.where(rr >= cc, 1.0, 0.0).astype(BF16)

    def prepare(i, s):
        sl = pl.ds(pl.multiple_of(i * group, group), group)
        g2s = _cumsum_rows([g_ref[sl, ls] for ls in lanes], lower)
        for hh, ls in enumerate(lanes):
            q, k, v, g2 = q_ref[sl, ls], k_ref[sl, ls], v_ref[sl, ls], g2s[hh]
            g_last = g2[group - 1:group, :]
            a_ref[s, hh] = _hgrn_intra(q, k, g2, levels).astype(BF16)
            qe_ref[s, hh] = (q * jnp.exp2(g2)).astype(BF16)
            ke_ref[s, hh] = (k * jnp.exp2(g_last - g2)).astype(BF16)
            vb_ref[s, hh] = v.astype(BF16)
            dg_ref[s, hh] = jnp.sum(q * k, axis=-1, keepdims=True) * v
            dec_ref[s, hh] = jnp.broadcast_to(jnp.exp2(g_last), (SUBLANES, HG_DIM))

    def finish(i, s):
        sl = pl.ds(pl.multiple_of(i * group, group), group)
        for hh, ls in enumerate(lanes):
            st = st_ref[hh]
            vb = vb_ref[s, hh]
            o = _nt(qe_ref[s, hh], st.astype(BF16)) + _mm(a_ref[s, hh], vb) + dg_ref[s, hh]
            upd = lax.dot_general(vb, ke_ref[s, hh], (((0,), (0,)), ((), ())), preferred_element_type=F32)
            st_ref[hh] = st * dec_ref[s, hh, 0:1, :] + upd
            o = o * lax.rsqrt(jnp.mean(o * o, axis=-1, keepdims=True) + RMS_EPS) * ng
            o_ref[sl, ls] = (o * gate_ref[sl, ls]).astype(o_ref.dtype)

    prepare(0, 0)
    if ngroups > 1:
        def step(ip, c):
            finish(2 * ip, 0)
            prepare(2 * ip + 1, 1)
            finish(2 * ip + 1, 1)
            prepare(2 * ip + 2, 0)
            return c

        lax.fori_loop(0, ngroups // 2 - 1, step, 0)
        finish(ngroups - 2, 0)
        prepare(ngroups - 1, 1)
        finish(ngroups - 1, 1)
    else:
        finish(0, 0)

    @pl.when(t == pl.num_programs(2) - 1)
    def _():
        for hh in range(nheads):
            so_ref[0, hh] = st_ref[hh].T


def _hgrn(act, norm_g, s0, *, layer, batch, seq_len, tt, nheads):
    m = act.shape[0]
    nt = seq_len // tt
    has_init = s0 is not None
    group = min(tt, HG_GROUP)
    assert tt == group or (tt // group) % 2 == 0, "groups are processed in pairs"
    width = nheads * HG_DIM
    blk = lambda cb: pl.BlockSpec((tt, width), lambda b, h, t, cb=cb: (b * nt + t, cb // width + h))
    in_specs = [blk(C_HQ), blk(C_HK), blk(C_HV), blk(C_HG), blk(C_HGATE),
                pl.BlockSpec((1, HG_DIM), lambda b, h, t: (0, 0))]
    args = [act, act, act, act, act, norm_g]
    if has_init:
        in_specs.append(pl.BlockSpec((1, 1, nheads, HG_DIM, HG_DIM), lambda b, h, t: (layer, b, h, 0, 0)))
        args.append(s0)
    return pl.pallas_call(
        functools.partial(_hgrn_kernel, group=group, has_init=has_init),
        grid=(batch, HG_HEADS // nheads, nt),
        in_specs=in_specs,
        out_specs=[
            pl.BlockSpec((tt, width), lambda b, h, t: (b * nt + t, h)),
            pl.BlockSpec((1, nheads, HG_DIM, HG_DIM), lambda b, h, t: (b, h, 0, 0)),
        ],
        out_shape=[
            jax.ShapeDtypeStruct((m, HG_WIDTH), BF16),
            jax.ShapeDtypeStruct((batch, HG_HEADS, HG_DIM, HG_DIM), F32),
        ],
        scratch_shapes=[
            pltpu.VMEM((nheads, HG_DIM, HG_DIM), F32),
            pltpu.VMEM((2, nheads, group, group), BF16),
            pltpu.VMEM((2, nheads, group, HG_DIM), BF16),
            pltpu.VMEM((2, nheads, group, HG_DIM), BF16),
            pltpu.VMEM((2, nheads, group, HG_DIM), BF16),
            pltpu.VMEM((2, nheads, group, HG_DIM), F32),
            pltpu.VMEM((2, nheads, SUBLANES, HG_DIM), F32),
        ],
        compiler_params=pltpu.CompilerParams(
            dimension_semantics=("arbitrary", "arbitrary", "arbitrary"), vmem_limit_bytes=VMEM_LIMIT),
        name="hgrn",
    )(*args)


VROWS = FX_DIM + 16


def _softmax_t(st, m):
    m_new = jnp.maximum(m, jnp.max(st, axis=0, keepdims=True))
    return m_new, jnp.exp2(m - m_new), jnp.exp2(st - m_new).astype(BF16)


def _with_ones(vt):
    return jnp.concatenate([vt.astype(BF16), jnp.ones((VROWS - FX_DIM, vt.shape[1]), BF16)], axis=0)


def _normalised(acc):
    return acc[0:FX_DIM] / acc[FX_DIM:FX_DIM + 1]


def _attn_prompt_kernel(qa_ref, fgate_ref, mq_ref, mgate_ref, ka_ref, vt_ref, mk_ref, mvt_ref,
                        o_ref, vtb_ref, mvtb_ref, st_ref, p_ref, acc_ref, *, tq):
    seq = qa_ref.shape[0]
    tk = tq
    ntiles = seq // tq
    heads = range(FX_HEADS)

    for j in range(seq // tk):
        for h in heads:
            vtb_ref[j, h] = _with_ones(vt_ref[0, 0, h * FX_DIM:(h + 1) * FX_DIM, j * tk:(j + 1) * tk])
    for h in range(MEM_HEADS):
        mvtb_ref[h] = _with_ones(mvt_ref[0, h * MEM_DIM:(h + 1) * MEM_DIM, :])

    key = lax.broadcasted_iota(jnp.int32, (tk, tq), 0)
    qry = lax.broadcasted_iota(jnp.int32, (tk, tq), 1)

    def tile_rows(i):
        return pl.ds(pl.multiple_of(i * tq, tq), tq)

    def scores(i, j, h):
        hl = slice(h * LANES, (h + 1) * LANES)
        return _nt(ka_ref[tile_rows(j), hl], qa_ref[tile_rows(i), hl])

    def start(i, s):
        for h in heads:
            st_ref[s, 0, h] = scores(i, 0, h)
            p_ref[s, 1, h] = jnp.zeros((tk, tq), BF16)
            acc_ref[s, h] = jnp.zeros((VROWS, tq), F32)

    def block(i, s, j, b, carry):
        prev = jnp.maximum(j - 1, 0)
        out = []
        for h in heads:
            m, alpha_prev = carry[h]
            pv = _mm(vtb_ref[prev, h], p_ref[s, 1 - b, h])
            st = st_ref[s, b, h]
            st_ref[s, 1 - b, h] = scores(i, j + 1, h)
            m, alpha, p = _softmax_t(st, m)
            acc_ref[s, h] = alpha_prev * acc_ref[s, h] + pv
            p_ref[s, b, h] = p
            out.append((m, alpha))
        return tuple(out)

    def finish(i, s, carry):
        rows = tile_rows(i)
        prev = jnp.maximum(i - 1, 0)
        o_heads = []
        for h in heads:
            m, alpha_prev = carry[h]
            acc = alpha_prev * acc_ref[s, h] + _mm(vtb_ref[prev, h], p_ref[s, 1 - s, h])
            m, alpha, p = _softmax_t(jnp.where(key <= qry, st_ref[s, s, h], -jnp.inf), m)
            acc = alpha * acc + _mm(vtb_ref[i, h], p)
            o_heads.append(_normalised(acc))
        for pr in range(FX_HEADS // 2):
            ls = _pair_lanes(2 * pr)
            o_t = jnp.concatenate(o_heads[2 * pr:2 * pr + 2], axis=0)
            o_ref[rows, ls] = (o_t.T * fgate_ref[rows, ls]).astype(o_ref.dtype)

    def mem_scores(i, s):
        rows = tile_rows(i)
        for h in range(MEM_HEADS):
            mq_pair = mq_ref[rows, _pair_lanes(h)] * (MEM_DIM ** -0.5 * LOG2E)
            qm = jnp.where(_head_mask(h % 2), mq_pair, 0.0).astype(BF16)
            st_ref[s, 0, h] = _nt(mk_ref[:, _pair_lanes(h)], qm)

    def mem_finish(i, s):
        rows = tile_rows(i)
        for pr in range(MEM_HEADS // 2):
            ls = _pair_lanes(2 * pr)
            halves = []
            for hh in range(2):
                st = st_ref[s, 0, 2 * pr + hh]
                pm = jnp.exp2(st - jnp.max(st, axis=0, keepdims=True))
                halves.append(_normalised(_mm(mvtb_ref[2 * pr + hh], pm.astype(BF16))))
            o_t = jnp.concatenate(halves, axis=0)
            lm = slice(FX_WIDTH + pr * LANES, FX_WIDTH + (pr + 1) * LANES)
            o_ref[rows, lm] = (o_t.T * mgate_ref[rows, ls]).astype(o_ref.dtype)

    mem_scores(0, 0)

    def mem_pair(ip, c):
        mem_finish(2 * ip, 0)
        mem_scores(2 * ip + 1, 1)
        mem_finish(2 * ip + 1, 1)
        mem_scores(jnp.minimum(2 * ip + 2, ntiles - 1), 0)
        return c

    lax.fori_loop(0, ntiles // 2, mem_pair, 0)

    def qtile(ip, s):
        i = 2 * ip + s

        def two_blocks(jp, carry):
            return block(i, s, 2 * jp + 1, 1, block(i, s, 2 * jp, 0, carry))

        init = tuple((jnp.full((1, tq), -jnp.inf, F32), jnp.ones((1, tq), F32)) for _ in heads)
        carry = lax.fori_loop(0, ip, two_blocks, init)
        if s == 1:
            carry = block(i, s, i - 1, 0, carry)
        finish(i, s, carry)
        start(jnp.minimum(i + 1, ntiles - 1), 1 - s)

    start(0, 0)

    def tile_pair(ip, c):
        qtile(ip, 0)
        qtile(ip, 1)
        return c

    lax.fori_loop(0, ntiles // 2, tile_pair, 0)


def _attn_prompt(act, qa, ka, vt, mk, mvt, *, layer, batch, seq_len, tq):
    m = act.shape[0]
    n_mem = mk.shape[0] // batch
    tk = tq
    assert (seq_len // tq) % 2 == 0, "query tiles are processed in pairs"
    ablk = lambda c0: pl.BlockSpec((seq_len, FX_WIDTH), lambda b, c0=c0: (b, c0 // FX_WIDTH))
    return pl.pallas_call(
        functools.partial(_attn_prompt_kernel, tq=tq),
        grid=(batch,),
        in_specs=[
            pl.BlockSpec((seq_len, FX_HEADS * LANES), lambda b: (b, 0)),
            ablk(ACT_PROMPT["FGATE"]), ablk(ACT_PROMPT["MQ"]), ablk(ACT_PROMPT["MGATE"]),
            pl.BlockSpec((seq_len, FX_HEADS * LANES), lambda b: (b, 0)),
            pl.BlockSpec((1, 1, FX_WIDTH, seq_len), lambda b: (layer, b, 0, 0)),
            pl.BlockSpec((n_mem, MEM_WIDTH), lambda b: (b, 0)),
            pl.BlockSpec((1, MEM_WIDTH, n_mem), lambda b: (b, 0, 0)),
        ],
        out_specs=pl.BlockSpec((seq_len, FX_WIDTH + MEM_WIDTH), lambda b: (b, 0)),
        out_shape=jax.ShapeDtypeStruct((m, FX_WIDTH + MEM_WIDTH), BF16),
        scratch_shapes=[
            pltpu.VMEM((seq_len // tk, FX_HEADS, VROWS, tk), BF16),
            pltpu.VMEM((MEM_HEADS, VROWS, n_mem), BF16),
            pltpu.VMEM((2, 2, FX_HEADS, tk, tq), F32),
            pltpu.VMEM((2, 2, FX_HEADS, tk, tq), BF16),
            pltpu.VMEM((2, FX_HEADS, VROWS, tq), F32),
        ],
        compiler_params=pltpu.CompilerParams(
            dimension_semantics=("arbitrary",), vmem_limit_bytes=VMEM_LIMIT),
        name="attn_prompt",
    )(qa, act, act, act, ka, vt, mk, mvt)


def _stack_heads(x, scale):
    lane = lax.broadcasted_iota(jnp.int32, (1, x.shape[1]), 1)
    return jnp.concatenate([jnp.where(lane // FX_DIM == h, x * scale, 0.0) for h in range(FX_HEADS)],
                           axis=0).astype(BF16)


def _unstack_heads(o, ts):
    lane = lax.broadcasted_iota(jnp.int32, (1, o.shape[1]), 1)
    out = jnp.zeros((ts, o.shape[1]), F32)
    for h in range(FX_HEADS):
        out = jnp.where(lane // FX_DIM == h, o[h * ts:(h + 1) * ts, :], out)
    return out


def _per_head_rows(vals, ts):
    return jnp.concatenate([jnp.broadcast_to(v, (ts, v.shape[1])) for v in vals], axis=0)


def _attn_sample_kernel(fq_ref, fgate_ref, mq_ref, mgate_ref, kn_ref, vn_ref, dn_ref, dt_ref,
                        pkt_ref, pvt_ref, plf_ref, mkt_ref, mvt_ref, o_ref, s_ref, *, chunk):
    ts = fq_ref.shape[0]
    n_new = kn_ref.shape[0]
    past = pkt_ref.shape[3]
    nchunks = past // chunk
    b = pl.program_id(0)
    hrows = FX_HEADS * ts

    r = lax.broadcasted_iota(jnp.int32, (chunk, chunk), 0)
    c = lax.broadcasted_iota(jnp.int32, (chunk, chunk), 1)
    upper = jnp.where(r <= c, 1.0, 0.0).astype(BF16)
    chunks = [slice(ci * chunk, (ci + 1) * chunk) for ci in range(nchunks)]
    pieces = jnp.concatenate([p.astype(BF16) for p in _split3(plf_ref[0, 0])], axis=0)
    local = _mm(pieces, upper)
    nrows = nchunks * SUBLANES
    local = local[0:nrows] + local[nrows:2 * nrows] + local[2 * nrows:]
    d_past, total = [], jnp.zeros((SUBLANES, 1), F32)
    for ci in range(nchunks):
        d = local[ci * SUBLANES:(ci + 1) * SUBLANES]
        d_past.append(d + total)
        total = total + d[:, chunk - 1:chunk]

    dn = dn_ref[...]
    dt_new = dt_ref[...]
    tot = [total[h:h + 1, :] for h in range(FX_HEADS)]
    dq = jnp.concatenate([tot[h] + dn[:, h:h + 1] for h in range(FX_HEADS)], axis=0)
    q4 = _stack_heads(fq_ref[...], FX_DIM ** -0.5)

    for ci, cs in enumerate(chunks):
        dk = _per_head_rows([d_past[ci][h:h + 1, :] for h in range(FX_HEADS)], ts)
        s_ref[:, cs] = _mm(q4, pkt_ref[0, 0, :, cs].astype(BF16)) + dq - dk
    row = lax.broadcasted_iota(jnp.int32, (hrows, n_new), 0)
    col = lax.broadcasted_iota(jnp.int32, (hrows, n_new), 1)
    visible = (col // ts == b) & (col % ts <= row % ts)
    dk = _per_head_rows([tot[h] + dt_new[h:h + 1, :] for h in range(FX_HEADS)], ts)
    s_new = jnp.where(visible, _nt(q4, kn_ref[...].astype(BF16)) + dq - dk, -jnp.inf)

    m = jnp.max(s_new, axis=-1, keepdims=True)
    for cs in chunks:
        m = jnp.maximum(m, jnp.max(s_ref[:, cs], axis=-1, keepdims=True))
    p_new = jnp.exp(s_new - m)
    l = jnp.sum(p_new, axis=-1, keepdims=True)
    acc = _mm(p_new.astype(BF16), vn_ref[...].astype(BF16))
    for cs in chunks:
        p = jnp.exp(s_ref[:, cs] - m)
        l = l + jnp.sum(p, axis=-1, keepdims=True)
        acc = acc + _nt(p.astype(BF16), pvt_ref[0, 0, :, cs].astype(BF16))
    o_fx = _unstack_heads(acc / l, ts) * fgate_ref[...]
    o_ref[:, 0:FX_WIDTH] = o_fx.astype(o_ref.dtype)

    mq4 = _stack_heads(mq_ref[...], MEM_DIM ** -0.5)
    s = _mm(mq4, mkt_ref[0, 0].astype(BF16))
    p = jnp.exp(s - jnp.max(s, axis=-1, keepdims=True))
    o = _nt(p.astype(BF16), mvt_ref[0, 0].astype(BF16)) / jnp.sum(p, axis=-1, keepdims=True)
    o_mem = _unstack_heads(o, ts) * mgate_ref[...]
    o_ref[:, FX_WIDTH:] = o_mem.astype(o_ref.dtype)


def _attn_sample(act, k_new, v_new, d_nat, d_t, past_kt, past_vt, past_lft, mem_kt, mem_vt,
                 *, layer, batch, ts):
    m = act.shape[0]
    past = past_kt.shape[3]
    n_mem = mem_kt.shape[3]
    chunk = past_lft.shape[3]
    ablk = lambda c0: pl.BlockSpec((ts, FX_WIDTH), lambda b, c0=c0: (b, c0 // FX_WIDTH))
    cache = lambda nfeat, n: pl.BlockSpec((1, 1, nfeat, n), lambda b: (layer, b, 0, 0))
    return pl.pallas_call(
        functools.partial(_attn_sample_kernel, chunk=chunk),
        grid=(batch,),
        in_specs=[
            ablk(ACT_SAMPLE["FQ"]), ablk(ACT_SAMPLE["FGATE"]), ablk(ACT_SAMPLE["MQ"]),
            ablk(ACT_SAMPLE["MGATE"]),
            pl.BlockSpec((m, FX_WIDTH), lambda b: (0, 0)),
            pl.BlockSpec((m, FX_WIDTH), lambda b: (0, 0)),
            pl.BlockSpec((ts, LANES), lambda b: (b, 0)),
            pl.BlockSpec((SUBLANES, m), lambda b: (0, 0)),
            cache(FX_WIDTH, past), cache(FX_WIDTH, past), cache(past // chunk * SUBLANES, chunk),
            cache(MEM_WIDTH, n_mem), cache(MEM_WIDTH, n_mem),
        ],
        out_specs=pl.BlockSpec((ts, FX_WIDTH + MEM_WIDTH), lambda b: (b, 0)),
        out_shape=jax.ShapeDtypeStruct((m, FX_WIDTH + MEM_WIDTH), BF16),
        scratch_shapes=[pltpu.VMEM((FX_HEADS * ts, past), F32)],
        compiler_params=pltpu.CompilerParams(
            dimension_semantics=("arbitrary",), vmem_limit_bytes=VMEM_LIMIT),
        name="attn_sample",
    )(act, act, act, act, k_new, v_new, d_nat, d_t, past_kt, past_vt, past_lft, mem_kt, mem_vt)


def _memkv_kernel(x_ref, w_ref, kb_ref, kt_ref, vt_ref):
    kv = _mm(x_ref[...].astype(BF16), w_ref[...])
    kb_ref[...] = kv[:, :MEM_WIDTH].astype(BF16)
    kt_ref[0] = kv[:, :MEM_WIDTH].T
    vt_ref[0] = kv[:, MEM_WIDTH:].T


def _memkv(mem, w, *, batch):
    m, d = mem.shape
    n_mem = m // batch
    return pl.pallas_call(
        _memkv_kernel,
        grid=(batch,),
        in_specs=[pl.BlockSpec((n_mem, d), lambda i: (i, 0)),
                  pl.BlockSpec(w.shape, lambda i: (0, 0))],
        out_specs=[pl.BlockSpec((n_mem, MEM_WIDTH), lambda i: (i, 0)),
                   pl.BlockSpec((1, MEM_WIDTH, n_mem), lambda i: (i, 0, 0)),
                   pl.BlockSpec((1, MEM_WIDTH, n_mem), lambda i: (i, 0, 0))],
        out_shape=[jax.ShapeDtypeStruct((m, MEM_WIDTH), BF16),
                   jax.ShapeDtypeStruct((batch, MEM_WIDTH, n_mem), F32),
                   jax.ShapeDtypeStruct((batch, MEM_WIDTH, n_mem), F32)],
        compiler_params=pltpu.CompilerParams(
            dimension_semantics=("arbitrary",), vmem_limit_bytes=VMEM_LIMIT),
        name="memkv",
    )(mem, w)


def _out_kernel(x_ref, hg_ref, at_ref, w_ref, g_ref, b_ref, y_ref, *, alpha):
    tm = x_ref.shape[0]
    rows_per = min(tm, OUT_ROWS)
    for r0 in range(0, tm, rows_per):
        rs = slice(r0, r0 + rows_per)
        acc = _mm(hg_ref[rs, :], w_ref[0:HG_WIDTH, :]) + _mm(at_ref[rs, :], w_ref[HG_WIDTH:, :])
        z = alpha * x_ref[rs, :] + acc
        mu = jnp.mean(z, axis=-1, keepdims=True)
        zc = z - mu
        var = jnp.mean(zc * zc, axis=-1, keepdims=True)
        y_ref[rs, :] = zc * lax.rsqrt(var + LN_EPS) * g_ref[...] + b_ref[...]


def _out(x, mix_hg, mix_at, w, g, b, *, alpha, tm):
    m, d = x.shape
    return pl.pallas_call(
        functools.partial(_out_kernel, alpha=alpha),
        grid=(m // tm,),
        in_specs=[
            pl.BlockSpec((tm, d), lambda i: (i, 0)),
            pl.BlockSpec((tm, HG_WIDTH), lambda i: (i, 0)),
            pl.BlockSpec((tm, FX_WIDTH + MEM_WIDTH), lambda i: (i, 0)),
            pl.BlockSpec(w.shape, lambda i: (0, 0)),
            pl.BlockSpec((1, d), lambda i: (0, 0)),
            pl.BlockSpec((1, d), lambda i: (0, 0)),
        ],
        out_specs=pl.BlockSpec((tm, d), lambda i: (i, 0)),
        out_shape=jax.ShapeDtypeStruct((m, d), F32),
        compiler_params=pltpu.CompilerParams(
            dimension_semantics=("arbitrary",), vmem_limit_bytes=VMEM_LIMIT),
        name="out",
    )(x, mix_hg, mix_at, w, g, b)


def _pick_tile(n, target):
    t = min(n, target)
    while n % t:
        t //= 2
    return t


def _feature_major(x):
    lead = x.shape[:-3]
    n, heads, dim = x.shape[-3:]
    nl = len(lead)
    perm = tuple(range(nl)) + (nl + 1, nl + 2, nl)
    return jnp.transpose(x, perm).reshape(lead + (heads * dim, n))


def _token_major(x, heads):
    lead = x.shape[:-2]
    width, n = x.shape[-2:]
    nl = len(lead)
    x = x.reshape(lead + (heads, width // heads, n))
    return jnp.transpose(x, tuple(range(nl)) + (nl + 2, nl, nl + 1))


def kernel(x_prompt, x_sample, mem_prompt, state_hgrn, cache_fox_k, cache_fox_v, cache_fox_logf,
           cache_mem_k, cache_mem_v, w_in, b_fox_forget, hgrn_lower_bounds, hgrn_norm_g,
           w_mem_kv, w_out, ln_g, ln_b):
    depth, d_model, _ = w_in.shape
    bp, seq, _ = x_prompt.shape
    bs, dseq, _ = x_sample.shape
    n_mem = mem_prompt.shape[1]
    alpha = (2 * depth) ** 0.25

    fl0 = 4 * HG_WIDTH + 4 * FX_WIDTH
    w_main = jnp.concatenate([w_in[:, :, :fl0], w_in[:, :, fl0 + FX_HEADS:]], axis=-1).astype(BF16)
    w_flc = w_in[:, :, fl0:fl0 + FX_HEADS]
    w_fl = jnp.pad(w_flc, ((0, 0), (0, 0), (0, LANES - FX_HEADS))).astype(BF16)
    w_flt = jnp.pad(jnp.swapaxes(w_flc, 1, 2), ((0, 0), (0, SUBLANES - FX_HEADS), (0, 0))).astype(BF16)
    b_fl = jnp.pad(b_fox_forget, ((0, 0), (0, LANES - FX_HEADS)))[:, None, :]
    b_flt = jnp.pad(b_fox_forget, ((0, 0), (0, SUBLANES - FX_HEADS)))[:, :, None]
    lbs = hgrn_lower_bounds.astype(F32)
    w_memb = w_mem_kv.astype(BF16)
    w_outb = w_out.astype(BF16)

    past_kt = _feature_major(cache_fox_k)
    past_vt = _feature_major(cache_fox_v)
    past = cache_fox_logf.shape[2]
    chunk = min(past, SAMPLE_CHUNK)
    past_lft = jnp.swapaxes(cache_fox_logf, 2, 3).reshape(depth, bs, FX_HEADS, past // chunk, chunk)
    past_lft = jnp.pad(jnp.swapaxes(past_lft, 2, 3),
                       ((0, 0), (0, 0), (0, 0), (0, SUBLANES - FX_HEADS), (0, 0)))
    past_lft = past_lft.reshape(depth, bs, past // chunk * SUBLANES, chunk)
    mem_kt = _feature_major(cache_mem_k)
    mem_vt = _feature_major(cache_mem_v)

    mp, ms = bp * seq, bs * dseq
    tb = _pick_tile(seq, 256)
    yp = x_prompt.reshape(mp, d_model)
    ys = x_sample.reshape(ms, d_model)
    mem2 = mem_prompt.reshape(bp * n_mem, d_model)

    p_hg, p_mkt, p_mvt = [], [], []
    s_hg, s_k, s_v, s_lft = [], [], [], []
    fox_leaves = ()
    for l in range(depth):
        wl = (w_main[l], w_fl[l], w_flt[l], b_fl[l], b_flt[l], lbs)
        ng = hgrn_norm_g[l][None, :]
        g_ln, b_ln = ln_g[l][None, :], ln_b[l][None, :]

        mkb, mkt, mvt = _memkv(mem2, w_memb[l], batch=bp)
        act, qa, ka, *fox_leaves = _proj(yp, *wl, fox_leaves, layer=l, batch=bp, seq_len=seq,
                                         tm=_pick_tile(seq, 512), prompt=True)
        mix_hg, hg = _hgrn(act, ng, None, layer=l, batch=bp, seq_len=seq, tt=_pick_tile(seq, 2048),
                           nheads=HG_HEADS)
        mix_at = _attn_prompt(act, qa, ka, fox_leaves[1], mkb, mvt, layer=l, batch=bp, seq_len=seq, tq=tb)
        yp = _out(yp, mix_hg, mix_at, w_outb[l], g_ln, b_ln, alpha=alpha, tm=_pick_tile(mp, 2048))
        p_hg.append(hg); p_mkt.append(mkt); p_mvt.append(mvt)

        act, k, v, lft, d_nat, d_t = _proj(ys, *wl, layer=l, batch=bs, seq_len=dseq, tm=ms, prompt=False)
        mix_hg, hg = _hgrn(act, ng, state_hgrn, layer=l, batch=bs, seq_len=dseq, tt=dseq, nheads=HG_HEADS)
        mix_at = _attn_sample(act, k, v, d_nat, d_t, past_kt, past_vt, past_lft, mem_kt, mem_vt,
                              layer=l, batch=bs, ts=dseq)
        ys = _out(ys, mix_hg, mix_at, w_outb[l], g_ln, b_ln, alpha=alpha, tm=ms)
        s_hg.append(hg); s_k.append(k); s_v.append(v); s_lft.append(lft)

    st = jnp.stack
    s_lf = st(s_lft)[:, :FX_HEADS, :].reshape(depth, FX_HEADS, bs, dseq)
    return (yp.reshape(bp, seq, d_model), ys.reshape(bs, dseq, d_model),
            st(p_hg),
            _token_major(fox_leaves[0], FX_HEADS),
            _token_major(fox_leaves[1], FX_HEADS),
            jnp.swapaxes(fox_leaves[2], 2, 3),
            _token_major(st(p_mkt), MEM_HEADS),
            _token_major(st(p_mvt), MEM_HEADS),
            st(s_hg),
            st(s_k).reshape(depth, bs, dseq, FX_HEADS, FX_DIM),
            st(s_v).reshape(depth, bs, dseq, FX_HEADS, FX_DIM),
            jnp.transpose(s_lf, (0, 2, 3, 1)))
```

```python
import functools

import jax
import jax.numpy as jnp
from jax import lax
from jax.experimental import pallas as pl
from jax.experimental.pallas import tpu as pltpu

F32 = jnp.float32
BF16 = jnp.bfloat16

HG_HEADS = 4
HG_DIM = 128
HG_WIDTH = HG_HEADS * HG_DIM
FX_HEADS = 4
FX_DIM = 64
FX_WIDTH = FX_HEADS * FX_DIM
MEM_HEADS = 4
MEM_DIM = 64
MEM_WIDTH = MEM_HEADS * MEM_DIM
HG_GROUP = 128
LN_EPS = 1e-5
RMS_EPS = 1e-6
LANES = 128
SUBLANES = 8
VMEM_LIMIT = 56 * 1024 * 1024
AUG = 3
LOG2E = 1.4426950408889634
SAMPLE_CHUNK = 512
OUT_ROWS = 256

C_HQ, C_HK, C_HV, C_HG, C_HGATE = 0, 512, 1024, 1536, 2048
ACT_PROMPT = dict(FGATE=2560, MQ=2816, MGATE=3072, WIDTH=3328)
ACT_SAMPLE = dict(FQ=2560, FGATE=2816, MQ=3072, MGATE=3328, WIDTH=3584)
W_HQ, W_HF, W_HI, W_HGATE, W_FQ, W_FK, W_FV, W_FGATE, W_MQ, W_MGATE = (
    0, 512, 1024, 1536, 2048, 2304, 2560, 2816, 3072, 3328)


def _nt(a, b, precision=None):
    return lax.dot_general(a, b, (((1,), (1,)), ((), ())), precision=precision,
                           preferred_element_type=F32)


def _mm(a, b, precision=None):
    return jnp.dot(a, b, precision=precision, preferred_element_type=F32)


def _silu(x):
    return x * jax.nn.sigmoid(x)


def _log_sigmoid(x):
    return jnp.minimum(x, 0.0) - jnp.log(1.0 + jnp.exp(-jnp.abs(x)))


def _head_mask(hh):
    lane = lax.broadcasted_iota(jnp.int32, (1, LANES), 1)
    return (lane // FX_DIM) == hh


def _pair_lanes(h):
    return slice((h // 2) * LANES, (h // 2 + 1) * LANES)


def _split3(x):
    hi = x.astype(BF16).astype(F32)
    r = x - hi
    mid = r.astype(BF16).astype(F32)
    return hi, mid, r - mid


def _aug_operand(x_pair, d, hh, is_query):
    lane = lax.broadcasted_iota(jnp.int32, (1, LANES), 1)
    a = lane - (1 - hh) * FX_DIM
    hi, mid, lo = _split3(d if is_query else -d)
    p0, o0 = (0, AUG) if is_query else (AUG, 0)
    extra = jnp.where(a == p0, hi, jnp.where(a == p0 + 1, mid, jnp.where(a == p0 + 2, lo,
                      jnp.where((a >= o0) & (a < o0 + AUG), 1.0, 0.0))))
    return jnp.where(_head_mask(hh), x_pair, extra).astype(BF16)


def _proj_kernel(x_ref, w_ref, wfl_ref, wflt_ref, bfl_ref, bflt_ref, lbs_ref, *refs,
                 layer, seg, tiles_per_seq, prompt):
    nprev = 3 if prompt and layer > 0 else 0
    prev, refs = refs[:nprev], refs[nprev:]
    act_ref = refs[0]
    cols = ACT_PROMPT if prompt else ACT_SAMPLE
    tm = x_ref.shape[0]
    xb = x_ref[...].astype(BF16)

    def proj(c0, width):
        return _mm(xb, w_ref[:, c0:c0 + width])

    lbs = lbs_ref[...]
    e = jnp.exp(lbs - jnp.max(lbs, axis=0, keepdims=True))
    sm = e / jnp.sum(e, axis=0, keepdims=True)
    lb = jnp.sum(sm[:layer + 1], axis=0, keepdims=True) - sm[0:1]

    lf_nat = _log_sigmoid(_mm(xb, wfl_ref[...]) + bfl_ref[...])
    lf_t = _log_sigmoid(_nt(wflt_ref[...], xb) + bflt_ref[...])

    r = lax.broadcasted_iota(jnp.int32, (tm, tm), 0)
    c = lax.broadcasted_iota(jnp.int32, (tm, tm), 1)
    same = (r // seg) == (c // seg)
    lower = jnp.where((c <= r) & same, 1.0, 0.0).astype(BF16)
    d_nat = _cumsum_rows([lf_nat], lower)[0]

    fq = proj(W_FQ, FX_WIDTH)
    k = proj(W_FK, FX_WIDTH)
    v = proj(W_FV, FX_WIDTH)
    if prompt:
        _, qa_ref, ka_ref, kt_ref, vt_ref, lft_ref, cn_ref, kv_ref = refs
        if tiles_per_seq > 1:
            @pl.when(pl.program_id(0) % tiles_per_seq == 0)
            def _():
                cn_ref[...] = jnp.zeros_like(cn_ref)
            d_nat = d_nat + cn_ref[0:1, :]
            cn_ref[...] = jnp.broadcast_to(d_nat[tm - 1:tm, :], cn_ref.shape)
        for src, dst in zip(prev, (kt_ref, vt_ref, lft_ref)):
            dst[0:layer] = src[...]
        lft_ref[layer, 0] = lf_t[:FX_HEADS]
        kv_ref[:, 0:FX_WIDTH] = k
        kv_ref[:, FX_WIDTH:] = v
        kt_ref[layer, 0] = kv_ref[:, 0:FX_WIDTH].T
        vt_ref[layer, 0] = kv_ref[:, FX_WIDTH:].T
        d2 = d_nat * LOG2E
        for h in range(FX_HEADS):
            hl = slice(h * LANES, (h + 1) * LANES)
            qa_ref[:, hl] = _aug_operand(fq[:, _pair_lanes(h)] * (FX_DIM ** -0.5 * LOG2E),
                                         d2[:, h:h + 1], h % 2, True)
            ka_ref[:, hl] = _aug_operand(k[:, _pair_lanes(h)], d2[:, h:h + 1], h % 2, False)
    else:
        _, k_ref, v_ref, lft_ref, dn_ref, dt_ref = refs
        upper = jnp.where((r <= c) & same, 1.0, 0.0).astype(F32)
        dt_ref[...] = _mm(lf_t, upper, precision=lax.Precision.HIGHEST)
        lft_ref[...] = lf_t
        k_ref[...] = k
        v_ref[...] = v
        dn_ref[...] = d_nat
        act_ref[:, cols["FQ"]:cols["FQ"] + FX_WIDTH] = fq

    act_ref[:, C_HQ:C_HQ + HG_WIDTH] = _silu(proj(W_HQ, HG_WIDTH))
    forget = lb + (1.0 - lb) * jax.nn.sigmoid(proj(W_HF, HG_WIDTH))
    act_ref[:, C_HK:C_HK + HG_WIDTH] = 1.0 - forget
    act_ref[:, C_HG:C_HG + HG_WIDTH] = jnp.log2(forget)
    act_ref[:, C_HV:C_HV + HG_WIDTH] = proj(W_HI, HG_WIDTH)
    act_ref[:, C_HGATE:C_HGATE + HG_WIDTH] = _silu(proj(W_HGATE, HG_WIDTH))
    act_ref[:, cols["FGATE"]:cols["FGATE"] + FX_WIDTH] = _silu(proj(W_FGATE, FX_WIDTH))
    act_ref[:, cols["MQ"]:cols["MQ"] + MEM_WIDTH] = proj(W_MQ, MEM_WIDTH)
    act_ref[:, cols["MGATE"]:cols["MGATE"] + MEM_WIDTH] = _silu(proj(W_MGATE, MEM_WIDTH))


def _proj(x, w_main, w_fl, w_flt, b_fl, b_flt, lbs, prev=(), *, layer, batch, seq_len, tm, prompt):
    m, d = x.shape
    ntiles = m // tm
    seg = min(seq_len, tm)
    tps = max(seq_len // tm, 1)
    assert prompt or ntiles == 1
    assert len(prev) == (3 if prompt and layer > 0 else 0)
    kern = functools.partial(_proj_kernel, layer=layer, seg=seg, tiles_per_seq=tps, prompt=prompt)
    full = lambda shape: pl.BlockSpec(shape, lambda i: (0,) * len(shape))
    rows = lambda width: pl.BlockSpec((tm, width), lambda i: (i, 0))
    feat = lambda nl, nfeat: pl.BlockSpec((nl, 1, nfeat, tm), lambda i: (0, i // tps, 0, i % tps))
    prev_specs = []
    if prompt:
        nfeats = (FX_WIDTH, FX_WIDTH, FX_HEADS)
        prev_specs = [feat(layer, nf) for nf in nfeats] if layer > 0 else []
        out_specs = [rows(ACT_PROMPT["WIDTH"]), rows(FX_HEADS * LANES), rows(FX_HEADS * LANES)]
        out_specs += [feat(layer + 1, nf) for nf in nfeats]
        out_shape = [jax.ShapeDtypeStruct((m, ACT_PROMPT["WIDTH"]), F32),
                     jax.ShapeDtypeStruct((m, FX_HEADS * LANES), BF16),
                     jax.ShapeDtypeStruct((m, FX_HEADS * LANES), BF16)]
        out_shape += [jax.ShapeDtypeStruct((layer + 1, batch, nf, seq_len), F32) for nf in nfeats]
        scratch = [pltpu.VMEM((SUBLANES, LANES), F32), pltpu.VMEM((tm, 2 * FX_WIDTH), F32)]
    else:
        out_specs = [rows(ACT_SAMPLE["WIDTH"]), rows(FX_WIDTH), rows(FX_WIDTH), full((SUBLANES, m)),
                     rows(LANES), full((SUBLANES, m))]
        out_shape = [jax.ShapeDtypeStruct((m, ACT_SAMPLE["WIDTH"]), F32),
                     jax.ShapeDtypeStruct((m, FX_WIDTH), F32),
                     jax.ShapeDtypeStruct((m, FX_WIDTH), F32),
                     jax.ShapeDtypeStruct((SUBLANES, m), F32),
                     jax.ShapeDtypeStruct((m, LANES), F32),
                     jax.ShapeDtypeStruct((SUBLANES, m), F32)]
        scratch = []
    return pl.pallas_call(
        kern,
        grid=(ntiles,),
        in_specs=[
            pl.BlockSpec((tm, d), lambda i: (i, 0)),
            full(w_main.shape), full(w_fl.shape), full(w_flt.shape),
            full(b_fl.shape), full(b_flt.shape), full(lbs.shape),
        ] + prev_specs,
        out_specs=out_specs,
        out_shape=out_shape,
        scratch_shapes=scratch,
        compiler_params=pltpu.CompilerParams(
            dimension_semantics=("arbitrary",), vmem_limit_bytes=VMEM_LIMIT),
        name="proj_prompt" if prompt else "proj_sample",
    )(x, w_main, w_fl, w_flt, b_fl, b_flt, lbs, *prev)


def _half_ref(gc, m, rows):
    n, w = gc.shape
    if 2 * m >= SUBLANES:
        g3 = gc.reshape(n // (2 * m), 2 * m, w)
        return jnp.broadcast_to(g3[:, m - 1:m, :], g3.shape).reshape(n, w)
    pos = rows % (2 * m)
    if m == 2:
        return jnp.where(pos == 0, pltpu.roll(gc, n - 1, axis=0),
                         jnp.where(pos == 1, gc,
                                   jnp.where(pos == 2, pltpu.roll(gc, 1, axis=0),
                                             pltpu.roll(gc, 2, axis=0))))
    return jnp.where(pos == 0, gc, pltpu.roll(gc, 1, axis=0))


def _level_exponent(g2, m, rows):
    ref = _half_ref(g2, m, rows)
    return jnp.where((rows % (2 * m)) < m, ref - g2, g2 - ref)


def _pair_levels(n):
    rr = lax.broadcasted_iota(jnp.int32, (n, n), 0)
    cc = lax.broadcasted_iota(jnp.int32, (n, n), 1)
    xf = (rr ^ cc).astype(F32)
    level = (lax.bitcast_convert_type(xf, jnp.int32) >> 23) - 127
    return jnp.where(rr > cc, level, -1)


def _cumsum_rows(gs, lower):
    w = gs[0].shape[1]
    pieces = [p.astype(BF16) for g in gs for p in _split3(g)]
    s = _mm(lower, jnp.concatenate(pieces, axis=1))
    return [s[:, (3 * i) * w:(3 * i + 1) * w] + s[:, (3 * i + 1) * w:(3 * i + 2) * w]
            + s[:, (3 * i + 2) * w:(3 * i + 3) * w] for i in range(len(gs))]


def _hgrn_intra(q, k, g2, levels):
    n, w = q.shape
    rows = lax.broadcasted_iota(jnp.int32, (n, w), 0)
    nslab = n // SUBLANES
    slab = lambda x, r: x[r * SUBLANES:(r + 1) * SUBLANES]
    a = [jnp.zeros((SUBLANES, n), F32)] * nslab
    m, bit = n // 2, n.bit_length() - 2
    while m >= 1:
        if m >= SUBLANES:
            qs, ks, targets = [], [], []
            for r0 in range(0, n, 2 * m):
                first, second = slice(r0, r0 + m), slice(r0 + m, r0 + 2 * m)
                ref = g2[r0 + m - 1:r0 + m]
                ks += [k[first] * jnp.exp2(ref - g2[first]), jnp.zeros((m, w), F32)]
                qs.append(q[second] * jnp.exp2(g2[second] - ref))
                targets += range((r0 + m) // SUBLANES, (r0 + 2 * m) // SUBLANES)
            a_m = _nt(jnp.concatenate(qs, axis=0).astype(BF16), jnp.concatenate(ks, axis=0).astype(BF16))
            for i, r in enumerate(targets):
                a[r] = jnp.where(slab(levels, r) == bit, slab(a_m, i), a[r])
        else:
            f = jnp.exp2(_level_exponent(g2, m, rows))
            a_m = _nt((q * f).astype(BF16), (k * f).astype(BF16))
            a = [jnp.where(slab(levels, r) == bit, slab(a_m, r), a[r]) for r in range(nslab)]
        m, bit = m // 2, bit - 1
    return jnp.concatenate(a, axis=0)


def _hgrn_kernel(*refs, group, has_init):
    if has_init:
        q_ref, k_ref, v_ref, g_ref, gate_ref, ng_ref, s0_ref, o_ref, so_ref = refs[:9]
    else:
        q_ref, k_ref, v_ref, g_ref, gate_ref, ng_ref, o_ref, so_ref = refs[:8]
    st_ref, a_ref, qe_ref, ke_ref, vb_ref, dg_ref, dec_ref = refs[-7:]
    t = pl.program_id(2)
    nheads = st_ref.shape[0]
    ngroups = q_ref.shape[0] // group
    lanes = [slice(hh * HG_DIM, (hh + 1) * HG_DIM) for hh in range(nheads)]

    @pl.when(t == 0)
    def _():
        for hh in range(nheads):
            st_ref[hh] = s0_ref[0, 0, hh].T if has_init else jnp.zeros((HG_DIM, HG_DIM), F32)

    ng = ng_ref[...]
    levels = _pair_levels(group)
    rr = lax.broadcasted_iota(jnp.int32, (group, group), 0)
    cc = lax.broadcasted_iota(jnp.int32, (group, group), 1)
    lower = jnp.where(rr >= cc, 1.0, 0.0).astype(BF16)

    def prepare(i, s):
        sl = pl.ds(pl.multiple_of(i * group, group), group)
        g2s = _cumsum_rows([g_ref[sl, ls] for ls in lanes], lower)
        for hh, ls in enumerate(lanes):
            q, k, v, g2 = q_ref[sl, ls], k_ref[sl, ls], v_ref[sl, ls], g2s[hh]
            g_last = g2[group - 1:group, :]
            a_ref[s, hh] = _hgrn_intra(q, k, g2, levels).astype(BF16)
            qe_ref[s, hh] = (q * jnp.exp2(g2)).astype(BF16)
            ke_ref[s, hh] = (k * jnp.exp2(g_last - g2)).astype(BF16)
            vb_ref[s, hh] = v.astype(BF16)
            dg_ref[s, hh] = jnp.sum(q * k, axis=-1, keepdims=True) * v
            dec_ref[s, hh] = jnp.broadcast_to(jnp.exp2(g_last), (SUBLANES, HG_DIM))

    def finish(i, s):
        sl = pl.ds(pl.multiple_of(i * group, group), group)
        for hh, ls in enumerate(lanes):
            st = st_ref[hh]
            vb = vb_ref[s, hh]
            o = _nt(qe_ref[s, hh], st.astype(BF16)) + _mm(a_ref[s, hh], vb) + dg_ref[s, hh]
            upd = lax.dot_general(vb, ke_ref[s, hh], (((0,), (0,)), ((), ())), preferred_element_type=F32)
            st_ref[hh] = st * dec_ref[s, hh, 0:1, :] + upd
            o = o * lax.rsqrt(jnp.mean(o * o, axis=-1, keepdims=True) + RMS_EPS) * ng
            o_ref[sl, ls] = (o * gate_ref[sl, ls]).astype(o_ref.dtype)

    prepare(0, 0)
    if ngroups > 1:
        def step(ip, c):
            finish(2 * ip, 0)
            prepare(2 * ip + 1, 1)
            finish(2 * ip + 1, 1)
            prepare(2 * ip + 2, 0)
            return c

        lax.fori_loop(0, ngroups // 2 - 1, step, 0)
        finish(ngroups - 2, 0)
        prepare(ngroups - 1, 1)
        finish(ngroups - 1, 1)
    else:
        finish(0, 0)

    @pl.when(t == pl.num_programs(2) - 1)
    def _():
        for hh in range(nheads):
            so_ref[0, hh] = st_ref[hh].T


def _hgrn(act, norm_g, s0, *, layer, batch, seq_len, tt, nheads):
    m = act.shape[0]
    nt = seq_len // tt
    has_init = s0 is not None
    group = min(tt, HG_GROUP)
    assert tt == group or (tt // group) % 2 == 0, "groups are processed in pairs"
    width = nheads * HG_DIM
    blk = lambda cb: pl.BlockSpec((tt, width), lambda b, h, t, cb=cb: (b * nt + t, cb // width + h))
    in_specs = [blk(C_HQ), blk(C_HK), blk(C_HV), blk(C_HG), blk(C_HGATE),
                pl.BlockSpec((1, HG_DIM), lambda b, h, t: (0, 0))]
    args = [act, act, act, act, act, norm_g]
    if has_init:
        in_specs.append(pl.BlockSpec((1, 1, nheads, HG_DIM, HG_DIM), lambda b, h, t: (layer, b, h, 0, 0)))
        args.append(s0)
    return pl.pallas_call(
        functools.partial(_hgrn_kernel, group=group, has_init=has_init),
        grid=(batch, HG_HEADS // nheads, nt),
        in_specs=in_specs,
        out_specs=[
            pl.BlockSpec((tt, width), lambda b, h, t: (b * nt + t, h)),
            pl.BlockSpec((1, nheads, HG_DIM, HG_DIM), lambda b, h, t: (b, h, 0, 0)),
        ],
        out_shape=[
            jax.ShapeDtypeStruct((m, HG_WIDTH), BF16),
            jax.ShapeDtypeStruct((batch, HG_HEADS, HG_DIM, HG_DIM), F32),
        ],
        scratch_shapes=[
            pltpu.VMEM((nheads, HG_DIM, HG_DIM), F32),
            pltpu.VMEM((2, nheads, group, group), BF16),
            pltpu.VMEM((2, nheads, group, HG_DIM), BF16),
            pltpu.VMEM((2, nheads, group, HG_DIM), BF16),
            pltpu.VMEM((2, nheads, group, HG_DIM), BF16),
            pltpu.VMEM((2, nheads, group, HG_DIM), F32),
            pltpu.VMEM((2, nheads, SUBLANES, HG_DIM), F32),
        ],
        compiler_params=pltpu.CompilerParams(
            dimension_semantics=("arbitrary", "arbitrary", "arbitrary"), vmem_limit_bytes=VMEM_LIMIT),
        name="hgrn",
    )(*args)


VROWS = FX_DIM + 16


def _softmax_t(st, m):
    m_new = jnp.maximum(m, jnp.max(st, axis=0, keepdims=True))
    return m_new, jnp.exp2(m - m_new), jnp.exp2(st - m_new).astype(BF16)


def _with_ones(vt):
    return jnp.concatenate([vt.astype(BF16), jnp.ones((VROWS - FX_DIM, vt.shape[1]), BF16)], axis=0)


def _normalised(acc):
    return acc[0:FX_DIM] / acc[FX_DIM:FX_DIM + 1]


def _attn_prompt_kernel(qa_ref, fgate_ref, mq_ref, mgate_ref, ka_ref, vt_ref, mk_ref, mvt_ref,
                        o_ref, vtb_ref, mvtb_ref, st_ref, p_ref, acc_ref, *, tq):
    seq = qa_ref.shape[0]
    tk = tq
    ntiles = seq // tq
    heads = range(FX_HEADS)

    for j in range(seq // tk):
        for h in heads:
            vtb_ref[j, h] = _with_ones(vt_ref[0, 0, h * FX_DIM:(h + 1) * FX_DIM, j * tk:(j + 1) * tk])
    for h in range(MEM_HEADS):
        mvtb_ref[h] = _with_ones(mvt_ref[0, 0, h * MEM_DIM:(h + 1) * MEM_DIM, :])

    key = lax.broadcasted_iota(jnp.int32, (tk, tq), 0)
    qry = lax.broadcasted_iota(jnp.int32, (tk, tq), 1)

    def tile_rows(i):
        return pl.ds(pl.multiple_of(i * tq, tq), tq)

    def scores(i, j, h):
        hl = slice(h * LANES, (h + 1) * LANES)
        return _nt(ka_ref[tile_rows(j), hl], qa_ref[tile_rows(i), hl])

    def start(i, s):
        for h in heads:
            st_ref[s, 0, h] = scores(i, 0, h)
            p_ref[s, 1, h] = jnp.zeros((tk, tq), BF16)
            acc_ref[s, h] = jnp.zeros((VROWS, tq), F32)

    def block(i, s, j, b, carry):
        prev = jnp.maximum(j - 1, 0)
        out = []
        for h in heads:
            m, alpha_prev = carry[h]
            pv = _mm(vtb_ref[prev, h], p_ref[s, 1 - b, h])
            st = st_ref[s, b, h]
            st_ref[s, 1 - b, h] = scores(i, j + 1, h)
            m, alpha, p = _softmax_t(st, m)
            acc_ref[s, h] = alpha_prev * acc_ref[s, h] + pv
            p_ref[s, b, h] = p
            out.append((m, alpha))
        return tuple(out)

    def finish(i, s, carry):
        rows = tile_rows(i)
        prev = jnp.maximum(i - 1, 0)
        o_heads = []
        for h in heads:
            m, alpha_prev = carry[h]
            acc = alpha_prev * acc_ref[s, h] + _mm(vtb_ref[prev, h], p_ref[s, 1 - s, h])
            m, alpha, p = _softmax_t(jnp.where(key <= qry, st_ref[s, s, h], -jnp.inf), m)
            acc = alpha * acc + _mm(vtb_ref[i, h], p)
            o_heads.append(_normalised(acc))
        for pr in range(FX_HEADS // 2):
            ls = _pair_lanes(2 * pr)
            o_t = jnp.concatenate(o_heads[2 * pr:2 * pr + 2], axis=0)
            o_ref[rows, ls] = (o_t.T * fgate_ref[rows, ls]).astype(o_ref.dtype)

    def mem_scores(i, s):
        rows = tile_rows(i)
        for h in range(MEM_HEADS):
            mq_pair = mq_ref[rows, _pair_lanes(h)] * (MEM_DIM ** -0.5 * LOG2E)
            qm = jnp.where(_head_mask(h % 2), mq_pair, 0.0).astype(BF16)
            st_ref[s, 0, h] = _nt(mk_ref[0, :, _pair_lanes(h)], qm)

    def mem_finish(i, s):
        rows = tile_rows(i)
        for pr in range(MEM_HEADS // 2):
            ls = _pair_lanes(2 * pr)
            halves = []
            for hh in range(2):
                st = st_ref[s, 0, 2 * pr + hh]
                pm = jnp.exp2(st - jnp.max(st, axis=0, keepdims=True))
                halves.append(_normalised(_mm(mvtb_ref[2 * pr + hh], pm.astype(BF16))))
            o_t = jnp.concatenate(halves, axis=0)
            lm = slice(FX_WIDTH + pr * LANES, FX_WIDTH + (pr + 1) * LANES)
            o_ref[rows, lm] = (o_t.T * mgate_ref[rows, ls]).astype(o_ref.dtype)

    mem_scores(0, 0)

    def mem_pair(ip, c):
        mem_finish(2 * ip, 0)
        mem_scores(2 * ip + 1, 1)
        mem_finish(2 * ip + 1, 1)
        mem_scores(jnp.minimum(2 * ip + 2, ntiles - 1), 0)
        return c

    lax.fori_loop(0, ntiles // 2, mem_pair, 0)

    def qtile(ip, s):
        i = 2 * ip + s

        def two_blocks(jp, carry):
            return block(i, s, 2 * jp + 1, 1, block(i, s, 2 * jp, 0, carry))

        init = tuple((jnp.full((1, tq), -jnp.inf, F32), jnp.ones((1, tq), F32)) for _ in heads)
        carry = lax.fori_loop(0, ip, two_blocks, init)
        if s == 1:
            carry = block(i, s, i - 1, 0, carry)
        finish(i, s, carry)
        start(jnp.minimum(i + 1, ntiles - 1), 1 - s)

    start(0, 0)

    def tile_pair(ip, c):
        qtile(ip, 0)
        qtile(ip, 1)
        return c

    lax.fori_loop(0, ntiles // 2, tile_pair, 0)


def _attn_prompt(act, qa, ka, vt, mk, mvt, *, layer, batch, seq_len, tq):
    m = act.shape[0]
    n_mem = mk.shape[1] // batch
    tk = tq
    assert (seq_len // tq) % 2 == 0, "query tiles are processed in pairs"
    ablk = lambda c0: pl.BlockSpec((seq_len, FX_WIDTH), lambda b, c0=c0: (b, c0 // FX_WIDTH))
    return pl.pallas_call(
        functools.partial(_attn_prompt_kernel, tq=tq),
        grid=(batch,),
        in_specs=[
            pl.BlockSpec((seq_len, FX_HEADS * LANES), lambda b: (b, 0)),
            ablk(ACT_PROMPT["FGATE"]), ablk(ACT_PROMPT["MQ"]), ablk(ACT_PROMPT["MGATE"]),
            pl.BlockSpec((seq_len, FX_HEADS * LANES), lambda b: (b, 0)),
            pl.BlockSpec((1, 1, FX_WIDTH, seq_len), lambda b: (layer, b, 0, 0)),
            pl.BlockSpec((1, n_mem, MEM_WIDTH), lambda b: (layer, b, 0)),
            pl.BlockSpec((1, 1, MEM_WIDTH, n_mem), lambda b: (layer, b, 0, 0)),
        ],
        out_specs=pl.BlockSpec((seq_len, FX_WIDTH + MEM_WIDTH), lambda b: (b, 0)),
        out_shape=jax.ShapeDtypeStruct((m, FX_WIDTH + MEM_WIDTH), BF16),
        scratch_shapes=[
            pltpu.VMEM((seq_len // tk, FX_HEADS, VROWS, tk), BF16),
            pltpu.VMEM((MEM_HEADS, VROWS, n_mem), BF16),
            pltpu.VMEM((2, 2, FX_HEADS, tk, tq), F32),
            pltpu.VMEM((2, 2, FX_HEADS, tk, tq), BF16),
            pltpu.VMEM((2, FX_HEADS, VROWS, tq), F32),
        ],
        compiler_params=pltpu.CompilerParams(
            dimension_semantics=("arbitrary",), vmem_limit_bytes=VMEM_LIMIT),
        name="attn_prompt",
    )(qa, act, act, act, ka, vt, mk, mvt)


def _stack_heads(x, scale):
    lane = lax.broadcasted_iota(jnp.int32, (1, x.shape[1]), 1)
    return jnp.concatenate([jnp.where(lane // FX_DIM == h, x * scale, 0.0) for h in range(FX_HEADS)],
                           axis=0).astype(BF16)


def _unstack_heads(o, ts):
    lane = lax.broadcasted_iota(jnp.int32, (1, o.shape[1]), 1)
    out = jnp.zeros((ts, o.shape[1]), F32)
    for h in range(FX_HEADS):
        out = jnp.where(lane // FX_DIM == h, o[h * ts:(h + 1) * ts, :], out)
    return out


def _per_head_rows(vals, ts):
    return jnp.concatenate([jnp.broadcast_to(v, (ts, v.shape[1])) for v in vals], axis=0)


def _attn_sample_kernel(fq_ref, fgate_ref, mq_ref, mgate_ref, kn_ref, vn_ref, dn_ref, dt_ref,
                        pkt_ref, pvt_ref, plf_ref, mkt_ref, mvt_ref, o_ref, s_ref, *, chunk):
    ts = fq_ref.shape[0]
    n_new = kn_ref.shape[0]
    past = pkt_ref.shape[3]
    nchunks = past // chunk
    b = pl.program_id(0)
    hrows = FX_HEADS * ts

    r = lax.broadcasted_iota(jnp.int32, (chunk, chunk), 0)
    c = lax.broadcasted_iota(jnp.int32, (chunk, chunk), 1)
    upper = jnp.where(r <= c, 1.0, 0.0).astype(BF16)
    chunks = [slice(ci * chunk, (ci + 1) * chunk) for ci in range(nchunks)]
    pieces = jnp.concatenate([p.astype(BF16) for p in _split3(plf_ref[0, 0])], axis=0)
    local = _mm(pieces, upper)
    nrows = nchunks * SUBLANES
    local = local[0:nrows] + local[nrows:2 * nrows] + local[2 * nrows:]
    d_past, total = [], jnp.zeros((SUBLANES, 1), F32)
    for ci in range(nchunks):
        d = local[ci * SUBLANES:(ci + 1) * SUBLANES]
        d_past.append(d + total)
        total = total + d[:, chunk - 1:chunk]

    dn = dn_ref[...]
    dt_new = dt_ref[...]
    tot = [total[h:h + 1, :] for h in range(FX_HEADS)]
    dq = jnp.concatenate([tot[h] + dn[:, h:h + 1] for h in range(FX_HEADS)], axis=0)
    q4 = _stack_heads(fq_ref[...], FX_DIM ** -0.5)

    for ci, cs in enumerate(chunks):
        dk = _per_head_rows([d_past[ci][h:h + 1, :] for h in range(FX_HEADS)], ts)
        s_ref[:, cs] = _mm(q4, pkt_ref[0, 0, :, cs].astype(BF16)) + dq - dk
    row = lax.broadcasted_iota(jnp.int32, (hrows, n_new), 0)
    col = lax.broadcasted_iota(jnp.int32, (hrows, n_new), 1)
    visible = (col // ts == b) & (col % ts <= row % ts)
    dk = _per_head_rows([tot[h] + dt_new[h:h + 1, :] for h in range(FX_HEADS)], ts)
    s_new = jnp.where(visible, _nt(q4, kn_ref[...].astype(BF16)) + dq - dk, -jnp.inf)

    m = jnp.max(s_new, axis=-1, keepdims=True)
    for cs in chunks:
        m = jnp.maximum(m, jnp.max(s_ref[:, cs], axis=-1, keepdims=True))
    p_new = jnp.exp(s_new - m)
    l = jnp.sum(p_new, axis=-1, keepdims=True)
    acc = _mm(p_new.astype(BF16), vn_ref[...].astype(BF16))
    for cs in chunks:
        p = jnp.exp(s_ref[:, cs] - m)
        l = l + jnp.sum(p, axis=-1, keepdims=True)
        acc = acc + _nt(p.astype(BF16), pvt_ref[0, 0, :, cs].astype(BF16))
    o_fx = _unstack_heads(acc / l, ts) * fgate_ref[...]
    o_ref[:, 0:FX_WIDTH] = o_fx.astype(o_ref.dtype)

    mq4 = _stack_heads(mq_ref[...], MEM_DIM ** -0.5)
    s = _mm(mq4, mkt_ref[0, 0].astype(BF16))
    p = jnp.exp(s - jnp.max(s, axis=-1, keepdims=True))
    o = _nt(p.astype(BF16), mvt_ref[0, 0].astype(BF16)) / jnp.sum(p, axis=-1, keepdims=True)
    o_mem = _unstack_heads(o, ts) * mgate_ref[...]
    o_ref[:, FX_WIDTH:] = o_mem.astype(o_ref.dtype)


def _attn_sample(act, k_new, v_new, d_nat, d_t, past_kt, past_vt, past_lft, mem_kt, mem_vt,
                 *, layer, batch, ts):
    m = act.shape[0]
    past = past_kt.shape[3]
    n_mem = mem_kt.shape[3]
    chunk = past_lft.shape[3]
    ablk = lambda c0: pl.BlockSpec((ts, FX_WIDTH), lambda b, c0=c0: (b, c0 // FX_WIDTH))
    cache = lambda nfeat, n: pl.BlockSpec((1, 1, nfeat, n), lambda b: (layer, b, 0, 0))
    return pl.pallas_call(
        functools.partial(_attn_sample_kernel, chunk=chunk),
        grid=(batch,),
        in_specs=[
            ablk(ACT_SAMPLE["FQ"]), ablk(ACT_SAMPLE["FGATE"]), ablk(ACT_SAMPLE["MQ"]),
            ablk(ACT_SAMPLE["MGATE"]),
            pl.BlockSpec((m, FX_WIDTH), lambda b: (0, 0)),
            pl.BlockSpec((m, FX_WIDTH), lambda b: (0, 0)),
            pl.BlockSpec((ts, LANES), lambda b: (b, 0)),
            pl.BlockSpec((SUBLANES, m), lambda b: (0, 0)),
            cache(FX_WIDTH, past), cache(FX_WIDTH, past), cache(past // chunk * SUBLANES, chunk),
            cache(MEM_WIDTH, n_mem), cache(MEM_WIDTH, n_mem),
        ],
        out_specs=pl.BlockSpec((ts, FX_WIDTH + MEM_WIDTH), lambda b: (b, 0)),
        out_shape=jax.ShapeDtypeStruct((m, FX_WIDTH + MEM_WIDTH), BF16),
        scratch_shapes=[pltpu.VMEM((FX_HEADS * ts, past), F32)],
        compiler_params=pltpu.CompilerParams(
            dimension_semantics=("arbitrary",), vmem_limit_bytes=VMEM_LIMIT),
        name="attn_sample",
    )(act, act, act, act, k_new, v_new, d_nat, d_t, past_kt, past_vt, past_lft, mem_kt, mem_vt)


def _memkv_kernel(x_ref, w_ref, kb_ref, kt_ref, vt_ref):
    batch, _, n_mem = kt_ref.shape[1:]
    kv = _mm(x_ref[...].astype(BF16), w_ref[0])
    kb_ref[0] = kv[:, :MEM_WIDTH].astype(BF16)
    for b in range(batch):
        rows = slice(b * n_mem, (b + 1) * n_mem)
        kt_ref[0, b] = kv[rows, :MEM_WIDTH].T
        vt_ref[0, b] = kv[rows, MEM_WIDTH:].T


def _memkv(mem, w, *, batch):
    m, d = mem.shape
    depth = w.shape[0]
    n_mem = m // batch
    return pl.pallas_call(
        _memkv_kernel,
        grid=(depth,),
        in_specs=[pl.BlockSpec((m, d), lambda l: (0, 0)),
                  pl.BlockSpec((1,) + w.shape[1:], lambda l: (l, 0, 0))],
        out_specs=[pl.BlockSpec((1, m, MEM_WIDTH), lambda l: (l, 0, 0)),
                   pl.BlockSpec((1, batch, MEM_WIDTH, n_mem), lambda l: (l, 0, 0, 0)),
                   pl.BlockSpec((1, batch, MEM_WIDTH, n_mem), lambda l: (l, 0, 0, 0))],
        out_shape=[jax.ShapeDtypeStruct((depth, m, MEM_WIDTH), BF16),
                   jax.ShapeDtypeStruct((depth, batch, MEM_WIDTH, n_mem), F32),
                   jax.ShapeDtypeStruct((depth, batch, MEM_WIDTH, n_mem), F32)],
        compiler_params=pltpu.CompilerParams(
            dimension_semantics=("arbitrary",), vmem_limit_bytes=VMEM_LIMIT),
        name="memkv",
    )(mem, w)


def _out_kernel(x_ref, hg_ref, at_ref, w_ref, g_ref, b_ref, y_ref, *, alpha):
    tm = x_ref.shape[0]
    rows_per = min(tm, OUT_ROWS)
    for r0 in range(0, tm, rows_per):
        rs = slice(r0, r0 + rows_per)
        acc = _mm(hg_ref[rs, :], w_ref[0:HG_WIDTH, :]) + _mm(at_ref[rs, :], w_ref[HG_WIDTH:, :])
        z = alpha * x_ref[rs, :] + acc
        mu = jnp.mean(z, axis=-1, keepdims=True)
        zc = z - mu
        var = jnp.mean(zc * zc, axis=-1, keepdims=True)
        y_ref[rs, :] = zc * lax.rsqrt(var + LN_EPS) * g_ref[...] + b_ref[...]


def _out(x, mix_hg, mix_at, w, g, b, *, alpha, tm):
    m, d = x.shape
    return pl.pallas_call(
        functools.partial(_out_kernel, alpha=alpha),
        grid=(m // tm,),
        in_specs=[
            pl.BlockSpec((tm, d), lambda i: (i, 0)),
            pl.BlockSpec((tm, HG_WIDTH), lambda i: (i, 0)),
            pl.BlockSpec((tm, FX_WIDTH + MEM_WIDTH), lambda i: (i, 0)),
            pl.BlockSpec(w.shape, lambda i: (0, 0)),
            pl.BlockSpec((1, d), lambda i: (0, 0)),
            pl.BlockSpec((1, d), lambda i: (0, 0)),
        ],
        out_specs=pl.BlockSpec((tm, d), lambda i: (i, 0)),
        out_shape=jax.ShapeDtypeStruct((m, d), F32),
        compiler_params=pltpu.CompilerParams(
            dimension_semantics=("arbitrary",), vmem_limit_bytes=VMEM_LIMIT),
        name="out",
    )(x, mix_hg, mix_at, w, g, b)


def _pick_tile(n, target):
    t = min(n, target)
    while n % t:
        t //= 2
    return t


def _feature_major(x):
    lead = x.shape[:-3]
    n, heads, dim = x.shape[-3:]
    nl = len(lead)
    perm = tuple(range(nl)) + (nl + 1, nl + 2, nl)
    return jnp.transpose(x, perm).reshape(lead + (heads * dim, n))


def _token_major(x, heads):
    lead = x.shape[:-2]
    width, n = x.shape[-2:]
    nl = len(lead)
    x = x.reshape(lead + (heads, width // heads, n))
    return jnp.transpose(x, tuple(range(nl)) + (nl + 2, nl, nl + 1))


def kernel(x_prompt, x_sample, mem_prompt, state_hgrn, cache_fox_k, cache_fox_v, cache_fox_logf,
           cache_mem_k, cache_mem_v, w_in, b_fox_forget, hgrn_lower_bounds, hgrn_norm_g,
           w_mem_kv, w_out, ln_g, ln_b):
    depth, d_model, _ = w_in.shape
    bp, seq, _ = x_prompt.shape
    bs, dseq, _ = x_sample.shape
    n_mem = mem_prompt.shape[1]
    alpha = (2 * depth) ** 0.25

    fl0 = 4 * HG_WIDTH + 4 * FX_WIDTH
    w_main = jnp.concatenate([w_in[:, :, :fl0], w_in[:, :, fl0 + FX_HEADS:]], axis=-1).astype(BF16)
    w_flc = w_in[:, :, fl0:fl0 + FX_HEADS]
    w_fl = jnp.pad(w_flc, ((0, 0), (0, 0), (0, LANES - FX_HEADS))).astype(BF16)
    w_flt = jnp.pad(jnp.swapaxes(w_flc, 1, 2), ((0, 0), (0, SUBLANES - FX_HEADS), (0, 0))).astype(BF16)
    b_fl = jnp.pad(b_fox_forget, ((0, 0), (0, LANES - FX_HEADS)))[:, None, :]
    b_flt = jnp.pad(b_fox_forget, ((0, 0), (0, SUBLANES - FX_HEADS)))[:, :, None]
    lbs = hgrn_lower_bounds.astype(F32)
    w_memb = w_mem_kv.astype(BF16)
    w_outb = w_out.astype(BF16)

    past_kt = _feature_major(cache_fox_k)
    past_vt = _feature_major(cache_fox_v)
    past = cache_fox_logf.shape[2]
    chunk = min(past, SAMPLE_CHUNK)
    past_lft = jnp.swapaxes(cache_fox_logf, 2, 3).reshape(depth, bs, FX_HEADS, past // chunk, chunk)
    past_lft = jnp.pad(jnp.swapaxes(past_lft, 2, 3),
                       ((0, 0), (0, 0), (0, 0), (0, SUBLANES - FX_HEADS), (0, 0)))
    past_lft = past_lft.reshape(depth, bs, past // chunk * SUBLANES, chunk)
    mem_kt = _feature_major(cache_mem_k)
    mem_vt = _feature_major(cache_mem_v)

    mp, ms = bp * seq, bs * dseq
    tb = _pick_tile(seq, 256)
    yp = x_prompt.reshape(mp, d_model)
    ys = x_sample.reshape(ms, d_model)
    mem2 = mem_prompt.reshape(bp * n_mem, d_model)

    p_hg = []
    s_hg, s_k, s_v, s_lft = [], [], [], []
    mkb, mkt, mvt = _memkv(mem2, w_memb, batch=bp)
    fox_leaves = ()
    for l in range(depth):
        wl = (w_main[l], w_fl[l], w_flt[l], b_fl[l], b_flt[l], lbs)
        ng = hgrn_norm_g[l][None, :]
        g_ln, b_ln = ln_g[l][None, :], ln_b[l][None, :]

        act, qa, ka, *fox_leaves = _proj(yp, *wl, fox_leaves, layer=l, batch=bp, seq_len=seq,
                                         tm=_pick_tile(seq, 512), prompt=True)
        mix_hg, hg = _hgrn(act, ng, None, layer=l, batch=bp, seq_len=seq, tt=_pick_tile(seq, 2048),
                           nheads=HG_HEADS)
        mix_at = _attn_prompt(act, qa, ka, fox_leaves[1], mkb, mvt, layer=l, batch=bp, seq_len=seq, tq=tb)
        yp = _out(yp, mix_hg, mix_at, w_outb[l], g_ln, b_ln, alpha=alpha, tm=_pick_tile(mp, 2048))
        p_hg.append(hg)

        act, k, v, lft, d_nat, d_t = _proj(ys, *wl, layer=l, batch=bs, seq_len=dseq, tm=ms, prompt=False)
        mix_hg, hg = _hgrn(act, ng, state_hgrn, layer=l, batch=bs, seq_len=dseq, tt=dseq, nheads=HG_HEADS)
        mix_at = _attn_sample(act, k, v, d_nat, d_t, past_kt, past_vt, past_lft, mem_kt, mem_vt,
                              layer=l, batch=bs, ts=dseq)
        ys = _out(ys, mix_hg, mix_at, w_outb[l], g_ln, b_ln, alpha=alpha, tm=ms)
        s_hg.append(hg); s_k.append(k); s_v.append(v); s_lft.append(lft)

    st = jnp.stack
    s_lf = st(s_lft)[:, :FX_HEADS, :].reshape(depth, FX_HEADS, bs, dseq)
    return (yp.reshape(bp, seq, d_model), ys.reshape(bs, dseq, d_model),
            st(p_hg),
            _token_major(fox_leaves[0], FX_HEADS),
            _token_major(fox_leaves[1], FX_HEADS),
            jnp.swapaxes(fox_leaves[2], 2, 3),
            _token_major(mkt, MEM_HEADS),
            _token_major(mvt, MEM_HEADS),
            st(s_hg),
            st(s_k).reshape(depth, bs, dseq, FX_HEADS, FX_DIM),
            st(s_v).reshape(depth, bs, dseq, FX_HEADS, FX_DIM),
            jnp.transpose(s_lf, (0, 2, 3, 1)))
```

```python
import functools

import jax
import jax.numpy as jnp
from jax import lax
from jax.experimental import pallas as pl
from jax.experimental.pallas import tpu as pltpu

F32 = jnp.float32
BF16 = jnp.bfloat16

HG_HEADS = 4
HG_DIM = 128
HG_WIDTH = HG_HEADS * HG_DIM
FX_HEADS = 4
FX_DIM = 64
FX_WIDTH = FX_HEADS * FX_DIM
MEM_HEADS = 4
MEM_DIM = 64
MEM_WIDTH = MEM_HEADS * MEM_DIM
HG_GROUP = 128
LN_EPS = 1e-5
RMS_EPS = 1e-6
LANES = 128
SUBLANES = 8
VMEM_LIMIT = 56 * 1024 * 1024
AUG = 3
LOG2E = 1.4426950408889634
SAMPLE_CHUNK = 512
OUT_ROWS = 256

C_HQ, C_HK, C_HV, C_HG, C_HGATE = 0, 512, 1024, 1536, 2048
ACT_PROMPT = dict(FGATE=2560, MQ=2816, MGATE=3072, WIDTH=3328)
ACT_SAMPLE = dict(FQ=2560, FGATE=2816, MQ=3072, MGATE=3328, WIDTH=3584)
W_HQ, W_HF, W_HI, W_HGATE, W_FQ, W_FK, W_FV, W_FGATE, W_MQ, W_MGATE = (
    0, 512, 1024, 1536, 2048, 2304, 2560, 2816, 3072, 3328)


def _nt(a, b, precision=None):
    return lax.dot_general(a, b, (((1,), (1,)), ((), ())), precision=precision,
                           preferred_element_type=F32)


def _mm(a, b, precision=None):
    return jnp.dot(a, b, precision=precision, preferred_element_type=F32)


def _silu(x):
    return x * jax.nn.sigmoid(x)


def _log_sigmoid(x):
    return jnp.minimum(x, 0.0) - jnp.log(1.0 + jnp.exp(-jnp.abs(x)))


def _head_mask(hh):
    lane = lax.broadcasted_iota(jnp.int32, (1, LANES), 1)
    return (lane // FX_DIM) == hh


def _pair_lanes(h):
    return slice((h // 2) * LANES, (h // 2 + 1) * LANES)


def _split3(x):
    hi = x.astype(BF16).astype(F32)
    r = x - hi
    mid = r.astype(BF16).astype(F32)
    return hi, mid, r - mid


def _aug_operand(x_pair, d, hh, is_query):
    lane = lax.broadcasted_iota(jnp.int32, (1, LANES), 1)
    a = lane - (1 - hh) * FX_DIM
    hi, mid, lo = _split3(d if is_query else -d)
    p0, o0 = (0, AUG) if is_query else (AUG, 0)
    extra = jnp.where(a == p0, hi, jnp.where(a == p0 + 1, mid, jnp.where(a == p0 + 2, lo,
                      jnp.where((a >= o0) & (a < o0 + AUG), 1.0, 0.0))))
    return jnp.where(_head_mask(hh), x_pair, extra).astype(BF16)


def _proj_kernel(x_ref, w_ref, wfl_ref, wflt_ref, bfl_ref, bflt_ref, lbs_ref, *refs,
                 layer, seg, tiles_per_seq, prompt):
    nprev = 3 if prompt and layer > 0 else 0
    prev, refs = refs[:nprev], refs[nprev:]
    act_ref = refs[0]
    cols = ACT_PROMPT if prompt else ACT_SAMPLE
    tm = x_ref.shape[0]
    xb = x_ref[...].astype(BF16)

    def proj(c0, width):
        return _mm(xb, w_ref[:, c0:c0 + width])

    lbs = lbs_ref[...]
    e = jnp.exp(lbs - jnp.max(lbs, axis=0, keepdims=True))
    sm = e / jnp.sum(e, axis=0, keepdims=True)
    lb = jnp.sum(sm[:layer + 1], axis=0, keepdims=True) - sm[0:1]

    lf_nat = _log_sigmoid(_mm(xb, wfl_ref[...]) + bfl_ref[...])
    lf_t = _log_sigmoid(_nt(wflt_ref[...], xb) + bflt_ref[...])

    r = lax.broadcasted_iota(jnp.int32, (tm, tm), 0)
    c = lax.broadcasted_iota(jnp.int32, (tm, tm), 1)
    same = (r // seg) == (c // seg)
    lower = jnp.where((c <= r) & same, 1.0, 0.0).astype(BF16)
    d_nat = _cumsum_rows([lf_nat], lower)[0]

    fq = proj(W_FQ, FX_WIDTH)
    k = proj(W_FK, FX_WIDTH)
    v = proj(W_FV, FX_WIDTH)
    if prompt:
        _, qa_ref, ka_ref, kt_ref, vt_ref, lft_ref, cn_ref, kv_ref = refs
        if tiles_per_seq > 1:
            @pl.when(pl.program_id(0) % tiles_per_seq == 0)
            def _():
                cn_ref[...] = jnp.zeros_like(cn_ref)
            d_nat = d_nat + cn_ref[0:1, :]
            cn_ref[...] = jnp.broadcast_to(d_nat[tm - 1:tm, :], cn_ref.shape)
        for src, dst in zip(prev, (kt_ref, vt_ref, lft_ref)):
            dst[0:layer] = src[...]
        lft_ref[layer, 0] = lf_t[:FX_HEADS]
        kv_ref[:, 0:FX_WIDTH] = k
        kv_ref[:, FX_WIDTH:] = v
        kt_ref[layer, 0] = kv_ref[:, 0:FX_WIDTH].T
        vt_ref[layer, 0] = kv_ref[:, FX_WIDTH:].T
        d2 = d_nat * LOG2E
        for h in range(FX_HEADS):
            hl = slice(h * LANES, (h + 1) * LANES)
            qa_ref[:, hl] = _aug_operand(fq[:, _pair_lanes(h)] * (FX_DIM ** -0.5 * LOG2E),
                                         d2[:, h:h + 1], h % 2, True)
            ka_ref[:, hl] = _aug_operand(k[:, _pair_lanes(h)], d2[:, h:h + 1], h % 2, False)
    else:
        _, k_ref, v_ref, lft_ref, dn_ref, dt_ref = refs
        upper = jnp.where((r <= c) & same, 1.0, 0.0).astype(F32)
        dt_ref[...] = _mm(lf_t, upper, precision=lax.Precision.HIGHEST)
        lft_ref[...] = lf_t
        k_ref[...] = k
        v_ref[...] = v
        dn_ref[...] = d_nat
        act_ref[:, cols["FQ"]:cols["FQ"] + FX_WIDTH] = fq

    act_ref[:, C_HQ:C_HQ + HG_WIDTH] = _silu(proj(W_HQ, HG_WIDTH))
    forget = lb + (1.0 - lb) * jax.nn.sigmoid(proj(W_HF, HG_WIDTH))
    act_ref[:, C_HK:C_HK + HG_WIDTH] = 1.0 - forget
    act_ref[:, C_HG:C_HG + HG_WIDTH] = jnp.log2(forget)
    act_ref[:, C_HV:C_HV + HG_WIDTH] = proj(W_HI, HG_WIDTH)
    act_ref[:, C_HGATE:C_HGATE + HG_WIDTH] = _silu(proj(W_HGATE, HG_WIDTH))
    act_ref[:, cols["FGATE"]:cols["FGATE"] + FX_WIDTH] = _silu(proj(W_FGATE, FX_WIDTH))
    act_ref[:, cols["MQ"]:cols["MQ"] + MEM_WIDTH] = proj(W_MQ, MEM_WIDTH)
    act_ref[:, cols["MGATE"]:cols["MGATE"] + MEM_WIDTH] = _silu(proj(W_MGATE, MEM_WIDTH))


def _proj(x, w_main, w_fl, w_flt, b_fl, b_flt, lbs, prev=(), *, layer, batch, seq_len, tm, prompt):
    m, d = x.shape
    ntiles = m // tm
    seg = min(seq_len, tm)
    tps = max(seq_len // tm, 1)
    assert prompt or ntiles == 1
    assert len(prev) == (3 if prompt and layer > 0 else 0)
    kern = functools.partial(_proj_kernel, layer=layer, seg=seg, tiles_per_seq=tps, prompt=prompt)
    full = lambda shape: pl.BlockSpec(shape, lambda i: (0,) * len(shape))
    rows = lambda width: pl.BlockSpec((tm, width), lambda i: (i, 0))
    feat = lambda nl, nfeat: pl.BlockSpec((nl, 1, nfeat, tm), lambda i: (0, i // tps, 0, i % tps))
    prev_specs = []
    if prompt:
        nfeats = (FX_WIDTH, FX_WIDTH, FX_HEADS)
        prev_specs = [feat(layer, nf) for nf in nfeats] if layer > 0 else []
        out_specs = [rows(ACT_PROMPT["WIDTH"]), rows(FX_HEADS * LANES), rows(FX_HEADS * LANES)]
        out_specs += [feat(layer + 1, nf) for nf in nfeats]
        out_shape = [jax.ShapeDtypeStruct((m, ACT_PROMPT["WIDTH"]), F32),
                     jax.ShapeDtypeStruct((m, FX_HEADS * LANES), BF16),
                     jax.ShapeDtypeStruct((m, FX_HEADS * LANES), BF16)]
        out_shape += [jax.ShapeDtypeStruct((layer + 1, batch, nf, seq_len), F32) for nf in nfeats]
        scratch = [pltpu.VMEM((SUBLANES, LANES), F32), pltpu.VMEM((tm, 2 * FX_WIDTH), F32)]
    else:
        out_specs = [rows(ACT_SAMPLE["WIDTH"]), rows(FX_WIDTH), rows(FX_WIDTH), full((SUBLANES, m)),
                     rows(LANES), full((SUBLANES, m))]
        out_shape = [jax.ShapeDtypeStruct((m, ACT_SAMPLE["WIDTH"]), F32),
                     jax.ShapeDtypeStruct((m, FX_WIDTH), F32),
                     jax.ShapeDtypeStruct((m, FX_WIDTH), F32),
                     jax.ShapeDtypeStruct((SUBLANES, m), F32),
                     jax.ShapeDtypeStruct((m, LANES), F32),
                     jax.ShapeDtypeStruct((SUBLANES, m), F32)]
        scratch = []
    return pl.pallas_call(
        kern,
        grid=(ntiles,),
        in_specs=[
            pl.BlockSpec((tm, d), lambda i: (i, 0)),
            full(w_main.shape), full(w_fl.shape), full(w_flt.shape),
            full(b_fl.shape), full(b_flt.shape), full(lbs.shape),
        ] + prev_specs,
        out_specs=out_specs,
        out_shape=out_shape,
        scratch_shapes=scratch,
        compiler_params=pltpu.CompilerParams(
            dimension_semantics=("arbitrary",), vmem_limit_bytes=VMEM_LIMIT),
        name="proj_prompt" if prompt else "proj_sample",
    )(x, w_main, w_fl, w_flt, b_fl, b_flt, lbs, *prev)


def _level_exponent(g, g2, m, rows):
    n, w = g.shape
    pos = rows % (2 * m)
    if m == 1:
        return jnp.where(pos == 1, g, 0.0)
    if m == 2:
        nxt, prv = pltpu.roll(g, n - 1, axis=0), pltpu.roll(g, 1, axis=0)
        return jnp.where(pos == 0, nxt, jnp.where(pos == 1, 0.0, jnp.where(pos == 2, g, g + prv)))
    g3 = g2.reshape(n // (2 * m), 2 * m, w)
    ref = jnp.broadcast_to(g3[:, m - 1:m, :], g3.shape).reshape(n, w)
    return jnp.where(pos < m, ref - g2, g2 - ref)


def _pair_levels(n):
    rr = lax.broadcasted_iota(jnp.int32, (n, n), 0)
    cc = lax.broadcasted_iota(jnp.int32, (n, n), 1)
    xf = (rr ^ cc).astype(F32)
    level = (lax.bitcast_convert_type(xf, jnp.int32) >> 23) - 127
    return jnp.where(rr > cc, level, -1)


def _cumsum_rows(gs, lower):
    w = gs[0].shape[1]
    pieces = [p.astype(BF16) for g in gs for p in _split3(g)]
    s = _mm(lower, jnp.concatenate(pieces, axis=1))
    return [s[:, (3 * i) * w:(3 * i + 1) * w] + s[:, (3 * i + 1) * w:(3 * i + 2) * w]
            + s[:, (3 * i + 2) * w:(3 * i + 3) * w] for i in range(len(gs))]


def _hgrn_intra(q, k, g, g2, levels):
    n, w = q.shape
    rows = lax.broadcasted_iota(jnp.int32, (n, w), 0)
    nslab = n // SUBLANES
    slab = lambda x, r: x[r * SUBLANES:(r + 1) * SUBLANES]
    a = [jnp.zeros((SUBLANES, n), F32)] * nslab
    m, bit = n // 2, n.bit_length() - 2
    while m >= 1:
        if m >= SUBLANES:
            qs, ks, targets = [], [], []
            for r0 in range(0, n, 2 * m):
                first, second = slice(r0, r0 + m), slice(r0 + m, r0 + 2 * m)
                ref = g2[r0 + m - 1:r0 + m]
                ks += [k[first] * jnp.exp2(ref - g2[first]), jnp.zeros((m, w), F32)]
                qs.append(q[second] * jnp.exp2(g2[second] - ref))
                targets += range((r0 + m) // SUBLANES, (r0 + 2 * m) // SUBLANES)
            a_m = _nt(jnp.concatenate(qs, axis=0).astype(BF16), jnp.concatenate(ks, axis=0).astype(BF16))
            for i, r in enumerate(targets):
                a[r] = jnp.where(slab(levels, r) == bit, slab(a_m, i), a[r])
        else:
            f = jnp.exp2(_level_exponent(g, g2, m, rows))
            a_m = _nt((q * f).astype(BF16), (k * f).astype(BF16))
            a = [jnp.where(slab(levels, r) == bit, slab(a_m, r), a[r]) for r in range(nslab)]
        m, bit = m // 2, bit - 1
    return jnp.concatenate(a, axis=0)


def _hgrn_kernel(*refs, group, has_init):
    if has_init:
        q_ref, k_ref, v_ref, g_ref, gate_ref, ng_ref, s0_ref, o_ref, so_ref = refs[:9]
    else:
        q_ref, k_ref, v_ref, g_ref, gate_ref, ng_ref, o_ref, so_ref = refs[:8]
    st_ref, a_ref, qe_ref, ke_ref, vb_ref, dg_ref, dec_ref = refs[-7:]
    t = pl.program_id(2)
    nheads = st_ref.shape[0]
    ngroups = q_ref.shape[0] // group
    lanes = [slice(hh * HG_DIM, (hh + 1) * HG_DIM) for hh in range(nheads)]

    @pl.when(t == 0)
    def _():
        for hh in range(nheads):
            st_ref[hh] = s0_ref[0, 0, hh].T if has_init else jnp.zeros((HG_DIM, HG_DIM), F32)

    ng = ng_ref[...]
    levels = _pair_levels(group)
    rr = lax.broadcasted_iota(jnp.int32, (group, group), 0)
    cc = lax.broadcasted_iota(jnp.int32, (group, group), 1)
    lower = jnp.where(rr >= cc, 1.0, 0.0).astype(BF16)

    def prepare(i, s):
        sl = pl.ds(pl.multiple_of(i * group, group), group)
        gs = [g_ref[sl, ls] for ls in lanes]
        g2s = _cumsum_rows(gs, lower)
        for hh, ls in enumerate(lanes):
            q, k, v, g2 = q_ref[sl, ls], k_ref[sl, ls], v_ref[sl, ls], g2s[hh]
            g_last = g2[group - 1:group, :]
            a_ref[s, hh] = _hgrn_intra(q, k, gs[hh], g2, levels).astype(BF16)
            qe_ref[s, hh] = (q * jnp.exp2(g2)).astype(BF16)
            ke_ref[s, hh] = (k * jnp.exp2(g_last - g2)).astype(BF16)
            vb_ref[s, hh] = v.astype(BF16)
            dg_ref[s, hh] = jnp.sum(q * k, axis=-1, keepdims=True) * v
            dec_ref[s, hh] = jnp.broadcast_to(jnp.exp2(g_last), (SUBLANES, HG_DIM))

    def finish(i, s):
        sl = pl.ds(pl.multiple_of(i * group, group), group)
        for hh, ls in enumerate(lanes):
            st = st_ref[hh]
            vb = vb_ref[s, hh]
            o = _nt(qe_ref[s, hh], st.astype(BF16)) + _mm(a_ref[s, hh], vb) + dg_ref[s, hh]
            upd = lax.dot_general(vb, ke_ref[s, hh], (((0,), (0,)), ((), ())), preferred_element_type=F32)
            st_ref[hh] = st * dec_ref[s, hh, 0:1, :] + upd
            o = o * lax.rsqrt(jnp.mean(o * o, axis=-1, keepdims=True) + RMS_EPS) * ng
            o_ref[sl, ls] = (o * gate_ref[sl, ls]).astype(o_ref.dtype)

    prepare(0, 0)
    if ngroups > 1:
        def step(ip, c):
            finish(2 * ip, 0)
            prepare(2 * ip + 1, 1)
            finish(2 * ip + 1, 1)
            prepare(2 * ip + 2, 0)
            return c

        lax.fori_loop(0, ngroups // 2 - 1, step, 0)
        finish(ngroups - 2, 0)
        prepare(ngroups - 1, 1)
        finish(ngroups - 1, 1)
    else:
        finish(0, 0)

    @pl.when(t == pl.num_programs(2) - 1)
    def _():
        for hh in range(nheads):
            so_ref[0, hh] = st_ref[hh].T


def _hgrn(act, norm_g, s0, *, layer, batch, seq_len, tt, nheads):
    m = act.shape[0]
    nt = seq_len // tt
    has_init = s0 is not None
    group = min(tt, HG_GROUP)
    assert tt == group or (tt // group) % 2 == 0, "groups are processed in pairs"
    width = nheads * HG_DIM
    blk = lambda cb: pl.BlockSpec((tt, width), lambda b, h, t, cb=cb: (b * nt + t, cb // width + h))
    in_specs = [blk(C_HQ), blk(C_HK), blk(C_HV), blk(C_HG), blk(C_HGATE),
                pl.BlockSpec((1, HG_DIM), lambda b, h, t: (0, 0))]
    args = [act, act, act, act, act, norm_g]
    if has_init:
        in_specs.append(pl.BlockSpec((1, 1, nheads, HG_DIM, HG_DIM), lambda b, h, t: (layer, b, h, 0, 0)))
        args.append(s0)
    return pl.pallas_call(
        functools.partial(_hgrn_kernel, group=group, has_init=has_init),
        grid=(batch, HG_HEADS // nheads, nt),
        in_specs=in_specs,
        out_specs=[
            pl.BlockSpec((tt, width), lambda b, h, t: (b * nt + t, h)),
            pl.BlockSpec((1, nheads, HG_DIM, HG_DIM), lambda b, h, t: (b, h, 0, 0)),
        ],
        out_shape=[
            jax.ShapeDtypeStruct((m, HG_WIDTH), BF16),
            jax.ShapeDtypeStruct((batch, HG_HEADS, HG_DIM, HG_DIM), F32),
        ],
        scratch_shapes=[
            pltpu.VMEM((nheads, HG_DIM, HG_DIM), F32),
            pltpu.VMEM((2, nheads, group, group), BF16),
            pltpu.VMEM((2, nheads, group, HG_DIM), BF16),
            pltpu.VMEM((2, nheads, group, HG_DIM), BF16),
            pltpu.VMEM((2, nheads, group, HG_DIM), BF16),
            pltpu.VMEM((2, nheads, group, HG_DIM), F32),
            pltpu.VMEM((2, nheads, SUBLANES, HG_DIM), F32),
        ],
        compiler_params=pltpu.CompilerParams(
            dimension_semantics=("arbitrary", "arbitrary", "arbitrary"), vmem_limit_bytes=VMEM_LIMIT),
        name="hgrn",
    )(*args)


VROWS = FX_DIM + 16


def _softmax_t(st, m):
    m_new = jnp.maximum(m, jnp.max(st, axis=0, keepdims=True))
    return m_new, jnp.exp2(m - m_new), jnp.exp2(st - m_new).astype(BF16)


def _with_ones(vt):
    return jnp.concatenate([vt.astype(BF16), jnp.ones((VROWS - FX_DIM, vt.shape[1]), BF16)], axis=0)


def _normalised(acc):
    return acc[0:FX_DIM] / acc[FX_DIM:FX_DIM + 1]


def _attn_prompt_kernel(qa_ref, fgate_ref, mq_ref, mgate_ref, ka_ref, vt_ref, mk_ref, mvt_ref,
                        o_ref, vtb_ref, mvtb_ref, st_ref, p_ref, acc_ref, *, tq):
    seq = qa_ref.shape[0]
    tk = tq
    ntiles = seq // tq
    heads = range(FX_HEADS)

    for j in range(seq // tk):
        for h in heads:
            vtb_ref[j, h] = _with_ones(vt_ref[0, 0, h * FX_DIM:(h + 1) * FX_DIM, j * tk:(j + 1) * tk])
    for h in range(MEM_HEADS):
        mvtb_ref[h] = _with_ones(mvt_ref[0, 0, h * MEM_DIM:(h + 1) * MEM_DIM, :])

    key = lax.broadcasted_iota(jnp.int32, (tk, tq), 0)
    qry = lax.broadcasted_iota(jnp.int32, (tk, tq), 1)

    def tile_rows(i):
        return pl.ds(pl.multiple_of(i * tq, tq), tq)

    def scores(i, j, h):
        hl = slice(h * LANES, (h + 1) * LANES)
        return _nt(ka_ref[tile_rows(j), hl], qa_ref[tile_rows(i), hl])

    def start(i, s):
        for h in heads:
            st_ref[s, 0, h] = scores(i, 0, h)
            p_ref[s, 1, h] = jnp.zeros((tk, tq), BF16)
            acc_ref[s, h] = jnp.zeros((VROWS, tq), F32)

    def block(i, s, j, b, carry):
        prev = jnp.maximum(j - 1, 0)
        out = []
        for h in heads:
            m, alpha_prev = carry[h]
            pv = _mm(vtb_ref[prev, h], p_ref[s, 1 - b, h])
            st = st_ref[s, b, h]
            st_ref[s, 1 - b, h] = scores(i, j + 1, h)
            m, alpha, p = _softmax_t(st, m)
            acc_ref[s, h] = alpha_prev * acc_ref[s, h] + pv
            p_ref[s, b, h] = p
            out.append((m, alpha))
        return tuple(out)

    def finish(i, s, carry):
        rows = tile_rows(i)
        prev = jnp.maximum(i - 1, 0)
        o_heads = []
        for h in heads:
            m, alpha_prev = carry[h]
            acc = alpha_prev * acc_ref[s, h] + _mm(vtb_ref[prev, h], p_ref[s, 1 - s, h])
            m, alpha, p = _softmax_t(jnp.where(key <= qry, st_ref[s, s, h], -jnp.inf), m)
            acc = alpha * acc + _mm(vtb_ref[i, h], p)
            o_heads.append(_normalised(acc))
        for pr in range(FX_HEADS // 2):
            ls = _pair_lanes(2 * pr)
            o_t = jnp.concatenate(o_heads[2 * pr:2 * pr + 2], axis=0)
            o_ref[rows, ls] = (o_t.T * fgate_ref[rows, ls]).astype(o_ref.dtype)

    def mem_scores(i, s):
        rows = tile_rows(i)
        for h in range(MEM_HEADS):
            mq_pair = mq_ref[rows, _pair_lanes(h)] * (MEM_DIM ** -0.5 * LOG2E)
            qm = jnp.where(_head_mask(h % 2), mq_pair, 0.0).astype(BF16)
            st_ref[s, 0, h] = _nt(mk_ref[0, :, _pair_lanes(h)], qm)

    def mem_finish(i, s):
        rows = tile_rows(i)
        for pr in range(MEM_HEADS // 2):
            ls = _pair_lanes(2 * pr)
            halves = []
            for hh in range(2):
                st = st_ref[s, 0, 2 * pr + hh]
                pm = jnp.exp2(st - jnp.max(st, axis=0, keepdims=True))
                halves.append(_normalised(_mm(mvtb_ref[2 * pr + hh], pm.astype(BF16))))
            o_t = jnp.concatenate(halves, axis=0)
            lm = slice(FX_WIDTH + pr * LANES, FX_WIDTH + (pr + 1) * LANES)
            o_ref[rows, lm] = (o_t.T * mgate_ref[rows, ls]).astype(o_ref.dtype)

    mem_scores(0, 0)

    def mem_pair(ip, c):
        mem_finish(2 * ip, 0)
        mem_scores(2 * ip + 1, 1)
        mem_finish(2 * ip + 1, 1)
        mem_scores(jnp.minimum(2 * ip + 2, ntiles - 1), 0)
        return c

    lax.fori_loop(0, ntiles // 2, mem_pair, 0)

    def qtile(ip, s):
        i = 2 * ip + s

        def two_blocks(jp, carry):
            return block(i, s, 2 * jp + 1, 1, block(i, s, 2 * jp, 0, carry))

        init = tuple((jnp.full((1, tq), -jnp.inf, F32), jnp.ones((1, tq), F32)) for _ in heads)
        carry = lax.fori_loop(0, ip, two_blocks, init)
        if s == 1:
            carry = block(i, s, i - 1, 0, carry)
        finish(i, s, carry)
        start(jnp.minimum(i + 1, ntiles - 1), 1 - s)

    start(0, 0)

    def tile_pair(ip, c):
        qtile(ip, 0)
        qtile(ip, 1)
        return c

    lax.fori_loop(0, ntiles // 2, tile_pair, 0)


def _attn_prompt(act, qa, ka, vt, mk, mvt, *, layer, batch, seq_len, tq):
    m = act.shape[0]
    n_mem = mk.shape[1] // batch
    tk = tq
    assert (seq_len // tq) % 2 == 0, "query tiles are processed in pairs"
    ablk = lambda c0: pl.BlockSpec((seq_len, FX_WIDTH), lambda b, c0=c0: (b, c0 // FX_WIDTH))
    return pl.pallas_call(
        functools.partial(_attn_prompt_kernel, tq=tq),
        grid=(batch,),
        in_specs=[
            pl.BlockSpec((seq_len, FX_HEADS * LANES), lambda b: (b, 0)),
            ablk(ACT_PROMPT["FGATE"]), ablk(ACT_PROMPT["MQ"]), ablk(ACT_PROMPT["MGATE"]),
            pl.BlockSpec((seq_len, FX_HEADS * LANES), lambda b: (b, 0)),
            pl.BlockSpec((1, 1, FX_WIDTH, seq_len), lambda b: (layer, b, 0, 0)),
            pl.BlockSpec((1, n_mem, MEM_WIDTH), lambda b: (layer, b, 0)),
            pl.BlockSpec((1, 1, MEM_WIDTH, n_mem), lambda b: (layer, b, 0, 0)),
        ],
        out_specs=pl.BlockSpec((seq_len, FX_WIDTH + MEM_WIDTH), lambda b: (b, 0)),
        out_shape=jax.ShapeDtypeStruct((m, FX_WIDTH + MEM_WIDTH), BF16),
        scratch_shapes=[
            pltpu.VMEM((seq_len // tk, FX_HEADS, VROWS, tk), BF16),
            pltpu.VMEM((MEM_HEADS, VROWS, n_mem), BF16),
            pltpu.VMEM((2, 2, FX_HEADS, tk, tq), F32),
            pltpu.VMEM((2, 2, FX_HEADS, tk, tq), BF16),
            pltpu.VMEM((2, FX_HEADS, VROWS, tq), F32),
        ],
        compiler_params=pltpu.CompilerParams(
            dimension_semantics=("arbitrary",), vmem_limit_bytes=VMEM_LIMIT),
        name="attn_prompt",
    )(qa, act, act, act, ka, vt, mk, mvt)


def _stack_heads(x, scale):
    lane = lax.broadcasted_iota(jnp.int32, (1, x.shape[1]), 1)
    return jnp.concatenate([jnp.where(lane // FX_DIM == h, x * scale, 0.0) for h in range(FX_HEADS)],
                           axis=0).astype(BF16)


def _unstack_heads(o, ts):
    lane = lax.broadcasted_iota(jnp.int32, (1, o.shape[1]), 1)
    out = jnp.zeros((ts, o.shape[1]), F32)
    for h in range(FX_HEADS):
        out = jnp.where(lane // FX_DIM == h, o[h * ts:(h + 1) * ts, :], out)
    return out


def _per_head_rows(vals, ts):
    return jnp.concatenate([jnp.broadcast_to(v, (ts, v.shape[1])) for v in vals], axis=0)


def _attn_sample_kernel(fq_ref, fgate_ref, mq_ref, mgate_ref, kn_ref, vn_ref, dn_ref, dt_ref,
                        pkt_ref, pvt_ref, plf_ref, mkt_ref, mvt_ref, o_ref, s_ref, *, chunk):
    ts = fq_ref.shape[0]
    n_new = kn_ref.shape[0]
    past = pkt_ref.shape[3]
    nchunks = past // chunk
    b = pl.program_id(0)
    hrows = FX_HEADS * ts

    r = lax.broadcasted_iota(jnp.int32, (chunk, chunk), 0)
    c = lax.broadcasted_iota(jnp.int32, (chunk, chunk), 1)
    upper = jnp.where(r <= c, 1.0, 0.0).astype(BF16)
    chunks = [slice(ci * chunk, (ci + 1) * chunk) for ci in range(nchunks)]
    pieces = jnp.concatenate([p.astype(BF16) for p in _split3(plf_ref[0, 0])], axis=0)
    local = _mm(pieces, upper)
    nrows = nchunks * SUBLANES
    local = local[0:nrows] + local[nrows:2 * nrows] + local[2 * nrows:]
    d_past, total = [], jnp.zeros((SUBLANES, 1), F32)
    for ci in range(nchunks):
        d = local[ci * SUBLANES:(ci + 1) * SUBLANES]
        d_past.append(d + total)
        total = total + d[:, chunk - 1:chunk]

    dn = dn_ref[...]
    dt_new = dt_ref[...]
    tot = [total[h:h + 1, :] for h in range(FX_HEADS)]
    dq = jnp.concatenate([tot[h] + dn[:, h:h + 1] for h in range(FX_HEADS)], axis=0)
    q4 = _stack_heads(fq_ref[...], FX_DIM ** -0.5)

    for ci, cs in enumerate(chunks):
        dk = _per_head_rows([d_past[ci][h:h + 1, :] for h in range(FX_HEADS)], ts)
        s_ref[:, cs] = _mm(q4, pkt_ref[0, 0, :, cs].astype(BF16)) + dq - dk
    row = lax.broadcasted_iota(jnp.int32, (hrows, n_new), 0)
    col = lax.broadcasted_iota(jnp.int32, (hrows, n_new), 1)
    visible = (col // ts == b) & (col % ts <= row % ts)
    dk = _per_head_rows([tot[h] + dt_new[h:h + 1, :] for h in range(FX_HEADS)], ts)
    s_new = jnp.where(visible, _nt(q4, kn_ref[...].astype(BF16)) + dq - dk, -jnp.inf)

    m = jnp.max(s_new, axis=-1, keepdims=True)
    for cs in chunks:
        m = jnp.maximum(m, jnp.max(s_ref[:, cs], axis=-1, keepdims=True))
    p_new = jnp.exp(s_new - m)
    l = jnp.sum(p_new, axis=-1, keepdims=True)
    acc = _mm(p_new.astype(BF16), vn_ref[...].astype(BF16))
    for cs in chunks:
        p = jnp.exp(s_ref[:, cs] - m)
        l = l + jnp.sum(p, axis=-1, keepdims=True)
        acc = acc + _nt(p.astype(BF16), pvt_ref[0, 0, :, cs].astype(BF16))
    o_fx = _unstack_heads(acc / l, ts) * fgate_ref[...]
    o_ref[:, 0:FX_WIDTH] = o_fx.astype(o_ref.dtype)

    mq4 = _stack_heads(mq_ref[...], MEM_DIM ** -0.5)
    s = _mm(mq4, mkt_ref[0, 0].astype(BF16))
    p = jnp.exp(s - jnp.max(s, axis=-1, keepdims=True))
    o = _nt(p.astype(BF16), mvt_ref[0, 0].astype(BF16)) / jnp.sum(p, axis=-1, keepdims=True)
    o_mem = _unstack_heads(o, ts) * mgate_ref[...]
    o_ref[:, FX_WIDTH:] = o_mem.astype(o_ref.dtype)


def _attn_sample(act, k_new, v_new, d_nat, d_t, past_kt, past_vt, past_lft, mem_kt, mem_vt,
                 *, layer, batch, ts):
    m = act.shape[0]
    past = past_kt.shape[3]
    n_mem = mem_kt.shape[3]
    chunk = past_lft.shape[3]
    ablk = lambda c0: pl.BlockSpec((ts, FX_WIDTH), lambda b, c0=c0: (b, c0 // FX_WIDTH))
    cache = lambda nfeat, n: pl.BlockSpec((1, 1, nfeat, n), lambda b: (layer, b, 0, 0))
    return pl.pallas_call(
        functools.partial(_attn_sample_kernel, chunk=chunk),
        grid=(batch,),
        in_specs=[
            ablk(ACT_SAMPLE["FQ"]), ablk(ACT_SAMPLE["FGATE"]), ablk(ACT_SAMPLE["MQ"]),
            ablk(ACT_SAMPLE["MGATE"]),
            pl.BlockSpec((m, FX_WIDTH), lambda b: (0, 0)),
            pl.BlockSpec((m, FX_WIDTH), lambda b: (0, 0)),
            pl.BlockSpec((ts, LANES), lambda b: (b, 0)),
            pl.BlockSpec((SUBLANES, m), lambda b: (0, 0)),
            cache(FX_WIDTH, past), cache(FX_WIDTH, past), cache(past // chunk * SUBLANES, chunk),
            cache(MEM_WIDTH, n_mem), cache(MEM_WIDTH, n_mem),
        ],
        out_specs=pl.BlockSpec((ts, FX_WIDTH + MEM_WIDTH), lambda b: (b, 0)),
        out_shape=jax.ShapeDtypeStruct((m, FX_WIDTH + MEM_WIDTH), BF16),
        scratch_shapes=[pltpu.VMEM((FX_HEADS * ts, past), F32)],
        compiler_params=pltpu.CompilerParams(
            dimension_semantics=("arbitrary",), vmem_limit_bytes=VMEM_LIMIT),
        name="attn_sample",
    )(act, act, act, act, k_new, v_new, d_nat, d_t, past_kt, past_vt, past_lft, mem_kt, mem_vt)


def _memkv_kernel(x_ref, w_ref, kb_ref, kt_ref, vt_ref):
    batch, _, n_mem = kt_ref.shape[1:]
    kv = _mm(x_ref[...].astype(BF16), w_ref[0])
    kb_ref[0] = kv[:, :MEM_WIDTH].astype(BF16)
    for b in range(batch):
        rows = slice(b * n_mem, (b + 1) * n_mem)
        kt_ref[0, b] = kv[rows, :MEM_WIDTH].T
        vt_ref[0, b] = kv[rows, MEM_WIDTH:].T


def _memkv(mem, w, *, batch):
    m, d = mem.shape
    depth = w.shape[0]
    n_mem = m // batch
    return pl.pallas_call(
        _memkv_kernel,
        grid=(depth,),
        in_specs=[pl.BlockSpec((m, d), lambda l: (0, 0)),
                  pl.BlockSpec((1,) + w.shape[1:], lambda l: (l, 0, 0))],
        out_specs=[pl.BlockSpec((1, m, MEM_WIDTH), lambda l: (l, 0, 0)),
                   pl.BlockSpec((1, batch, MEM_WIDTH, n_mem), lambda l: (l, 0, 0, 0)),
                   pl.BlockSpec((1, batch, MEM_WIDTH, n_mem), lambda l: (l, 0, 0, 0))],
        out_shape=[jax.ShapeDtypeStruct((depth, m, MEM_WIDTH), BF16),
                   jax.ShapeDtypeStruct((depth, batch, MEM_WIDTH, n_mem), F32),
                   jax.ShapeDtypeStruct((depth, batch, MEM_WIDTH, n_mem), F32)],
        compiler_params=pltpu.CompilerParams(
            dimension_semantics=("arbitrary",), vmem_limit_bytes=VMEM_LIMIT),
        name="memkv",
    )(mem, w)


def _out_kernel(x_ref, hg_ref, at_ref, w_ref, g_ref, b_ref, y_ref, *, alpha):
    tm = x_ref.shape[0]
    rows_per = min(tm, OUT_ROWS)
    for r0 in range(0, tm, rows_per):
        rs = slice(r0, r0 + rows_per)
        acc = _mm(hg_ref[rs, :], w_ref[0:HG_WIDTH, :]) + _mm(at_ref[rs, :], w_ref[HG_WIDTH:, :])
        z = alpha * x_ref[rs, :] + acc
        mu = jnp.mean(z, axis=-1, keepdims=True)
        zc = z - mu
        var = jnp.mean(zc * zc, axis=-1, keepdims=True)
        y_ref[rs, :] = zc * lax.rsqrt(var + LN_EPS) * g_ref[...] + b_ref[...]


def _out(x, mix_hg, mix_at, w, g, b, *, alpha, tm):
    m, d = x.shape
    return pl.pallas_call(
        functools.partial(_out_kernel, alpha=alpha),
        grid=(m // tm,),
        in_specs=[
            pl.BlockSpec((tm, d), lambda i: (i, 0)),
            pl.BlockSpec((tm, HG_WIDTH), lambda i: (i, 0)),
            pl.BlockSpec((tm, FX_WIDTH + MEM_WIDTH), lambda i: (i, 0)),
            pl.BlockSpec(w.shape, lambda i: (0, 0)),
            pl.BlockSpec((1, d), lambda i: (0, 0)),
            pl.BlockSpec((1, d), lambda i: (0, 0)),
        ],
        out_specs=pl.BlockSpec((tm, d), lambda i: (i, 0)),
        out_shape=jax.ShapeDtypeStruct((m, d), F32),
        compiler_params=pltpu.CompilerParams(
            dimension_semantics=("arbitrary",), vmem_limit_bytes=VMEM_LIMIT),
        name="out",
    )(x, mix_hg, mix_at, w, g, b)


def _pick_tile(n, target):
    t = min(n, target)
    while n % t:
        t //= 2
    return t


def _feature_major(x):
    lead = x.shape[:-3]
    n, heads, dim = x.shape[-3:]
    nl = len(lead)
    perm = tuple(range(nl)) + (nl + 1, nl + 2, nl)
    return jnp.transpose(x, perm).reshape(lead + (heads * dim, n))


def _token_major(x, heads):
    lead = x.shape[:-2]
    width, n = x.shape[-2:]
    nl = len(lead)
    x = x.reshape(lead + (heads, width // heads, n))
    return jnp.transpose(x, tuple(range(nl)) + (nl + 2, nl, nl + 1))


def kernel(x_prompt, x_sample, mem_prompt, state_hgrn, cache_fox_k, cache_fox_v, cache_fox_logf,
           cache_mem_k, cache_mem_v, w_in, b_fox_forget, hgrn_lower_bounds, hgrn_norm_g,
           w_mem_kv, w_out, ln_g, ln_b):
    depth, d_model, _ = w_in.shape
    bp, seq, _ = x_prompt.shape
    bs, dseq, _ = x_sample.shape
    n_mem = mem_prompt.shape[1]
    alpha = (2 * depth) ** 0.25

    fl0 = 4 * HG_WIDTH + 4 * FX_WIDTH
    w_main = jnp.concatenate([w_in[:, :, :fl0], w_in[:, :, fl0 + FX_HEADS:]], axis=-1).astype(BF16)
    w_flc = w_in[:, :, fl0:fl0 + FX_HEADS]
    w_fl = jnp.pad(w_flc, ((0, 0), (0, 0), (0, LANES - FX_HEADS))).astype(BF16)
    w_flt = jnp.pad(jnp.swapaxes(w_flc, 1, 2), ((0, 0), (0, SUBLANES - FX_HEADS), (0, 0))).astype(BF16)
    b_fl = jnp.pad(b_fox_forget, ((0, 0), (0, LANES - FX_HEADS)))[:, None, :]
    b_flt = jnp.pad(b_fox_forget, ((0, 0), (0, SUBLANES - FX_HEADS)))[:, :, None]
    lbs = hgrn_lower_bounds.astype(F32)
    w_memb = w_mem_kv.astype(BF16)
    w_outb = w_out.astype(BF16)

    past_kt = _feature_major(cache_fox_k)
    past_vt = _feature_major(cache_fox_v)
    past = cache_fox_logf.shape[2]
    chunk = min(past, SAMPLE_CHUNK)
    past_lft = jnp.swapaxes(cache_fox_logf, 2, 3).reshape(depth, bs, FX_HEADS, past // chunk, chunk)
    past_lft = jnp.pad(jnp.swapaxes(past_lft, 2, 3),
                       ((0, 0), (0, 0), (0, 0), (0, SUBLANES - FX_HEADS), (0, 0)))
    past_lft = past_lft.reshape(depth, bs, past // chunk * SUBLANES, chunk)
    mem_kt = _feature_major(cache_mem_k)
    mem_vt = _feature_major(cache_mem_v)

    mp, ms = bp * seq, bs * dseq
    tb = _pick_tile(seq, 256)
    yp = x_prompt.reshape(mp, d_model)
    ys = x_sample.reshape(ms, d_model)
    mem2 = mem_prompt.reshape(bp * n_mem, d_model)

    p_hg = []
    s_hg, s_k, s_v, s_lft = [], [], [], []
    mkb, mkt, mvt = _memkv(mem2, w_memb, batch=bp)
    fox_leaves = ()
    for l in range(depth):
        wl = (w_main[l], w_fl[l], w_flt[l], b_fl[l], b_flt[l], lbs)
        ng = hgrn_norm_g[l][None, :]
        g_ln, b_ln = ln_g[l][None, :], ln_b[l][None, :]

        act, qa, ka, *fox_leaves = _proj(yp, *wl, fox_leaves, layer=l, batch=bp, seq_len=seq,
                                         tm=_pick_tile(seq, 512), prompt=True)
        mix_hg, hg = _hgrn(act, ng, None, layer=l, batch=bp, seq_len=seq, tt=_pick_tile(seq, 2048),
                           nheads=HG_HEADS)
        mix_at = _attn_prompt(act, qa, ka, fox_leaves[1], mkb, mvt, layer=l, batch=bp, seq_len=seq, tq=tb)
        yp = _out(yp, mix_hg, mix_at, w_outb[l], g_ln, b_ln, alpha=alpha, tm=_pick_tile(mp, 2048))
        p_hg.append(hg)

        act, k, v, lft, d_nat, d_t = _proj(ys, *wl, layer=l, batch=bs, seq_len=dseq, tm=ms, prompt=False)
        mix_hg, hg = _hgrn(act, ng, state_hgrn, layer=l, batch=bs, seq_len=dseq, tt=dseq, nheads=HG_HEADS)
        mix_at = _attn_sample(act, k, v, d_nat, d_t, past_kt, past_vt, past_lft, mem_kt, mem_vt,
                              layer=l, batch=bs, ts=dseq)
        ys = _out(ys, mix_hg, mix_at, w_outb[l], g_ln, b_ln, alpha=alpha, tm=ms)
        s_hg.append(hg); s_k.append(k); s_v.append(v); s_lft.append(lft)

    st = jnp.stack
    s_lf = st(s_lft)[:, :FX_HEADS, :].reshape(depth, FX_HEADS, bs, dseq)
    return (yp.reshape(bp, seq, d_model), ys.reshape(bs, dseq, d_model),
            st(p_hg),
            _token_major(fox_leaves[0], FX_HEADS),
            _token_major(fox_leaves[1], FX_HEADS),
            jnp.swapaxes(fox_leaves[2], 2, 3),
            _token_major(mkt, MEM_HEADS),
            _token_major(mvt, MEM_HEADS),
            st(s_hg),
            st(s_k).reshape(depth, bs, dseq, FX_HEADS, FX_DIM),
            st(s_v).reshape(depth, bs, dseq, FX_HEADS, FX_DIM),
            jnp.transpose(s_lf, (0, 2, 3, 1)))
```

```python
import functools

import jax
import jax.numpy as jnp
from jax import lax
from jax.experimental import pallas as pl
from jax.experimental.pallas import tpu as pltpu

F32 = jnp.float32
BF16 = jnp.bfloat16

HG_HEADS = 4
HG_DIM = 128
HG_WIDTH = HG_HEADS * HG_DIM
FX_HEADS = 4
FX_DIM = 64
FX_WIDTH = FX_HEADS * FX_DIM
MEM_HEADS = 4
MEM_DIM = 64
MEM_WIDTH = MEM_HEADS * MEM_DIM
HG_GROUP = 128
LN_EPS = 1e-5
RMS_EPS = 1e-6
LANES = 128
SUBLANES = 8
VMEM_LIMIT = 56 * 1024 * 1024
AUG = 3
LOG2E = 1.4426950408889634
SAMPLE_CHUNK = 512
OUT_ROWS = 256

C_HQ, C_HK, C_HV, C_HG, C_HGATE = 0, 512, 1024, 1536, 2048
ACT_PROMPT = dict(FGATE=2560, MQ=2816, MGATE=3072, WIDTH=3328)
ACT_SAMPLE = dict(FQ=2560, FGATE=2816, MQ=3072, MGATE=3328, WIDTH=3584)
W_HQ, W_HF, W_HI, W_HGATE, W_FQ, W_FK, W_FV, W_FGATE, W_MQ, W_MGATE = (
    0, 512, 1024, 1536, 2048, 2304, 2560, 2816, 3072, 3328)


def _nt(a, b, precision=None):
    return lax.dot_general(a, b, (((1,), (1,)), ((), ())), precision=precision,
                           preferred_element_type=F32)


def _mm(a, b, precision=None):
    return jnp.dot(a, b, precision=precision, preferred_element_type=F32)


def _silu(x):
    return x * jax.nn.sigmoid(x)


def _log_sigmoid(x):
    return jnp.minimum(x, 0.0) - jnp.log(1.0 + jnp.exp(-jnp.abs(x)))


def _head_mask(hh):
    lane = lax.broadcasted_iota(jnp.int32, (1, LANES), 1)
    return (lane // FX_DIM) == hh


def _pair_lanes(h):
    return slice((h // 2) * LANES, (h // 2 + 1) * LANES)


def _split3(x):
    hi = x.astype(BF16).astype(F32)
    r = x - hi
    mid = r.astype(BF16).astype(F32)
    return hi, mid, r - mid


def _aug_operand(x_pair, d, hh, is_query):
    lane = lax.broadcasted_iota(jnp.int32, (1, LANES), 1)
    a = lane - (1 - hh) * FX_DIM
    hi, mid, lo = _split3(d if is_query else -d)
    p0, o0 = (0, AUG) if is_query else (AUG, 0)
    extra = jnp.where(a == p0, hi, jnp.where(a == p0 + 1, mid, jnp.where(a == p0 + 2, lo,
                      jnp.where((a >= o0) & (a < o0 + AUG), 1.0, 0.0))))
    return jnp.where(_head_mask(hh), x_pair, extra).astype(BF16)


def _proj_kernel(x_ref, wa_ref, wb_ref, wfl_ref, wflt_ref, bfl_ref, bflt_ref, lbs_ref, *refs,
                 layer, seg, tiles_per_seq, prompt):
    nprev = 3 if prompt and layer > 0 else 0
    prev, refs = refs[:nprev], refs[nprev:]
    act_ref = refs[0]
    cols = ACT_PROMPT if prompt else ACT_SAMPLE
    tm = x_ref.shape[0]
    xb = x_ref[...].astype(BF16)

    def proj(c0, width):
        na = wa_ref.shape[2]
        if c0 < na:
            return _mm(xb, wa_ref[0, :, c0:c0 + width])
        return _mm(xb, wb_ref[0, :, c0 - na:c0 - na + width])

    lbs = lbs_ref[...]
    e = jnp.exp(lbs - jnp.max(lbs, axis=0, keepdims=True))
    sm = e / jnp.sum(e, axis=0, keepdims=True)
    lb = jnp.sum(sm[:layer + 1], axis=0, keepdims=True) - sm[0:1]

    lf_nat = _log_sigmoid(_mm(xb, wfl_ref[...]) + bfl_ref[...])
    lf_t = _log_sigmoid(_nt(wflt_ref[...], xb) + bflt_ref[...])

    r = lax.broadcasted_iota(jnp.int32, (tm, tm), 0)
    c = lax.broadcasted_iota(jnp.int32, (tm, tm), 1)
    same = (r // seg) == (c // seg)
    lower = jnp.where((c <= r) & same, 1.0, 0.0).astype(BF16)
    d_nat = _cumsum_rows([lf_nat], lower)[0]

    fq = proj(W_FQ, FX_WIDTH)
    k = proj(W_FK, FX_WIDTH)
    v = proj(W_FV, FX_WIDTH)
    if prompt:
        _, qa_ref, ka_ref, kt_ref, vt_ref, lft_ref, cn_ref, kv_ref = refs
        if tiles_per_seq > 1:
            @pl.when(pl.program_id(0) % tiles_per_seq == 0)
            def _():
                cn_ref[...] = jnp.zeros_like(cn_ref)
            d_nat = d_nat + cn_ref[0:1, :]
            cn_ref[...] = jnp.broadcast_to(d_nat[tm - 1:tm, :], cn_ref.shape)
        for src, dst in zip(prev, (kt_ref, vt_ref, lft_ref)):
            dst[0:layer] = src[...]
        lft_ref[layer, 0] = lf_t[:FX_HEADS]
        kv_ref[:, 0:FX_WIDTH] = k
        kv_ref[:, FX_WIDTH:] = v
        kt_ref[layer, 0] = kv_ref[:, 0:FX_WIDTH].T
        vt_ref[layer, 0] = kv_ref[:, FX_WIDTH:].T
        d2 = d_nat * LOG2E
        for h in range(FX_HEADS):
            hl = slice(h * LANES, (h + 1) * LANES)
            qa_ref[:, hl] = _aug_operand(fq[:, _pair_lanes(h)] * (FX_DIM ** -0.5 * LOG2E),
                                         d2[:, h:h + 1], h % 2, True)
            ka_ref[:, hl] = _aug_operand(k[:, _pair_lanes(h)], d2[:, h:h + 1], h % 2, False)
    else:
        _, k_ref, v_ref, lft_ref, dn_ref, dt_ref = refs
        upper = jnp.where((r <= c) & same, 1.0, 0.0).astype(F32)
        dt_ref[...] = _mm(lf_t, upper, precision=lax.Precision.HIGHEST)
        lft_ref[...] = lf_t
        k_ref[...] = k
        v_ref[...] = v
        dn_ref[...] = d_nat
        act_ref[:, cols["FQ"]:cols["FQ"] + FX_WIDTH] = fq

    act_ref[:, C_HQ:C_HQ + HG_WIDTH] = _silu(proj(W_HQ, HG_WIDTH))
    forget = lb + (1.0 - lb) * jax.nn.sigmoid(proj(W_HF, HG_WIDTH))
    act_ref[:, C_HK:C_HK + HG_WIDTH] = 1.0 - forget
    act_ref[:, C_HG:C_HG + HG_WIDTH] = jnp.log2(forget)
    act_ref[:, C_HV:C_HV + HG_WIDTH] = proj(W_HI, HG_WIDTH)
    act_ref[:, C_HGATE:C_HGATE + HG_WIDTH] = _silu(proj(W_HGATE, HG_WIDTH))
    act_ref[:, cols["FGATE"]:cols["FGATE"] + FX_WIDTH] = _silu(proj(W_FGATE, FX_WIDTH))
    act_ref[:, cols["MQ"]:cols["MQ"] + MEM_WIDTH] = proj(W_MQ, MEM_WIDTH)
    act_ref[:, cols["MGATE"]:cols["MGATE"] + MEM_WIDTH] = _silu(proj(W_MGATE, MEM_WIDTH))


def _proj(x, w_a, w_b, w_fl, w_flt, b_fl, b_flt, lbs, prev=(), *, layer, batch, seq_len, tm, prompt):
    m, d = x.shape
    ntiles = m // tm
    seg = min(seq_len, tm)
    tps = max(seq_len // tm, 1)
    assert prompt or ntiles == 1
    assert len(prev) == (3 if prompt and layer > 0 else 0)
    kern = functools.partial(_proj_kernel, layer=layer, seg=seg, tiles_per_seq=tps, prompt=prompt)
    full = lambda shape: pl.BlockSpec(shape, lambda i: (0,) * len(shape))
    rows = lambda width: pl.BlockSpec((tm, width), lambda i: (i, 0))
    feat = lambda nl, nfeat: pl.BlockSpec((nl, 1, nfeat, tm), lambda i: (0, i // tps, 0, i % tps))
    prev_specs = []
    if prompt:
        nfeats = (FX_WIDTH, FX_WIDTH, FX_HEADS)
        prev_specs = [feat(layer, nf) for nf in nfeats] if layer > 0 else []
        out_specs = [rows(ACT_PROMPT["WIDTH"]), rows(FX_HEADS * LANES), rows(FX_HEADS * LANES)]
        out_specs += [feat(layer + 1, nf) for nf in nfeats]
        out_shape = [jax.ShapeDtypeStruct((m, ACT_PROMPT["WIDTH"]), F32),
                     jax.ShapeDtypeStruct((m, FX_HEADS * LANES), BF16),
                     jax.ShapeDtypeStruct((m, FX_HEADS * LANES), BF16)]
        out_shape += [jax.ShapeDtypeStruct((layer + 1, batch, nf, seq_len), F32) for nf in nfeats]
        scratch = [pltpu.VMEM((SUBLANES, LANES), F32), pltpu.VMEM((tm, 2 * FX_WIDTH), F32)]
    else:
        out_specs = [rows(ACT_SAMPLE["WIDTH"]), rows(FX_WIDTH), rows(FX_WIDTH), full((SUBLANES, m)),
                     rows(LANES), full((SUBLANES, m))]
        out_shape = [jax.ShapeDtypeStruct((m, ACT_SAMPLE["WIDTH"]), F32),
                     jax.ShapeDtypeStruct((m, FX_WIDTH), F32),
                     jax.ShapeDtypeStruct((m, FX_WIDTH), F32),
                     jax.ShapeDtypeStruct((SUBLANES, m), F32),
                     jax.ShapeDtypeStruct((m, LANES), F32),
                     jax.ShapeDtypeStruct((SUBLANES, m), F32)]
        scratch = []
    return pl.pallas_call(
        kern,
        grid=(ntiles,),
        in_specs=[
            pl.BlockSpec((tm, d), lambda i: (i, 0)),
            pl.BlockSpec((1,) + w_a.shape[1:], lambda i: (layer, 0, 0)),
            pl.BlockSpec((1,) + w_b.shape[1:], lambda i: (layer, 0, 0)),
            full(w_fl.shape), full(w_flt.shape),
            full(b_fl.shape), full(b_flt.shape), full(lbs.shape),
        ] + prev_specs,
        out_specs=out_specs,
        out_shape=out_shape,
        scratch_shapes=scratch,
        compiler_params=pltpu.CompilerParams(
            dimension_semantics=("arbitrary",), vmem_limit_bytes=VMEM_LIMIT),
        name="proj_prompt" if prompt else "proj_sample",
    )(x, w_a, w_b, w_fl, w_flt, b_fl, b_flt, lbs, *prev)


def _level_exponent(g, g2, m, rows):
    n, w = g.shape
    pos = rows % (2 * m)
    if m == 1:
        return jnp.where(pos == 1, g, 0.0)
    if m == 2:
        nxt, prv = pltpu.roll(g, n - 1, axis=0), pltpu.roll(g, 1, axis=0)
        return jnp.where(pos == 0, nxt, jnp.where(pos == 1, 0.0, jnp.where(pos == 2, g, g + prv)))
    g3 = g2.reshape(n // (2 * m), 2 * m, w)
    ref = jnp.broadcast_to(g3[:, m - 1:m, :], g3.shape).reshape(n, w)
    return jnp.where(pos < m, ref - g2, g2 - ref)


def _pair_levels(n):
    rr = lax.broadcasted_iota(jnp.int32, (n, n), 0)
    cc = lax.broadcasted_iota(jnp.int32, (n, n), 1)
    xf = (rr ^ cc).astype(F32)
    level = (lax.bitcast_convert_type(xf, jnp.int32) >> 23) - 127
    return jnp.where(rr > cc, level, -1)


def _cumsum_rows(gs, lower):
    w = gs[0].shape[1]
    pieces = [p.astype(BF16) for g in gs for p in _split3(g)]
    s = _mm(lower, jnp.concatenate(pieces, axis=1))
    return [s[:, (3 * i) * w:(3 * i + 1) * w] + s[:, (3 * i + 1) * w:(3 * i + 2) * w]
            + s[:, (3 * i + 2) * w:(3 * i + 3) * w] for i in range(len(gs))]


def _hgrn_intra(q, k, g, g2, levels):
    n, w = q.shape
    rows = lax.broadcasted_iota(jnp.int32, (n, w), 0)
    nslab = n // SUBLANES
    slab = lambda x, r: x[r * SUBLANES:(r + 1) * SUBLANES]
    a = [jnp.zeros((SUBLANES, n), F32)] * nslab
    m, bit = n // 2, n.bit_length() - 2
    while m >= 1:
        if m >= SUBLANES:
            qs, ks, targets = [], [], []
            for r0 in range(0, n, 2 * m):
                first, second = slice(r0, r0 + m), slice(r0 + m, r0 + 2 * m)
                ref = g2[r0 + m - 1:r0 + m]
                ks += [k[first] * jnp.exp2(ref - g2[first]), jnp.zeros((m, w), F32)]
                qs.append(q[second] * jnp.exp2(g2[second] - ref))
                targets += range((r0 + m) // SUBLANES, (r0 + 2 * m) // SUBLANES)
            a_m = _nt(jnp.concatenate(qs, axis=0).astype(BF16), jnp.concatenate(ks, axis=0).astype(BF16))
            for i, r in enumerate(targets):
                a[r] = jnp.where(slab(levels, r) == bit, slab(a_m, i), a[r])
        else:
            f = jnp.exp2(_level_exponent(g, g2, m, rows))
            a_m = _nt((q * f).astype(BF16), (k * f).astype(BF16))
            a = [jnp.where(slab(levels, r) == bit, slab(a_m, r), a[r]) for r in range(nslab)]
        m, bit = m // 2, bit - 1
    return jnp.concatenate(a, axis=0)


def _hgrn_kernel(*refs, group, has_init):
    if has_init:
        q_ref, k_ref, v_ref, g_ref, gate_ref, ng_ref, s0_ref, o_ref, so_ref = refs[:9]
    else:
        q_ref, k_ref, v_ref, g_ref, gate_ref, ng_ref, o_ref, so_ref = refs[:8]
    st_ref, a_ref, qe_ref, ke_ref, vb_ref, dg_ref, dec_ref = refs[-7:]
    t = pl.program_id(2)
    nheads = st_ref.shape[0]
    ngroups = q_ref.shape[0] // group
    lanes = [slice(hh * HG_DIM, (hh + 1) * HG_DIM) for hh in range(nheads)]

    @pl.when(t == 0)
    def _():
        for hh in range(nheads):
            st_ref[hh] = s0_ref[0, 0, hh].T if has_init else jnp.zeros((HG_DIM, HG_DIM), F32)

    ng = ng_ref[...]
    levels = _pair_levels(group)
    rr = lax.broadcasted_iota(jnp.int32, (group, group), 0)
    cc = lax.broadcasted_iota(jnp.int32, (group, group), 1)
    lower = jnp.where(rr >= cc, 1.0, 0.0).astype(BF16)

    def prepare(i, s):
        sl = pl.ds(pl.multiple_of(i * group, group), group)
        gs = [g_ref[sl, ls] for ls in lanes]
        g2s = _cumsum_rows(gs, lower)
        for hh, ls in enumerate(lanes):
            q, k, v, g2 = q_ref[sl, ls], k_ref[sl, ls], v_ref[sl, ls], g2s[hh]
            g_last = g2[group - 1:group, :]
            a_ref[s, hh] = _hgrn_intra(q, k, gs[hh], g2, levels).astype(BF16)
            qe_ref[s, hh] = (q * jnp.exp2(g2)).astype(BF16)
            ke_ref[s, hh] = (k * jnp.exp2(g_last - g2)).astype(BF16)
            vb_ref[s, hh] = v.astype(BF16)
            dg_ref[s, hh] = jnp.sum(q * k, axis=-1, keepdims=True) * v
            dec_ref[s, hh] = jnp.broadcast_to(jnp.exp2(g_last), (SUBLANES, HG_DIM))

    def finish(i, s):
        sl = pl.ds(pl.multiple_of(i * group, group), group)
        for hh, ls in enumerate(lanes):
            st = st_ref[hh]
            vb = vb_ref[s, hh]
            o = _nt(qe_ref[s, hh], st.astype(BF16)) + _mm(a_ref[s, hh], vb) + dg_ref[s, hh]
            upd = lax.dot_general(vb, ke_ref[s, hh], (((0,), (0,)), ((), ())), preferred_element_type=F32)
            st_ref[hh] = st * dec_ref[s, hh, 0:1, :] + upd
            o = o * lax.rsqrt(jnp.mean(o * o, axis=-1, keepdims=True) + RMS_EPS) * ng
            o_ref[sl, ls] = (o * gate_ref[sl, ls]).astype(o_ref.dtype)

    prepare(0, 0)
    if ngroups > 1:
        def step(ip, c):
            finish(2 * ip, 0)
            prepare(2 * ip + 1, 1)
            finish(2 * ip + 1, 1)
            prepare(2 * ip + 2, 0)
            return c

        lax.fori_loop(0, ngroups // 2 - 1, step, 0)
        finish(ngroups - 2, 0)
        prepare(ngroups - 1, 1)
        finish(ngroups - 1, 1)
    else:
        finish(0, 0)

    @pl.when(t == pl.num_programs(2) - 1)
    def _():
        for hh in range(nheads):
            so_ref[0, hh] = st_ref[hh].T


def _hgrn(act, norm_g, s0, *, layer, batch, seq_len, tt, nheads):
    m = act.shape[0]
    nt = seq_len // tt
    has_init = s0 is not None
    group = min(tt, HG_GROUP)
    assert tt == group or (tt // group) % 2 == 0, "groups are processed in pairs"
    width = nheads * HG_DIM
    blk = lambda cb: pl.BlockSpec((tt, width), lambda b, h, t, cb=cb: (b * nt + t, cb // width + h))
    in_specs = [blk(C_HQ), blk(C_HK), blk(C_HV), blk(C_HG), blk(C_HGATE),
                pl.BlockSpec((1, HG_DIM), lambda b, h, t: (0, 0))]
    args = [act, act, act, act, act, norm_g]
    if has_init:
        in_specs.append(pl.BlockSpec((1, 1, nheads, HG_DIM, HG_DIM), lambda b, h, t: (layer, b, h, 0, 0)))
        args.append(s0)
    return pl.pallas_call(
        functools.partial(_hgrn_kernel, group=group, has_init=has_init),
        grid=(batch, HG_HEADS // nheads, nt),
        in_specs=in_specs,
        out_specs=[
            pl.BlockSpec((tt, width), lambda b, h, t: (b * nt + t, h)),
            pl.BlockSpec((1, nheads, HG_DIM, HG_DIM), lambda b, h, t: (b, h, 0, 0)),
        ],
        out_shape=[
            jax.ShapeDtypeStruct((m, HG_WIDTH), BF16),
            jax.ShapeDtypeStruct((batch, HG_HEADS, HG_DIM, HG_DIM), F32),
        ],
        scratch_shapes=[
            pltpu.VMEM((nheads, HG_DIM, HG_DIM), F32),
            pltpu.VMEM((2, nheads, group, group), BF16),
            pltpu.VMEM((2, nheads, group, HG_DIM), BF16),
            pltpu.VMEM((2, nheads, group, HG_DIM), BF16),
            pltpu.VMEM((2, nheads, group, HG_DIM), BF16),
            pltpu.VMEM((2, nheads, group, HG_DIM), F32),
            pltpu.VMEM((2, nheads, SUBLANES, HG_DIM), F32),
        ],
        compiler_params=pltpu.CompilerParams(
            dimension_semantics=("arbitrary", "arbitrary", "arbitrary"), vmem_limit_bytes=VMEM_LIMIT),
        name="hgrn",
    )(*args)


VROWS = FX_DIM + 16


def _softmax_t(st, m):
    m_new = jnp.maximum(m, jnp.max(st, axis=0, keepdims=True))
    return m_new, jnp.exp2(m - m_new), jnp.exp2(st - m_new).astype(BF16)


def _with_ones(vt):
    return jnp.concatenate([vt.astype(BF16), jnp.ones((VROWS - FX_DIM, vt.shape[1]), BF16)], axis=0)


def _normalised(acc):
    return acc[0:FX_DIM] / acc[FX_DIM:FX_DIM + 1]


def _attn_prompt_kernel(qa_ref, fgate_ref, mq_ref, mgate_ref, ka_ref, vt_ref, mk_ref, mvt_ref,
                        o_ref, vtb_ref, mvtb_ref, st_ref, p_ref, acc_ref, *, tq):
    seq = qa_ref.shape[0]
    tk = tq
    ntiles = seq // tq
    heads = range(FX_HEADS)

    for j in range(seq // tk):
        for h in heads:
            vtb_ref[j, h] = _with_ones(vt_ref[0, 0, h * FX_DIM:(h + 1) * FX_DIM, j * tk:(j + 1) * tk])
    for h in range(MEM_HEADS):
        mvtb_ref[h] = _with_ones(mvt_ref[0, 0, h * MEM_DIM:(h + 1) * MEM_DIM, :])

    key = lax.broadcasted_iota(jnp.int32, (tk, tq), 0)
    qry = lax.broadcasted_iota(jnp.int32, (tk, tq), 1)

    def tile_rows(i):
        return pl.ds(pl.multiple_of(i * tq, tq), tq)

    def scores(i, j, h):
        hl = slice(h * LANES, (h + 1) * LANES)
        return _nt(ka_ref[tile_rows(j), hl], qa_ref[tile_rows(i), hl])

    def start(i, s):
        for h in heads:
            st_ref[s, 0, h] = scores(i, 0, h)
            p_ref[s, 1, h] = jnp.zeros((tk, tq), BF16)
            acc_ref[s, h] = jnp.zeros((VROWS, tq), F32)

    def block(i, s, j, b, carry):
        prev = jnp.maximum(j - 1, 0)
        out = []
        for h in heads:
            m, alpha_prev = carry[h]
            pv = _mm(vtb_ref[prev, h], p_ref[s, 1 - b, h])
            st = st_ref[s, b, h]
            st_ref[s, 1 - b, h] = scores(i, j + 1, h)
            m, alpha, p = _softmax_t(st, m)
            acc_ref[s, h] = alpha_prev * acc_ref[s, h] + pv
            p_ref[s, b, h] = p
            out.append((m, alpha))
        return tuple(out)

    def finish(i, s, carry):
        rows = tile_rows(i)
        prev = jnp.maximum(i - 1, 0)
        o_heads = []
        for h in heads:
            m, alpha_prev = carry[h]
            acc = alpha_prev * acc_ref[s, h] + _mm(vtb_ref[prev, h], p_ref[s, 1 - s, h])
            m, alpha, p = _softmax_t(jnp.where(key <= qry, st_ref[s, s, h], -jnp.inf), m)
            acc = alpha * acc + _mm(vtb_ref[i, h], p)
            o_heads.append(_normalised(acc))
        for pr in range(FX_HEADS // 2):
            ls = _pair_lanes(2 * pr)
            o_t = jnp.concatenate(o_heads[2 * pr:2 * pr + 2], axis=0)
            o_ref[rows, ls] = (o_t.T * fgate_ref[rows, ls]).astype(o_ref.dtype)

    def mem_scores(i, s):
        rows = tile_rows(i)
        for h in range(MEM_HEADS):
            mq_pair = mq_ref[rows, _pair_lanes(h)] * (MEM_DIM ** -0.5 * LOG2E)
            qm = jnp.where(_head_mask(h % 2), mq_pair, 0.0).astype(BF16)
            st_ref[s, 0, h] = _nt(mk_ref[0, :, _pair_lanes(h)], qm)

    def mem_finish(i, s):
        rows = tile_rows(i)
        for pr in range(MEM_HEADS // 2):
            ls = _pair_lanes(2 * pr)
            halves = []
            for hh in range(2):
                st = st_ref[s, 0, 2 * pr + hh]
                pm = jnp.exp2(st - jnp.max(st, axis=0, keepdims=True))
                halves.append(_normalised(_mm(mvtb_ref[2 * pr + hh], pm.astype(BF16))))
            o_t = jnp.concatenate(halves, axis=0)
            lm = slice(FX_WIDTH + pr * LANES, FX_WIDTH + (pr + 1) * LANES)
            o_ref[rows, lm] = (o_t.T * mgate_ref[rows, ls]).astype(o_ref.dtype)

    mem_scores(0, 0)

    def mem_pair(ip, c):
        mem_finish(2 * ip, 0)
        mem_scores(2 * ip + 1, 1)
        mem_finish(2 * ip + 1, 1)
        mem_scores(jnp.minimum(2 * ip + 2, ntiles - 1), 0)
        return c

    lax.fori_loop(0, ntiles // 2, mem_pair, 0)

    def qtile(ip, s):
        i = 2 * ip + s

        def two_blocks(jp, carry):
            return block(i, s, 2 * jp + 1, 1, block(i, s, 2 * jp, 0, carry))

        init = tuple((jnp.full((1, tq), -jnp.inf, F32), jnp.ones((1, tq), F32)) for _ in heads)
        carry = lax.fori_loop(0, ip, two_blocks, init)
        if s == 1:
            carry = block(i, s, i - 1, 0, carry)
        finish(i, s, carry)
        start(jnp.minimum(i + 1, ntiles - 1), 1 - s)

    start(0, 0)

    def tile_pair(ip, c):
        qtile(ip, 0)
        qtile(ip, 1)
        return c

    lax.fori_loop(0, ntiles // 2, tile_pair, 0)


def _attn_prompt(act, qa, ka, vt, mk, mvt, *, layer, batch, seq_len, tq):
    m = act.shape[0]
    n_mem = mk.shape[1] // batch
    tk = tq
    assert (seq_len // tq) % 2 == 0, "query tiles are processed in pairs"
    ablk = lambda c0: pl.BlockSpec((seq_len, FX_WIDTH), lambda b, c0=c0: (b, c0 // FX_WIDTH))
    return pl.pallas_call(
        functools.partial(_attn_prompt_kernel, tq=tq),
        grid=(batch,),
        in_specs=[
            pl.BlockSpec((seq_len, FX_HEADS * LANES), lambda b: (b, 0)),
            ablk(ACT_PROMPT["FGATE"]), ablk(ACT_PROMPT["MQ"]), ablk(ACT_PROMPT["MGATE"]),
            pl.BlockSpec((seq_len, FX_HEADS * LANES), lambda b: (b, 0)),
            pl.BlockSpec((1, 1, FX_WIDTH, seq_len), lambda b: (layer, b, 0, 0)),
            pl.BlockSpec((1, n_mem, MEM_WIDTH), lambda b: (layer, b, 0)),
            pl.BlockSpec((1, 1, MEM_WIDTH, n_mem), lambda b: (layer, b, 0, 0)),
        ],
        out_specs=pl.BlockSpec((seq_len, FX_WIDTH + MEM_WIDTH), lambda b: (b, 0)),
        out_shape=jax.ShapeDtypeStruct((m, FX_WIDTH + MEM_WIDTH), BF16),
        scratch_shapes=[
            pltpu.VMEM((seq_len // tk, FX_HEADS, VROWS, tk), BF16),
            pltpu.VMEM((MEM_HEADS, VROWS, n_mem), BF16),
            pltpu.VMEM((2, 2, FX_HEADS, tk, tq), F32),
            pltpu.VMEM((2, 2, FX_HEADS, tk, tq), BF16),
            pltpu.VMEM((2, FX_HEADS, VROWS, tq), F32),
        ],
        compiler_params=pltpu.CompilerParams(
            dimension_semantics=("arbitrary",), vmem_limit_bytes=VMEM_LIMIT),
        name="attn_prompt",
    )(qa, act, act, act, ka, vt, mk, mvt)


def _stack_heads(x, scale):
    lane = lax.broadcasted_iota(jnp.int32, (1, x.shape[1]), 1)
    return jnp.concatenate([jnp.where(lane // FX_DIM == h, x * scale, 0.0) for h in range(FX_HEADS)],
                           axis=0).astype(BF16)


def _unstack_heads(o, ts):
    lane = lax.broadcasted_iota(jnp.int32, (1, o.shape[1]), 1)
    out = jnp.zeros((ts, o.shape[1]), F32)
    for h in range(FX_HEADS):
        out = jnp.where(lane // FX_DIM == h, o[h * ts:(h + 1) * ts, :], out)
    return out


def _per_head_rows(vals, ts):
    return jnp.concatenate([jnp.broadcast_to(v, (ts, v.shape[1])) for v in vals], axis=0)


def _attn_sample_kernel(fq_ref, fgate_ref, mq_ref, mgate_ref, kn_ref, vn_ref, dn_ref, dt_ref,
                        pkt_ref, pvt_ref, plf_ref, mkt_ref, mvt_ref, o_ref, s_ref, *, chunk):
    ts = fq_ref.shape[0]
    n_new = kn_ref.shape[0]
    past = pkt_ref.shape[3]
    nchunks = past // chunk
    b = pl.program_id(0)
    hrows = FX_HEADS * ts

    r = lax.broadcasted_iota(jnp.int32, (chunk, chunk), 0)
    c = lax.broadcasted_iota(jnp.int32, (chunk, chunk), 1)
    upper = jnp.where(r <= c, 1.0, 0.0).astype(BF16)
    chunks = [slice(ci * chunk, (ci + 1) * chunk) for ci in range(nchunks)]
    pieces = jnp.concatenate([p.astype(BF16) for p in _split3(plf_ref[0, 0])], axis=0)
    local = _mm(pieces, upper)
    nrows = nchunks * SUBLANES
    local = local[0:nrows] + local[nrows:2 * nrows] + local[2 * nrows:]
    d_past, total = [], jnp.zeros((SUBLANES, 1), F32)
    for ci in range(nchunks):
        d = local[ci * SUBLANES:(ci + 1) * SUBLANES]
        d_past.append(d + total)
        total = total + d[:, chunk - 1:chunk]

    dn = dn_ref[...]
    dt_new = dt_ref[...]
    tot = [total[h:h + 1, :] for h in range(FX_HEADS)]
    dq = jnp.concatenate([tot[h] + dn[:, h:h + 1] for h in range(FX_HEADS)], axis=0)
    q4 = _stack_heads(fq_ref[...], FX_DIM ** -0.5)

    for ci, cs in enumerate(chunks):
        dk = _per_head_rows([d_past[ci][h:h + 1, :] for h in range(FX_HEADS)], ts)
        s_ref[:, cs] = _mm(q4, pkt_ref[0, 0, :, cs].astype(BF16)) + dq - dk
    row = lax.broadcasted_iota(jnp.int32, (hrows, n_new), 0)
    col = lax.broadcasted_iota(jnp.int32, (hrows, n_new), 1)
    visible = (col // ts == b) & (col % ts <= row % ts)
    dk = _per_head_rows([tot[h] + dt_new[h:h + 1, :] for h in range(FX_HEADS)], ts)
    s_new = jnp.where(visible, _nt(q4, kn_ref[...].astype(BF16)) + dq - dk, -jnp.inf)

    m = jnp.max(s_new, axis=-1, keepdims=True)
    for cs in chunks:
        m = jnp.maximum(m, jnp.max(s_ref[:, cs], axis=-1, keepdims=True))
    p_new = jnp.exp(s_new - m)
    l = jnp.sum(p_new, axis=-1, keepdims=True)
    acc = _mm(p_new.astype(BF16), vn_ref[...].astype(BF16))
    for cs in chunks:
        p = jnp.exp(s_ref[:, cs] - m)
        l = l + jnp.sum(p, axis=-1, keepdims=True)
        acc = acc + _nt(p.astype(BF16), pvt_ref[0, 0, :, cs].astype(BF16))
    o_fx = _unstack_heads(acc / l, ts) * fgate_ref[...]
    o_ref[:, 0:FX_WIDTH] = o_fx.astype(o_ref.dtype)

    mq4 = _stack_heads(mq_ref[...], MEM_DIM ** -0.5)
    s = _mm(mq4, mkt_ref[0, 0].astype(BF16))
    p = jnp.exp(s - jnp.max(s, axis=-1, keepdims=True))
    o = _nt(p.astype(BF16), mvt_ref[0, 0].astype(BF16)) / jnp.sum(p, axis=-1, keepdims=True)
    o_mem = _unstack_heads(o, ts) * mgate_ref[...]
    o_ref[:, FX_WIDTH:] = o_mem.astype(o_ref.dtype)


def _attn_sample(act, k_new, v_new, d_nat, d_t, past_kt, past_vt, past_lft, mem_kt, mem_vt,
                 *, layer, batch, ts):
    m = act.shape[0]
    past = past_kt.shape[3]
    n_mem = mem_kt.shape[3]
    chunk = past_lft.shape[3]
    ablk = lambda c0: pl.BlockSpec((ts, FX_WIDTH), lambda b, c0=c0: (b, c0 // FX_WIDTH))
    cache = lambda nfeat, n: pl.BlockSpec((1, 1, nfeat, n), lambda b: (layer, b, 0, 0))
    return pl.pallas_call(
        functools.partial(_attn_sample_kernel, chunk=chunk),
        grid=(batch,),
        in_specs=[
            ablk(ACT_SAMPLE["FQ"]), ablk(ACT_SAMPLE["FGATE"]), ablk(ACT_SAMPLE["MQ"]),
            ablk(ACT_SAMPLE["MGATE"]),
            pl.BlockSpec((m, FX_WIDTH), lambda b: (0, 0)),
            pl.BlockSpec((m, FX_WIDTH), lambda b: (0, 0)),
            pl.BlockSpec((ts, LANES), lambda b: (b, 0)),
            pl.BlockSpec((SUBLANES, m), lambda b: (0, 0)),
            cache(FX_WIDTH, past), cache(FX_WIDTH, past), cache(past // chunk * SUBLANES, chunk),
            cache(MEM_WIDTH, n_mem), cache(MEM_WIDTH, n_mem),
        ],
        out_specs=pl.BlockSpec((ts, FX_WIDTH + MEM_WIDTH), lambda b: (b, 0)),
        out_shape=jax.ShapeDtypeStruct((m, FX_WIDTH + MEM_WIDTH), BF16),
        scratch_shapes=[pltpu.VMEM((FX_HEADS * ts, past), F32)],
        compiler_params=pltpu.CompilerParams(
            dimension_semantics=("arbitrary",), vmem_limit_bytes=VMEM_LIMIT),
        name="attn_sample",
    )(act, act, act, act, k_new, v_new, d_nat, d_t, past_kt, past_vt, past_lft, mem_kt, mem_vt)


def _memkv_kernel(x_ref, w_ref, kb_ref, kt_ref, vt_ref):
    batch, _, n_mem = kt_ref.shape[1:]
    kv = _mm(x_ref[...].astype(BF16), w_ref[0])
    kb_ref[0] = kv[:, :MEM_WIDTH].astype(BF16)
    for b in range(batch):
        rows = slice(b * n_mem, (b + 1) * n_mem)
        kt_ref[0, b] = kv[rows, :MEM_WIDTH].T
        vt_ref[0, b] = kv[rows, MEM_WIDTH:].T


def _memkv(mem, w, *, batch):
    m, d = mem.shape
    depth = w.shape[0]
    n_mem = m // batch
    return pl.pallas_call(
        _memkv_kernel,
        grid=(depth,),
        in_specs=[pl.BlockSpec((m, d), lambda l: (0, 0)),
                  pl.BlockSpec((1,) + w.shape[1:], lambda l: (l, 0, 0))],
        out_specs=[pl.BlockSpec((1, m, MEM_WIDTH), lambda l: (l, 0, 0)),
                   pl.BlockSpec((1, batch, MEM_WIDTH, n_mem), lambda l: (l, 0, 0, 0)),
                   pl.BlockSpec((1, batch, MEM_WIDTH, n_mem), lambda l: (l, 0, 0, 0))],
        out_shape=[jax.ShapeDtypeStruct((depth, m, MEM_WIDTH), BF16),
                   jax.ShapeDtypeStruct((depth, batch, MEM_WIDTH, n_mem), F32),
                   jax.ShapeDtypeStruct((depth, batch, MEM_WIDTH, n_mem), F32)],
        compiler_params=pltpu.CompilerParams(
            dimension_semantics=("arbitrary",), vmem_limit_bytes=VMEM_LIMIT),
        name="memkv",
    )(mem, w)


def _out_kernel(x_ref, hg_ref, at_ref, w_ref, g_ref, b_ref, y_ref, *, alpha):
    tm = x_ref.shape[0]
    rows_per = min(tm, OUT_ROWS)
    for r0 in range(0, tm, rows_per):
        rs = slice(r0, r0 + rows_per)
        acc = _mm(hg_ref[rs, :], w_ref[0:HG_WIDTH, :]) + _mm(at_ref[rs, :], w_ref[HG_WIDTH:, :])
        z = alpha * x_ref[rs, :] + acc
        mu = jnp.mean(z, axis=-1, keepdims=True)
        zc = z - mu
        var = jnp.mean(zc * zc, axis=-1, keepdims=True)
        y_ref[rs, :] = zc * lax.rsqrt(var + LN_EPS) * g_ref[...] + b_ref[...]


def _out(x, mix_hg, mix_at, w, g, b, *, alpha, tm):
    m, d = x.shape
    return pl.pallas_call(
        functools.partial(_out_kernel, alpha=alpha),
        grid=(m // tm,),
        in_specs=[
            pl.BlockSpec((tm, d), lambda i: (i, 0)),
            pl.BlockSpec((tm, HG_WIDTH), lambda i: (i, 0)),
            pl.BlockSpec((tm, FX_WIDTH + MEM_WIDTH), lambda i: (i, 0)),
            pl.BlockSpec(w.shape, lambda i: (0, 0)),
            pl.BlockSpec((1, d), lambda i: (0, 0)),
            pl.BlockSpec((1, d), lambda i: (0, 0)),
        ],
        out_specs=pl.BlockSpec((tm, d), lambda i: (i, 0)),
        out_shape=jax.ShapeDtypeStruct((m, d), F32),
        compiler_params=pltpu.CompilerParams(
            dimension_semantics=("arbitrary",), vmem_limit_bytes=VMEM_LIMIT),
        name="out",
    )(x, mix_hg, mix_at, w, g, b)


def _pick_tile(n, target):
    t = min(n, target)
    while n % t:
        t //= 2
    return t


def _feature_major(x):
    lead = x.shape[:-3]
    n, heads, dim = x.shape[-3:]
    nl = len(lead)
    perm = tuple(range(nl)) + (nl + 1, nl + 2, nl)
    return jnp.transpose(x, perm).reshape(lead + (heads * dim, n))


def _token_major(x, heads):
    lead = x.shape[:-2]
    width, n = x.shape[-2:]
    nl = len(lead)
    x = x.reshape(lead + (heads, width // heads, n))
    return jnp.transpose(x, tuple(range(nl)) + (nl + 2, nl, nl + 1))


def kernel(x_prompt, x_sample, mem_prompt, state_hgrn, cache_fox_k, cache_fox_v, cache_fox_logf,
           cache_mem_k, cache_mem_v, w_in, b_fox_forget, hgrn_lower_bounds, hgrn_norm_g,
           w_mem_kv, w_out, ln_g, ln_b):
    depth, d_model, _ = w_in.shape
    bp, seq, _ = x_prompt.shape
    bs, dseq, _ = x_sample.shape
    n_mem = mem_prompt.shape[1]
    alpha = (2 * depth) ** 0.25

    fl0 = 4 * HG_WIDTH + 4 * FX_WIDTH
    w_a = w_in[:, :, :fl0].astype(BF16)
    w_b = w_in[:, :, fl0 + FX_HEADS:].astype(BF16)
    w_flc = w_in[:, :, fl0:fl0 + FX_HEADS]
    w_fl = jnp.pad(w_flc, ((0, 0), (0, 0), (0, LANES - FX_HEADS))).astype(BF16)
    w_flt = jnp.pad(jnp.swapaxes(w_flc, 1, 2), ((0, 0), (0, SUBLANES - FX_HEADS), (0, 0))).astype(BF16)
    b_fl = jnp.pad(b_fox_forget, ((0, 0), (0, LANES - FX_HEADS)))[:, None, :]
    b_flt = jnp.pad(b_fox_forget, ((0, 0), (0, SUBLANES - FX_HEADS)))[:, :, None]
    lbs = hgrn_lower_bounds.astype(F32)
    w_memb = w_mem_kv.astype(BF16)
    w_outb = w_out.astype(BF16)

    past_kt = _feature_major(cache_fox_k)
    past_vt = _feature_major(cache_fox_v)
    past = cache_fox_logf.shape[2]
    chunk = min(past, SAMPLE_CHUNK)
    past_lft = jnp.swapaxes(cache_fox_logf, 2, 3).reshape(depth, bs, FX_HEADS, past // chunk, chunk)
    past_lft = jnp.pad(jnp.swapaxes(past_lft, 2, 3),
                       ((0, 0), (0, 0), (0, 0), (0, SUBLANES - FX_HEADS), (0, 0)))
    past_lft = past_lft.reshape(depth, bs, past // chunk * SUBLANES, chunk)
    mem_kt = _feature_major(cache_mem_k)
    mem_vt = _feature_major(cache_mem_v)

    mp, ms = bp * seq, bs * dseq
    tb = _pick_tile(seq, 256)
    yp = x_prompt.reshape(mp, d_model)
    ys = x_sample.reshape(ms, d_model)
    mem2 = mem_prompt.reshape(bp * n_mem, d_model)

    p_hg = []
    s_hg, s_k, s_v, s_lft = [], [], [], []
    mkb, mkt, mvt = _memkv(mem2, w_memb, batch=bp)
    fox_leaves = ()
    for l in range(depth):
        wl = (w_a, w_b, w_fl[l], w_flt[l], b_fl[l], b_flt[l], lbs)
        ng = hgrn_norm_g[l][None, :]
        g_ln, b_ln = ln_g[l][None, :], ln_b[l][None, :]

        act, qa, ka, *fox_leaves = _proj(yp, *wl, fox_leaves, layer=l, batch=bp, seq_len=seq,
                                         tm=_pick_tile(seq, 512), prompt=True)
        mix_hg, hg = _hgrn(act, ng, None, layer=l, batch=bp, seq_len=seq, tt=_pick_tile(seq, 2048),
                           nheads=HG_HEADS)
        mix_at = _attn_prompt(act, qa, ka, fox_leaves[1], mkb, mvt, layer=l, batch=bp, seq_len=seq, tq=tb)
        yp = _out(yp, mix_hg, mix_at, w_outb[l], g_ln, b_ln, alpha=alpha, tm=_pick_tile(mp, 2048))
        p_hg.append(hg)

        act, k, v, lft, d_nat, d_t = _proj(ys, *wl, layer=l, batch=bs, seq_len=dseq, tm=ms, prompt=False)
        mix_hg, hg = _hgrn(act, ng, state_hgrn, layer=l, batch=bs, seq_len=dseq, tt=dseq, nheads=HG_HEADS)
        mix_at = _attn_sample(act, k, v, d_nat, d_t, past_kt, past_vt, past_lft, mem_kt, mem_vt,
                              layer=l, batch=bs, ts=dseq)
        ys = _out(ys, mix_hg, mix_at, w_outb[l], g_ln, b_ln, alpha=alpha, tm=ms)
        s_hg.append(hg); s_k.append(k); s_v.append(v); s_lft.append(lft)

    st = jnp.stack
    s_lf = st(s_lft)[:, :FX_HEADS, :].reshape(depth, FX_HEADS, bs, dseq)
    return (yp.reshape(bp, seq, d_model), ys.reshape(bs, dseq, d_model),
            st(p_hg),
            _token_major(fox_leaves[0], FX_HEADS),
            _token_major(fox_leaves[1], FX_HEADS),
            jnp.swapaxes(fox_leaves[2], 2, 3),
            _token_major(mkt, MEM_HEADS),
            _token_major(mvt, MEM_HEADS),
            st(s_hg),
            st(s_k).reshape(depth, bs, dseq, FX_HEADS, FX_DIM),
            st(s_v).reshape(depth, bs, dseq, FX_HEADS, FX_DIM),
            jnp.transpose(s_lf, (0, 2, 3, 1)))
```

```python
import functools

import jax
import jax.numpy as jnp
from jax import lax
from jax.experimental import pallas as pl
from jax.experimental.pallas import tpu as pltpu

F32 = jnp.float32
BF16 = jnp.bfloat16

HG_HEADS = 4
HG_DIM = 128
HG_WIDTH = HG_HEADS * HG_DIM
FX_HEADS = 4
FX_DIM = 64
FX_WIDTH = FX_HEADS * FX_DIM
MEM_HEADS = 4
MEM_DIM = 64
MEM_WIDTH = MEM_HEADS * MEM_DIM
HG_GROUP = 128
LN_EPS = 1e-5
RMS_EPS = 1e-6
LANES = 128
SUBLANES = 8
VMEM_LIMIT = 56 * 1024 * 1024
AUG = 3
LOG2E = 1.4426950408889634
SAMPLE_CHUNK = 512
OUT_ROWS = 256
PROJ_TILE = 512
ATTN_TILE = 256
HGRN_TILE = 2048
OUT_TILE = 2048

C_HQ, C_HK, C_HV, C_HG, C_HGATE = 0, 512, 1024, 1536, 2048
ACT_PROMPT = dict(FGATE=2560, MQ=2816, MGATE=3072, WIDTH=3328)
ACT_SAMPLE = dict(FQ=2560, FGATE=2816, MQ=3072, MGATE=3328, WIDTH=3584)
W_HQ, W_HF, W_HI, W_HGATE, W_FQ, W_FK, W_FV, W_FGATE, W_MQ, W_MGATE = (
    0, 512, 1024, 1536, 2048, 2304, 2560, 2816, 3072, 3328)


def _nt(a, b, precision=None):
    return lax.dot_general(a, b, (((1,), (1,)), ((), ())), precision=precision,
                           preferred_element_type=F32)


def _mm(a, b, precision=None):
    return jnp.dot(a, b, precision=precision, preferred_element_type=F32)


def _silu(x):
    return x * jax.nn.sigmoid(x)


def _log_sigmoid(x):
    return jnp.minimum(x, 0.0) - jnp.log(1.0 + jnp.exp(-jnp.abs(x)))


def _head_mask(hh):
    lane = lax.broadcasted_iota(jnp.int32, (1, LANES), 1)
    return (lane // FX_DIM) == hh


def _pair_lanes(h):
    return slice((h // 2) * LANES, (h // 2 + 1) * LANES)


def _split3(x):
    hi = x.astype(BF16).astype(F32)
    r = x - hi
    mid = r.astype(BF16).astype(F32)
    return hi, mid, r - mid


def _aug_operand(x_pair, d, hh, is_query):
    lane = lax.broadcasted_iota(jnp.int32, (1, LANES), 1)
    a = lane - (1 - hh) * FX_DIM
    hi, mid, lo = _split3(d if is_query else -d)
    p0, o0 = (0, AUG) if is_query else (AUG, 0)
    extra = jnp.where(a == p0, hi, jnp.where(a == p0 + 1, mid, jnp.where(a == p0 + 2, lo,
                      jnp.where((a >= o0) & (a < o0 + AUG), 1.0, 0.0))))
    return jnp.where(_head_mask(hh), x_pair, extra).astype(BF16)


def _proj_kernel(x_ref, w_ref, wfl_ref, wflt_ref, bfl_ref, bflt_ref, lbs_ref, *refs,
                 layer, seg, tiles_per_seq, prompt):
    nprev = 3 if prompt and layer > 0 else 0
    prev, refs = refs[:nprev], refs[nprev:]
    act_ref = refs[0]
    cols = ACT_PROMPT if prompt else ACT_SAMPLE
    tm = x_ref.shape[0]
    xb = x_ref[...].astype(BF16)

    def proj(c0, width):
        return _mm(xb, w_ref[:, c0:c0 + width])

    lbs = lbs_ref[...]
    e = jnp.exp(lbs - jnp.max(lbs, axis=0, keepdims=True))
    sm = e / jnp.sum(e, axis=0, keepdims=True)
    lb = jnp.sum(sm[:layer + 1], axis=0, keepdims=True) - sm[0:1]

    lf_nat = _log_sigmoid(_mm(xb, wfl_ref[...]) + bfl_ref[...])
    lf_t = _log_sigmoid(_nt(wflt_ref[...], xb) + bflt_ref[...])

    r = lax.broadcasted_iota(jnp.int32, (tm, tm), 0)
    c = lax.broadcasted_iota(jnp.int32, (tm, tm), 1)
    same = (r // seg) == (c // seg)
    lower = jnp.where((c <= r) & same, 1.0, 0.0).astype(BF16)
    d_nat = _cumsum_rows([lf_nat], lower)[0]

    fq = proj(W_FQ, FX_WIDTH)
    k = proj(W_FK, FX_WIDTH)
    v = proj(W_FV, FX_WIDTH)
    if prompt:
        _, qa_ref, ka_ref, kt_ref, vt_ref, lft_ref, cn_ref, kv_ref = refs
        if tiles_per_seq > 1:
            @pl.when(pl.program_id(0) % tiles_per_seq == 0)
            def _():
                cn_ref[...] = jnp.zeros_like(cn_ref)
            d_nat = d_nat + cn_ref[0:1, :]
            cn_ref[...] = jnp.broadcast_to(d_nat[tm - 1:tm, :], cn_ref.shape)
        for src, dst in zip(prev, (kt_ref, vt_ref, lft_ref)):
            dst[0:layer] = src[...]
        lft_ref[layer, 0] = lf_t[:FX_HEADS]
        kv_ref[:, 0:FX_WIDTH] = k
        kv_ref[:, FX_WIDTH:] = v
        kt_ref[layer, 0] = kv_ref[:, 0:FX_WIDTH].T
        vt_ref[layer, 0] = kv_ref[:, FX_WIDTH:].T
        d2 = d_nat * LOG2E
        for h in range(FX_HEADS):
            hl = slice(h * LANES, (h + 1) * LANES)
            qa_ref[:, hl] = _aug_operand(fq[:, _pair_lanes(h)] * (FX_DIM ** -0.5 * LOG2E),
                                         d2[:, h:h + 1], h % 2, True)
            ka_ref[:, hl] = _aug_operand(k[:, _pair_lanes(h)], d2[:, h:h + 1], h % 2, False)
    else:
        _, k_ref, v_ref, lft_ref, dn_ref, dt_ref = refs
        upper = jnp.where((r <= c) & same, 1.0, 0.0).astype(F32)
        dt_ref[...] = _mm(lf_t, upper, precision=lax.Precision.HIGHEST)
        lft_ref[...] = lf_t
        k_ref[...] = k
        v_ref[...] = v
        dn_ref[...] = d_nat
        act_ref[:, cols["FQ"]:cols["FQ"] + FX_WIDTH] = fq

    act_ref[:, C_HQ:C_HQ + HG_WIDTH] = _silu(proj(W_HQ, HG_WIDTH))
    forget = lb + (1.0 - lb) * jax.nn.sigmoid(proj(W_HF, HG_WIDTH))
    act_ref[:, C_HK:C_HK + HG_WIDTH] = 1.0 - forget
    act_ref[:, C_HG:C_HG + HG_WIDTH] = jnp.log2(forget)
    act_ref[:, C_HV:C_HV + HG_WIDTH] = proj(W_HI, HG_WIDTH)
    act_ref[:, C_HGATE:C_HGATE + HG_WIDTH] = _silu(proj(W_HGATE, HG_WIDTH))
    act_ref[:, cols["FGATE"]:cols["FGATE"] + FX_WIDTH] = _silu(proj(W_FGATE, FX_WIDTH))
    act_ref[:, cols["MQ"]:cols["MQ"] + MEM_WIDTH] = proj(W_MQ, MEM_WIDTH)
    act_ref[:, cols["MGATE"]:cols["MGATE"] + MEM_WIDTH] = _silu(proj(W_MGATE, MEM_WIDTH))


def _proj(x, w_main, w_fl, w_flt, b_fl, b_flt, lbs, prev=(), *, layer, batch, seq_len, tm, prompt):
    m, d = x.shape
    ntiles = m // tm
    seg = min(seq_len, tm)
    tps = max(seq_len // tm, 1)
    assert prompt or ntiles == 1
    assert len(prev) == (3 if prompt and layer > 0 else 0)
    kern = functools.partial(_proj_kernel, layer=layer, seg=seg, tiles_per_seq=tps, prompt=prompt)
    full = lambda shape: pl.BlockSpec(shape, lambda i: (0,) * len(shape))
    rows = lambda width: pl.BlockSpec((tm, width), lambda i: (i, 0))
    feat = lambda nl, nfeat: pl.BlockSpec((nl, 1, nfeat, tm), lambda i: (0, i // tps, 0, i % tps))
    prev_specs = []
    if prompt:
        nfeats = (FX_WIDTH, FX_WIDTH, FX_HEADS)
        prev_specs = [feat(layer, nf) for nf in nfeats] if layer > 0 else []
        out_specs = [rows(ACT_PROMPT["WIDTH"]), rows(FX_HEADS * LANES), rows(FX_HEADS * LANES)]
        out_specs += [feat(layer + 1, nf) for nf in nfeats]
        out_shape = [jax.ShapeDtypeStruct((m, ACT_PROMPT["WIDTH"]), F32),
                     jax.ShapeDtypeStruct((m, FX_HEADS * LANES), BF16),
                     jax.ShapeDtypeStruct((m, FX_HEADS * LANES), BF16)]
        out_shape += [jax.ShapeDtypeStruct((layer + 1, batch, nf, seq_len), F32) for nf in nfeats]
        scratch = [pltpu.VMEM((SUBLANES, LANES), F32), pltpu.VMEM((tm, 2 * FX_WIDTH), F32)]
    else:
        out_specs = [rows(ACT_SAMPLE["WIDTH"]), rows(FX_WIDTH), rows(FX_WIDTH), full((SUBLANES, m)),
                     rows(LANES), full((SUBLANES, m))]
        out_shape = [jax.ShapeDtypeStruct((m, ACT_SAMPLE["WIDTH"]), F32),
                     jax.ShapeDtypeStruct((m, FX_WIDTH), F32),
                     jax.ShapeDtypeStruct((m, FX_WIDTH), F32),
                     jax.ShapeDtypeStruct((SUBLANES, m), F32),
                     jax.ShapeDtypeStruct((m, LANES), F32),
                     jax.ShapeDtypeStruct((SUBLANES, m), F32)]
        scratch = []
    return pl.pallas_call(
        kern,
        grid=(ntiles,),
        in_specs=[
            pl.BlockSpec((tm, d), lambda i: (i, 0)),
            full(w_main.shape), full(w_fl.shape), full(w_flt.shape),
            full(b_fl.shape), full(b_flt.shape), full(lbs.shape),
        ] + prev_specs,
        out_specs=out_specs,
        out_shape=out_shape,
        scratch_shapes=scratch,
        compiler_params=pltpu.CompilerParams(
            dimension_semantics=("arbitrary",), vmem_limit_bytes=VMEM_LIMIT),
        name="proj_prompt" if prompt else "proj_sample",
    )(x, w_main, w_fl, w_flt, b_fl, b_flt, lbs, *prev)


def _level_exponent(g, g2, m, rows):
    n, w = g.shape
    pos = rows % (2 * m)
    if m == 1:
        return jnp.where(pos == 1, g, 0.0)
    if m == 2:
        nxt, prv = pltpu.roll(g, n - 1, axis=0), pltpu.roll(g, 1, axis=0)
        return jnp.where(pos == 0, nxt, jnp.where(pos == 1, 0.0, jnp.where(pos == 2, g, g + prv)))
    g3 = g2.reshape(n // (2 * m), 2 * m, w)
    ref = jnp.broadcast_to(g3[:, m - 1:m, :], g3.shape).reshape(n, w)
    return jnp.where(pos < m, ref - g2, g2 - ref)


def _pair_levels(n):
    rr = lax.broadcasted_iota(jnp.int32, (n, n), 0)
    cc = lax.broadcasted_iota(jnp.int32, (n, n), 1)
    xf = (rr ^ cc).astype(F32)
    level = (lax.bitcast_convert_type(xf, jnp.int32) >> 23) - 127
    return jnp.where(rr > cc, level, -1)


def _cumsum_rows(gs, lower):
    w = gs[0].shape[1]
    pieces = [p.astype(BF16) for g in gs for p in _split3(g)]
    s = _mm(lower, jnp.concatenate(pieces, axis=1))
    return [s[:, (3 * i) * w:(3 * i + 1) * w] + s[:, (3 * i + 1) * w:(3 * i + 2) * w]
            + s[:, (3 * i + 2) * w:(3 * i + 3) * w] for i in range(len(gs))]


def _hgrn_intra(q, k, g, g2, levels):
    n, w = q.shape
    rows = lax.broadcasted_iota(jnp.int32, (n, w), 0)
    nslab = n // SUBLANES
    slab = lambda x, r: x[r * SUBLANES:(r + 1) * SUBLANES]
    a = [jnp.zeros((SUBLANES, n), F32)] * nslab
    m, bit = n // 2, n.bit_length() - 2
    while m >= 1:
        if m >= SUBLANES:
            qs, ks, targets = [], [], []
            for r0 in range(0, n, 2 * m):
                first, second = slice(r0, r0 + m), slice(r0 + m, r0 + 2 * m)
                ref = g2[r0 + m - 1:r0 + m]
                ks += [k[first] * jnp.exp2(ref - g2[first]), jnp.zeros((m, w), F32)]
                qs.append(q[second] * jnp.exp2(g2[second] - ref))
                targets += range((r0 + m) // SUBLANES, (r0 + 2 * m) // SUBLANES)
            a_m = _nt(jnp.concatenate(qs, axis=0).astype(BF16), jnp.concatenate(ks, axis=0).astype(BF16))
            for i, r in enumerate(targets):
                a[r] = jnp.where(slab(levels, r) == bit, slab(a_m, i), a[r])
        else:
            f = jnp.exp2(_level_exponent(g, g2, m, rows))
            a_m = _nt((q * f).astype(BF16), (k * f).astype(BF16))
            a = [jnp.where(slab(levels, r) == bit, slab(a_m, r), a[r]) for r in range(nslab)]
        m, bit = m // 2, bit - 1
    return jnp.concatenate(a, axis=0)


def _hgrn_kernel(*refs, group, has_init):
    if has_init:
        q_ref, k_ref, v_ref, g_ref, gate_ref, ng_ref, s0_ref, o_ref, so_ref = refs[:9]
    else:
        q_ref, k_ref, v_ref, g_ref, gate_ref, ng_ref, o_ref, so_ref = refs[:8]
    st_ref, a_ref, qe_ref, ke_ref, vb_ref, dg_ref, dec_ref = refs[-7:]
    t = pl.program_id(2)
    nheads = st_ref.shape[0]
    ngroups = q_ref.shape[0] // group
    lanes = [slice(hh * HG_DIM, (hh + 1) * HG_DIM) for hh in range(nheads)]

    @pl.when(t == 0)
    def _():
        for hh in range(nheads):
            st_ref[hh] = s0_ref[0, 0, hh].T if has_init else jnp.zeros((HG_DIM, HG_DIM), F32)

    ng = ng_ref[...]
    levels = _pair_levels(group)
    rr = lax.broadcasted_iota(jnp.int32, (group, group), 0)
    cc = lax.broadcasted_iota(jnp.int32, (group, group), 1)
    lower = jnp.where(rr >= cc, 1.0, 0.0).astype(BF16)

    def prepare(i, s):
        sl = pl.ds(pl.multiple_of(i * group, group), group)
        gs = [g_ref[sl, ls] for ls in lanes]
        g2s = _cumsum_rows(gs, lower)
        for hh, ls in enumerate(lanes):
            q, k, v, g2 = q_ref[sl, ls], k_ref[sl, ls], v_ref[sl, ls], g2s[hh]
            g_last = g2[group - 1:group, :]
            a_ref[s, hh] = _hgrn_intra(q, k, gs[hh], g2, levels).astype(BF16)
            qe_ref[s, hh] = (q * jnp.exp2(g2)).astype(BF16)
            ke_ref[s, hh] = (k * jnp.exp2(g_last - g2)).astype(BF16)
            vb_ref[s, hh] = v.astype(BF16)
            dg_ref[s, hh] = jnp.sum(q * k, axis=-1, keepdims=True) * v
            dec_ref[s, hh] = jnp.broadcast_to(jnp.exp2(g_last), (SUBLANES, HG_DIM))

    def finish(i, s):
        sl = pl.ds(pl.multiple_of(i * group, group), group)
        for hh, ls in enumerate(lanes):
            st = st_ref[hh]
            vb = vb_ref[s, hh]
            o = _nt(qe_ref[s, hh], st.astype(BF16)) + _mm(a_ref[s, hh], vb) + dg_ref[s, hh]
            upd = lax.dot_general(vb, ke_ref[s, hh], (((0,), (0,)), ((), ())), preferred_element_type=F32)
            st_ref[hh] = st * dec_ref[s, hh, 0:1, :] + upd
            o = o * lax.rsqrt(jnp.mean(o * o, axis=-1, keepdims=True) + RMS_EPS) * ng
            o_ref[sl, ls] = (o * gate_ref[sl, ls]).astype(o_ref.dtype)

    prepare(0, 0)
    if ngroups > 1:
        def step(ip, c):
            finish(2 * ip, 0)
            prepare(2 * ip + 1, 1)
            finish(2 * ip + 1, 1)
            prepare(2 * ip + 2, 0)
            return c

        lax.fori_loop(0, ngroups // 2 - 1, step, 0)
        finish(ngroups - 2, 0)
        prepare(ngroups - 1, 1)
        finish(ngroups - 1, 1)
    else:
        finish(0, 0)

    @pl.when(t == pl.num_programs(2) - 1)
    def _():
        for hh in range(nheads):
            so_ref[0, hh] = st_ref[hh].T


def _hgrn(act, norm_g, s0, *, layer, batch, seq_len, tt, nheads):
    m = act.shape[0]
    nt = seq_len // tt
    has_init = s0 is not None
    group = min(tt, HG_GROUP)
    assert tt == group or (tt // group) % 2 == 0, "groups are processed in pairs"
    width = nheads * HG_DIM
    blk = lambda cb: pl.BlockSpec((tt, width), lambda b, h, t, cb=cb: (b * nt + t, cb // width + h))
    in_specs = [blk(C_HQ), blk(C_HK), blk(C_HV), blk(C_HG), blk(C_HGATE),
                pl.BlockSpec((1, HG_DIM), lambda b, h, t: (0, 0))]
    args = [act, act, act, act, act, norm_g]
    if has_init:
        in_specs.append(pl.BlockSpec((1, 1, nheads, HG_DIM, HG_DIM), lambda b, h, t: (layer, b, h, 0, 0)))
        args.append(s0)
    return pl.pallas_call(
        functools.partial(_hgrn_kernel, group=group, has_init=has_init),
        grid=(batch, HG_HEADS // nheads, nt),
        in_specs=in_specs,
        out_specs=[
            pl.BlockSpec((tt, width), lambda b, h, t: (b * nt + t, h)),
            pl.BlockSpec((1, nheads, HG_DIM, HG_DIM), lambda b, h, t: (b, h, 0, 0)),
        ],
        out_shape=[
            jax.ShapeDtypeStruct((m, HG_WIDTH), BF16),
            jax.ShapeDtypeStruct((batch, HG_HEADS, HG_DIM, HG_DIM), F32),
        ],
        scratch_shapes=[
            pltpu.VMEM((nheads, HG_DIM, HG_DIM), F32),
            pltpu.VMEM((2, nheads, group, group), BF16),
            pltpu.VMEM((2, nheads, group, HG_DIM), BF16),
            pltpu.VMEM((2, nheads, group, HG_DIM), BF16),
            pltpu.VMEM((2, nheads, group, HG_DIM), BF16),
            pltpu.VMEM((2, nheads, group, HG_DIM), F32),
            pltpu.VMEM((2, nheads, SUBLANES, HG_DIM), F32),
        ],
        compiler_params=pltpu.CompilerParams(
            dimension_semantics=("arbitrary", "arbitrary", "arbitrary"), vmem_limit_bytes=VMEM_LIMIT),
        name="hgrn",
    )(*args)


VROWS = FX_DIM + 16


def _softmax_t(st, m):
    m_new = jnp.maximum(m, jnp.max(st, axis=0, keepdims=True))
    return m_new, jnp.exp2(m - m_new), jnp.exp2(st - m_new).astype(BF16)


def _with_ones(vt):
    return jnp.concatenate([vt.astype(BF16), jnp.ones((VROWS - FX_DIM, vt.shape[1]), BF16)], axis=0)


def _normalised(acc):
    return acc[0:FX_DIM] / acc[FX_DIM:FX_DIM + 1]


def _attn_prompt_kernel(qa_ref, fgate_ref, mq_ref, mgate_ref, ka_ref, vt_ref, mk_ref, mvt_ref,
                        o_ref, vtb_ref, mvtb_ref, st_ref, p_ref, acc_ref, *, tq):
    seq = qa_ref.shape[0]
    tk = tq
    ntiles = seq // tq
    heads = range(FX_HEADS)

    for j in range(seq // tk):
        for h in heads:
            vtb_ref[j, h] = _with_ones(vt_ref[0, 0, h * FX_DIM:(h + 1) * FX_DIM, j * tk:(j + 1) * tk])
    for h in range(MEM_HEADS):
        mvtb_ref[h] = _with_ones(mvt_ref[0, 0, h * MEM_DIM:(h + 1) * MEM_DIM, :])

    key = lax.broadcasted_iota(jnp.int32, (tk, tq), 0)
    qry = lax.broadcasted_iota(jnp.int32, (tk, tq), 1)

    def tile_rows(i):
        return pl.ds(pl.multiple_of(i * tq, tq), tq)

    def scores(i, j, h):
        hl = slice(h * LANES, (h + 1) * LANES)
        return _nt(ka_ref[tile_rows(j), hl], qa_ref[tile_rows(i), hl])

    def start(i, s):
        for h in heads:
            st_ref[s, 0, h] = scores(i, 0, h)
            p_ref[s, 1, h] = jnp.zeros((tk, tq), BF16)
            acc_ref[s, h] = jnp.zeros((VROWS, tq), F32)

    def block(i, s, j, b, carry):
        prev = jnp.maximum(j - 1, 0)
        out = []
        for h in heads:
            m, alpha_prev = carry[h]
            pv = _mm(vtb_ref[prev, h], p_ref[s, 1 - b, h])
            st = st_ref[s, b, h]
            st_ref[s, 1 - b, h] = scores(i, j + 1, h)
            m, alpha, p = _softmax_t(st, m)
            acc_ref[s, h] = alpha_prev * acc_ref[s, h] + pv
            p_ref[s, b, h] = p
            out.append((m, alpha))
        return tuple(out)

    def finish(i, s, carry):
        rows = tile_rows(i)
        prev = jnp.maximum(i - 1, 0)
        o_heads = []
        for h in heads:
            m, alpha_prev = carry[h]
            acc = alpha_prev * acc_ref[s, h] + _mm(vtb_ref[prev, h], p_ref[s, 1 - s, h])
            m, alpha, p = _softmax_t(jnp.where(key <= qry, st_ref[s, s, h], -jnp.inf), m)
            acc = alpha * acc + _mm(vtb_ref[i, h], p)
            o_heads.append(_normalised(acc))
        for pr in range(FX_HEADS // 2):
            ls = _pair_lanes(2 * pr)
            o_t = jnp.concatenate(o_heads[2 * pr:2 * pr + 2], axis=0)
            o_ref[rows, ls] = (o_t.T * fgate_ref[rows, ls]).astype(o_ref.dtype)

    def mem_scores(i, s):
        rows = tile_rows(i)
        for h in range(MEM_HEADS):
            mq_pair = mq_ref[rows, _pair_lanes(h)] * (MEM_DIM ** -0.5 * LOG2E)
            qm = jnp.where(_head_mask(h % 2), mq_pair, 0.0).astype(BF16)
            st_ref[s, 0, h] = _nt(mk_ref[0, :, _pair_lanes(h)], qm)

    def mem_finish(i, s):
        rows = tile_rows(i)
        for pr in range(MEM_HEADS // 2):
            ls = _pair_lanes(2 * pr)
            halves = []
            for hh in range(2):
                st = st_ref[s, 0, 2 * pr + hh]
                pm = jnp.exp2(st - jnp.max(st, axis=0, keepdims=True))
                halves.append(_normalised(_mm(mvtb_ref[2 * pr + hh], pm.astype(BF16))))
            o_t = jnp.concatenate(halves, axis=0)
            lm = slice(FX_WIDTH + pr * LANES, FX_WIDTH + (pr + 1) * LANES)
            o_ref[rows, lm] = (o_t.T * mgate_ref[rows, ls]).astype(o_ref.dtype)

    mem_scores(0, 0)

    def mem_pair(ip, c):
        mem_finish(2 * ip, 0)
        mem_scores(2 * ip + 1, 1)
        mem_finish(2 * ip + 1, 1)
        mem_scores(jnp.minimum(2 * ip + 2, ntiles - 1), 0)
        return c

    lax.fori_loop(0, ntiles // 2, mem_pair, 0)

    def qtile(ip, s):
        i = 2 * ip + s

        def two_blocks(jp, carry):
            return block(i, s, 2 * jp + 1, 1, block(i, s, 2 * jp, 0, carry))

        init = tuple((jnp.full((1, tq), -jnp.inf, F32), jnp.ones((1, tq), F32)) for _ in heads)
        carry = lax.fori_loop(0, ip, two_blocks, init)
        if s == 1:
            carry = block(i, s, i - 1, 0, carry)
        finish(i, s, carry)
        start(jnp.minimum(i + 1, ntiles - 1), 1 - s)

    start(0, 0)

    def tile_pair(ip, c):
        qtile(ip, 0)
        qtile(ip, 1)
        return c

    lax.fori_loop(0, ntiles // 2, tile_pair, 0)


def _attn_prompt(act, qa, ka, vt, mk, mvt, *, layer, batch, seq_len, tq):
    m = act.shape[0]
    n_mem = mk.shape[1] // batch
    tk = tq
    assert (seq_len // tq) % 2 == 0, "query tiles are processed in pairs"
    ablk = lambda c0: pl.BlockSpec((seq_len, FX_WIDTH), lambda b, c0=c0: (b, c0 // FX_WIDTH))
    return pl.pallas_call(
        functools.partial(_attn_prompt_kernel, tq=tq),
        grid=(batch,),
        in_specs=[
            pl.BlockSpec((seq_len, FX_HEADS * LANES), lambda b: (b, 0)),
            ablk(ACT_PROMPT["FGATE"]), ablk(ACT_PROMPT["MQ"]), ablk(ACT_PROMPT["MGATE"]),
            pl.BlockSpec((seq_len, FX_HEADS * LANES), lambda b: (b, 0)),
            pl.BlockSpec((1, 1, FX_WIDTH, seq_len), lambda b: (layer, b, 0, 0)),
            pl.BlockSpec((1, n_mem, MEM_WIDTH), lambda b: (layer, b, 0)),
            pl.BlockSpec((1, 1, MEM_WIDTH, n_mem), lambda b: (layer, b, 0, 0)),
        ],
        out_specs=pl.BlockSpec((seq_len, FX_WIDTH + MEM_WIDTH), lambda b: (b, 0)),
        out_shape=jax.ShapeDtypeStruct((m, FX_WIDTH + MEM_WIDTH), BF16),
        scratch_shapes=[
            pltpu.VMEM((seq_len // tk, FX_HEADS, VROWS, tk), BF16),
            pltpu.VMEM((MEM_HEADS, VROWS, n_mem), BF16),
            pltpu.VMEM((2, 2, FX_HEADS, tk, tq), F32),
            pltpu.VMEM((2, 2, FX_HEADS, tk, tq), BF16),
            pltpu.VMEM((2, FX_HEADS, VROWS, tq), F32),
        ],
        compiler_params=pltpu.CompilerParams(
            dimension_semantics=("arbitrary",), vmem_limit_bytes=VMEM_LIMIT),
        name="attn_prompt",
    )(qa, act, act, act, ka, vt, mk, mvt)


def _stack_heads(x, scale):
    lane = lax.broadcasted_iota(jnp.int32, (1, x.shape[1]), 1)
    return jnp.concatenate([jnp.where(lane // FX_DIM == h, x * scale, 0.0) for h in range(FX_HEADS)],
                           axis=0).astype(BF16)


def _unstack_heads(o, ts):
    lane = lax.broadcasted_iota(jnp.int32, (1, o.shape[1]), 1)
    out = jnp.zeros((ts, o.shape[1]), F32)
    for h in range(FX_HEADS):
        out = jnp.where(lane // FX_DIM == h, o[h * ts:(h + 1) * ts, :], out)
    return out


def _per_head_rows(vals, ts):
    return jnp.concatenate([jnp.broadcast_to(v, (ts, v.shape[1])) for v in vals], axis=0)


def _attn_sample_kernel(fq_ref, fgate_ref, mq_ref, mgate_ref, kn_ref, vn_ref, dn_ref, dt_ref,
                        pkt_ref, pvt_ref, plf_ref, mkt_ref, mvt_ref, o_ref, s_ref, *, chunk):
    ts = fq_ref.shape[0]
    n_new = kn_ref.shape[0]
    past = pkt_ref.shape[3]
    nchunks = past // chunk
    b = pl.program_id(0)
    hrows = FX_HEADS * ts

    r = lax.broadcasted_iota(jnp.int32, (chunk, chunk), 0)
    c = lax.broadcasted_iota(jnp.int32, (chunk, chunk), 1)
    upper = jnp.where(r <= c, 1.0, 0.0).astype(BF16)
    chunks = [slice(ci * chunk, (ci + 1) * chunk) for ci in range(nchunks)]
    pieces = jnp.concatenate([p.astype(BF16) for p in _split3(plf_ref[0, 0])], axis=0)
    local = _mm(pieces, upper)
    nrows = nchunks * SUBLANES
    local = local[0:nrows] + local[nrows:2 * nrows] + local[2 * nrows:]
    d_past, total = [], jnp.zeros((SUBLANES, 1), F32)
    for ci in range(nchunks):
        d = local[ci * SUBLANES:(ci + 1) * SUBLANES]
        d_past.append(d + total)
        total = total + d[:, chunk - 1:chunk]

    dn = dn_ref[...]
    dt_new = dt_ref[...]
    tot = [total[h:h + 1, :] for h in range(FX_HEADS)]
    dq = jnp.concatenate([tot[h] + dn[:, h:h + 1] for h in range(FX_HEADS)], axis=0)
    q4 = _stack_heads(fq_ref[...], FX_DIM ** -0.5)

    for ci, cs in enumerate(chunks):
        dk = _per_head_rows([d_past[ci][h:h + 1, :] for h in range(FX_HEADS)], ts)
        s_ref[:, cs] = _mm(q4, pkt_ref[0, 0, :, cs].astype(BF16)) + dq - dk
    row = lax.broadcasted_iota(jnp.int32, (hrows, n_new), 0)
    col = lax.broadcasted_iota(jnp.int32, (hrows, n_new), 1)
    visible = (col // ts == b) & (col % ts <= row % ts)
    dk = _per_head_rows([tot[h] + dt_new[h:h + 1, :] for h in range(FX_HEADS)], ts)
    s_new = jnp.where(visible, _nt(q4, kn_ref[...].astype(BF16)) + dq - dk, -jnp.inf)

    m = jnp.max(s_new, axis=-1, keepdims=True)
    for cs in chunks:
        m = jnp.maximum(m, jnp.max(s_ref[:, cs], axis=-1, keepdims=True))
    p_new = jnp.exp(s_new - m)
    l = jnp.sum(p_new, axis=-1, keepdims=True)
    acc = _mm(p_new.astype(BF16), vn_ref[...].astype(BF16))
    for cs in chunks:
        p = jnp.exp(s_ref[:, cs] - m)
        l = l + jnp.sum(p, axis=-1, keepdims=True)
        acc = acc + _nt(p.astype(BF16), pvt_ref[0, 0, :, cs].astype(BF16))
    o_fx = _unstack_heads(acc / l, ts) * fgate_ref[...]
    o_ref[:, 0:FX_WIDTH] = o_fx.astype(o_ref.dtype)

    mq4 = _stack_heads(mq_ref[...], MEM_DIM ** -0.5)
    s = _mm(mq4, mkt_ref[0, 0].astype(BF16))
    p = jnp.exp(s - jnp.max(s, axis=-1, keepdims=True))
    o = _nt(p.astype(BF16), mvt_ref[0, 0].astype(BF16)) / jnp.sum(p, axis=-1, keepdims=True)
    o_mem = _unstack_heads(o, ts) * mgate_ref[...]
    o_ref[:, FX_WIDTH:] = o_mem.astype(o_ref.dtype)


def _attn_sample(act, k_new, v_new, d_nat, d_t, past_kt, past_vt, past_lft, mem_kt, mem_vt,
                 *, layer, batch, ts):
    m = act.shape[0]
    past = past_kt.shape[3]
    n_mem = mem_kt.shape[3]
    chunk = past_lft.shape[3]
    ablk = lambda c0: pl.BlockSpec((ts, FX_WIDTH), lambda b, c0=c0: (b, c0 // FX_WIDTH))
    cache = lambda nfeat, n: pl.BlockSpec((1, 1, nfeat, n), lambda b: (layer, b, 0, 0))
    return pl.pallas_call(
        functools.partial(_attn_sample_kernel, chunk=chunk),
        grid=(batch,),
        in_specs=[
            ablk(ACT_SAMPLE["FQ"]), ablk(ACT_SAMPLE["FGATE"]), ablk(ACT_SAMPLE["MQ"]),
            ablk(ACT_SAMPLE["MGATE"]),
            pl.BlockSpec((m, FX_WIDTH), lambda b: (0, 0)),
            pl.BlockSpec((m, FX_WIDTH), lambda b: (0, 0)),
            pl.BlockSpec((ts, LANES), lambda b: (b, 0)),
            pl.BlockSpec((SUBLANES, m), lambda b: (0, 0)),
            cache(FX_WIDTH, past), cache(FX_WIDTH, past), cache(past // chunk * SUBLANES, chunk),
            cache(MEM_WIDTH, n_mem), cache(MEM_WIDTH, n_mem),
        ],
        out_specs=pl.BlockSpec((ts, FX_WIDTH + MEM_WIDTH), lambda b: (b, 0)),
        out_shape=jax.ShapeDtypeStruct((m, FX_WIDTH + MEM_WIDTH), BF16),
        scratch_shapes=[pltpu.VMEM((FX_HEADS * ts, past), F32)],
        compiler_params=pltpu.CompilerParams(
            dimension_semantics=("arbitrary",), vmem_limit_bytes=VMEM_LIMIT),
        name="attn_sample",
    )(act, act, act, act, k_new, v_new, d_nat, d_t, past_kt, past_vt, past_lft, mem_kt, mem_vt)


def _memkv_kernel(x_ref, w_ref, kb_ref, kt_ref, vt_ref):
    batch, _, n_mem = kt_ref.shape[1:]
    kv = _mm(x_ref[...].astype(BF16), w_ref[0])
    kb_ref[0] = kv[:, :MEM_WIDTH].astype(BF16)
    for b in range(batch):
        rows = slice(b * n_mem, (b + 1) * n_mem)
        kt_ref[0, b] = kv[rows, :MEM_WIDTH].T
        vt_ref[0, b] = kv[rows, MEM_WIDTH:].T


def _memkv(mem, w, *, batch):
    m, d = mem.shape
    depth = w.shape[0]
    n_mem = m // batch
    return pl.pallas_call(
        _memkv_kernel,
        grid=(depth,),
        in_specs=[pl.BlockSpec((m, d), lambda l: (0, 0)),
                  pl.BlockSpec((1,) + w.shape[1:], lambda l: (l, 0, 0))],
        out_specs=[pl.BlockSpec((1, m, MEM_WIDTH), lambda l: (l, 0, 0)),
                   pl.BlockSpec((1, batch, MEM_WIDTH, n_mem), lambda l: (l, 0, 0, 0)),
                   pl.BlockSpec((1, batch, MEM_WIDTH, n_mem), lambda l: (l, 0, 0, 0))],
        out_shape=[jax.ShapeDtypeStruct((depth, m, MEM_WIDTH), BF16),
                   jax.ShapeDtypeStruct((depth, batch, MEM_WIDTH, n_mem), F32),
                   jax.ShapeDtypeStruct((depth, batch, MEM_WIDTH, n_mem), F32)],
        compiler_params=pltpu.CompilerParams(
            dimension_semantics=("arbitrary",), vmem_limit_bytes=VMEM_LIMIT),
        name="memkv",
    )(mem, w)


def _out_kernel(x_ref, hg_ref, at_ref, w_ref, g_ref, b_ref, y_ref, *, alpha):
    tm = x_ref.shape[0]
    rows_per = min(tm, OUT_ROWS)
    for r0 in range(0, tm, rows_per):
        rs = slice(r0, r0 + rows_per)
        acc = _mm(hg_ref[rs, :], w_ref[0:HG_WIDTH, :]) + _mm(at_ref[rs, :], w_ref[HG_WIDTH:, :])
        z = alpha * x_ref[rs, :] + acc
        mu = jnp.mean(z, axis=-1, keepdims=True)
        zc = z - mu
        var = jnp.mean(zc * zc, axis=-1, keepdims=True)
        y_ref[rs, :] = zc * lax.rsqrt(var + LN_EPS) * g_ref[...] + b_ref[...]


def _out(x, mix_hg, mix_at, w, g, b, *, alpha, tm):
    m, d = x.shape
    return pl.pallas_call(
        functools.partial(_out_kernel, alpha=alpha),
        grid=(m // tm,),
        in_specs=[
            pl.BlockSpec((tm, d), lambda i: (i, 0)),
            pl.BlockSpec((tm, HG_WIDTH), lambda i: (i, 0)),
            pl.BlockSpec((tm, FX_WIDTH + MEM_WIDTH), lambda i: (i, 0)),
            pl.BlockSpec(w.shape, lambda i: (0, 0)),
            pl.BlockSpec((1, d), lambda i: (0, 0)),
            pl.BlockSpec((1, d), lambda i: (0, 0)),
        ],
        out_specs=pl.BlockSpec((tm, d), lambda i: (i, 0)),
        out_shape=jax.ShapeDtypeStruct((m, d), F32),
        compiler_params=pltpu.CompilerParams(
            dimension_semantics=("arbitrary",), vmem_limit_bytes=VMEM_LIMIT),
        name="out",
    )(x, mix_hg, mix_at, w, g, b)


def _pick_tile(n, target):
    t = min(n, target)
    while n % t:
        t //= 2
    return t


def _feature_major(x):
    lead = x.shape[:-3]
    n, heads, dim = x.shape[-3:]
    nl = len(lead)
    perm = tuple(range(nl)) + (nl + 1, nl + 2, nl)
    return jnp.transpose(x, perm).reshape(lead + (heads * dim, n))


def _token_major(x, heads):
    lead = x.shape[:-2]
    width, n = x.shape[-2:]
    nl = len(lead)
    x = x.reshape(lead + (heads, width // heads, n))
    return jnp.transpose(x, tuple(range(nl)) + (nl + 2, nl, nl + 1))


def kernel(x_prompt, x_sample, mem_prompt, state_hgrn, cache_fox_k, cache_fox_v, cache_fox_logf,
           cache_mem_k, cache_mem_v, w_in, b_fox_forget, hgrn_lower_bounds, hgrn_norm_g,
           w_mem_kv, w_out, ln_g, ln_b):
    depth, d_model, _ = w_in.shape
    bp, seq, _ = x_prompt.shape
    bs, dseq, _ = x_sample.shape
    n_mem = mem_prompt.shape[1]
    alpha = (2 * depth) ** 0.25

    fl0 = 4 * HG_WIDTH + 4 * FX_WIDTH
    w_main = jnp.concatenate([w_in[:, :, :fl0], w_in[:, :, fl0 + FX_HEADS:]], axis=-1).astype(BF16)
    w_flc = w_in[:, :, fl0:fl0 + FX_HEADS]
    w_fl = jnp.pad(w_flc, ((0, 0), (0, 0), (0, LANES - FX_HEADS))).astype(BF16)
    w_flt = jnp.pad(jnp.swapaxes(w_flc, 1, 2), ((0, 0), (0, SUBLANES - FX_HEADS), (0, 0))).astype(BF16)
    b_fl = jnp.pad(b_fox_forget, ((0, 0), (0, LANES - FX_HEADS)))[:, None, :]
    b_flt = jnp.pad(b_fox_forget, ((0, 0), (0, SUBLANES - FX_HEADS)))[:, :, None]
    lbs = hgrn_lower_bounds.astype(F32)
    w_memb = w_mem_kv.astype(BF16)
    w_outb = w_out.astype(BF16)

    past_kt = _feature_major(cache_fox_k)
    past_vt = _feature_major(cache_fox_v)
    past = cache_fox_logf.shape[2]
    chunk = min(past, SAMPLE_CHUNK)
    past_lft = jnp.swapaxes(cache_fox_logf, 2, 3).reshape(depth, bs, FX_HEADS, past // chunk, chunk)
    past_lft = jnp.pad(jnp.swapaxes(past_lft, 2, 3),
                       ((0, 0), (0, 0), (0, 0), (0, SUBLANES - FX_HEADS), (0, 0)))
    past_lft = past_lft.reshape(depth, bs, past // chunk * SUBLANES, chunk)
    mem_kt = _feature_major(cache_mem_k)
    mem_vt = _feature_major(cache_mem_v)

    mp, ms = bp * seq, bs * dseq
    tb = _pick_tile(seq, ATTN_TILE)
    yp = x_prompt.reshape(mp, d_model)
    ys = x_sample.reshape(ms, d_model)
    mem2 = mem_prompt.reshape(bp * n_mem, d_model)

    p_hg = []
    s_hg, s_k, s_v, s_lft = [], [], [], []
    mkb, mkt, mvt = _memkv(mem2, w_memb, batch=bp)
    fox_leaves = ()
    for l in range(depth):
        wl = (w_main[l], w_fl[l], w_flt[l], b_fl[l], b_flt[l], lbs)
        ng = hgrn_norm_g[l][None, :]
        g_ln, b_ln = ln_g[l][None, :], ln_b[l][None, :]

        act, qa, ka, *fox_leaves = _proj(yp, *wl, fox_leaves, layer=l, batch=bp, seq_len=seq,
                                         tm=_pick_tile(seq, PROJ_TILE), prompt=True)
        mix_hg, hg = _hgrn(act, ng, None, layer=l, batch=bp, seq_len=seq, tt=_pick_tile(seq, HGRN_TILE),
                           nheads=HG_HEADS)
        mix_at = _attn_prompt(act, qa, ka, fox_leaves[1], mkb, mvt, layer=l, batch=bp, seq_len=seq, tq=tb)
        yp = _out(yp, mix_hg, mix_at, w_outb[l], g_ln, b_ln, alpha=alpha, tm=_pick_tile(mp, OUT_TILE))
        p_hg.append(hg)

        act, k, v, lft, d_nat, d_t = _proj(ys, *wl, layer=l, batch=bs, seq_len=dseq, tm=ms, prompt=False)
        mix_hg, hg = _hgrn(act, ng, state_hgrn, layer=l, batch=bs, seq_len=dseq, tt=dseq, nheads=HG_HEADS)
        mix_at = _attn_sample(act, k, v, d_nat, d_t, past_kt, past_vt, past_lft, mem_kt, mem_vt,
                              layer=l, batch=bs, ts=dseq)
        ys = _out(ys, mix_hg, mix_at, w_outb[l], g_ln, b_ln, alpha=alpha, tm=ms)
        s_hg.append(hg); s_k.append(k); s_v.append(v); s_lft.append(lft)

    st = jnp.stack
    s_lf = st(s_lft)[:, :FX_HEADS, :].reshape(depth, FX_HEADS, bs, dseq)
    return (yp.reshape(bp, seq, d_model), ys.reshape(bs, dseq, d_model),
            st(p_hg),
            _token_major(fox_leaves[0], FX_HEADS),
            _token_major(fox_leaves[1], FX_HEADS),
            jnp.swapaxes(fox_leaves[2], 2, 3),
            _token_major(mkt, MEM_HEADS),
            _token_major(mvt, MEM_HEADS),
            st(s_hg),
            st(s_k).reshape(depth, bs, dseq, FX_HEADS, FX_DIM),
            st(s_v).reshape(depth, bs, dseq, FX_HEADS, FX_DIM),
            jnp.transpose(s_lf, (0, 2, 3, 1)))
```
